```python
import math
import jax, jax.numpy as jnp
from jax import lax
import numpy as np

D_MODEL = 2048
BATCH = 4
SEQ = 2048
DEPTH = 2
DEC_BATCH = 128
DEC_SEQ = 8
PAST_LEN = 2048
PAGE_SIZE = 128

HEAD_DIM = 128
NSA_HEADS = D_MODEL // (4 * HEAD_DIM)
DSA_HEADS = D_MODEL // (4 * HEAD_DIM)
GDN_HEADS = D_MODEL // (2 * HEAD_DIM)
W_A = NSA_HEADS * HEAD_DIM
W_B = DSA_HEADS * HEAD_DIM
W_C = GDN_HEADS * HEAD_DIM
CMP_LEN = 32
CMP_STRIDE = 16
CMP_HID = 2 * HEAD_DIM
SEL_BLOCK = 64
SEL_COUNT = 16
WINDOW = 512
IDX_HEADS = 16
IDX_DIM = 64
DSA_TOPK = 256
GDN_CONV = 4
GDN_CHUNK = 64
MEM_TOKENS = 256
MEM_HEADS = 4
D_FF = -(-(8 * D_MODEL) // (3 * 256)) * 256
ROPE_THETA = 10000.0
Q_BLOCK = 128
EPS = 1e-6
NEG = -1e30
FORCED = 1e9
IN_SPLITS = (W_A, 6 * HEAD_DIM, 3 * NSA_HEADS,
             W_B, 2 * HEAD_DIM, IDX_HEADS * IDX_DIM, IDX_HEADS, IDX_DIM,
             3 * W_C, GDN_HEADS, GDN_HEADS, W_C,
             3 * D_MODEL)
IN_WIDTH = sum(IN_SPLITS)

kernel_name = 'hybrid_nsa_dsa_gdn_decode_step'


def rms_norm(x, g):
    xf = x.astype(jnp.float32)
    y = xf * lax.rsqrt(jnp.mean(xf * xf, axis=-1, keepdims=True) + EPS)
    return (y * g.astype(jnp.float32)).astype(x.dtype)


def l2norm(x):
    xf = x.astype(jnp.float32)
    return xf * lax.rsqrt(jnp.sum(xf * xf, axis=-1, keepdims=True) + EPS)


def rope(x, pos):
    half = x.shape[-1] // 2
    inv = ROPE_THETA ** (-jnp.arange(half, dtype=jnp.float32) / half)
    ang = pos.astype(jnp.float32)[:, None] * inv
    cos, sin = jnp.cos(ang)[:, None, :], jnp.sin(ang)[:, None, :]
    xf = x.astype(jnp.float32)
    x1, x2 = xf[..., :half], xf[..., half:]
    return jnp.concatenate([x1 * cos - x2 * sin, x2 * cos + x1 * sin], axis=-1).astype(x.dtype)


def masked_softmax(s, mask):
    s = jnp.where(mask, s.astype(jnp.float32), NEG)
    e = jnp.where(mask, jnp.exp(s - jnp.max(s, axis=-1, keepdims=True)), 0.0)
    return e / jnp.maximum(jnp.sum(e, axis=-1, keepdims=True), 1e-30)


def split_cols(h):
    return jnp.split(h, [int(o) for o in np.cumsum(IN_SPLITS)[:-1]], axis=-1)


def gather_pages(pool, page_table):
    rows = pool[page_table]
    return rows.reshape(rows.shape[0], -1, *rows.shape[3:])


def over_query_blocks(fn, *qs):
    b, t = qs[0].shape[:2]
    if t <= Q_BLOCK:
        return fn(0, *qs)
    nb = t // Q_BLOCK
    blocks = tuple(q.reshape(b, nb, Q_BLOCK, *q.shape[2:]).swapaxes(0, 1) for q in qs)
    out = lax.map(lambda a: fn(a[0] * Q_BLOCK, *a[1]), (jnp.arange(nb), blocks))
    out = out.swapaxes(0, 1)
    return out.reshape(b, t, *out.shape[3:])


def nsa_compress(rows, pe, w1, w2):
    b, n_keys, d = rows.shape
    r = CMP_LEN // CMP_STRIDE
    n_cmp = (n_keys - CMP_LEN) // CMP_STRIDE + 1
    seg = rows[:, :(n_cmp + r - 1) * CMP_STRIDE].reshape(b, n_cmp + r - 1, CMP_STRIDE * d)
    w1r = w1.reshape(r, CMP_STRIDE * d, CMP_HID)
    h = pe.reshape(-1) @ w1 + sum(seg[:, i:i + n_cmp] @ w1r[i] for i in range(r))
    return jax.nn.gelu(h) @ w2


def nsa_attention(q, gates, k_cmp, v_cmp, k_slc, v_slc, k_win, v_win, past, win_pos0, banded):
    b, n_keys, d = k_slc.shape
    scale = d ** -0.5
    n_cmp = k_cmp.shape[1]
    cmp_end = jnp.arange(n_cmp) * CMP_STRIDE + CMP_LEN - 1
    n_blk = -(-n_keys // SEL_BLOCK)
    n_sel = min(SEL_COUNT, n_blk)
    pad = ((0, 0), (0, n_blk * SEL_BLOCK - n_keys), (0, 0))
    k_blk = jnp.pad(k_slc, pad).reshape(b, n_blk, SEL_BLOCK, d)
    v_blk = jnp.pad(v_slc, pad).reshape(b, n_blk, SEL_BLOCK, d)
    c0 = jnp.arange(n_cmp)[:, None] * CMP_STRIDE
    j0 = jnp.arange(n_blk)[None, :] * SEL_BLOCK
    share = jnp.clip(jnp.minimum(c0 + CMP_LEN, j0 + SEL_BLOCK) - jnp.maximum(c0, j0), 0, None).astype(jnp.float32) / CMP_LEN
    blk_id = jnp.arange(n_blk)
    if banded:
        wpad = ((0, 0), (WINDOW, 0), (0, 0))
        k_win, v_win = jnp.pad(k_win, wpad), jnp.pad(v_win, wpad)

    def block(start, qb, gb):
        nq = qb.shape[1]
        qp = past + start + jnp.arange(nq)
        s = jnp.einsum('bqhd,bcd->bqhc', qb, k_cmp) * scale
        p_cmp = masked_softmax(s, (cmp_end[None, :] <= qp[:, None])[None, :, None, :])
        o_cmp = jnp.einsum('bqhc,bcd->bqhd', p_cmp.astype(v_cmp.dtype), v_cmp)
        imp = jnp.einsum('bqhc,cj->bqj', p_cmp, share)
        forced = (blk_id[None, :] == qp[:, None] // SEL_BLOCK) | (blk_id[None, :] == 0)
        future = blk_id[None, :] * SEL_BLOCK > qp[:, None]
        imp = jnp.where(forced[None], FORCED, jnp.where(future[None], -1.0, imp))
        _, sel = lax.top_k(imp, n_sel)
        ks = jax.vmap(lambda kb, i: kb[i])(k_blk, sel).reshape(b, nq, n_sel * SEL_BLOCK, d)
        vs = jax.vmap(lambda vb, i: vb[i])(v_blk, sel).reshape(b, nq, n_sel * SEL_BLOCK, d)
        kpos = (sel[..., None] * SEL_BLOCK + jnp.arange(SEL_BLOCK)).reshape(b, nq, n_sel * SEL_BLOCK)
        s = jnp.einsum('bqhd,bqkd->bqhk', qb, ks) * scale
        p = masked_softmax(s, (kpos <= qp[None, :, None])[:, :, None, :])
        o_slc = jnp.einsum('bqhk,bqkd->bqhd', p.astype(vs.dtype), vs)
        if banded:
            n_w = WINDOW + nq
            kw = lax.dynamic_slice_in_dim(k_win, start, n_w, axis=1)
            vw = lax.dynamic_slice_in_dim(v_win, start, n_w, axis=1)
            kp = past + start - WINDOW + jnp.arange(n_w)
        else:
            kw, vw = k_win, v_win
            kp = win_pos0 + jnp.arange(k_win.shape[1])
        dlt = qp[:, None] - kp[None, :]
        s = jnp.einsum('bqhd,bkd->bqhk', qb, kw) * scale
        p = masked_softmax(s, ((dlt >= 0) & (dlt < WINDOW) & (kp[None, :] >= 0))[None, :, None, :])
        o_win = jnp.einsum('bqhk,bkd->bqhd', p.astype(vw.dtype), vw)
        g = jax.nn.sigmoid(gb.astype(jnp.float32))
        o = g[..., 0:1] * o_cmp + g[..., 1:2] * o_slc + g[..., 2:3] * o_win
        return o.astype(qb.dtype)

    return over_query_blocks(block, q, gates)


def dsa_attention(q, q_idx, w_idx, k, v, k_idx, past):
    b, n_keys, d = k.shape
    topk = min(DSA_TOPK, n_keys // 4)
    kpos = jnp.arange(n_keys)

    def block(start, qb, qib, wb):
        qp = past + start + jnp.arange(qb.shape[1])
        causal = kpos[None, :] <= qp[:, None]
        dots = jnp.einsum('bqhd,bsd->bqhs', qib, k_idx).astype(jnp.float32) * IDX_DIM ** -0.5
        score = jnp.einsum('bqh,bqhs->bqs', wb.astype(jnp.float32) * IDX_HEADS ** -0.5, jax.nn.relu(dots))
        score = jnp.where(causal[None], score, NEG)
        _, sel = lax.top_k(score, topk)
        ks = jax.vmap(lambda kk, i: kk[i])(k, sel)
        vs = jax.vmap(lambda vv, i: vv[i])(v, sel)
        s = jnp.einsum('bqhd,bqkd->bqhk', qb, ks) * d ** -0.5
        p = masked_softmax(s, (sel <= qp[None, :, None])[:, :, None, :])
        return jnp.einsum('bqhk,bqkd->bqhd', p.astype(vs.dtype), vs)

    return over_query_blocks(block, q, q_idx, w_idx)


def gated_delta_chunked(q, k, v, g, beta, s0):
    f32 = jnp.float32
    b, t, h, dk = k.shape
    dv = v.shape[-1]
    c = min(GDN_CHUNK, t)
    n = -(-t // c)
    pad = n * c - t

    def to_chunks(a):
        a = jnp.pad(a.astype(f32), [(0, 0), (0, pad)] + [(0, 0)] * (a.ndim - 2))
        a = a.reshape(b, n, c, *a.shape[2:])
        return jnp.moveaxis(a, (1, 3), (0, 2))

    qc, kc, vc, gc, bc = (to_chunks(a) for a in (q, k, v, g, beta))
    gcum = jnp.cumsum(gc, axis=-1)
    pos = jnp.arange(c)
    strict = pos[:, None] > pos[None, :]
    incl = pos[:, None] >= pos[None, :]
    diff = gcum[..., :, None] - gcum[..., None, :]
    decay = jnp.where(incl, jnp.exp(jnp.where(incl, diff, 0.0)), 0.0)
    k_beta = kc * bc[..., None]
    a_mat = jnp.where(strict, jnp.einsum('nbhid,nbhjd->nbhij', k_beta, kc) * decay, 0.0)
    rhs = jnp.concatenate([vc * bc[..., None], k_beta * jnp.exp(gcum)[..., None]], axis=-1)
    sol = lax.linalg.triangular_solve(a_mat + jnp.eye(c, dtype=f32), rhs,
                                      left_side=True, lower=True, unit_diagonal=True)
    u, w = sol[..., :dv], sol[..., dv:]
    qk = jnp.einsum('nbhid,nbhjd->nbhij', qc, kc) * decay
    q_dec = qc * jnp.exp(gcum)[..., None]
    k_dec = kc * jnp.exp(gcum[..., -1:] - gcum)[..., None]
    chunk_decay = jnp.exp(gcum[..., -1])

    def step(state, xs):
        u_i, w_i, qd_i, qk_i, kd_i, cd_i = xs
        new_v = u_i - jnp.einsum('bhck,bhkv->bhcv', w_i, state)
        o_i = jnp.einsum('bhck,bhkv->bhcv', qd_i, state) + jnp.einsum('bhij,bhjv->bhiv', qk_i, new_v)
        state = state * cd_i[..., None, None] + jnp.einsum('bhck,bhcv->bhkv', kd_i, new_v)
        return state, o_i

    s_final, o = lax.scan(step, s0.astype(f32), (u, w, q_dec, qk, k_dec, chunk_decay))
    o = jnp.moveaxis(o, (0, 2), (1, 3)).reshape(b, n * c, h, dv)[:, :t]
    return o, s_final


def gdn_mixer(qkv, a, beta_logit, z, conv_prev, s0, conv_w, a_log, dt_bias, norm_g):
    b, t, _ = qkv.shape
    xp = jnp.concatenate([conv_prev.astype(qkv.dtype), qkv], axis=1)
    conv = jax.nn.silu(sum(xp[:, i:i + t] * conv_w[i] for i in range(GDN_CONV)))
    q, k, v = (cc.reshape(b, t, GDN_HEADS, HEAD_DIM) for cc in jnp.split(conv, 3, axis=-1))
    q = l2norm(q) * HEAD_DIM ** -0.5
    k = l2norm(k)
    g = -jnp.exp(a_log.astype(jnp.float32)) * jax.nn.softplus(a.astype(jnp.float32) + dt_bias.astype(jnp.float32))
    beta = jax.nn.sigmoid(beta_logit.astype(jnp.float32))
    o, s_new = gated_delta_chunked(q, k, v, g, beta, s0)
    o = rms_norm(o, norm_g) * jax.nn.silu(z.reshape(b, t, GDN_HEADS, HEAD_DIM).astype(jnp.float32))
    return o.reshape(b, t, W_C).astype(qkv.dtype), s_new, xp[:, t:]


def mem_attention(h, mem_kv, w_q, w_o):
    b, t, _ = h.shape
    q = (h @ w_q).reshape(b, t, MEM_HEADS, HEAD_DIM)
    s = jnp.einsum('bqhd,bmhd->bhqm', q, mem_kv[:, :, 0]) * HEAD_DIM ** -0.5
    p = jax.nn.softmax(s.astype(jnp.float32), axis=-1).astype(h.dtype)
    o = jnp.einsum('bhqm,bmhd->bqhd', p, mem_kv[:, :, 1])
    return o.reshape(b, t, MEM_HEADS * HEAD_DIM) @ w_o


def run_group(x, mem, cache, p):
    prompt = cache is None
    b, t, _ = x.shape
    past = 0 if prompt else cache['page_table'].shape[1] * cache['nsa_kv'].shape[2]
    qpos = past + jnp.arange(t)
    names = ('nsa_kv', 'dsa_kv', 'idx_k', 'win_kv', 'gdn', 'conv') + (('mem_kv',) if prompt else ())
    out = {nm: [] for nm in names}
    for l in range(DEPTH):
        h = rms_norm(x, p['norm_mix_g'][l])
        (q_a, kv_a, g_a, q_b, kv_b, q_i, w_i, k_i, qkv_c, a_c, b_c, z_c, merge) = split_cols(h @ p['w_in'][l])
        q_a = rope(q_a.reshape(b, t, NSA_HEADS, HEAD_DIM), qpos)
        kv_a = kv_a.reshape(b, t, 6, HEAD_DIM)
        rot_a = rope(kv_a[:, :, 2::2], qpos)
        nsa_new = jnp.stack([kv_a[:, :, 0], kv_a[:, :, 1], rot_a[:, :, 0], kv_a[:, :, 3]], axis=2)
        win_new = jnp.stack([rot_a[:, :, 1], kv_a[:, :, 5]], axis=2)
        q_b = rope(q_b.reshape(b, t, DSA_HEADS, HEAD_DIM), qpos)
        kv_b = kv_b.reshape(b, t, 2, HEAD_DIM)
        dsa_new = jnp.stack([rope(kv_b[:, :, :1], qpos)[:, :, 0], kv_b[:, :, 1]], axis=2)
        q_i = rope(q_i.reshape(b, t, IDX_HEADS, IDX_DIM), qpos)
        idx_new = rope(k_i[:, :, None], qpos)[:, :, 0]
        if prompt:
            nsa_all, dsa_all, idx_all = nsa_new, dsa_new, idx_new
            win_all, win_pos0 = win_new, 0
            conv_prev = jnp.zeros((b, GDN_CONV - 1, 3 * W_C), x.dtype)
            s0 = jnp.zeros((b, GDN_HEADS, HEAD_DIM, HEAD_DIM), jnp.float32)
            mem_kv = (rms_norm(mem, p['mem_norm_g'][l]) @ p['w_mem_kv'][l]).reshape(b, -1, 2, MEM_HEADS, HEAD_DIM)
        else:
            pt = cache['page_table']
            nsa_all = jnp.concatenate([gather_pages(cache['nsa_kv'][l], pt), nsa_new], axis=1)
            dsa_all = jnp.concatenate([gather_pages(cache['dsa_kv'][l], pt), dsa_new], axis=1)
            idx_all = jnp.concatenate([gather_pages(cache['idx_k'][l], pt), idx_new], axis=1)
            win_all = jnp.concatenate([cache['win_kv'][l], win_new], axis=1)
            win_pos0 = past - cache['win_kv'].shape[2]
            conv_prev, s0 = cache['conv'][l], cache['gdn'][l]
            mem_kv = cache['mem_kv'][l]
        k_cmp = nsa_compress(nsa_all[:, :, 0], p['nsa_cmp_pe'][l, 0], p['nsa_cmp_w1'][l, 0], p['nsa_cmp_w2'][l, 0])
        v_cmp = nsa_compress(nsa_all[:, :, 1], p['nsa_cmp_pe'][l, 1], p['nsa_cmp_w1'][l, 1], p['nsa_cmp_w2'][l, 1])
        cmp_end = jnp.arange(k_cmp.shape[1]) * CMP_STRIDE + CMP_LEN - 1
        k_cmp = rope(k_cmp[:, :, None], cmp_end)[:, :, 0]
        o_a = nsa_attention(q_a, g_a.reshape(b, t, NSA_HEADS, 3), k_cmp, v_cmp, nsa_all[:, :, 2], nsa_all[:, :, 3],
                            win_all[:, :, 0], win_all[:, :, 1], past, win_pos0, prompt)
        o_b = dsa_attention(q_b, q_i, w_i, dsa_all[:, :, 0], dsa_all[:, :, 1], idx_all, past)
        o_c, s_new, conv_new = gdn_mixer(qkv_c, a_c, b_c, z_c, conv_prev, s0, p['gdn_conv_w'][l],
                                         p['gdn_a_log'][l], p['gdn_dt_bias'][l], p['gdn_norm_g'][l])
        m_a, m_b, m_c = jnp.split(jax.nn.sigmoid(merge), 3, axis=-1)
        mix = (m_a * (o_a.reshape(b, t, W_A) @ p['w_branch_a'][l])
               + m_b * (o_b.reshape(b, t, W_B) @ p['w_branch_b'][l])
               + m_c * (o_c @ p['w_branch_c'][l]))
        x = x + mix @ p['w_mix_out'][l]
        x = x + mem_attention(rms_norm(x, p['norm_mem_g'][l]), mem_kv, p['w_mem_q'][l], p['w_mem_o'][l])
        h = rms_norm(x, p['norm_ffn_g'][l])
        x = x + (jax.nn.silu(h @ p['w_ffn_gate'][l]) * (h @ p['w_ffn_up'][l])) @ p['w_ffn_down'][l]
        out['nsa_kv'].append(nsa_new)
        out['dsa_kv'].append(dsa_new)
        out['idx_k'].append(idx_new)
        out['win_kv'].append(win_new[:, -min(WINDOW, t):] if prompt else win_all[:, -cache['win_kv'].shape[2]:])
        out['gdn'].append(s_new)
        out['conv'].append(conv_new)
        if prompt:
            out['mem_kv'].append(mem_kv)
    y = rms_norm(x, p['norm_final_g'])
    return y, {nm: jnp.stack(v) for nm, v in out.items()}


def setup_inputs(seed: int = 0) -> dict:
    key = jax.random.key(seed)
    keys = iter(jax.random.split(key, 64))

    def nrm(shape, scale=1.0):
        return jax.random.normal(next(keys), shape, jnp.float32) * scale

    def lin(shape):
        return nrm(shape, shape[-2] ** -0.5)

    def gain(shape):
        return 1.0 + nrm(shape, 0.1)

    n_pages = PAST_LEN // PAGE_SIZE
    n_used = DEC_BATCH * n_pages
    n_phys = n_used + max(1, n_used // 4)
    wbuf = min(WINDOW, PAST_LEN)
    page_table = jax.random.permutation(next(keys), n_phys)[:n_used].reshape(DEC_BATCH, n_pages).astype(jnp.int32)
    dt = jnp.exp(jax.random.uniform(next(keys), (DEPTH, GDN_HEADS), jnp.float32, math.log(1e-3), math.log(1e-1)))
    dt_bias = dt + jnp.log(-jnp.expm1(-dt))
    a_log = jnp.log(jax.random.uniform(next(keys), (DEPTH, GDN_HEADS), jnp.float32, 1.0, 16.0))
    return {
        'x_prompt': nrm((BATCH, SEQ, D_MODEL)),
        'x_sample': nrm((DEC_BATCH, DEC_SEQ, D_MODEL)),
        'mem_prompt': nrm((BATCH, MEM_TOKENS, D_MODEL)),
        'cache_nsa_kv': nrm((DEPTH, n_phys, PAGE_SIZE, 4, HEAD_DIM)),
        'cache_dsa_kv': nrm((DEPTH, n_phys, PAGE_SIZE, 2, HEAD_DIM)),
        'cache_dsa_idx_k': nrm((DEPTH, n_phys, PAGE_SIZE, IDX_DIM)),
        'cache_win_kv': nrm((DEPTH, DEC_BATCH, wbuf, 2, HEAD_DIM)),
        'cache_mem_kv': nrm((DEPTH, DEC_BATCH, MEM_TOKENS, 2, MEM_HEADS, HEAD_DIM)),
        'state_gdn': nrm((DEPTH, DEC_BATCH, GDN_HEADS, HEAD_DIM, HEAD_DIM), 0.1),
        'state_conv': nrm((DEPTH, DEC_BATCH, GDN_CONV - 1, 3 * W_C)),
        'page_table': page_table,
        'norm_mix_g': gain((DEPTH, D_MODEL)),
        'w_in': lin((DEPTH, D_MODEL, IN_WIDTH)),
        'nsa_cmp_pe': nrm((DEPTH, 2, CMP_LEN, HEAD_DIM), 0.1),
        'nsa_cmp_w1': lin((DEPTH, 2, CMP_LEN * HEAD_DIM, CMP_HID)),
        'nsa_cmp_w2': lin((DEPTH, 2, CMP_HID, HEAD_DIM)),
        'gdn_conv_w': nrm((DEPTH, GDN_CONV, 3 * W_C), GDN_CONV ** -0.5),
        'gdn_a_log': a_log,
        'gdn_dt_bias': dt_bias,
        'gdn_norm_g': gain((DEPTH, HEAD_DIM)),
        'w_branch_a': lin((DEPTH, W_A, D_MODEL)),
        'w_branch_b': lin((DEPTH, W_B, D_MODEL)),
        'w_branch_c': lin((DEPTH, W_C, D_MODEL)),
        'w_mix_out': lin((DEPTH, D_MODEL, D_MODEL)),
        'norm_mem_g': gain((DEPTH, D_MODEL)),
        'mem_norm_g': gain((DEPTH, D_MODEL)),
        'w_mem_q': lin((DEPTH, D_MODEL, MEM_HEADS * HEAD_DIM)),
        'w_mem_kv': lin((DEPTH, D_MODEL, 2 * MEM_HEADS * HEAD_DIM)),
        'w_mem_o': lin((DEPTH, MEM_HEADS * HEAD_DIM, D_MODEL)),
        'norm_ffn_g': gain((DEPTH, D_MODEL)),
        'w_ffn_gate': lin((DEPTH, D_MODEL, D_FF)),
        'w_ffn_up': lin((DEPTH, D_MODEL, D_FF)),
        'w_ffn_down': lin((DEPTH, D_FF, D_MODEL)),
        'norm_final_g': gain((D_MODEL,)),
    }


def reference(x_prompt, x_sample, mem_prompt, cache_nsa_kv, cache_dsa_kv, cache_dsa_idx_k, cache_win_kv,
              cache_mem_kv, state_gdn, state_conv, page_table, norm_mix_g, w_in, nsa_cmp_pe, nsa_cmp_w1,
              nsa_cmp_w2, gdn_conv_w, gdn_a_log, gdn_dt_bias, gdn_norm_g, w_branch_a, w_branch_b, w_branch_c,
              w_mix_out, norm_mem_g, mem_norm_g, w_mem_q, w_mem_kv, w_mem_o, norm_ffn_g, w_ffn_gate, w_ffn_up,
              w_ffn_down, norm_final_g):
    p = dict(norm_mix_g=norm_mix_g, w_in=w_in, nsa_cmp_pe=nsa_cmp_pe, nsa_cmp_w1=nsa_cmp_w1,
             nsa_cmp_w2=nsa_cmp_w2, gdn_conv_w=gdn_conv_w, gdn_a_log=gdn_a_log, gdn_dt_bias=gdn_dt_bias,
             gdn_norm_g=gdn_norm_g, w_branch_a=w_branch_a, w_branch_b=w_branch_b, w_branch_c=w_branch_c,
             w_mix_out=w_mix_out, norm_mem_g=norm_mem_g, mem_norm_g=mem_norm_g, w_mem_q=w_mem_q,
             w_mem_kv=w_mem_kv, w_mem_o=w_mem_o, norm_ffn_g=norm_ffn_g, w_ffn_gate=w_ffn_gate,
             w_ffn_up=w_ffn_up, w_ffn_down=w_ffn_down, norm_final_g=norm_final_g)
    cache = dict(nsa_kv=cache_nsa_kv, dsa_kv=cache_dsa_kv, idx_k=cache_dsa_idx_k, win_kv=cache_win_kv,
                 mem_kv=cache_mem_kv, gdn=state_gdn, conv=state_conv, page_table=page_table)
    y_prompt, sp = run_group(x_prompt, mem_prompt, None, p)
    y_sample, ss = run_group(x_sample, None, cache, p)
    return (y_prompt, y_sample,
            sp['nsa_kv'], sp['dsa_kv'], sp['idx_k'], sp['win_kv'], sp['gdn'], sp['conv'], sp['mem_kv'],
            ss['nsa_kv'], ss['dsa_kv'], ss['idx_k'], ss['win_kv'], ss['gdn'], ss['conv'])
```

```python
import functools
import math

import jax
import jax.numpy as jnp
import numpy as np
from jax import lax
from jax.experimental import pallas as pl
from jax.experimental.pallas import tpu as pltpu

D_MODEL = 2048
DEPTH = 2
PAGE_SIZE = 128
HEAD_DIM = 128
NSA_HEADS = D_MODEL // (4 * HEAD_DIM)
DSA_HEADS = D_MODEL // (4 * HEAD_DIM)
GDN_HEADS = D_MODEL // (2 * HEAD_DIM)
W_A = NSA_HEADS * HEAD_DIM
W_B = DSA_HEADS * HEAD_DIM
W_C = GDN_HEADS * HEAD_DIM
CMP_LEN = 32
CMP_STRIDE = 16
CMP_HID = 2 * HEAD_DIM
SEL_BLOCK = 64
SEL_COUNT = 16
WINDOW = 512
IDX_HEADS = 16
IDX_DIM = 64
DSA_TOPK = 256
GDN_CONV = 4
GDN_CHUNK = 64
MEM_HEADS = 4
ROPE_THETA = 10000.0
Q_BLOCK = 128
EPS = 1e-6
NEG = -1e30
FORCED = 1e9
IN_SPLITS = (W_A, 6 * HEAD_DIM, 3 * NSA_HEADS,
             W_B, 2 * HEAD_DIM, IDX_HEADS * IDX_DIM, IDX_HEADS, IDX_DIM,
             3 * W_C, GDN_HEADS, GDN_HEADS, W_C,
             3 * D_MODEL)

LANE = 128
VMEM_LIMIT = 48 * 1024 * 1024


def _mm_kernel(x_ref, w_ref, o_ref, acc_ref):
    k = pl.program_id(2)

    @pl.when(k == 0)
    def _():
        acc_ref[...] = jnp.zeros_like(acc_ref)

    acc_ref[...] += jnp.dot(x_ref[...].astype(jnp.bfloat16), w_ref[...].astype(jnp.bfloat16),
                            preferred_element_type=jnp.float32)

    @pl.when(k == pl.num_programs(2) - 1)
    def _():
        o_ref[...] = acc_ref[...]


def _pick_tile(n, cap):
    best = LANE
    for t in range(LANE, cap + 1, LANE):
        if n % t == 0:
            best = t
    return best


def pmm(x, w):
    lead = x.shape[:-1]
    kdim = x.shape[-1]
    n = w.shape[-1]
    x2 = x.reshape(-1, kdim)
    m = x2.shape[0]
    n_pad = -(-n // LANE) * LANE
    if n_pad != n:
        w = jnp.pad(w, ((0, 0), (0, n_pad - n)))
    tm = _pick_tile(m, 512)
    tn = _pick_tile(n_pad, 1024)
    tk = _pick_tile(kdim, 1024)
    out = pl.pallas_call(
        _mm_kernel,
        grid=(m // tm, n_pad // tn, kdim // tk),
        in_specs=[pl.BlockSpec((tm, tk), lambda i, j, k: (i, k)),
                  pl.BlockSpec((tk, tn), lambda i, j, k: (k, j))],
        out_specs=pl.BlockSpec((tm, tn), lambda i, j, k: (i, j)),
        out_shape=jax.ShapeDtypeStruct((m, n_pad), jnp.float32),
        scratch_shapes=[pltpu.VMEM((tm, tn), jnp.float32)],
        compiler_params=pltpu.CompilerParams(
            dimension_semantics=("parallel", "parallel", "arbitrary"),
            vmem_limit_bytes=VMEM_LIMIT),
        name="pmm",
    )(x2, w)
    if n_pad != n:
        out = out[:, :n]
    return out.reshape(*lead, n)


def rms_norm(x, g):
    xf = x.astype(jnp.float32)
    y = xf * lax.rsqrt(jnp.mean(xf * xf, axis=-1, keepdims=True) + EPS)
    return (y * g.astype(jnp.float32)).astype(x.dtype)


def l2norm(x):
    xf = x.astype(jnp.float32)
    return xf * lax.rsqrt(jnp.sum(xf * xf, axis=-1, keepdims=True) + EPS)


def rope(x, pos):
    half = x.shape[-1] // 2
    inv = ROPE_THETA ** (-jnp.arange(half, dtype=jnp.float32) / half)
    ang = pos.astype(jnp.float32)[:, None] * inv
    cos, sin = jnp.cos(ang)[:, None, :], jnp.sin(ang)[:, None, :]
    xf = x.astype(jnp.float32)
    x1, x2 = xf[..., :half], xf[..., half:]
    return jnp.concatenate([x1 * cos - x2 * sin, x2 * cos + x1 * sin], axis=-1).astype(x.dtype)


def masked_softmax(s, mask):
    s = jnp.where(mask, s.astype(jnp.float32), NEG)
    e = jnp.where(mask, jnp.exp(s - jnp.max(s, axis=-1, keepdims=True)), 0.0)
    return e / jnp.maximum(jnp.sum(e, axis=-1, keepdims=True), 1e-30)


def split_cols(h):
    return jnp.split(h, [int(o) for o in np.cumsum(IN_SPLITS)[:-1]], axis=-1)


def gather_pages(pool, page_table):
    rows = pool[page_table]
    return rows.reshape(rows.shape[0], -1, *rows.shape[3:])


def over_query_blocks(fn, *qs):
    b, t = qs[0].shape[:2]
    if t <= Q_BLOCK:
        return fn(0, *qs)
    nb = t // Q_BLOCK
    blocks = tuple(q.reshape(b, nb, Q_BLOCK, *q.shape[2:]).swapaxes(0, 1) for q in qs)
    out = lax.map(lambda a: fn(a[0] * Q_BLOCK, *a[1]), (jnp.arange(nb), blocks))
    out = out.swapaxes(0, 1)
    return out.reshape(b, t, *out.shape[3:])


def nsa_compress(rows, pe, w1, w2):
    b, n_keys, d = rows.shape
    r = CMP_LEN // CMP_STRIDE
    n_cmp = (n_keys - CMP_LEN) // CMP_STRIDE + 1
    seg = rows[:, :(n_cmp + r - 1) * CMP_STRIDE].reshape(b, n_cmp + r - 1, CMP_STRIDE * d)
    w1r = w1.reshape(r, CMP_STRIDE * d, CMP_HID)
    h = pe.reshape(-1) @ w1 + sum(seg[:, i:i + n_cmp] @ w1r[i] for i in range(r))
    return jax.nn.gelu(h) @ w2


def nsa_attention(q, gates, k_cmp, v_cmp, k_slc, v_slc, k_win, v_win, past, win_pos0, banded):
    b, n_keys, d = k_slc.shape
    scale = d ** -0.5
    n_cmp = k_cmp.shape[1]
    cmp_end = jnp.arange(n_cmp) * CMP_STRIDE + CMP_LEN - 1
    n_blk = -(-n_keys // SEL_BLOCK)
    n_sel = min(SEL_COUNT, n_blk)
    pad = ((0, 0), (0, n_blk * SEL_BLOCK - n_keys), (0, 0))
    k_blk = jnp.pad(k_slc, pad).reshape(b, n_blk, SEL_BLOCK, d)
    v_blk = jnp.pad(v_slc, pad).reshape(b, n_blk, SEL_BLOCK, d)
    c0 = jnp.arange(n_cmp)[:, None] * CMP_STRIDE
    j0 = jnp.arange(n_blk)[None, :] * SEL_BLOCK
    share = jnp.clip(jnp.minimum(c0 + CMP_LEN, j0 + SEL_BLOCK) - jnp.maximum(c0, j0), 0, None).astype(jnp.float32) / CMP_LEN
    blk_id = jnp.arange(n_blk)
    if banded:
        wpad = ((0, 0), (WINDOW, 0), (0, 0))
        k_win, v_win = jnp.pad(k_win, wpad), jnp.pad(v_win, wpad)

    def block(start, qb, gb):
        nq = qb.shape[1]
        qp = past + start + jnp.arange(nq)
        s = jnp.einsum('bqhd,bcd->bqhc', qb, k_cmp) * scale
        p_cmp = masked_softmax(s, (cmp_end[None, :] <= qp[:, None])[None, :, None, :])
        o_cmp = jnp.einsum('bqhc,bcd->bqhd', p_cmp.astype(v_cmp.dtype), v_cmp)
        imp = jnp.einsum('bqhc,cj->bqj', p_cmp, share)
        forced = (blk_id[None, :] == qp[:, None] // SEL_BLOCK) | (blk_id[None, :] == 0)
        future = blk_id[None, :] * SEL_BLOCK > qp[:, None]
        imp = jnp.where(forced[None], FORCED, jnp.where(future[None], -1.0, imp))
        _, sel = lax.top_k(imp, n_sel)
        ks = jax.vmap(lambda kb, i: kb[i])(k_blk, sel).reshape(b, nq, n_sel * SEL_BLOCK, d)
        vs = jax.vmap(lambda vb, i: vb[i])(v_blk, sel).reshape(b, nq, n_sel * SEL_BLOCK, d)
        kpos = (sel[..., None] * SEL_BLOCK + jnp.arange(SEL_BLOCK)).reshape(b, nq, n_sel * SEL_BLOCK)
        s = jnp.einsum('bqhd,bqkd->bqhk', qb, ks) * scale
        p = masked_softmax(s, (kpos <= qp[None, :, None])[:, :, None, :])
        o_slc = jnp.einsum('bqhk,bqkd->bqhd', p.astype(vs.dtype), vs)
        if banded:
            n_w = WINDOW + nq
            kw = lax.dynamic_slice_in_dim(k_win, start, n_w, axis=1)
            vw = lax.dynamic_slice_in_dim(v_win, start, n_w, axis=1)
            kp = past + start - WINDOW + jnp.arange(n_w)
        else:
            kw, vw = k_win, v_win
            kp = win_pos0 + jnp.arange(k_win.shape[1])
        dlt = qp[:, None] - kp[None, :]
        s = jnp.einsum('bqhd,bkd->bqhk', qb, kw) * scale
        p = masked_softmax(s, ((dlt >= 0) & (dlt < WINDOW) & (kp[None, :] >= 0))[None, :, None, :])
        o_win = jnp.einsum('bqhk,bkd->bqhd', p.astype(vw.dtype), vw)
        g = jax.nn.sigmoid(gb.astype(jnp.float32))
        o = g[..., 0:1] * o_cmp + g[..., 1:2] * o_slc + g[..., 2:3] * o_win
        return o.astype(qb.dtype)

    return over_query_blocks(block, q, gates)


def dsa_attention(q, q_idx, w_idx, k, v, k_idx, past):
    b, n_keys, d = k.shape
    topk = min(DSA_TOPK, n_keys // 4)
    kpos = jnp.arange(n_keys)

    def block(start, qb, qib, wb):
        qp = past + start + jnp.arange(qb.shape[1])
        causal = kpos[None, :] <= qp[:, None]
        dots = jnp.einsum('bqhd,bsd->bqhs', qib, k_idx).astype(jnp.float32) * IDX_DIM ** -0.5
        score = jnp.einsum('bqh,bqhs->bqs', wb.astype(jnp.float32) * IDX_HEADS ** -0.5, jax.nn.relu(dots))
        score = jnp.where(causal[None], score, NEG)
        _, sel = lax.top_k(score, topk)
        ks = jax.vmap(lambda kk, i: kk[i])(k, sel)
        vs = jax.vmap(lambda vv, i: vv[i])(v, sel)
        s = jnp.einsum('bqhd,bqkd->bqhk', qb, ks) * d ** -0.5
        p = masked_softmax(s, (sel <= qp[None, :, None])[:, :, None, :])
        return jnp.einsum('bqhk,bqkd->bqhd', p.astype(vs.dtype), vs)

    return over_query_blocks(block, q, q_idx, w_idx)


def gated_delta_chunked(q, k, v, g, beta, s0):
    f32 = jnp.float32
    b, t, h, dk = k.shape
    dv = v.shape[-1]
    c = min(GDN_CHUNK, t)
    n = -(-t // c)
    pad = n * c - t

    def to_chunks(a):
        a = jnp.pad(a.astype(f32), [(0, 0), (0, pad)] + [(0, 0)] * (a.ndim - 2))
        a = a.reshape(b, n, c, *a.shape[2:])
        return jnp.moveaxis(a, (1, 3), (0, 2))

    qc, kc, vc, gc, bc = (to_chunks(a) for a in (q, k, v, g, beta))
    gcum = jnp.cumsum(gc, axis=-1)
    pos = jnp.arange(c)
    strict = pos[:, None] > pos[None, :]
    incl = pos[:, None] >= pos[None, :]
    diff = gcum[..., :, None] - gcum[..., None, :]
    decay = jnp.where(incl, jnp.exp(jnp.where(incl, diff, 0.0)), 0.0)
    k_beta = kc * bc[..., None]
    a_mat = jnp.where(strict, jnp.einsum('nbhid,nbhjd->nbhij', k_beta, kc) * decay, 0.0)
    rhs = jnp.concatenate([vc * bc[..., None], k_beta * jnp.exp(gcum)[..., None]], axis=-1)
    sol = lax.linalg.triangular_solve(a_mat + jnp.eye(c, dtype=f32), rhs,
                                      left_side=True, lower=True, unit_diagonal=True)
    u, w = sol[..., :dv], sol[..., dv:]
    qk = jnp.einsum('nbhid,nbhjd->nbhij', qc, kc) * decay
    q_dec = qc * jnp.exp(gcum)[..., None]
    k_dec = kc * jnp.exp(gcum[..., -1:] - gcum)[..., None]
    chunk_decay = jnp.exp(gcum[..., -1])

    def step(state, xs):
        u_i, w_i, qd_i, qk_i, kd_i, cd_i = xs
        new_v = u_i - jnp.einsum('bhck,bhkv->bhcv', w_i, state)
        o_i = jnp.einsum('bhck,bhkv->bhcv', qd_i, state) + jnp.einsum('bhij,bhjv->bhiv', qk_i, new_v)
        state = state * cd_i[..., None, None] + jnp.einsum('bhck,bhcv->bhkv', kd_i, new_v)
        return state, o_i

    s_final, o = lax.scan(step, s0.astype(f32), (u, w, q_dec, qk, k_dec, chunk_decay))
    o = jnp.moveaxis(o, (0, 2), (1, 3)).reshape(b, n * c, h, dv)[:, :t]
    return o, s_final


def gdn_mixer(qkv, a, beta_logit, z, conv_prev, s0, conv_w, a_log, dt_bias, norm_g):
    b, t, _ = qkv.shape
    xp = jnp.concatenate([conv_prev.astype(qkv.dtype), qkv], axis=1)
    conv = jax.nn.silu(sum(xp[:, i:i + t] * conv_w[i] for i in range(GDN_CONV)))
    q, k, v = (cc.reshape(b, t, GDN_HEADS, HEAD_DIM) for cc in jnp.split(conv, 3, axis=-1))
    q = l2norm(q) * HEAD_DIM ** -0.5
    k = l2norm(k)
    g = -jnp.exp(a_log.astype(jnp.float32)) * jax.nn.softplus(a.astype(jnp.float32) + dt_bias.astype(jnp.float32))
    beta = jax.nn.sigmoid(beta_logit.astype(jnp.float32))
    o, s_new = gated_delta_chunked(q, k, v, g, beta, s0)
    o = rms_norm(o, norm_g) * jax.nn.silu(z.reshape(b, t, GDN_HEADS, HEAD_DIM).astype(jnp.float32))
    return o.reshape(b, t, W_C).astype(qkv.dtype), s_new, xp[:, t:]


def mem_attention(h, mem_kv, w_q, w_o):
    b, t, _ = h.shape
    q = pmm(h, w_q).reshape(b, t, MEM_HEADS, HEAD_DIM)
    s = jnp.einsum('bqhd,bmhd->bhqm', q, mem_kv[:, :, 0]) * HEAD_DIM ** -0.5
    p = jax.nn.softmax(s.astype(jnp.float32), axis=-1).astype(h.dtype)
    o = jnp.einsum('bhqm,bmhd->bqhd', p, mem_kv[:, :, 1])
    return pmm(o.reshape(b, t, MEM_HEADS * HEAD_DIM), w_o)


def run_group(x, mem, cache, p):
    prompt = cache is None
    b, t, _ = x.shape
    past = 0 if prompt else cache['page_table'].shape[1] * cache['nsa_kv'].shape[2]
    qpos = past + jnp.arange(t)
    names = ('nsa_kv', 'dsa_kv', 'idx_k', 'win_kv', 'gdn', 'conv') + (('mem_kv',) if prompt else ())
    out = {nm: [] for nm in names}
    for l in range(DEPTH):
        h = rms_norm(x, p['norm_mix_g'][l])
        (q_a, kv_a, g_a, q_b, kv_b, q_i, w_i, k_i, qkv_c, a_c, b_c, z_c, merge) = split_cols(pmm(h, p['w_in'][l]))
        q_a = rope(q_a.reshape(b, t, NSA_HEADS, HEAD_DIM), qpos)
        kv_a = kv_a.reshape(b, t, 6, HEAD_DIM)
        rot_a = rope(kv_a[:, :, 2::2], qpos)
        nsa_new = jnp.stack([kv_a[:, :, 0], kv_a[:, :, 1], rot_a[:, :, 0], kv_a[:, :, 3]], axis=2)
        win_new = jnp.stack([rot_a[:, :, 1], kv_a[:, :, 5]], axis=2)
        q_b = rope(q_b.reshape(b, t, DSA_HEADS, HEAD_DIM), qpos)
        kv_b = kv_b.reshape(b, t, 2, HEAD_DIM)
        dsa_new = jnp.stack([rope(kv_b[:, :, :1], qpos)[:, :, 0], kv_b[:, :, 1]], axis=2)
        q_i = rope(q_i.reshape(b, t, IDX_HEADS, IDX_DIM), qpos)
        idx_new = rope(k_i[:, :, None], qpos)[:, :, 0]
        if prompt:
            nsa_all, dsa_all, idx_all = nsa_new, dsa_new, idx_new
            win_all, win_pos0 = win_new, 0
            conv_prev = jnp.zeros((b, GDN_CONV - 1, 3 * W_C), x.dtype)
            s0 = jnp.zeros((b, GDN_HEADS, HEAD_DIM, HEAD_DIM), jnp.float32)
            mem_kv = pmm(rms_norm(mem, p['mem_norm_g'][l]), p['w_mem_kv'][l]).reshape(b, -1, 2, MEM_HEADS, HEAD_DIM)
        else:
            pt = cache['page_table']
            nsa_all = jnp.concatenate([gather_pages(cache['nsa_kv'][l], pt), nsa_new], axis=1)
            dsa_all = jnp.concatenate([gather_pages(cache['dsa_kv'][l], pt), dsa_new], axis=1)
            idx_all = jnp.concatenate([gather_pages(cache['idx_k'][l], pt), idx_new], axis=1)
            win_all = jnp.concatenate([cache['win_kv'][l], win_new], axis=1)
            win_pos0 = past - cache['win_kv'].shape[2]
            conv_prev, s0 = cache['conv'][l], cache['gdn'][l]
            mem_kv = cache['mem_kv'][l]
        k_cmp = nsa_compress(nsa_all[:, :, 0], p['nsa_cmp_pe'][l, 0], p['nsa_cmp_w1'][l, 0], p['nsa_cmp_w2'][l, 0])
        v_cmp = nsa_compress(nsa_all[:, :, 1], p['nsa_cmp_pe'][l, 1], p['nsa_cmp_w1'][l, 1], p['nsa_cmp_w2'][l, 1])
        cmp_end = jnp.arange(k_cmp.shape[1]) * CMP_STRIDE + CMP_LEN - 1
        k_cmp = rope(k_cmp[:, :, None], cmp_end)[:, :, 0]
        o_a = nsa_attention(q_a, g_a.reshape(b, t, NSA_HEADS, 3), k_cmp, v_cmp, nsa_all[:, :, 2], nsa_all[:, :, 3],
                            win_all[:, :, 0], win_all[:, :, 1], past, win_pos0, prompt)
        o_b = dsa_attention(q_b, q_i, w_i, dsa_all[:, :, 0], dsa_all[:, :, 1], idx_all, past)
        o_c, s_new, conv_new = gdn_mixer(qkv_c, a_c, b_c, z_c, conv_prev, s0, p['gdn_conv_w'][l],
                                         p['gdn_a_log'][l], p['gdn_dt_bias'][l], p['gdn_norm_g'][l])
        m_a, m_b, m_c = jnp.split(jax.nn.sigmoid(merge), 3, axis=-1)
        mix = (m_a * pmm(o_a.reshape(b, t, W_A), p['w_branch_a'][l])
               + m_b * pmm(o_b.reshape(b, t, W_B), p['w_branch_b'][l])
               + m_c * pmm(o_c, p['w_branch_c'][l]))
        x = x + pmm(mix, p['w_mix_out'][l])
        x = x + mem_attention(rms_norm(x, p['norm_mem_g'][l]), mem_kv, p['w_mem_q'][l], p['w_mem_o'][l])
        h = rms_norm(x, p['norm_ffn_g'][l])
        x = x + pmm(jax.nn.silu(pmm(h, p['w_ffn_gate'][l])) * pmm(h, p['w_ffn_up'][l]), p['w_ffn_down'][l])
        out['nsa_kv'].append(nsa_new)
        out['dsa_kv'].append(dsa_new)
        out['idx_k'].append(idx_new)
        out['win_kv'].append(win_new[:, -min(WINDOW, t):] if prompt else win_all[:, -cache['win_kv'].shape[2]:])
        out['gdn'].append(s_new)
        out['conv'].append(conv_new)
        if prompt:
            out['mem_kv'].append(mem_kv)
    y = rms_norm(x, p['norm_final_g'])
    return y, {nm: jnp.stack(v) for nm, v in out.items()}


def kernel(x_prompt, x_sample, mem_prompt, cache_nsa_kv, cache_dsa_kv, cache_dsa_idx_k, cache_win_kv, cache_mem_kv, state_gdn, state_conv, page_table, norm_mix_g, w_in, nsa_cmp_pe, nsa_cmp_w1, nsa_cmp_w2, gdn_conv_w, gdn_a_log, gdn_dt_bias, gdn_norm_g, w_branch_a, w_branch_b, w_branch_c, w_mix_out, norm_mem_g, mem_norm_g, w_mem_q, w_mem_kv, w_mem_o, norm_ffn_g, w_ffn_gate, w_ffn_up, w_ffn_down, norm_final_g):
    p = dict(norm_mix_g=norm_mix_g, w_in=w_in, nsa_cmp_pe=nsa_cmp_pe, nsa_cmp_w1=nsa_cmp_w1,
             nsa_cmp_w2=nsa_cmp_w2, gdn_conv_w=gdn_conv_w, gdn_a_log=gdn_a_log, gdn_dt_bias=gdn_dt_bias,
             gdn_norm_g=gdn_norm_g, w_branch_a=w_branch_a, w_branch_b=w_branch_b, w_branch_c=w_branch_c,
             w_mix_out=w_mix_out, norm_mem_g=norm_mem_g, mem_norm_g=mem_norm_g, w_mem_q=w_mem_q,
             w_mem_kv=w_mem_kv, w_mem_o=w_mem_o, norm_ffn_g=norm_ffn_g, w_ffn_gate=w_ffn_gate,
             w_ffn_up=w_ffn_up, w_ffn_down=w_ffn_down, norm_final_g=norm_final_g)
    cache = dict(nsa_kv=cache_nsa_kv, dsa_kv=cache_dsa_kv, idx_k=cache_dsa_idx_k, win_kv=cache_win_kv,
                 mem_kv=cache_mem_kv, gdn=state_gdn, conv=state_conv, page_table=page_table)
    y_prompt, sp = run_group(x_prompt, mem_prompt, None, p)
    y_sample, ss = run_group(x_sample, None, cache, p)
    return (y_prompt, y_sample,
            sp['nsa_kv'], sp['dsa_kv'], sp['idx_k'], sp['win_kv'], sp['gdn'], sp['conv'], sp['mem_kv'],
            ss['nsa_kv'], ss['dsa_kv'], ss['idx_k'], ss['win_kv'], ss['gdn'], ss['conv'])
```

```python
import functools
import math

import jax
import jax.numpy as jnp
import numpy as np
from jax import lax
from jax.experimental import pallas as pl
from jax.experimental.pallas import tpu as pltpu

D_MODEL = 2048
DEPTH = 2
PAGE_SIZE = 128
HEAD_DIM = 128
NSA_HEADS = D_MODEL // (4 * HEAD_DIM)
DSA_HEADS = D_MODEL // (4 * HEAD_DIM)
GDN_HEADS = D_MODEL // (2 * HEAD_DIM)
W_A = NSA_HEADS * HEAD_DIM
W_B = DSA_HEADS * HEAD_DIM
W_C = GDN_HEADS * HEAD_DIM
CMP_LEN = 32
CMP_STRIDE = 16
CMP_HID = 2 * HEAD_DIM
SEL_BLOCK = 64
SEL_COUNT = 16
WINDOW = 512
IDX_HEADS = 16
IDX_DIM = 64
DSA_TOPK = 256
GDN_CONV = 4
GDN_CHUNK = 64
MEM_HEADS = 4
ROPE_THETA = 10000.0
Q_BLOCK = 128
EPS = 1e-6
NEG = -1e30
FORCED = 1e9
IN_SPLITS = (W_A, 6 * HEAD_DIM, 3 * NSA_HEADS,
             W_B, 2 * HEAD_DIM, IDX_HEADS * IDX_DIM, IDX_HEADS, IDX_DIM,
             3 * W_C, GDN_HEADS, GDN_HEADS, W_C,
             3 * D_MODEL)

LANE = 128
VMEM_LIMIT = 48 * 1024 * 1024


def _mm_kernel(x_ref, w_ref, o_ref, acc_ref):
    k = pl.program_id(2)

    @pl.when(k == 0)
    def _():
        acc_ref[...] = jnp.zeros_like(acc_ref)

    acc_ref[...] += jnp.dot(x_ref[...].astype(jnp.bfloat16), w_ref[...].astype(jnp.bfloat16),
                            preferred_element_type=jnp.float32)

    @pl.when(k == pl.num_programs(2) - 1)
    def _():
        o_ref[...] = acc_ref[...]


def _pick_tile(n, cap):
    best = LANE
    for t in range(LANE, cap + 1, LANE):
        if n % t == 0:
            best = t
    return best


def pmm(x, w):
    lead = x.shape[:-1]
    kdim = x.shape[-1]
    n = w.shape[-1]
    x2 = x.reshape(-1, kdim)
    m = x2.shape[0]
    n_pad = -(-n // LANE) * LANE
    if n_pad != n:
        w = jnp.pad(w, ((0, 0), (0, n_pad - n)))
    tm = _pick_tile(m, 512)
    tn = _pick_tile(n_pad, 1024)
    tk = _pick_tile(kdim, 1024)
    out = pl.pallas_call(
        _mm_kernel,
        grid=(m // tm, n_pad // tn, kdim // tk),
        in_specs=[pl.BlockSpec((tm, tk), lambda i, j, k: (i, k)),
                  pl.BlockSpec((tk, tn), lambda i, j, k: (k, j))],
        out_specs=pl.BlockSpec((tm, tn), lambda i, j, k: (i, j)),
        out_shape=jax.ShapeDtypeStruct((m, n_pad), jnp.float32),
        scratch_shapes=[pltpu.VMEM((tm, tn), jnp.float32)],
        compiler_params=pltpu.CompilerParams(
            dimension_semantics=("parallel", "parallel", "arbitrary"),
            vmem_limit_bytes=VMEM_LIMIT),
        name="pmm",
    )(x2, w)
    if n_pad != n:
        out = out[:, :n]
    return out.reshape(*lead, n)


_NT = (((1,), (1,)), ((), ()))


def _softmax_tiles(qh, k_ref, v_ref, lo, hi, tk, mask_fn, m_ref, l_ref, acc_ref, nh, tq):
    m_ref[...] = jnp.full(m_ref.shape, NEG, jnp.float32)
    l_ref[...] = jnp.zeros(l_ref.shape, jnp.float32)
    acc_ref[...] = jnp.zeros(acc_ref.shape, jnp.float32)

    def body(kt, carry):
        off = pl.multiple_of(kt * tk, tk)
        k = k_ref[0, pl.ds(off, tk), :]
        v = v_ref[0, pl.ds(off, tk), :]
        s = lax.dot_general(qh, k, _NT, preferred_element_type=jnp.float32).reshape(nh, tq, tk)
        maskf = mask_fn(kt)
        bias = (maskf - 1.0) * (-NEG)
        sm = s + bias[None]
        m_old = m_ref[...]
        m_new = jnp.maximum(m_old, jnp.max(sm, axis=-1, keepdims=True))
        p = jnp.exp(sm - m_new) * maskf[None]
        alpha = jnp.exp(m_old - m_new)
        l_ref[...] = alpha * l_ref[...] + jnp.sum(p, axis=-1, keepdims=True)
        pv = jnp.dot(p.reshape(nh * tq, tk).astype(jnp.bfloat16), v, preferred_element_type=jnp.float32)
        acc_ref[...] = alpha * acc_ref[...] + pv.reshape(nh, tq, pv.shape[-1])
        m_ref[...] = m_new
        return carry

    lax.fori_loop(lo, hi, body, 0)
    return acc_ref[...] / jnp.maximum(l_ref[...], 1e-30)


def _split3_bf16(x):
    hi = x.astype(jnp.bfloat16)
    r1 = x - hi.astype(jnp.float32)
    mid = r1.astype(jnp.bfloat16)
    lo = (r1 - mid.astype(jnp.float32)).astype(jnp.bfloat16)
    return hi, mid, lo


NSA_TQ = 128
NSA_TK_SLC = 512
NSA_TK_WIN = 128


def _nsa_prompt_kernel(q_ref, g_ref, kc_ref, vc_ref, ks_ref, vs_ref, kw_ref, vw_ref, share_ref, o_ref,
                       m_ref, l_ref, acc_ref, *, n_cmp, n_blk, n_keys):
    nh, tq, d = NSA_HEADS, NSA_TQ, HEAD_DIM
    nbp = share_ref.shape[0]
    ncp = share_ref.shape[1]
    start = pl.program_id(1) * tq
    qpos_c = start + lax.broadcasted_iota(jnp.int32, (tq, 1), 0)
    qpos_r = start + lax.broadcasted_iota(jnp.int32, (1, tq), 1)

    q = q_ref[0] * (d ** -0.5)
    qh = jnp.concatenate([q[:, h * d:(h + 1) * d] for h in range(nh)], axis=0).astype(jnp.bfloat16)

    c_r = lax.broadcasted_iota(jnp.int32, (1, ncp), 1)
    mask_c = ((c_r * CMP_STRIDE + (CMP_LEN - 1) <= qpos_c) & (c_r < n_cmp)).astype(jnp.float32)
    s = lax.dot_general(qh, kc_ref[0], _NT, preferred_element_type=jnp.float32).reshape(nh, tq, ncp)
    sm = s + ((mask_c - 1.0) * (-NEG))[None]
    e = jnp.exp(sm - jnp.max(sm, axis=-1, keepdims=True)) * mask_c[None]
    p_cmp = e / jnp.maximum(jnp.sum(e, axis=-1, keepdims=True), 1e-30)
    o_cmp = jnp.dot(p_cmp.reshape(nh * tq, ncp).astype(jnp.bfloat16), vc_ref[0],
                    preferred_element_type=jnp.float32)

    psum = jnp.sum(p_cmp, axis=0)
    share = share_ref[...]
    imp_t = sum(lax.dot_general(share, part, _NT, preferred_element_type=jnp.float32)
                for part in _split3_bf16(psum))
    blk = lax.broadcasted_iota(jnp.int32, (nbp, tq), 0)
    forced = (blk == jnp.right_shift(qpos_r, 6)) | (blk == 0)
    future = blk * SEL_BLOCK > qpos_r
    imp_t = jnp.where(forced, FORCED, jnp.where(future, -1.0, imp_t))
    imp_t = jnp.where(blk < n_blk, imp_t, -2.0)
    rank = jnp.zeros((nbp, tq), jnp.float32)
    for i in range(n_blk):
        row = imp_t[i:i + 1, :]
        beats = (row > imp_t) | ((row == imp_t) & (blk > i))
        rank = rank + beats.astype(jnp.float32)
    sel_t = (rank < float(min(SEL_COUNT, n_blk))).astype(jnp.bfloat16)
    eye = (lax.broadcasted_iota(jnp.int32, (tq, tq), 0) == lax.broadcasted_iota(jnp.int32, (tq, tq), 1))
    sel = lax.dot_general(eye.astype(jnp.bfloat16), sel_t, _NT,
                          preferred_element_type=jnp.float32).astype(jnp.bfloat16)

    def slc_mask(kt):
        kpos = kt * NSA_TK_SLC + lax.broadcasted_iota(jnp.int32, (1, NSA_TK_SLC), 1)
        kblk = kt * (NSA_TK_SLC // SEL_BLOCK) + jnp.right_shift(
            lax.broadcasted_iota(jnp.int32, (nbp, NSA_TK_SLC), 1), 6)
        expand = (kblk == lax.broadcasted_iota(jnp.int32, (nbp, NSA_TK_SLC), 0)).astype(jnp.bfloat16)
        chosen = jnp.dot(sel, expand, preferred_element_type=jnp.float32)
        return chosen * (kpos <= qpos_c).astype(jnp.float32)

    hi_slc = jnp.minimum((start + tq - 1) // NSA_TK_SLC + 1, n_keys // NSA_TK_SLC)
    o_slc = _softmax_tiles(qh, ks_ref, vs_ref, 0, hi_slc, NSA_TK_SLC, slc_mask, m_ref, l_ref, acc_ref, nh, tq)

    def win_mask(kt):
        kpos = kt * NSA_TK_WIN + lax.broadcasted_iota(jnp.int32, (1, NSA_TK_WIN), 1)
        dlt = qpos_c - kpos
        return ((dlt >= 0) & (dlt < WINDOW)).astype(jnp.float32)

    lo_win = jnp.maximum(start - (WINDOW - 1), 0) // NSA_TK_WIN
    hi_win = (start + tq - 1) // NSA_TK_WIN + 1
    o_win = _softmax_tiles(qh, kw_ref, vw_ref, lo_win, hi_win, NSA_TK_WIN, win_mask, m_ref, l_ref, acc_ref, nh, tq)

    gate = 1.0 / (1.0 + jnp.exp(-g_ref[0]))
    o_cmp = o_cmp.reshape(nh, tq, d)
    for h in range(nh):
        o_ref[0, :, h * d:(h + 1) * d] = (gate[:, 3 * h:3 * h + 1] * o_cmp[h]
                                          + gate[:, 3 * h + 1:3 * h + 2] * o_slc[h]
                                          + gate[:, 3 * h + 2:3 * h + 3] * o_win[h])


def _share_matrix_t(n_cmp, n_blk, ncp, nbp):
    c0 = np.arange(n_cmp)[None, :] * CMP_STRIDE
    j0 = np.arange(n_blk)[:, None] * SEL_BLOCK
    share = np.clip(np.minimum(c0 + CMP_LEN, j0 + SEL_BLOCK) - np.maximum(c0, j0), 0, None) / CMP_LEN
    out = np.zeros((nbp, ncp), np.float32)
    out[:n_blk, :n_cmp] = share
    return jnp.asarray(out, jnp.bfloat16)


def nsa_prompt_attention(q, gates, k_cmp, v_cmp, k_slc, v_slc, k_win, v_win):
    b, t, _ = q.shape
    d = HEAD_DIM
    n_cmp = k_cmp.shape[1]
    n_blk = -(-t // SEL_BLOCK)
    ncp = -(-n_cmp // LANE) * LANE
    nbp = -(-n_blk // 16) * 16
    assert t % NSA_TK_SLC == 0 and t % NSA_TQ == 0
    bf = jnp.bfloat16
    cpad = ((0, 0), (0, ncp - n_cmp), (0, 0))
    kc, vc = jnp.pad(k_cmp, cpad).astype(bf), jnp.pad(v_cmp, cpad).astype(bf)
    whole = lambda n: pl.BlockSpec((1, n, d), lambda bi, qi: (bi, 0, 0))
    return pl.pallas_call(
        functools.partial(_nsa_prompt_kernel, n_cmp=n_cmp, n_blk=n_blk, n_keys=t),
        grid=(b, t // NSA_TQ),
        in_specs=[pl.BlockSpec((1, NSA_TQ, NSA_HEADS * d), lambda bi, qi: (bi, qi, 0)),
                  pl.BlockSpec((1, NSA_TQ, 3 * NSA_HEADS), lambda bi, qi: (bi, qi, 0)),
                  whole(ncp), whole(ncp), whole(t), whole(t), whole(t), whole(t),
                  pl.BlockSpec((nbp, ncp), lambda bi, qi: (0, 0))],
        out_specs=pl.BlockSpec((1, NSA_TQ, NSA_HEADS * d), lambda bi, qi: (bi, qi, 0)),
        out_shape=jax.ShapeDtypeStruct((b, t, NSA_HEADS * d), jnp.float32),
        scratch_shapes=[pltpu.VMEM((NSA_HEADS, NSA_TQ, 1), jnp.float32),
                        pltpu.VMEM((NSA_HEADS, NSA_TQ, 1), jnp.float32),
                        pltpu.VMEM((NSA_HEADS, NSA_TQ, d), jnp.float32)],
        compiler_params=pltpu.CompilerParams(
            dimension_semantics=("parallel", "arbitrary"), vmem_limit_bytes=VMEM_LIMIT),
        name="nsa_prompt",
    )(q, gates, kc, vc, k_slc.astype(bf), v_slc.astype(bf), k_win.astype(bf), v_win.astype(bf),
      _share_matrix_t(n_cmp, n_blk, ncp, nbp))


DSA_TQ = 128
DSA_TK = 256
INT_MIN = -2 ** 31


def _dsa_prompt_kernel(q_ref, qi_ref, wt_ref, k_ref, v_ref, ki_ref, tri_ref, o_ref,
                       key_ref, sel_ref, m_ref, l_ref, acc_ref, *, n_keys, topk):
    nh, tq, d, tk = DSA_HEADS, DSA_TQ, HEAD_DIM, DSA_TK
    start = pl.program_id(1) * tq
    qpos_r = start + lax.broadcasted_iota(jnp.int32, (1, tq), 1)
    n_kt = jnp.minimum((start + tq - 1) // tk + 1, n_keys // tk)

    wt = wt_ref[0] * (IDX_HEADS ** -0.5 * IDX_DIM ** -0.5)

    def score_body(kt, carry):
        off = pl.multiple_of(kt * tk, tk)
        ki = ki_ref[0, pl.ds(off, tk), :]
        acc = jnp.zeros((tk, tq), jnp.float32)
        for h in range(IDX_HEADS):
            dots = lax.dot_general(ki, qi_ref[0, h], _NT, preferred_element_type=jnp.float32)
            acc = acc + jnp.maximum(dots, 0.0) * wt[h:h + 1, :]
        kpos_c = off + lax.broadcasted_iota(jnp.int32, (tk, 1), 0)
        score = jnp.where(kpos_c <= qpos_r, acc, NEG)
        bits = pltpu.bitcast(score, jnp.int32)
        key_ref[kt] = bits ^ (jnp.right_shift(bits, 31) & 0x7FFFFFFF)
        return carry

    lax.fori_loop(0, n_kt, score_body, 0)

    def count(pred):
        def body(kt, part):
            hit = pred(key_ref[kt]).astype(jnp.int32)
            return part + jnp.sum(hit.reshape(tk // 8, 8, tq), axis=0)
        part = lax.fori_loop(0, n_kt, body, jnp.zeros((8, tq), jnp.int32))
        return jnp.sum(part, axis=0, keepdims=True)

    thr = jnp.where(count(lambda kk: kk >= 0) >= topk, 0, INT_MIN).astype(jnp.int32)

    def bit_body(i, t):
        cand = t | jnp.left_shift(jnp.int32(1), 30 - i)
        return jnp.where(count(lambda kk: kk >= cand) >= topk, cand, t)

    thr = lax.fori_loop(0, 31, bit_body, thr)
    room = (topk - count(lambda kk: kk > thr)).astype(jnp.float32)

    def sel_body(kt, seen):
        kk = key_ref[kt]
        eq = (kk == thr).astype(jnp.float32)
        prefix = jnp.dot(tri_ref[...], eq.astype(jnp.bfloat16), preferred_element_type=jnp.float32)
        keep = (kk > thr) | ((eq > 0.0) & (prefix + seen <= room))
        kpos_c = kt * tk + lax.broadcasted_iota(jnp.int32, (tk, 1), 0)
        sel_ref[kt] = (keep & (kpos_c <= qpos_r)).astype(jnp.bfloat16)
        return seen + jnp.sum(eq, axis=0, keepdims=True)

    lax.fori_loop(0, n_kt, sel_body, jnp.zeros((1, tq), jnp.float32))

    eye = (lax.broadcasted_iota(jnp.int32, (tq, tq), 0)
           == lax.broadcasted_iota(jnp.int32, (tq, tq), 1)).astype(jnp.bfloat16)

    def sel_mask(kt):
        return lax.dot_general(eye, sel_ref[kt], _NT, preferred_element_type=jnp.float32)

    q = q_ref[0] * (d ** -0.5)
    qh = jnp.concatenate([q[:, h * d:(h + 1) * d] for h in range(nh)], axis=0).astype(jnp.bfloat16)
    o = _softmax_tiles(qh, k_ref, v_ref, 0, n_kt, tk, sel_mask, m_ref, l_ref, acc_ref, nh, tq)
    for h in range(nh):
        o_ref[0, :, h * d:(h + 1) * d] = o[h]


def dsa_prompt_attention(q, q_idx, w_idx, k, v, k_idx):
    b, t, _ = q.shape
    d = HEAD_DIM
    topk = min(DSA_TOPK, t // 4)
    assert t % DSA_TK == 0 and t % DSA_TQ == 0
    bf = jnp.bfloat16
    tri = jnp.asarray(np.tril(np.ones((DSA_TK, DSA_TK), np.float32)), bf)
    whole = lambda n, w: pl.BlockSpec((1, n, w), lambda bi, qi: (bi, 0, 0))
    return pl.pallas_call(
        functools.partial(_dsa_prompt_kernel, n_keys=t, topk=topk),
        grid=(b, t // DSA_TQ),
        in_specs=[pl.BlockSpec((1, DSA_TQ, DSA_HEADS * d), lambda bi, qi: (bi, qi, 0)),
                  pl.BlockSpec((1, IDX_HEADS, DSA_TQ, IDX_DIM), lambda bi, qi: (bi, 0, qi, 0)),
                  pl.BlockSpec((1, IDX_HEADS, DSA_TQ), lambda bi, qi: (bi, 0, qi)),
                  whole(t, d), whole(t, d), whole(t, IDX_DIM),
                  pl.BlockSpec((DSA_TK, DSA_TK), lambda bi, qi: (0, 0))],
        out_specs=pl.BlockSpec((1, DSA_TQ, DSA_HEADS * d), lambda bi, qi: (bi, qi, 0)),
        out_shape=jax.ShapeDtypeStruct((b, t, DSA_HEADS * d), jnp.float32),
        scratch_shapes=[pltpu.VMEM((t // DSA_TK, DSA_TK, DSA_TQ), jnp.int32),
                        pltpu.VMEM((t // DSA_TK, DSA_TK, DSA_TQ), bf),
                        pltpu.VMEM((DSA_HEADS, DSA_TQ, 1), jnp.float32),
                        pltpu.VMEM((DSA_HEADS, DSA_TQ, 1), jnp.float32),
                        pltpu.VMEM((DSA_HEADS, DSA_TQ, d), jnp.float32)],
        compiler_params=pltpu.CompilerParams(
            dimension_semantics=("parallel", "arbitrary"), vmem_limit_bytes=VMEM_LIMIT),
        name="dsa_prompt",
    )(q, jnp.swapaxes(q_idx, 1, 2).astype(bf), jnp.swapaxes(w_idx, 1, 2),
      k.astype(bf), v.astype(bf), k_idx.astype(bf), tri)


def rms_norm(x, g):
    xf = x.astype(jnp.float32)
    y = xf * lax.rsqrt(jnp.mean(xf * xf, axis=-1, keepdims=True) + EPS)
    return (y * g.astype(jnp.float32)).astype(x.dtype)


def l2norm(x):
    xf = x.astype(jnp.float32)
    return xf * lax.rsqrt(jnp.sum(xf * xf, axis=-1, keepdims=True) + EPS)


def rope(x, pos):
    half = x.shape[-1] // 2
    inv = ROPE_THETA ** (-jnp.arange(half, dtype=jnp.float32) / half)
    ang = pos.astype(jnp.float32)[:, None] * inv
    cos, sin = jnp.cos(ang)[:, None, :], jnp.sin(ang)[:, None, :]
    xf = x.astype(jnp.float32)
    x1, x2 = xf[..., :half], xf[..., half:]
    return jnp.concatenate([x1 * cos - x2 * sin, x2 * cos + x1 * sin], axis=-1).astype(x.dtype)


def masked_softmax(s, mask):
    s = jnp.where(mask, s.astype(jnp.float32), NEG)
    e = jnp.where(mask, jnp.exp(s - jnp.max(s, axis=-1, keepdims=True)), 0.0)
    return e / jnp.maximum(jnp.sum(e, axis=-1, keepdims=True), 1e-30)


def split_cols(h):
    return jnp.split(h, [int(o) for o in np.cumsum(IN_SPLITS)[:-1]], axis=-1)


def gather_pages(pool, page_table):
    rows = pool[page_table]
    return rows.reshape(rows.shape[0], -1, *rows.shape[3:])


def over_query_blocks(fn, *qs):
    b, t = qs[0].shape[:2]
    if t <= Q_BLOCK:
        return fn(0, *qs)
    nb = t // Q_BLOCK
    blocks = tuple(q.reshape(b, nb, Q_BLOCK, *q.shape[2:]).swapaxes(0, 1) for q in qs)
    out = lax.map(lambda a: fn(a[0] * Q_BLOCK, *a[1]), (jnp.arange(nb), blocks))
    out = out.swapaxes(0, 1)
    return out.reshape(b, t, *out.shape[3:])


def nsa_compress(rows, pe, w1, w2):
    b, n_keys, d = rows.shape
    r = CMP_LEN // CMP_STRIDE
    n_cmp = (n_keys - CMP_LEN) // CMP_STRIDE + 1
    seg = rows[:, :(n_cmp + r - 1) * CMP_STRIDE].reshape(b, n_cmp + r - 1, CMP_STRIDE * d)
    w1r = w1.reshape(r, CMP_STRIDE * d, CMP_HID)
    h = pe.reshape(-1) @ w1 + sum(seg[:, i:i + n_cmp] @ w1r[i] for i in range(r))
    return jax.nn.gelu(h) @ w2


def nsa_attention(q, gates, k_cmp, v_cmp, k_slc, v_slc, k_win, v_win, past, win_pos0, banded):
    b, n_keys, d = k_slc.shape
    scale = d ** -0.5
    n_cmp = k_cmp.shape[1]
    cmp_end = jnp.arange(n_cmp) * CMP_STRIDE + CMP_LEN - 1
    n_blk = -(-n_keys // SEL_BLOCK)
    n_sel = min(SEL_COUNT, n_blk)
    pad = ((0, 0), (0, n_blk * SEL_BLOCK - n_keys), (0, 0))
    k_blk = jnp.pad(k_slc, pad).reshape(b, n_blk, SEL_BLOCK, d)
    v_blk = jnp.pad(v_slc, pad).reshape(b, n_blk, SEL_BLOCK, d)
    c0 = jnp.arange(n_cmp)[:, None] * CMP_STRIDE
    j0 = jnp.arange(n_blk)[None, :] * SEL_BLOCK
    share = jnp.clip(jnp.minimum(c0 + CMP_LEN, j0 + SEL_BLOCK) - jnp.maximum(c0, j0), 0, None).astype(jnp.float32) / CMP_LEN
    blk_id = jnp.arange(n_blk)
    if banded:
        wpad = ((0, 0), (WINDOW, 0), (0, 0))
        k_win, v_win = jnp.pad(k_win, wpad), jnp.pad(v_win, wpad)

    def block(start, qb, gb):
        nq = qb.shape[1]
        qp = past + start + jnp.arange(nq)
        s = jnp.einsum('bqhd,bcd->bqhc', qb, k_cmp) * scale
        p_cmp = masked_softmax(s, (cmp_end[None, :] <= qp[:, None])[None, :, None, :])
        o_cmp = jnp.einsum('bqhc,bcd->bqhd', p_cmp.astype(v_cmp.dtype), v_cmp)
        imp = jnp.einsum('bqhc,cj->bqj', p_cmp, share)
        forced = (blk_id[None, :] == qp[:, None] // SEL_BLOCK) | (blk_id[None, :] == 0)
        future = blk_id[None, :] * SEL_BLOCK > qp[:, None]
        imp = jnp.where(forced[None], FORCED, jnp.where(future[None], -1.0, imp))
        _, sel = lax.top_k(imp, n_sel)
        ks = jax.vmap(lambda kb, i: kb[i])(k_blk, sel).reshape(b, nq, n_sel * SEL_BLOCK, d)
        vs = jax.vmap(lambda vb, i: vb[i])(v_blk, sel).reshape(b, nq, n_sel * SEL_BLOCK, d)
        kpos = (sel[..., None] * SEL_BLOCK + jnp.arange(SEL_BLOCK)).reshape(b, nq, n_sel * SEL_BLOCK)
        s = jnp.einsum('bqhd,bqkd->bqhk', qb, ks) * scale
        p = masked_softmax(s, (kpos <= qp[None, :, None])[:, :, None, :])
        o_slc = jnp.einsum('bqhk,bqkd->bqhd', p.astype(vs.dtype), vs)
        if banded:
            n_w = WINDOW + nq
            kw = lax.dynamic_slice_in_dim(k_win, start, n_w, axis=1)
            vw = lax.dynamic_slice_in_dim(v_win, start, n_w, axis=1)
            kp = past + start - WINDOW + jnp.arange(n_w)
        else:
            kw, vw = k_win, v_win
            kp = win_pos0 + jnp.arange(k_win.shape[1])
        dlt = qp[:, None] - kp[None, :]
        s = jnp.einsum('bqhd,bkd->bqhk', qb, kw) * scale
        p = masked_softmax(s, ((dlt >= 0) & (dlt < WINDOW) & (kp[None, :] >= 0))[None, :, None, :])
        o_win = jnp.einsum('bqhk,bkd->bqhd', p.astype(vw.dtype), vw)
        g = jax.nn.sigmoid(gb.astype(jnp.float32))
        o = g[..., 0:1] * o_cmp + g[..., 1:2] * o_slc + g[..., 2:3] * o_win
        return o.astype(qb.dtype)

    return over_query_blocks(block, q, gates)


def dsa_attention(q, q_idx, w_idx, k, v, k_idx, past):
    b, n_keys, d = k.shape
    topk = min(DSA_TOPK, n_keys // 4)
    kpos = jnp.arange(n_keys)

    def block(start, qb, qib, wb):
        qp = past + start + jnp.arange(qb.shape[1])
        causal = kpos[None, :] <= qp[:, None]
        dots = jnp.einsum('bqhd,bsd->bqhs', qib, k_idx).astype(jnp.float32) * IDX_DIM ** -0.5
        score = jnp.einsum('bqh,bqhs->bqs', wb.astype(jnp.float32) * IDX_HEADS ** -0.5, jax.nn.relu(dots))
        score = jnp.where(causal[None], score, NEG)
        _, sel = lax.top_k(score, topk)
        ks = jax.vmap(lambda kk, i: kk[i])(k, sel)
        vs = jax.vmap(lambda vv, i: vv[i])(v, sel)
        s = jnp.einsum('bqhd,bqkd->bqhk', qb, ks) * d ** -0.5
        p = masked_softmax(s, (sel <= qp[None, :, None])[:, :, None, :])
        return jnp.einsum('bqhk,bqkd->bqhd', p.astype(vs.dtype), vs)

    return over_query_blocks(block, q, q_idx, w_idx)


def gated_delta_chunked(q, k, v, g, beta, s0):
    f32 = jnp.float32
    b, t, h, dk = k.shape
    dv = v.shape[-1]
    c = min(GDN_CHUNK, t)
    n = -(-t // c)
    pad = n * c - t

    def to_chunks(a):
        a = jnp.pad(a.astype(f32), [(0, 0), (0, pad)] + [(0, 0)] * (a.ndim - 2))
        a = a.reshape(b, n, c, *a.shape[2:])
        return jnp.moveaxis(a, (1, 3), (0, 2))

    qc, kc, vc, gc, bc = (to_chunks(a) for a in (q, k, v, g, beta))
    gcum = jnp.cumsum(gc, axis=-1)
    pos = jnp.arange(c)
    strict = pos[:, None] > pos[None, :]
    incl = pos[:, None] >= pos[None, :]
    diff = gcum[..., :, None] - gcum[..., None, :]
    decay = jnp.where(incl, jnp.exp(jnp.where(incl, diff, 0.0)), 0.0)
    k_beta = kc * bc[..., None]
    a_mat = jnp.where(strict, jnp.einsum('nbhid,nbhjd->nbhij', k_beta, kc) * decay, 0.0)
    rhs = jnp.concatenate([vc * bc[..., None], k_beta * jnp.exp(gcum)[..., None]], axis=-1)
    sol = lax.linalg.triangular_solve(a_mat + jnp.eye(c, dtype=f32), rhs,
                                      left_side=True, lower=True, unit_diagonal=True)
    u, w = sol[..., :dv], sol[..., dv:]
    qk = jnp.einsum('nbhid,nbhjd->nbhij', qc, kc) * decay
    q_dec = qc * jnp.exp(gcum)[..., None]
    k_dec = kc * jnp.exp(gcum[..., -1:] - gcum)[..., None]
    chunk_decay = jnp.exp(gcum[..., -1])

    def step(state, xs):
        u_i, w_i, qd_i, qk_i, kd_i, cd_i = xs
        new_v = u_i - jnp.einsum('bhck,bhkv->bhcv', w_i, state)
        o_i = jnp.einsum('bhck,bhkv->bhcv', qd_i, state) + jnp.einsum('bhij,bhjv->bhiv', qk_i, new_v)
        state = state * cd_i[..., None, None] + jnp.einsum('bhck,bhcv->bhkv', kd_i, new_v)
        return state, o_i

    s_final, o = lax.scan(step, s0.astype(f32), (u, w, q_dec, qk, k_dec, chunk_decay))
    o = jnp.moveaxis(o, (0, 2), (1, 3)).reshape(b, n * c, h, dv)[:, :t]
    return o, s_final


def gdn_mixer(qkv, a, beta_logit, z, conv_prev, s0, conv_w, a_log, dt_bias, norm_g):
    b, t, _ = qkv.shape
    xp = jnp.concatenate([conv_prev.astype(qkv.dtype), qkv], axis=1)
    conv = jax.nn.silu(sum(xp[:, i:i + t] * conv_w[i] for i in range(GDN_CONV)))
    q, k, v = (cc.reshape(b, t, GDN_HEADS, HEAD_DIM) for cc in jnp.split(conv, 3, axis=-1))
    q = l2norm(q) * HEAD_DIM ** -0.5
    k = l2norm(k)
    g = -jnp.exp(a_log.astype(jnp.float32)) * jax.nn.softplus(a.astype(jnp.float32) + dt_bias.astype(jnp.float32))
    beta = jax.nn.sigmoid(beta_logit.astype(jnp.float32))
    o, s_new = gated_delta_chunked(q, k, v, g, beta, s0)
    o = rms_norm(o, norm_g) * jax.nn.silu(z.reshape(b, t, GDN_HEADS, HEAD_DIM).astype(jnp.float32))
    return o.reshape(b, t, W_C).astype(qkv.dtype), s_new, xp[:, t:]


def mem_attention(h, mem_kv, w_q, w_o):
    b, t, _ = h.shape
    q = pmm(h, w_q).reshape(b, t, MEM_HEADS, HEAD_DIM)
    s = jnp.einsum('bqhd,bmhd->bhqm', q, mem_kv[:, :, 0]) * HEAD_DIM ** -0.5
    p = jax.nn.softmax(s.astype(jnp.float32), axis=-1).astype(h.dtype)
    o = jnp.einsum('bhqm,bmhd->bqhd', p, mem_kv[:, :, 1])
    return pmm(o.reshape(b, t, MEM_HEADS * HEAD_DIM), w_o)


def run_group(x, mem, cache, p):
    prompt = cache is None
    b, t, _ = x.shape
    past = 0 if prompt else cache['page_table'].shape[1] * cache['nsa_kv'].shape[2]
    qpos = past + jnp.arange(t)
    names = ('nsa_kv', 'dsa_kv', 'idx_k', 'win_kv', 'gdn', 'conv') + (('mem_kv',) if prompt else ())
    out = {nm: [] for nm in names}
    for l in range(DEPTH):
        h = rms_norm(x, p['norm_mix_g'][l])
        (q_a, kv_a, g_a, q_b, kv_b, q_i, w_i, k_i, qkv_c, a_c, b_c, z_c, merge) = split_cols(pmm(h, p['w_in'][l]))
        q_a = rope(q_a.reshape(b, t, NSA_HEADS, HEAD_DIM), qpos)
        kv_a = kv_a.reshape(b, t, 6, HEAD_DIM)
        rot_a = rope(kv_a[:, :, 2::2], qpos)
        nsa_new = jnp.stack([kv_a[:, :, 0], kv_a[:, :, 1], rot_a[:, :, 0], kv_a[:, :, 3]], axis=2)
        win_new = jnp.stack([rot_a[:, :, 1], kv_a[:, :, 5]], axis=2)
        q_b = rope(q_b.reshape(b, t, DSA_HEADS, HEAD_DIM), qpos)
        kv_b = kv_b.reshape(b, t, 2, HEAD_DIM)
        dsa_new = jnp.stack([rope(kv_b[:, :, :1], qpos)[:, :, 0], kv_b[:, :, 1]], axis=2)
        q_i = rope(q_i.reshape(b, t, IDX_HEADS, IDX_DIM), qpos)
        idx_new = rope(k_i[:, :, None], qpos)[:, :, 0]
        if prompt:
            nsa_all, dsa_all, idx_all = nsa_new, dsa_new, idx_new
            win_all, win_pos0 = win_new, 0
            conv_prev = jnp.zeros((b, GDN_CONV - 1, 3 * W_C), x.dtype)
            s0 = jnp.zeros((b, GDN_HEADS, HEAD_DIM, HEAD_DIM), jnp.float32)
            mem_kv = pmm(rms_norm(mem, p['mem_norm_g'][l]), p['w_mem_kv'][l]).reshape(b, -1, 2, MEM_HEADS, HEAD_DIM)
        else:
            pt = cache['page_table']
            nsa_all = jnp.concatenate([gather_pages(cache['nsa_kv'][l], pt), nsa_new], axis=1)
            dsa_all = jnp.concatenate([gather_pages(cache['dsa_kv'][l], pt), dsa_new], axis=1)
            idx_all = jnp.concatenate([gather_pages(cache['idx_k'][l], pt), idx_new], axis=1)
            win_all = jnp.concatenate([cache['win_kv'][l], win_new], axis=1)
            win_pos0 = past - cache['win_kv'].shape[2]
            conv_prev, s0 = cache['conv'][l], cache['gdn'][l]
            mem_kv = cache['mem_kv'][l]
        k_cmp = nsa_compress(nsa_all[:, :, 0], p['nsa_cmp_pe'][l, 0], p['nsa_cmp_w1'][l, 0], p['nsa_cmp_w2'][l, 0])
        v_cmp = nsa_compress(nsa_all[:, :, 1], p['nsa_cmp_pe'][l, 1], p['nsa_cmp_w1'][l, 1], p['nsa_cmp_w2'][l, 1])
        cmp_end = jnp.arange(k_cmp.shape[1]) * CMP_STRIDE + CMP_LEN - 1
        k_cmp = rope(k_cmp[:, :, None], cmp_end)[:, :, 0]
        if prompt:
            o_a = nsa_prompt_attention(q_a.reshape(b, t, W_A), g_a, k_cmp, v_cmp, nsa_all[:, :, 2], nsa_all[:, :, 3],
                                       win_all[:, :, 0], win_all[:, :, 1])
            o_b = dsa_prompt_attention(q_b.reshape(b, t, W_B), q_i, w_i, dsa_all[:, :, 0], dsa_all[:, :, 1], idx_all)
        else:
            o_a = nsa_attention(q_a, g_a.reshape(b, t, NSA_HEADS, 3), k_cmp, v_cmp, nsa_all[:, :, 2], nsa_all[:, :, 3],
                                win_all[:, :, 0], win_all[:, :, 1], past, win_pos0, prompt)
            o_b = dsa_attention(q_b, q_i, w_i, dsa_all[:, :, 0], dsa_all[:, :, 1], idx_all, past)
        o_c, s_new, conv_new = gdn_mixer(qkv_c, a_c, b_c, z_c, conv_prev, s0, p['gdn_conv_w'][l],
                                         p['gdn_a_log'][l], p['gdn_dt_bias'][l], p['gdn_norm_g'][l])
        m_a, m_b, m_c = jnp.split(jax.nn.sigmoid(merge), 3, axis=-1)
        mix = (m_a * pmm(o_a.reshape(b, t, W_A), p['w_branch_a'][l])
               + m_b * pmm(o_b.reshape(b, t, W_B), p['w_branch_b'][l])
               + m_c * pmm(o_c, p['w_branch_c'][l]))
        x = x + pmm(mix, p['w_mix_out'][l])
        x = x + mem_attention(rms_norm(x, p['norm_mem_g'][l]), mem_kv, p['w_mem_q'][l], p['w_mem_o'][l])
        h = rms_norm(x, p['norm_ffn_g'][l])
        x = x + pmm(jax.nn.silu(pmm(h, p['w_ffn_gate'][l])) * pmm(h, p['w_ffn_up'][l]), p['w_ffn_down'][l])
        out['nsa_kv'].append(nsa_new)
        out['dsa_kv'].append(dsa_new)
        out['idx_k'].append(idx_new)
        out['win_kv'].append(win_new[:, -min(WINDOW, t):] if prompt else win_all[:, -cache['win_kv'].shape[2]:])
        out['gdn'].append(s_new)
        out['conv'].append(conv_new)
        if prompt:
            out['mem_kv'].append(mem_kv)
    y = rms_norm(x, p['norm_final_g'])
    return y, {nm: jnp.stack(v) for nm, v in out.items()}


def kernel(x_prompt, x_sample, mem_prompt, cache_nsa_kv, cache_dsa_kv, cache_dsa_idx_k, cache_win_kv, cache_mem_kv, state_gdn, state_conv, page_table, norm_mix_g, w_in, nsa_cmp_pe, nsa_cmp_w1, nsa_cmp_w2, gdn_conv_w, gdn_a_log, gdn_dt_bias, gdn_norm_g, w_branch_a, w_branch_b, w_branch_c, w_mix_out, norm_mem_g, mem_norm_g, w_mem_q, w_mem_kv, w_mem_o, norm_ffn_g, w_ffn_gate, w_ffn_up, w_ffn_down, norm_final_g):
    p = dict(norm_mix_g=norm_mix_g, w_in=w_in, nsa_cmp_pe=nsa_cmp_pe, nsa_cmp_w1=nsa_cmp_w1,
             nsa_cmp_w2=nsa_cmp_w2, gdn_conv_w=gdn_conv_w, gdn_a_log=gdn_a_log, gdn_dt_bias=gdn_dt_bias,
             gdn_norm_g=gdn_norm_g, w_branch_a=w_branch_a, w_branch_b=w_branch_b, w_branch_c=w_branch_c,
             w_mix_out=w_mix_out, norm_mem_g=norm_mem_g, mem_norm_g=mem_norm_g, w_mem_q=w_mem_q,
             w_mem_kv=w_mem_kv, w_mem_o=w_mem_o, norm_ffn_g=norm_ffn_g, w_ffn_gate=w_ffn_gate,
             w_ffn_up=w_ffn_up, w_ffn_down=w_ffn_down, norm_final_g=norm_final_g)
    cache = dict(nsa_kv=cache_nsa_kv, dsa_kv=cache_dsa_kv, idx_k=cache_dsa_idx_k, win_kv=cache_win_kv,
                 mem_kv=cache_mem_kv, gdn=state_gdn, conv=state_conv, page_table=page_table)
    y_prompt, sp = run_group(x_prompt, mem_prompt, None, p)
    y_sample, ss = run_group(x_sample, None, cache, p)
    return (y_prompt, y_sample,
            sp['nsa_kv'], sp['dsa_kv'], sp['idx_k'], sp['win_kv'], sp['gdn'], sp['conv'], sp['mem_kv'],
            ss['nsa_kv'], ss['dsa_kv'], ss['idx_k'], ss['win_kv'], ss['gdn'], ss['conv'])
```

```python
import functools
import math

import jax
import jax.numpy as jnp
import numpy as np
from jax import lax
from jax.experimental import pallas as pl
from jax.experimental.pallas import tpu as pltpu

D_MODEL = 2048
DEPTH = 2
PAGE_SIZE = 128
HEAD_DIM = 128
NSA_HEADS = D_MODEL // (4 * HEAD_DIM)
DSA_HEADS = D_MODEL // (4 * HEAD_DIM)
GDN_HEADS = D_MODEL // (2 * HEAD_DIM)
W_A = NSA_HEADS * HEAD_DIM
W_B = DSA_HEADS * HEAD_DIM
W_C = GDN_HEADS * HEAD_DIM
CMP_LEN = 32
CMP_STRIDE = 16
CMP_HID = 2 * HEAD_DIM
SEL_BLOCK = 64
SEL_COUNT = 16
WINDOW = 512
IDX_HEADS = 16
IDX_DIM = 64
DSA_TOPK = 256
GDN_CONV = 4
GDN_CHUNK = 64
MEM_HEADS = 4
ROPE_THETA = 10000.0
Q_BLOCK = 128
EPS = 1e-6
NEG = -1e30
FORCED = 1e9
IN_SPLITS = (W_A, 6 * HEAD_DIM, 3 * NSA_HEADS,
             W_B, 2 * HEAD_DIM, IDX_HEADS * IDX_DIM, IDX_HEADS, IDX_DIM,
             3 * W_C, GDN_HEADS, GDN_HEADS, W_C,
             3 * D_MODEL)

LANE = 128
VMEM_LIMIT = 48 * 1024 * 1024


DENSE_VMEM_BUDGET = 40 * 1024 * 1024
NORM_ROWS = 128


def _divisor_tiles(n, cap):
    return [t for t in range(min(cap, n), 0, -LANE) if t % LANE == 0 and n % t == 0]


def _dense_tiles(m, n, per_row_bytes, per_col_bytes, per_out_bytes):
    for tm in (1024, 512, 256, 128):
        if m % tm:
            continue
        for tn in _divisor_tiles(n, 1024):
            if tm * per_row_bytes + tn * per_col_bytes + tm * tn * per_out_bytes <= DENSE_VMEM_BUDGET:
                return tm, tn
    raise ValueError("no dense tile fits VMEM")


def _stage_rows(x_ref, g_ref, xn_ref):
    def body(r, carry):
        rows = pl.ds(pl.multiple_of(r * NORM_ROWS, NORM_ROWS), NORM_ROWS)
        xf = x_ref[rows, :].astype(jnp.float32)
        if g_ref is not None:
            xf = xf * lax.rsqrt(jnp.mean(xf * xf, axis=-1, keepdims=True) + EPS) * g_ref[...]
        xn_ref[rows, :] = xf.astype(jnp.bfloat16)
        return carry
    lax.fori_loop(0, x_ref.shape[0] // NORM_ROWS, body, 0)


def _dense_kernel(*refs, has_gain, has_resid):
    it = iter(refs)
    x_ref = next(it)
    g_ref = next(it) if has_gain else None
    w_ref = next(it)
    r_ref = next(it) if has_resid else None
    o_ref = next(it)
    xn_ref = next(it)

    @pl.when(pl.program_id(1) == 0)
    def _():
        _stage_rows(x_ref, g_ref, xn_ref)

    acc = jnp.dot(xn_ref[...], w_ref[...], preferred_element_type=jnp.float32)
    if has_resid:
        acc = acc + r_ref[...]
    o_ref[...] = acc.astype(o_ref.dtype)


def dense(x, w, gain=None, resid=None, out_dtype=jnp.float32):
    lead, kdim, n = x.shape[:-1], x.shape[-1], w.shape[-1]
    x2 = x.reshape(-1, kdim)
    m = x2.shape[0]
    ob = jnp.dtype(out_dtype).itemsize
    tm, tn = _dense_tiles(m, n, per_row_bytes=kdim * (2 * x2.dtype.itemsize + 2), per_col_bytes=kdim * 2 * 2,
                          per_out_bytes=2 * ob + (8 if resid is not None else 0))
    args, specs = [x2], [pl.BlockSpec((tm, kdim), lambda i, j: (i, 0))]
    if gain is not None:
        args.append(gain.reshape(1, kdim).astype(jnp.float32))
        specs.append(pl.BlockSpec((1, kdim), lambda i, j: (0, 0)))
    args.append(w)
    specs.append(pl.BlockSpec((kdim, tn), lambda i, j: (0, j)))
    if resid is not None:
        args.append(resid.reshape(m, n))
        specs.append(pl.BlockSpec((tm, tn), lambda i, j: (i, j)))
    out = pl.pallas_call(
        functools.partial(_dense_kernel, has_gain=gain is not None, has_resid=resid is not None),
        grid=(m // tm, n // tn),
        in_specs=specs,
        out_specs=pl.BlockSpec((tm, tn), lambda i, j: (i, j)),
        out_shape=jax.ShapeDtypeStruct((m, n), out_dtype),
        scratch_shapes=[pltpu.VMEM((tm, kdim), jnp.bfloat16)],
        compiler_params=pltpu.CompilerParams(
            dimension_semantics=("parallel", "arbitrary"), vmem_limit_bytes=VMEM_LIMIT),
        name="dense",
    )(*args)
    return out.reshape(*lead, n)


def _ffn_act_kernel(x_ref, g_ref, wg_ref, wu_ref, o_ref, xn_ref):
    @pl.when(pl.program_id(1) == 0)
    def _():
        _stage_rows(x_ref, g_ref, xn_ref)

    xn = xn_ref[...]
    a = jnp.dot(xn, wg_ref[...], preferred_element_type=jnp.float32)
    u = jnp.dot(xn, wu_ref[...], preferred_element_type=jnp.float32)
    o_ref[...] = (a / (1.0 + jnp.exp(-a)) * u).astype(o_ref.dtype)


def ffn_act(x, gain, w_gate, w_up):
    lead, kdim, n = x.shape[:-1], x.shape[-1], w_gate.shape[-1]
    x2 = x.reshape(-1, kdim)
    m = x2.shape[0]
    tm, tn = _dense_tiles(m, n, per_row_bytes=kdim * (2 * 4 + 2), per_col_bytes=2 * kdim * 2 * 2, per_out_bytes=2 * 2)
    out = pl.pallas_call(
        _ffn_act_kernel,
        grid=(m // tm, n // tn),
        in_specs=[pl.BlockSpec((tm, kdim), lambda i, j: (i, 0)),
                  pl.BlockSpec((1, kdim), lambda i, j: (0, 0)),
                  pl.BlockSpec((kdim, tn), lambda i, j: (0, j)),
                  pl.BlockSpec((kdim, tn), lambda i, j: (0, j))],
        out_specs=pl.BlockSpec((tm, tn), lambda i, j: (i, j)),
        out_shape=jax.ShapeDtypeStruct((m, n), jnp.bfloat16),
        scratch_shapes=[pltpu.VMEM((tm, kdim), jnp.bfloat16)],
        compiler_params=pltpu.CompilerParams(
            dimension_semantics=("parallel", "arbitrary"), vmem_limit_bytes=VMEM_LIMIT),
        name="ffn_act",
    )(x2, gain.reshape(1, kdim).astype(jnp.float32), w_gate, w_up)
    return out.reshape(*lead, n)


def _branch_mix_kernel(oa_ref, ob_ref, oc_ref, ma_ref, mb_ref, mc_ref, wa_ref, wb_ref, wc_ref, o_ref, xn_ref):
    @pl.when(pl.program_id(1) == 0)
    def _():
        off = 0
        for src in (oa_ref, ob_ref, oc_ref):
            xn_ref[:, off:off + src.shape[1]] = src[...].astype(jnp.bfloat16)
            off += src.shape[1]

    def gated(m_ref, w_ref, lo, hi):
        y = jnp.dot(xn_ref[:, lo:hi], w_ref[...], preferred_element_type=jnp.float32)
        return y / (1.0 + jnp.exp(-m_ref[...]))

    o_ref[...] = (gated(ma_ref, wa_ref, 0, W_A) + gated(mb_ref, wb_ref, W_A, W_A + W_B)
                  + gated(mc_ref, wc_ref, W_A + W_B, W_A + W_B + W_C)).astype(o_ref.dtype)


def branch_mix(o_a, o_b, o_c, merge, w_a, w_b, w_c):
    m, n = merge.shape[0], w_a.shape[-1]
    ktot = W_A + W_B + W_C
    tm, tn = _dense_tiles(m, n, per_row_bytes=ktot * (2 * 4 + 2), per_col_bytes=ktot * 2 * 2, per_out_bytes=2 * 2 + 3 * 8)
    nb = n // tn
    row = lambda width: pl.BlockSpec((tm, width), lambda i, j: (i, 0))
    return pl.pallas_call(
        _branch_mix_kernel,
        grid=(m // tm, nb),
        in_specs=[row(W_A), row(W_B), row(W_C),
                  pl.BlockSpec((tm, tn), lambda i, j: (i, j)),
                  pl.BlockSpec((tm, tn), lambda i, j: (i, j + nb)),
                  pl.BlockSpec((tm, tn), lambda i, j: (i, j + 2 * nb)),
                  pl.BlockSpec((W_A, tn), lambda i, j: (0, j)),
                  pl.BlockSpec((W_B, tn), lambda i, j: (0, j)),
                  pl.BlockSpec((W_C, tn), lambda i, j: (0, j))],
        out_specs=pl.BlockSpec((tm, tn), lambda i, j: (i, j)),
        out_shape=jax.ShapeDtypeStruct((m, n), jnp.bfloat16),
        scratch_shapes=[pltpu.VMEM((tm, ktot), jnp.bfloat16)],
        compiler_params=pltpu.CompilerParams(
            dimension_semantics=("parallel", "arbitrary"), vmem_limit_bytes=VMEM_LIMIT),
        name="branch_mix",
    )(o_a, o_b, o_c, merge, merge, merge, w_a, w_b, w_c)


_NT = (((1,), (1,)), ((), ()))


def _softmax_tiles(qh, k_ref, v_ref, lo, hi, tk, mask_fn, m_ref, l_ref, acc_ref, nh, tq):
    m_ref[...] = jnp.full(m_ref.shape, NEG, jnp.float32)
    l_ref[...] = jnp.zeros(l_ref.shape, jnp.float32)
    acc_ref[...] = jnp.zeros(acc_ref.shape, jnp.float32)

    def body(kt, carry):
        off = pl.multiple_of(kt * tk, tk)
        k = k_ref[0, pl.ds(off, tk), :]
        v = v_ref[0, pl.ds(off, tk), :]
        s = lax.dot_general(qh, k, _NT, preferred_element_type=jnp.float32).reshape(nh, tq, tk)
        maskf = mask_fn(kt)
        bias = (maskf - 1.0) * (-NEG)
        sm = s + bias[None]
        m_old = m_ref[...]
        m_new = jnp.maximum(m_old, jnp.max(sm, axis=-1, keepdims=True))
        p = jnp.exp(sm - m_new) * maskf[None]
        alpha = jnp.exp(m_old - m_new)
        l_ref[...] = alpha * l_ref[...] + jnp.sum(p, axis=-1, keepdims=True)
        pv = jnp.dot(p.reshape(nh * tq, tk).astype(jnp.bfloat16), v, preferred_element_type=jnp.float32)
        acc_ref[...] = alpha * acc_ref[...] + pv.reshape(nh, tq, pv.shape[-1])
        m_ref[...] = m_new
        return carry

    lax.fori_loop(lo, hi, body, 0)
    return acc_ref[...] / jnp.maximum(l_ref[...], 1e-30)


def _split3_bf16(x):
    hi = x.astype(jnp.bfloat16)
    r1 = x - hi.astype(jnp.float32)
    mid = r1.astype(jnp.bfloat16)
    lo = (r1 - mid.astype(jnp.float32)).astype(jnp.bfloat16)
    return hi, mid, lo


def _select_blocks(psum, share, qpos_r, n_blk, tq):
    nbp, tqp = share.shape[0], psum.shape[0]
    imp_t = sum(lax.dot_general(share, part, _NT, preferred_element_type=jnp.float32)
                for part in _split3_bf16(psum))
    blk = lax.broadcasted_iota(jnp.int32, (nbp, tqp), 0)
    forced = (blk == jnp.right_shift(qpos_r, 6)) | (blk == 0)
    future = blk * SEL_BLOCK > qpos_r
    imp_t = jnp.where(forced, FORCED, jnp.where(future, -1.0, imp_t))
    imp_t = jnp.where(blk < n_blk, imp_t, -2.0)
    rank = jnp.zeros((nbp, tqp), jnp.float32)
    for i in range(n_blk):
        row = imp_t[i:i + 1, :]
        beats = (row > imp_t) | ((row == imp_t) & (blk > i))
        rank = rank + beats.astype(jnp.float32)
    sel_t = (rank < float(min(SEL_COUNT, n_blk))).astype(jnp.bfloat16)
    eye = (lax.broadcasted_iota(jnp.int32, (tq, tqp), 0) == lax.broadcasted_iota(jnp.int32, (tq, tqp), 1))
    return lax.dot_general(eye.astype(jnp.bfloat16), sel_t, _NT,
                           preferred_element_type=jnp.float32).astype(jnp.bfloat16)


def _softmax_flat(qh, tiles, nh, tq):
    scores = []
    for k, _, maskf in tiles:
        s = lax.dot_general(qh, k, _NT, preferred_element_type=jnp.float32).reshape(nh, tq, k.shape[0])
        scores.append(s + ((maskf - 1.0) * (-NEG))[None])
    m = scores[0]
    for s in scores[1:]:
        m = jnp.maximum(m, s)
    m = jnp.max(m, axis=-1, keepdims=True)
    lsum, acc = None, None
    for s, (_, v, maskf) in zip(scores, tiles):
        p = jnp.exp(s - m) * maskf[None]
        pv = jnp.dot(p.reshape(nh * tq, p.shape[-1]).astype(jnp.bfloat16), v, preferred_element_type=jnp.float32)
        lsum = p if lsum is None else lsum + p
        acc = pv if acc is None else acc + pv
    l = jnp.sum(lsum, axis=-1, keepdims=True)
    return acc.reshape(nh, tq, acc.shape[-1]) / jnp.maximum(l, 1e-30)


NSA_TQ = 128
NSA_TK_SLC = 512
NSA_TK_WIN = 128


def _nsa_prompt_kernel(q_ref, g_ref, kc_ref, vc_ref, ks_ref, vs_ref, kw_ref, vw_ref, share_ref, o_ref,
                       m_ref, l_ref, acc_ref, *, n_cmp, n_blk, n_keys):
    nh, tq, d = NSA_HEADS, NSA_TQ, HEAD_DIM
    nbp = share_ref.shape[0]
    ncp = share_ref.shape[1]
    start = pl.program_id(1) * tq
    qpos_c = start + lax.broadcasted_iota(jnp.int32, (tq, 1), 0)
    qpos_r = start + lax.broadcasted_iota(jnp.int32, (1, tq), 1)

    q = q_ref[0] * (d ** -0.5)
    qh = jnp.concatenate([q[:, h * d:(h + 1) * d] for h in range(nh)], axis=0).astype(jnp.bfloat16)

    c_r = lax.broadcasted_iota(jnp.int32, (1, ncp), 1)
    mask_c = ((c_r * CMP_STRIDE + (CMP_LEN - 1) <= qpos_c) & (c_r < n_cmp)).astype(jnp.float32)
    s = lax.dot_general(qh, kc_ref[0], _NT, preferred_element_type=jnp.float32).reshape(nh, tq, ncp)
    sm = s + ((mask_c - 1.0) * (-NEG))[None]
    e = jnp.exp(sm - jnp.max(sm, axis=-1, keepdims=True)) * mask_c[None]
    p_cmp = e / jnp.maximum(jnp.sum(e, axis=-1, keepdims=True), 1e-30)
    o_cmp = jnp.dot(p_cmp.reshape(nh * tq, ncp).astype(jnp.bfloat16), vc_ref[0],
                    preferred_element_type=jnp.float32)

    sel = _select_blocks(jnp.sum(p_cmp, axis=0), share_ref[...], qpos_r, n_blk, tq)

    def slc_mask(kt):
        kpos = kt * NSA_TK_SLC + lax.broadcasted_iota(jnp.int32, (1, NSA_TK_SLC), 1)
        kblk = kt * (NSA_TK_SLC // SEL_BLOCK) + jnp.right_shift(
            lax.broadcasted_iota(jnp.int32, (nbp, NSA_TK_SLC), 1), 6)
        expand = (kblk == lax.broadcasted_iota(jnp.int32, (nbp, NSA_TK_SLC), 0)).astype(jnp.bfloat16)
        chosen = jnp.dot(sel, expand, preferred_element_type=jnp.float32)
        return chosen * (kpos <= qpos_c).astype(jnp.float32)

    hi_slc = jnp.minimum((start + tq - 1) // NSA_TK_SLC + 1, n_keys // NSA_TK_SLC)
    o_slc = _softmax_tiles(qh, ks_ref, vs_ref, 0, hi_slc, NSA_TK_SLC, slc_mask, m_ref, l_ref, acc_ref, nh, tq)

    def win_mask(kt):
        kpos = kt * NSA_TK_WIN + lax.broadcasted_iota(jnp.int32, (1, NSA_TK_WIN), 1)
        dlt = qpos_c - kpos
        return ((dlt >= 0) & (dlt < WINDOW)).astype(jnp.float32)

    lo_win = jnp.maximum(start - (WINDOW - 1), 0) // NSA_TK_WIN
    hi_win = (start + tq - 1) // NSA_TK_WIN + 1
    o_win = _softmax_tiles(qh, kw_ref, vw_ref, lo_win, hi_win, NSA_TK_WIN, win_mask, m_ref, l_ref, acc_ref, nh, tq)

    gate = 1.0 / (1.0 + jnp.exp(-g_ref[0]))
    o_cmp = o_cmp.reshape(nh, tq, d)
    for h in range(nh):
        o_ref[0, :, h * d:(h + 1) * d] = (gate[:, 3 * h:3 * h + 1] * o_cmp[h]
                                          + gate[:, 3 * h + 1:3 * h + 2] * o_slc[h]
                                          + gate[:, 3 * h + 2:3 * h + 3] * o_win[h])


def _share_matrix_t(n_cmp, n_blk, ncp, nbp):
    c0 = np.arange(n_cmp)[None, :] * CMP_STRIDE
    j0 = np.arange(n_blk)[:, None] * SEL_BLOCK
    share = np.clip(np.minimum(c0 + CMP_LEN, j0 + SEL_BLOCK) - np.maximum(c0, j0), 0, None) / CMP_LEN
    out = np.zeros((nbp, ncp), np.float32)
    out[:n_blk, :n_cmp] = share
    return jnp.asarray(out, jnp.bfloat16)


def nsa_prompt_attention(q, gates, k_cmp, v_cmp, k_slc, v_slc, k_win, v_win):
    b, t, _ = q.shape
    d = HEAD_DIM
    n_cmp = k_cmp.shape[1]
    n_blk = -(-t // SEL_BLOCK)
    ncp = -(-n_cmp // LANE) * LANE
    nbp = -(-n_blk // 16) * 16
    assert t % NSA_TK_SLC == 0 and t % NSA_TQ == 0
    bf = jnp.bfloat16
    cpad = ((0, 0), (0, ncp - n_cmp), (0, 0))
    kc, vc = jnp.pad(k_cmp, cpad).astype(bf), jnp.pad(v_cmp, cpad).astype(bf)
    whole = lambda n: pl.BlockSpec((1, n, d), lambda bi, qi: (bi, 0, 0))
    return pl.pallas_call(
        functools.partial(_nsa_prompt_kernel, n_cmp=n_cmp, n_blk=n_blk, n_keys=t),
        grid=(b, t // NSA_TQ),
        in_specs=[pl.BlockSpec((1, NSA_TQ, NSA_HEADS * d), lambda bi, qi: (bi, qi, 0)),
                  pl.BlockSpec((1, NSA_TQ, 3 * NSA_HEADS), lambda bi, qi: (bi, qi, 0)),
                  whole(ncp), whole(ncp), whole(t), whole(t), whole(t), whole(t),
                  pl.BlockSpec((nbp, ncp), lambda bi, qi: (0, 0))],
        out_specs=pl.BlockSpec((1, NSA_TQ, NSA_HEADS * d), lambda bi, qi: (bi, qi, 0)),
        out_shape=jax.ShapeDtypeStruct((b, t, NSA_HEADS * d), jnp.float32),
        scratch_shapes=[pltpu.VMEM((NSA_HEADS, NSA_TQ, 1), jnp.float32),
                        pltpu.VMEM((NSA_HEADS, NSA_TQ, 1), jnp.float32),
                        pltpu.VMEM((NSA_HEADS, NSA_TQ, d), jnp.float32)],
        compiler_params=pltpu.CompilerParams(
            dimension_semantics=("parallel", "arbitrary"), vmem_limit_bytes=VMEM_LIMIT),
        name="nsa_prompt",
    )(q, gates, kc, vc, k_slc.astype(bf), v_slc.astype(bf), k_win.astype(bf), v_win.astype(bf),
      _share_matrix_t(n_cmp, n_blk, ncp, nbp))


NSA_ROWS = 4


def _gelu_tanh(x):
    return 0.5 * x * (1.0 + jnp.tanh(math.sqrt(2.0 / math.pi) * (x + 0.044715 * (x * x * x))))


def _pad_rows(x, rows):
    return jnp.concatenate([x, jnp.zeros((rows - x.shape[0], x.shape[1]), x.dtype)], axis=0)


def _gated_store(o_ref, g_ref, branches, nh, d):
    gate = 1.0 / (1.0 + jnp.exp(-g_ref[...]))
    for h in range(nh):
        o_ref[:, h * d:(h + 1) * d] = sum(gate[:, 3 * h + i:3 * h + i + 1] * br[h] for i, br in enumerate(branches))


def _nsa_sample_kernel(pt_ref, q_ref, g_ref, new_ref, winc_ref, winn_ref, pe_ref, w1_ref, w2_ref, cos_ref, sin_ref,
                       share_ref, *rest, past, n_cmp, n_blk):
    del pt_ref
    pages, o_ref = rest[:-1], rest[-1]
    nh, d, pg = NSA_HEADS, HEAD_DIM, PAGE_SIZE
    tq = q_ref.shape[0]
    nbp, ncp = share_ref.shape
    bf = jnp.bfloat16
    seg_per_page = pg // CMP_STRIDE
    assert len(pages) * seg_per_page == ncp == LANE and CMP_LEN == 2 * CMP_STRIDE
    qpos_c = past + lax.broadcasted_iota(jnp.int32, (tq, 1), 0)
    qpos_r = past + lax.broadcasted_iota(jnp.int32, (1, LANE), 1)
    lane = lax.broadcasted_iota(jnp.int32, (1, LANE), 1)

    def compress(kv):
        first = jnp.zeros((ncp, CMP_HID), jnp.float32)
        second = jnp.zeros((ncp, CMP_HID), jnp.float32)

        def seg_rows(r):
            return jnp.concatenate([p[pl.ds(NSA_ROWS * r + kv, seg_per_page, stride=NSA_ROWS * CMP_STRIDE), :]
                                    for p in pages], axis=0)

        for r in range(0, CMP_STRIDE, 2):
            ya, yb = seg_rows(r), seg_rows(r + 1)
            for acc_off, name in ((0, 'first'), (CMP_STRIDE, 'second')):
                lhs = jnp.concatenate([ya + pe_ref[kv, acc_off + r:acc_off + r + 1, :],
                                       yb + pe_ref[kv, acc_off + r + 1:acc_off + r + 2, :]], axis=1).astype(bf)
                w = w1_ref[kv, acc_off + r:acc_off + r + 2].reshape(2 * d, CMP_HID)
                y = jnp.dot(lhs, w, preferred_element_type=jnp.float32)
                if name == 'first':
                    first = first + y
                else:
                    second = second + y
        hid = first + pltpu.roll(second, ncp - 1, 0)
        return jnp.dot(_gelu_tanh(hid).astype(bf), w2_ref[kv], preferred_element_type=jnp.float32)

    kc = compress(0)
    kc = (kc * cos_ref[...] + pltpu.roll(kc, d // 2, 1) * sin_ref[...]).astype(bf)
    vc = compress(1).astype(bf)

    q = q_ref[...] * (d ** -0.5)
    qh = jnp.concatenate([q[:, h * d:(h + 1) * d] for h in range(nh)], axis=0).astype(bf)

    mask_c = ((lane * CMP_STRIDE + (CMP_LEN - 1) <= qpos_c) & (lane < n_cmp)).astype(jnp.float32)
    s = lax.dot_general(qh, kc, _NT, preferred_element_type=jnp.float32).reshape(nh, tq, ncp)
    sm = s + ((mask_c - 1.0) * (-NEG))[None]
    e = jnp.exp(sm - jnp.max(sm, axis=-1, keepdims=True)) * mask_c[None]
    p_cmp = e / jnp.maximum(jnp.sum(e, axis=-1, keepdims=True), 1e-30)
    o_cmp = jnp.dot(p_cmp.reshape(nh * tq, ncp).astype(bf), vc, preferred_element_type=jnp.float32).reshape(nh, tq, d)

    sel = _select_blocks(_pad_rows(jnp.sum(p_cmp, axis=0), LANE), share_ref[...], qpos_r, n_blk, LANE)

    def slc_mask(key0):
        kblk = key0 // SEL_BLOCK + jnp.right_shift(lax.broadcasted_iota(jnp.int32, (nbp, LANE), 1), 6)
        expand = (kblk == lax.broadcasted_iota(jnp.int32, (nbp, LANE), 0)).astype(bf)
        chosen = jnp.dot(sel, expand, preferred_element_type=jnp.float32)[:tq]
        return chosen * (key0 + lane <= qpos_c).astype(jnp.float32)

    def component(ref, comp, n_comp, row0, rows):
        return ref[pl.ds(n_comp * row0 + comp, rows, stride=n_comp), :]

    tiles = [(component(p, 2, NSA_ROWS, 0, pg).astype(bf), component(p, 3, NSA_ROWS, 0, pg).astype(bf), slc_mask(j * pg))
             for j, p in enumerate(pages)]
    tiles.append((_pad_rows(component(new_ref, 2, NSA_ROWS, 0, tq), LANE).astype(bf),
                  _pad_rows(component(new_ref, 3, NSA_ROWS, 0, tq), LANE).astype(bf), slc_mask(past)))
    o_slc = _softmax_flat(qh, tiles, nh, tq)

    def win_mask(key0):
        dlt = qpos_c - (key0 + lane)
        return ((dlt >= 0) & (dlt < WINDOW)).astype(jnp.float32)

    n_wc = winc_ref.shape[0] // 2
    tiles = [(component(winc_ref, 0, 2, j * LANE, LANE).astype(bf), component(winc_ref, 1, 2, j * LANE, LANE).astype(bf),
              win_mask(past - n_wc + j * LANE)) for j in range(n_wc // LANE)]
    tiles.append((_pad_rows(component(winn_ref, 0, 2, 0, tq), LANE).astype(bf),
                  _pad_rows(component(winn_ref, 1, 2, 0, tq), LANE).astype(bf), win_mask(past)))
    o_win = _softmax_flat(qh, tiles, nh, tq)

    _gated_store(o_ref, g_ref, (o_cmp, o_slc, o_win), nh, d)


def _rope_tables(pos, rows):
    half = HEAD_DIM // 2
    inv = ROPE_THETA ** (-jnp.arange(half, dtype=jnp.float32) / half)
    ang = pos.astype(jnp.float32)[:, None] * inv
    cos, sin = jnp.cos(ang), jnp.sin(ang)
    pad = ((0, rows - pos.shape[0]), (0, 0))
    return (jnp.pad(jnp.concatenate([cos, cos], axis=-1), pad), jnp.pad(jnp.concatenate([-sin, sin], axis=-1), pad))


def nsa_sample_attention(layer, q, gates, nsa_new, win_new, pool, win_cache, page_table, pe, w1, w2):
    b, t, _ = q.shape
    wlen = win_cache.shape[2]
    nsa_new = nsa_new.reshape(b, t * NSA_ROWS, HEAD_DIM)
    win_new = win_new.reshape(b, t * 2, HEAD_DIM)
    pool = pool.reshape(pool.shape[0], pool.shape[1], PAGE_SIZE * NSA_ROWS, HEAD_DIM)
    win_cache = win_cache.reshape(win_cache.shape[0], b, wlen * 2, HEAD_DIM)
    d, n_pages = HEAD_DIM, page_table.shape[1]
    past = n_pages * PAGE_SIZE
    n_keys = past + t
    n_cmp = (n_keys - CMP_LEN) // CMP_STRIDE + 1
    n_blk = -(-n_keys // SEL_BLOCK)
    ncp = -(-n_cmp // LANE) * LANE
    nbp = -(-n_blk // 16) * 16
    assert (n_cmp + 1) * CMP_STRIDE <= past, "compressed tokens must come from cached rows only"
    cos, sin = _rope_tables(jnp.arange(n_cmp) * CMP_STRIDE + CMP_LEN - 1, ncp)
    bf = jnp.bfloat16
    per_seq = lambda *tail: pl.BlockSpec((None,) + tail, lambda i, pt: (i,) + (0,) * len(tail))
    const = lambda *shape: pl.BlockSpec(shape, lambda i, pt: (0,) * len(shape))
    page = lambda j: pl.BlockSpec((None, None, PAGE_SIZE * NSA_ROWS, d), lambda i, pt: (layer, pt[i, j], 0, 0))
    grid_spec = pltpu.PrefetchScalarGridSpec(
        num_scalar_prefetch=1, grid=(b,),
        in_specs=[per_seq(t, NSA_HEADS * d), per_seq(t, 3 * NSA_HEADS), per_seq(t * NSA_ROWS, d),
                  pl.BlockSpec((None, None, wlen * 2, d), lambda i, pt: (layer, i, 0, 0)),
                  per_seq(t * 2, d),
                  const(2, CMP_LEN, d), const(2, CMP_LEN, d, CMP_HID), const(2, CMP_HID, d),
                  const(ncp, d), const(ncp, d), const(nbp, ncp)] + [page(j) for j in range(n_pages)],
        out_specs=per_seq(t, NSA_HEADS * d))
    return pl.pallas_call(
        functools.partial(_nsa_sample_kernel, past=past, n_cmp=n_cmp, n_blk=n_blk),
        grid_spec=grid_spec,
        out_shape=jax.ShapeDtypeStruct((b, t, NSA_HEADS * d), jnp.float32),
        compiler_params=pltpu.CompilerParams(dimension_semantics=("parallel",), vmem_limit_bytes=VMEM_LIMIT),
        name="nsa_sample",
    )(page_table, q, gates, nsa_new, win_cache, win_new, pe, w1.reshape(2, CMP_LEN, d, CMP_HID).astype(bf),
      w2.astype(bf), cos, sin, _share_matrix_t(n_cmp, n_blk, ncp, nbp), *([pool] * n_pages))


DSA_TQ = 128
DSA_TK = 256
INT_MIN = -2 ** 31


def _dsa_prompt_kernel(q_ref, qi_ref, wt_ref, k_ref, v_ref, ki_ref, tri_ref, o_ref,
                       key_ref, sel_ref, m_ref, l_ref, acc_ref, *, n_keys, topk):
    nh, tq, d, tk = DSA_HEADS, DSA_TQ, HEAD_DIM, DSA_TK
    start = pl.program_id(1) * tq
    qpos_r = start + lax.broadcasted_iota(jnp.int32, (1, tq), 1)
    n_kt = jnp.minimum((start + tq - 1) // tk + 1, n_keys // tk)

    wt = wt_ref[0] * (IDX_HEADS ** -0.5 * IDX_DIM ** -0.5)

    def score_body(kt, carry):
        off = pl.multiple_of(kt * tk, tk)
        ki = ki_ref[0, pl.ds(off, tk), :]
        acc = jnp.zeros((tk, tq), jnp.float32)
        for h in range(IDX_HEADS):
            dots = lax.dot_general(ki, qi_ref[0, h], _NT, preferred_element_type=jnp.float32)
            acc = acc + jnp.maximum(dots, 0.0) * wt[h:h + 1, :]
        kpos_c = off + lax.broadcasted_iota(jnp.int32, (tk, 1), 0)
        score = jnp.where(kpos_c <= qpos_r, acc, NEG)
        bits = pltpu.bitcast(score, jnp.int32)
        key_ref[kt] = bits ^ (jnp.right_shift(bits, 31) & 0x7FFFFFFF)
        return carry

    lax.fori_loop(0, n_kt, score_body, 0)

    def count(pred):
        def body(kt, part):
            hit = pred(key_ref[kt]).astype(jnp.int32)
            return part + jnp.sum(hit.reshape(tk // 8, 8, tq), axis=0)
        part = lax.fori_loop(0, n_kt, body, jnp.zeros((8, tq), jnp.int32))
        return jnp.sum(part, axis=0, keepdims=True)

    thr = jnp.where(count(lambda kk: kk >= 0) >= topk, 0, INT_MIN).astype(jnp.int32)

    def bit_body(i, t):
        cand = t | jnp.left_shift(jnp.int32(1), 30 - i)
        return jnp.where(count(lambda kk: kk >= cand) >= topk, cand, t)

    thr = lax.fori_loop(0, 31, bit_body, thr)
    room = (topk - count(lambda kk: kk > thr)).astype(jnp.float32)

    def sel_body(kt, seen):
        kk = key_ref[kt]
        eq = (kk == thr).astype(jnp.float32)
        prefix = jnp.dot(tri_ref[...], eq.astype(jnp.bfloat16), preferred_element_type=jnp.float32)
        keep = (kk > thr) | ((eq > 0.0) & (prefix + seen <= room))
        kpos_c = kt * tk + lax.broadcasted_iota(jnp.int32, (tk, 1), 0)
        sel_ref[kt] = (keep & (kpos_c <= qpos_r)).astype(jnp.bfloat16)
        return seen + jnp.sum(eq, axis=0, keepdims=True)

    lax.fori_loop(0, n_kt, sel_body, jnp.zeros((1, tq), jnp.float32))

    eye = (lax.broadcasted_iota(jnp.int32, (tq, tq), 0)
           == lax.broadcasted_iota(jnp.int32, (tq, tq), 1)).astype(jnp.bfloat16)

    def sel_mask(kt):
        return lax.dot_general(eye, sel_ref[kt], _NT, preferred_element_type=jnp.float32)

    q = q_ref[0] * (d ** -0.5)
    qh = jnp.concatenate([q[:, h * d:(h + 1) * d] for h in range(nh)], axis=0).astype(jnp.bfloat16)
    o = _softmax_tiles(qh, k_ref, v_ref, 0, n_kt, tk, sel_mask, m_ref, l_ref, acc_ref, nh, tq)
    for h in range(nh):
        o_ref[0, :, h * d:(h + 1) * d] = o[h]


def dsa_prompt_attention(q, q_idx, w_idx, k, v, k_idx):
    b, t, _ = q.shape
    d = HEAD_DIM
    topk = min(DSA_TOPK, t // 4)
    assert t % DSA_TK == 0 and t % DSA_TQ == 0
    bf = jnp.bfloat16
    tri = jnp.asarray(np.tril(np.ones((DSA_TK, DSA_TK), np.float32)), bf)
    whole = lambda n, w: pl.BlockSpec((1, n, w), lambda bi, qi: (bi, 0, 0))
    return pl.pallas_call(
        functools.partial(_dsa_prompt_kernel, n_keys=t, topk=topk),
        grid=(b, t // DSA_TQ),
        in_specs=[pl.BlockSpec((1, DSA_TQ, DSA_HEADS * d), lambda bi, qi: (bi, qi, 0)),
                  pl.BlockSpec((1, IDX_HEADS, DSA_TQ, IDX_DIM), lambda bi, qi: (bi, 0, qi, 0)),
                  pl.BlockSpec((1, IDX_HEADS, DSA_TQ), lambda bi, qi: (bi, 0, qi)),
                  whole(t, d), whole(t, d), whole(t, IDX_DIM),
                  pl.BlockSpec((DSA_TK, DSA_TK), lambda bi, qi: (0, 0))],
        out_specs=pl.BlockSpec((1, DSA_TQ, DSA_HEADS * d), lambda bi, qi: (bi, qi, 0)),
        out_shape=jax.ShapeDtypeStruct((b, t, DSA_HEADS * d), jnp.float32),
        scratch_shapes=[pltpu.VMEM((t // DSA_TK, DSA_TK, DSA_TQ), jnp.int32),
                        pltpu.VMEM((t // DSA_TK, DSA_TK, DSA_TQ), bf),
                        pltpu.VMEM((DSA_HEADS, DSA_TQ, 1), jnp.float32),
                        pltpu.VMEM((DSA_HEADS, DSA_TQ, 1), jnp.float32),
                        pltpu.VMEM((DSA_HEADS, DSA_TQ, d), jnp.float32)],
        compiler_params=pltpu.CompilerParams(
            dimension_semantics=("parallel", "arbitrary"), vmem_limit_bytes=VMEM_LIMIT),
        name="dsa_prompt",
    )(q, jnp.swapaxes(q_idx, 1, 2).astype(bf), jnp.swapaxes(w_idx, 1, 2),
      k.astype(bf), v.astype(bf), k_idx.astype(bf), tri)


def _dsa_sample_kernel(pt_ref, q_ref, qi_ref, w_ref, new_ref, inew_ref, tri_ref, *rest, past, topk):
    del pt_ref
    n_pages = (len(rest) - 1) // 2
    kv_pages, idx_pages, o_ref = rest[:n_pages], rest[n_pages:2 * n_pages], rest[-1]
    nh, d, pg = DSA_HEADS, HEAD_DIM, PAGE_SIZE
    tq = q_ref.shape[0]
    bf = jnp.bfloat16
    qpos_c = past + lax.broadcasted_iota(jnp.int32, (tq, 1), 0)
    lane = lax.broadcasted_iota(jnp.int32, (1, LANE), 1)

    qi = qi_ref[...]
    wb = jnp.broadcast_to(w_ref[...] * (IDX_HEADS ** -0.5 * IDX_DIM ** -0.5), (IDX_HEADS * tq, LANE))

    def key_tile(ki, key0):
        dots = lax.dot_general(qi, ki, _NT, preferred_element_type=jnp.float32)
        score = jnp.sum((jnp.maximum(dots, 0.0) * wb).reshape(IDX_HEADS, tq, LANE), axis=0)
        score = jnp.where(key0 + lane <= qpos_c, score, NEG)
        bits = pltpu.bitcast(score, jnp.int32)
        return bits ^ (jnp.right_shift(bits, 31) & 0x7FFFFFFF)

    keys = [key_tile(p[...].astype(bf), j * pg) for j, p in enumerate(idx_pages)]
    keys.append(key_tile(_pad_rows(inew_ref[...], LANE).astype(bf), past))

    def count(pred):
        hits = pred(keys[0]).astype(jnp.int32)
        for kk in keys[1:]:
            hits = hits + pred(kk).astype(jnp.int32)
        return jnp.sum(hits, axis=-1, keepdims=True)

    thr = jnp.where(count(lambda kk: kk >= 0) >= topk, 0, INT_MIN).astype(jnp.int32)

    def bit_body(i, t):
        cand = t | jnp.left_shift(jnp.int32(1), 30 - i)
        return jnp.where(count(lambda kk: kk >= cand) >= topk, cand, t)

    thr = lax.fori_loop(0, 31, bit_body, thr)
    room = (topk - count(lambda kk: kk > thr)).astype(jnp.float32)
    eqs = [(kk == thr).astype(jnp.float32) for kk in keys]
    totals = [jnp.sum(e, axis=-1, keepdims=True) for e in eqs]
    seen = jnp.zeros((tq, 1), jnp.float32)
    masks = []
    for j, (kk, e) in enumerate(zip(keys, eqs)):
        prefix = jnp.dot(e, tri_ref[...], preferred_element_type=jnp.float32)
        keep = (kk > thr) | ((e > 0.0) & (prefix + seen <= room))
        key0 = j * pg if j < n_pages else past
        masks.append((keep & (key0 + lane <= qpos_c)).astype(jnp.float32))
        seen = seen + totals[j]

    def component(ref, comp, rows):
        return ref[pl.ds(comp, rows, stride=2), :]

    tiles = [(component(p, 0, pg).astype(bf), component(p, 1, pg).astype(bf), masks[j]) for j, p in enumerate(kv_pages)]
    tiles.append((_pad_rows(component(new_ref, 0, tq), LANE).astype(bf),
                  _pad_rows(component(new_ref, 1, tq), LANE).astype(bf), masks[n_pages]))
    q = q_ref[...] * (d ** -0.5)
    qh = jnp.concatenate([q[:, h * d:(h + 1) * d] for h in range(nh)], axis=0).astype(bf)
    o = _softmax_flat(qh, tiles, nh, tq)
    for h in range(nh):
        o_ref[:, h * d:(h + 1) * d] = o[h]


def dsa_sample_attention(layer, q, q_idx, w_idx, dsa_new, idx_new, pool, idx_pool, page_table):
    b, t, _ = q.shape
    d, n_pages = HEAD_DIM, page_table.shape[1]
    past = n_pages * PAGE_SIZE
    topk = min(DSA_TOPK, (past + t) // 4)
    pool = pool.reshape(pool.shape[0], pool.shape[1], PAGE_SIZE * 2, d)
    tri = jnp.asarray(np.triu(np.ones((LANE, LANE), np.float32)))
    per_seq = lambda *tail: pl.BlockSpec((None,) + tail, lambda i, pt: (i,) + (0,) * len(tail))
    kv_page = lambda j: pl.BlockSpec((None, None, PAGE_SIZE * 2, d), lambda i, pt: (layer, pt[i, j], 0, 0))
    idx_page = lambda j: pl.BlockSpec((None, None, PAGE_SIZE, IDX_DIM), lambda i, pt: (layer, pt[i, j], 0, 0))
    grid_spec = pltpu.PrefetchScalarGridSpec(
        num_scalar_prefetch=1, grid=(b,),
        in_specs=[per_seq(t, DSA_HEADS * d), per_seq(IDX_HEADS * t, IDX_DIM), per_seq(IDX_HEADS * t, 1),
                  per_seq(t * 2, d), per_seq(t, IDX_DIM), pl.BlockSpec((LANE, LANE), lambda i, pt: (0, 0))]
                 + [kv_page(j) for j in range(n_pages)] + [idx_page(j) for j in range(n_pages)],
        out_specs=per_seq(t, DSA_HEADS * d))
    return pl.pallas_call(
        functools.partial(_dsa_sample_kernel, past=past, topk=topk),
        grid_spec=grid_spec,
        out_shape=jax.ShapeDtypeStruct((b, t, DSA_HEADS * d), jnp.float32),
        compiler_params=pltpu.CompilerParams(dimension_semantics=("parallel",), vmem_limit_bytes=VMEM_LIMIT),
        name="dsa_sample",
    )(page_table, q, jnp.swapaxes(q_idx, 1, 2).reshape(b, IDX_HEADS * t, IDX_DIM).astype(jnp.bfloat16),
      jnp.swapaxes(w_idx, 1, 2).reshape(b, IDX_HEADS * t, 1), dsa_new.reshape(b, t * 2, d), idx_new, tri,
      *([pool] * n_pages), *([idx_pool] * n_pages))


def rms_norm(x, g):
    xf = x.astype(jnp.float32)
    y = xf * lax.rsqrt(jnp.mean(xf * xf, axis=-1, keepdims=True) + EPS)
    return (y * g.astype(jnp.float32)).astype(x.dtype)


def l2norm(x):
    xf = x.astype(jnp.float32)
    return xf * lax.rsqrt(jnp.sum(xf * xf, axis=-1, keepdims=True) + EPS)


def rope(x, pos):
    half = x.shape[-1] // 2
    inv = ROPE_THETA ** (-jnp.arange(half, dtype=jnp.float32) / half)
    ang = pos.astype(jnp.float32)[:, None] * inv
    cos, sin = jnp.cos(ang)[:, None, :], jnp.sin(ang)[:, None, :]
    xf = x.astype(jnp.float32)
    x1, x2 = xf[..., :half], xf[..., half:]
    return jnp.concatenate([x1 * cos - x2 * sin, x2 * cos + x1 * sin], axis=-1).astype(x.dtype)


def masked_softmax(s, mask):
    s = jnp.where(mask, s.astype(jnp.float32), NEG)
    e = jnp.where(mask, jnp.exp(s - jnp.max(s, axis=-1, keepdims=True)), 0.0)
    return e / jnp.maximum(jnp.sum(e, axis=-1, keepdims=True), 1e-30)


def split_cols(h):
    return jnp.split(h[..., :sum(IN_SPLITS)], [int(o) for o in np.cumsum(IN_SPLITS)[:-1]], axis=-1)


def gather_pages(pool, page_table):
    rows = pool[page_table]
    return rows.reshape(rows.shape[0], -1, *rows.shape[3:])


def over_query_blocks(fn, *qs):
    b, t = qs[0].shape[:2]
    if t <= Q_BLOCK:
        return fn(0, *qs)
    nb = t // Q_BLOCK
    blocks = tuple(q.reshape(b, nb, Q_BLOCK, *q.shape[2:]).swapaxes(0, 1) for q in qs)
    out = lax.map(lambda a: fn(a[0] * Q_BLOCK, *a[1]), (jnp.arange(nb), blocks))
    out = out.swapaxes(0, 1)
    return out.reshape(b, t, *out.shape[3:])


def nsa_compress(rows, pe, w1, w2):
    b, n_keys, d = rows.shape
    r = CMP_LEN // CMP_STRIDE
    n_cmp = (n_keys - CMP_LEN) // CMP_STRIDE + 1
    seg = rows[:, :(n_cmp + r - 1) * CMP_STRIDE].reshape(b, n_cmp + r - 1, CMP_STRIDE * d)
    w1r = w1.reshape(r, CMP_STRIDE * d, CMP_HID)
    h = pe.reshape(-1) @ w1 + sum(seg[:, i:i + n_cmp] @ w1r[i] for i in range(r))
    return jax.nn.gelu(h) @ w2


def nsa_attention(q, gates, k_cmp, v_cmp, k_slc, v_slc, k_win, v_win, past, win_pos0, banded):
    b, n_keys, d = k_slc.shape
    scale = d ** -0.5
    n_cmp = k_cmp.shape[1]
    cmp_end = jnp.arange(n_cmp) * CMP_STRIDE + CMP_LEN - 1
    n_blk = -(-n_keys // SEL_BLOCK)
    n_sel = min(SEL_COUNT, n_blk)
    pad = ((0, 0), (0, n_blk * SEL_BLOCK - n_keys), (0, 0))
    k_blk = jnp.pad(k_slc, pad).reshape(b, n_blk, SEL_BLOCK, d)
    v_blk = jnp.pad(v_slc, pad).reshape(b, n_blk, SEL_BLOCK, d)
    c0 = jnp.arange(n_cmp)[:, None] * CMP_STRIDE
    j0 = jnp.arange(n_blk)[None, :] * SEL_BLOCK
    share = jnp.clip(jnp.minimum(c0 + CMP_LEN, j0 + SEL_BLOCK) - jnp.maximum(c0, j0), 0, None).astype(jnp.float32) / CMP_LEN
    blk_id = jnp.arange(n_blk)
    if banded:
        wpad = ((0, 0), (WINDOW, 0), (0, 0))
        k_win, v_win = jnp.pad(k_win, wpad), jnp.pad(v_win, wpad)

    def block(start, qb, gb):
        nq = qb.shape[1]
        qp = past + start + jnp.arange(nq)
        s = jnp.einsum('bqhd,bcd->bqhc', qb, k_cmp) * scale
        p_cmp = masked_softmax(s, (cmp_end[None, :] <= qp[:, None])[None, :, None, :])
        o_cmp = jnp.einsum('bqhc,bcd->bqhd', p_cmp.astype(v_cmp.dtype), v_cmp)
        imp = jnp.einsum('bqhc,cj->bqj', p_cmp, share)
        forced = (blk_id[None, :] == qp[:, None] // SEL_BLOCK) | (blk_id[None, :] == 0)
        future = blk_id[None, :] * SEL_BLOCK > qp[:, None]
        imp = jnp.where(forced[None], FORCED, jnp.where(future[None], -1.0, imp))
        _, sel = lax.top_k(imp, n_sel)
        ks = jax.vmap(lambda kb, i: kb[i])(k_blk, sel).reshape(b, nq, n_sel * SEL_BLOCK, d)
        vs = jax.vmap(lambda vb, i: vb[i])(v_blk, sel).reshape(b, nq, n_sel * SEL_BLOCK, d)
        kpos = (sel[..., None] * SEL_BLOCK + jnp.arange(SEL_BLOCK)).reshape(b, nq, n_sel * SEL_BLOCK)
        s = jnp.einsum('bqhd,bqkd->bqhk', qb, ks) * scale
        p = masked_softmax(s, (kpos <= qp[None, :, None])[:, :, None, :])
        o_slc = jnp.einsum('bqhk,bqkd->bqhd', p.astype(vs.dtype), vs)
        if banded:
            n_w = WINDOW + nq
            kw = lax.dynamic_slice_in_dim(k_win, start, n_w, axis=1)
            vw = lax.dynamic_slice_in_dim(v_win, start, n_w, axis=1)
            kp = past + start - WINDOW + jnp.arange(n_w)
        else:
            kw, vw = k_win, v_win
            kp = win_pos0 + jnp.arange(k_win.shape[1])
        dlt = qp[:, None] - kp[None, :]
        s = jnp.einsum('bqhd,bkd->bqhk', qb, kw) * scale
        p = masked_softmax(s, ((dlt >= 0) & (dlt < WINDOW) & (kp[None, :] >= 0))[None, :, None, :])
        o_win = jnp.einsum('bqhk,bkd->bqhd', p.astype(vw.dtype), vw)
        g = jax.nn.sigmoid(gb.astype(jnp.float32))
        o = g[..., 0:1] * o_cmp + g[..., 1:2] * o_slc + g[..., 2:3] * o_win
        return o.astype(qb.dtype)

    return over_query_blocks(block, q, gates)


def dsa_attention(q, q_idx, w_idx, k, v, k_idx, past):
    b, n_keys, d = k.shape
    topk = min(DSA_TOPK, n_keys // 4)
    kpos = jnp.arange(n_keys)

    def block(start, qb, qib, wb):
        qp = past + start + jnp.arange(qb.shape[1])
        causal = kpos[None, :] <= qp[:, None]
        dots = jnp.einsum('bqhd,bsd->bqhs', qib, k_idx).astype(jnp.float32) * IDX_DIM ** -0.5
        score = jnp.einsum('bqh,bqhs->bqs', wb.astype(jnp.float32) * IDX_HEADS ** -0.5, jax.nn.relu(dots))
        score = jnp.where(causal[None], score, NEG)
        _, sel = lax.top_k(score, topk)
        ks = jax.vmap(lambda kk, i: kk[i])(k, sel)
        vs = jax.vmap(lambda vv, i: vv[i])(v, sel)
        s = jnp.einsum('bqhd,bqkd->bqhk', qb, ks) * d ** -0.5
        p = masked_softmax(s, (sel <= qp[None, :, None])[:, :, None, :])
        return jnp.einsum('bqhk,bqkd->bqhd', p.astype(vs.dtype), vs)

    return over_query_blocks(block, q, q_idx, w_idx)


def gated_delta_chunked(q, k, v, g, beta, s0):
    f32 = jnp.float32
    b, t, h, dk = k.shape
    dv = v.shape[-1]
    c = min(GDN_CHUNK, t)
    n = -(-t // c)
    pad = n * c - t

    def to_chunks(a):
        a = jnp.pad(a.astype(f32), [(0, 0), (0, pad)] + [(0, 0)] * (a.ndim - 2))
        a = a.reshape(b, n, c, *a.shape[2:])
        return jnp.moveaxis(a, (1, 3), (0, 2))

    qc, kc, vc, gc, bc = (to_chunks(a) for a in (q, k, v, g, beta))
    gcum = jnp.cumsum(gc, axis=-1)
    pos = jnp.arange(c)
    strict = pos[:, None] > pos[None, :]
    incl = pos[:, None] >= pos[None, :]
    diff = gcum[..., :, None] - gcum[..., None, :]
    decay = jnp.where(incl, jnp.exp(jnp.where(incl, diff, 0.0)), 0.0)
    k_beta = kc * bc[..., None]
    a_mat = jnp.where(strict, jnp.einsum('nbhid,nbhjd->nbhij', k_beta, kc) * decay, 0.0)
    rhs = jnp.concatenate([vc * bc[..., None], k_beta * jnp.exp(gcum)[..., None]], axis=-1)
    sol = lax.linalg.triangular_solve(a_mat + jnp.eye(c, dtype=f32), rhs,
                                      left_side=True, lower=True, unit_diagonal=True)
    u, w = sol[..., :dv], sol[..., dv:]
    qk = jnp.einsum('nbhid,nbhjd->nbhij', qc, kc) * decay
    q_dec = qc * jnp.exp(gcum)[..., None]
    k_dec = kc * jnp.exp(gcum[..., -1:] - gcum)[..., None]
    chunk_decay = jnp.exp(gcum[..., -1])

    def step(state, xs):
        u_i, w_i, qd_i, qk_i, kd_i, cd_i = xs
        new_v = u_i - jnp.einsum('bhck,bhkv->bhcv', w_i, state)
        o_i = jnp.einsum('bhck,bhkv->bhcv', qd_i, state) + jnp.einsum('bhij,bhjv->bhiv', qk_i, new_v)
        state = state * cd_i[..., None, None] + jnp.einsum('bhck,bhcv->bhkv', kd_i, new_v)
        return state, o_i

    s_final, o = lax.scan(step, s0.astype(f32), (u, w, q_dec, qk, k_dec, chunk_decay))
    o = jnp.moveaxis(o, (0, 2), (1, 3)).reshape(b, n * c, h, dv)[:, :t]
    return o, s_final


def gdn_mixer(qkv, a, beta_logit, z, conv_prev, s0, conv_w, a_log, dt_bias, norm_g):
    b, t, _ = qkv.shape
    xp = jnp.concatenate([conv_prev.astype(qkv.dtype), qkv], axis=1)
    conv = jax.nn.silu(sum(xp[:, i:i + t] * conv_w[i] for i in range(GDN_CONV)))
    q, k, v = (cc.reshape(b, t, GDN_HEADS, HEAD_DIM) for cc in jnp.split(conv, 3, axis=-1))
    q = l2norm(q) * HEAD_DIM ** -0.5
    k = l2norm(k)
    g = -jnp.exp(a_log.astype(jnp.float32)) * jax.nn.softplus(a.astype(jnp.float32) + dt_bias.astype(jnp.float32))
    beta = jax.nn.sigmoid(beta_logit.astype(jnp.float32))
    o, s_new = gated_delta_chunked(q, k, v, g, beta, s0)
    o = rms_norm(o, norm_g) * jax.nn.silu(z.reshape(b, t, GDN_HEADS, HEAD_DIM).astype(jnp.float32))
    return o.reshape(b, t, W_C).astype(qkv.dtype), s_new, xp[:, t:]


def mem_attention(x, gain, mem_kv, w_q, w_o):
    b, t, _ = x.shape
    q = dense(x, w_q, gain=gain).reshape(b, t, MEM_HEADS, HEAD_DIM)
    s = jnp.einsum('bqhd,bmhd->bhqm', q, mem_kv[:, :, 0]) * HEAD_DIM ** -0.5
    p = jax.nn.softmax(s.astype(jnp.float32), axis=-1).astype(x.dtype)
    o = jnp.einsum('bhqm,bmhd->bqhd', p, mem_kv[:, :, 1])
    return dense(o.reshape(b, t, MEM_HEADS * HEAD_DIM), w_o, resid=x)


def run_group(x, mem, cache, p):
    prompt = cache is None
    b, t, _ = x.shape
    past = 0 if prompt else cache['page_table'].shape[1] * cache['nsa_kv'].shape[2]
    qpos = past + jnp.arange(t)
    names = ('nsa_kv', 'dsa_kv', 'idx_k', 'win_kv', 'gdn', 'conv') + (('mem_kv',) if prompt else ())
    out = {nm: [] for nm in names}
    for l in range(DEPTH):
        (q_a, kv_a, g_a, q_b, kv_b, q_i, w_i, k_i, qkv_c, a_c, b_c, z_c, merge) = split_cols(
            dense(x, p['w_in'][l], gain=p['norm_mix_g'][l]))
        q_a = rope(q_a.reshape(b, t, NSA_HEADS, HEAD_DIM), qpos)
        kv_a = kv_a.reshape(b, t, 6, HEAD_DIM)
        rot_a = rope(kv_a[:, :, 2::2], qpos)
        nsa_new = jnp.stack([kv_a[:, :, 0], kv_a[:, :, 1], rot_a[:, :, 0], kv_a[:, :, 3]], axis=2)
        win_new = jnp.stack([rot_a[:, :, 1], kv_a[:, :, 5]], axis=2)
        q_b = rope(q_b.reshape(b, t, DSA_HEADS, HEAD_DIM), qpos)
        kv_b = kv_b.reshape(b, t, 2, HEAD_DIM)
        dsa_new = jnp.stack([rope(kv_b[:, :, :1], qpos)[:, :, 0], kv_b[:, :, 1]], axis=2)
        q_i = rope(q_i.reshape(b, t, IDX_HEADS, IDX_DIM), qpos)
        idx_new = rope(k_i[:, :, None], qpos)[:, :, 0]
        if prompt:
            conv_prev = jnp.zeros((b, GDN_CONV - 1, 3 * W_C), x.dtype)
            s0 = jnp.zeros((b, GDN_HEADS, HEAD_DIM, HEAD_DIM), jnp.float32)
            mem_kv = dense(mem, p['w_mem_kv'][l], gain=p['mem_norm_g'][l]).reshape(b, -1, 2, MEM_HEADS, HEAD_DIM)
            k_cmp = nsa_compress(nsa_new[:, :, 0], p['nsa_cmp_pe'][l, 0], p['nsa_cmp_w1'][l, 0], p['nsa_cmp_w2'][l, 0])
            v_cmp = nsa_compress(nsa_new[:, :, 1], p['nsa_cmp_pe'][l, 1], p['nsa_cmp_w1'][l, 1], p['nsa_cmp_w2'][l, 1])
            cmp_end = jnp.arange(k_cmp.shape[1]) * CMP_STRIDE + CMP_LEN - 1
            k_cmp = rope(k_cmp[:, :, None], cmp_end)[:, :, 0]
            o_a = nsa_prompt_attention(q_a.reshape(b, t, W_A), g_a, k_cmp, v_cmp, nsa_new[:, :, 2], nsa_new[:, :, 3],
                                       win_new[:, :, 0], win_new[:, :, 1])
            o_b = dsa_prompt_attention(q_b.reshape(b, t, W_B), q_i, w_i, dsa_new[:, :, 0], dsa_new[:, :, 1], idx_new)
            win_out = win_new[:, -min(WINDOW, t):]
        else:
            pt = cache['page_table']
            conv_prev, s0 = cache['conv'][l], cache['gdn'][l]
            mem_kv = cache['mem_kv'][l]
            o_a = nsa_sample_attention(l, q_a.reshape(b, t, W_A), g_a, nsa_new, win_new, cache['nsa_kv'], cache['win_kv'],
                                       pt, p['nsa_cmp_pe'][l], p['nsa_cmp_w1'][l], p['nsa_cmp_w2'][l])
            o_b = dsa_sample_attention(l, q_b.reshape(b, t, W_B), q_i, w_i, dsa_new, idx_new, cache['dsa_kv'],
                                       cache['idx_k'], pt)
            win_out = jnp.concatenate([cache['win_kv'][l], win_new], axis=1)[:, -cache['win_kv'].shape[2]:]
        o_c, s_new, conv_new = gdn_mixer(qkv_c, a_c, b_c, z_c, conv_prev, s0, p['gdn_conv_w'][l],
                                         p['gdn_a_log'][l], p['gdn_dt_bias'][l], p['gdn_norm_g'][l])
        mix = branch_mix(o_a.reshape(b * t, W_A), o_b.reshape(b * t, W_B), o_c.reshape(b * t, W_C),
                         merge.reshape(b * t, 3 * D_MODEL), p['w_branch_a'][l], p['w_branch_b'][l], p['w_branch_c'][l])
        x = dense(mix, p['w_mix_out'][l], resid=x).reshape(b, t, D_MODEL)
        x = mem_attention(x, p['norm_mem_g'][l], mem_kv, p['w_mem_q'][l], p['w_mem_o'][l])
        x = dense(ffn_act(x, p['norm_ffn_g'][l], p['w_ffn_gate'][l], p['w_ffn_up'][l]), p['w_ffn_down'][l], resid=x)
        out['nsa_kv'].append(nsa_new)
        out['dsa_kv'].append(dsa_new)
        out['idx_k'].append(idx_new)
        out['win_kv'].append(win_out)
        out['gdn'].append(s_new)
        out['conv'].append(conv_new)
        if prompt:
            out['mem_kv'].append(mem_kv)
    y = rms_norm(x, p['norm_final_g'])
    return y, {nm: jnp.stack(v) for nm, v in out.items()}


def kernel(x_prompt, x_sample, mem_prompt, cache_nsa_kv, cache_dsa_kv, cache_dsa_idx_k, cache_win_kv, cache_mem_kv, state_gdn, state_conv, page_table, norm_mix_g, w_in, nsa_cmp_pe, nsa_cmp_w1, nsa_cmp_w2, gdn_conv_w, gdn_a_log, gdn_dt_bias, gdn_norm_g, w_branch_a, w_branch_b, w_branch_c, w_mix_out, norm_mem_g, mem_norm_g, w_mem_q, w_mem_kv, w_mem_o, norm_ffn_g, w_ffn_gate, w_ffn_up, w_ffn_down, norm_final_g):
    p = dict(norm_mix_g=norm_mix_g, w_in=w_in, nsa_cmp_pe=nsa_cmp_pe, nsa_cmp_w1=nsa_cmp_w1,
             nsa_cmp_w2=nsa_cmp_w2, gdn_conv_w=gdn_conv_w, gdn_a_log=gdn_a_log, gdn_dt_bias=gdn_dt_bias,
             gdn_norm_g=gdn_norm_g, w_branch_a=w_branch_a, w_branch_b=w_branch_b, w_branch_c=w_branch_c,
             w_mix_out=w_mix_out, norm_mem_g=norm_mem_g, mem_norm_g=mem_norm_g, w_mem_q=w_mem_q,
             w_mem_kv=w_mem_kv, w_mem_o=w_mem_o, norm_ffn_g=norm_ffn_g, w_ffn_gate=w_ffn_gate,
             w_ffn_up=w_ffn_up, w_ffn_down=w_ffn_down, norm_final_g=norm_final_g)
    for nm in ('w_branch_a', 'w_branch_b', 'w_branch_c', 'w_mix_out', 'w_mem_q', 'w_mem_kv', 'w_mem_o',
               'w_ffn_gate', 'w_ffn_up', 'w_ffn_down'):
        p[nm] = p[nm].astype(jnp.bfloat16)
    in_pad = -sum(IN_SPLITS) % LANE
    p['w_in'] = jnp.pad(w_in, ((0, 0), (0, 0), (0, in_pad))).astype(jnp.bfloat16)
    cache = dict(nsa_kv=cache_nsa_kv, dsa_kv=cache_dsa_kv, idx_k=cache_dsa_idx_k, win_kv=cache_win_kv,
                 mem_kv=cache_mem_kv, gdn=state_gdn, conv=state_conv, page_table=page_table)
    y_prompt, sp = run_group(x_prompt, mem_prompt, None, p)
    y_sample, ss = run_group(x_sample, None, cache, p)
    return (y_prompt, y_sample,
            sp['nsa_kv'], sp['dsa_kv'], sp['idx_k'], sp['win_kv'], sp['gdn'], sp['conv'], sp['mem_kv'],
            ss['nsa_kv'], ss['dsa_kv'], ss['idx_k'], ss['win_kv'], ss['gdn'], ss['conv'])
```

```python
import functools
import math

import jax
import jax.numpy as jnp
import numpy as np
from jax import lax
from jax.experimental import pallas as pl
from jax.experimental.pallas import tpu as pltpu

D_MODEL = 2048
DEPTH = 2
PAGE_SIZE = 128
HEAD_DIM = 128
NSA_HEADS = D_MODEL // (4 * HEAD_DIM)
DSA_HEADS = D_MODEL // (4 * HEAD_DIM)
GDN_HEADS = D_MODEL // (2 * HEAD_DIM)
W_A = NSA_HEADS * HEAD_DIM
W_B = DSA_HEADS * HEAD_DIM
W_C = GDN_HEADS * HEAD_DIM
CMP_LEN = 32
CMP_STRIDE = 16
CMP_HID = 2 * HEAD_DIM
SEL_BLOCK = 64
SEL_COUNT = 16
WINDOW = 512
IDX_HEADS = 16
IDX_DIM = 64
DSA_TOPK = 256
GDN_CONV = 4
GDN_CHUNK = 64
MEM_HEADS = 4
ROPE_THETA = 10000.0
Q_BLOCK = 128
EPS = 1e-6
NEG = -1e30
FORCED = 1e9
IN_SPLITS = (W_A, 6 * HEAD_DIM, 3 * NSA_HEADS,
             W_B, 2 * HEAD_DIM, IDX_HEADS * IDX_DIM, IDX_HEADS, IDX_DIM,
             3 * W_C, GDN_HEADS, GDN_HEADS, W_C,
             3 * D_MODEL)

LANE = 128
VMEM_LIMIT = 48 * 1024 * 1024


DENSE_VMEM_BUDGET = 40 * 1024 * 1024
NORM_ROWS = 128


def _divisor_tiles(n, cap):
    return [t for t in range(min(cap, n), 0, -LANE) if t % LANE == 0 and n % t == 0]


def _dense_tiles(m, n, per_row_bytes, per_col_bytes, per_out_bytes):
    for tm in (1024, 512, 256, 128):
        if m % tm:
            continue
        for tn in _divisor_tiles(n, 1024):
            if tm * per_row_bytes + tn * per_col_bytes + tm * tn * per_out_bytes <= DENSE_VMEM_BUDGET:
                return tm, tn
    raise ValueError("no dense tile fits VMEM")


def _stage_rows(x_ref, g_ref, xn_ref):
    def body(r, carry):
        rows = pl.ds(pl.multiple_of(r * NORM_ROWS, NORM_ROWS), NORM_ROWS)
        xf = x_ref[rows, :].astype(jnp.float32)
        if g_ref is not None:
            xf = xf * lax.rsqrt(jnp.mean(xf * xf, axis=-1, keepdims=True) + EPS) * g_ref[...]
        xn_ref[rows, :] = xf.astype(jnp.bfloat16)
        return carry
    lax.fori_loop(0, x_ref.shape[0] // NORM_ROWS, body, 0)


def _dense_kernel(*refs, has_gain, has_resid):
    it = iter(refs)
    x_ref = next(it)
    g_ref = next(it) if has_gain else None
    w_ref = next(it)
    r_ref = next(it) if has_resid else None
    o_ref = next(it)
    xn_ref = next(it)

    @pl.when(pl.program_id(1) == 0)
    def _():
        _stage_rows(x_ref, g_ref, xn_ref)

    acc = jnp.dot(xn_ref[...], w_ref[...], preferred_element_type=jnp.float32)
    if has_resid:
        acc = acc + r_ref[...]
    o_ref[...] = acc.astype(o_ref.dtype)


def dense(x, w, gain=None, resid=None, out_dtype=jnp.float32):
    lead, kdim, n = x.shape[:-1], x.shape[-1], w.shape[-1]
    x2 = x.reshape(-1, kdim)
    m = x2.shape[0]
    ob = jnp.dtype(out_dtype).itemsize
    tm, tn = _dense_tiles(m, n, per_row_bytes=kdim * (2 * x2.dtype.itemsize + 2), per_col_bytes=kdim * 2 * 2,
                          per_out_bytes=2 * ob + (8 if resid is not None else 0))
    args, specs = [x2], [pl.BlockSpec((tm, kdim), lambda i, j: (i, 0))]
    if gain is not None:
        args.append(gain.reshape(1, kdim).astype(jnp.float32))
        specs.append(pl.BlockSpec((1, kdim), lambda i, j: (0, 0)))
    args.append(w)
    specs.append(pl.BlockSpec((kdim, tn), lambda i, j: (0, j)))
    if resid is not None:
        args.append(resid.reshape(m, n))
        specs.append(pl.BlockSpec((tm, tn), lambda i, j: (i, j)))
    out = pl.pallas_call(
        functools.partial(_dense_kernel, has_gain=gain is not None, has_resid=resid is not None),
        grid=(m // tm, n // tn),
        in_specs=specs,
        out_specs=pl.BlockSpec((tm, tn), lambda i, j: (i, j)),
        out_shape=jax.ShapeDtypeStruct((m, n), out_dtype),
        scratch_shapes=[pltpu.VMEM((tm, kdim), jnp.bfloat16)],
        compiler_params=pltpu.CompilerParams(
            dimension_semantics=("parallel", "arbitrary"), vmem_limit_bytes=VMEM_LIMIT),
        name="dense",
    )(*args)
    return out.reshape(*lead, n)


def _ffn_act_kernel(x_ref, g_ref, wg_ref, wu_ref, o_ref, xn_ref):
    @pl.when(pl.program_id(1) == 0)
    def _():
        _stage_rows(x_ref, g_ref, xn_ref)

    xn = xn_ref[...]
    a = jnp.dot(xn, wg_ref[...], preferred_element_type=jnp.float32)
    u = jnp.dot(xn, wu_ref[...], preferred_element_type=jnp.float32)
    o_ref[...] = (a / (1.0 + jnp.exp(-a)) * u).astype(o_ref.dtype)


def ffn_act(x, gain, w_gate, w_up):
    lead, kdim, n = x.shape[:-1], x.shape[-1], w_gate.shape[-1]
    x2 = x.reshape(-1, kdim)
    m = x2.shape[0]
    tm, tn = _dense_tiles(m, n, per_row_bytes=kdim * (2 * 4 + 2), per_col_bytes=2 * kdim * 2 * 2, per_out_bytes=2 * 2)
    out = pl.pallas_call(
        _ffn_act_kernel,
        grid=(m // tm, n // tn),
        in_specs=[pl.BlockSpec((tm, kdim), lambda i, j: (i, 0)),
                  pl.BlockSpec((1, kdim), lambda i, j: (0, 0)),
                  pl.BlockSpec((kdim, tn), lambda i, j: (0, j)),
                  pl.BlockSpec((kdim, tn), lambda i, j: (0, j))],
        out_specs=pl.BlockSpec((tm, tn), lambda i, j: (i, j)),
        out_shape=jax.ShapeDtypeStruct((m, n), jnp.bfloat16),
        scratch_shapes=[pltpu.VMEM((tm, kdim), jnp.bfloat16)],
        compiler_params=pltpu.CompilerParams(
            dimension_semantics=("parallel", "arbitrary"), vmem_limit_bytes=VMEM_LIMIT),
        name="ffn_act",
    )(x2, gain.reshape(1, kdim).astype(jnp.float32), w_gate, w_up)
    return out.reshape(*lead, n)


def _branch_mix_kernel(oa_ref, ob_ref, oc_ref, ma_ref, mb_ref, mc_ref, wa_ref, wb_ref, wc_ref, o_ref, xn_ref):
    @pl.when(pl.program_id(1) == 0)
    def _():
        off = 0
        for src in (oa_ref, ob_ref, oc_ref):
            xn_ref[:, off:off + src.shape[1]] = src[...].astype(jnp.bfloat16)
            off += src.shape[1]

    def gated(m_ref, w_ref, lo, hi):
        y = jnp.dot(xn_ref[:, lo:hi], w_ref[...], preferred_element_type=jnp.float32)
        return y / (1.0 + jnp.exp(-m_ref[...]))

    o_ref[...] = (gated(ma_ref, wa_ref, 0, W_A) + gated(mb_ref, wb_ref, W_A, W_A + W_B)
                  + gated(mc_ref, wc_ref, W_A + W_B, W_A + W_B + W_C)).astype(o_ref.dtype)


def branch_mix(o_a, o_b, o_c, merge, merge_off, w_a, w_b, w_c):
    m, n = merge.shape[0], w_a.shape[-1]
    ktot = W_A + W_B + W_C
    tm, tn = _dense_tiles(m, n, per_row_bytes=ktot * (2 * 4 + 2), per_col_bytes=ktot * 2 * 2, per_out_bytes=2 * 2 + 3 * 8)
    nb = n // tn
    assert merge_off % tn == 0
    mb = merge_off // tn
    row = lambda width: pl.BlockSpec((tm, width), lambda i, j: (i, 0))
    return pl.pallas_call(
        _branch_mix_kernel,
        grid=(m // tm, nb),
        in_specs=[row(W_A), row(W_B), row(W_C),
                  pl.BlockSpec((tm, tn), lambda i, j: (i, mb + j)),
                  pl.BlockSpec((tm, tn), lambda i, j: (i, mb + j + nb)),
                  pl.BlockSpec((tm, tn), lambda i, j: (i, mb + j + 2 * nb)),
                  pl.BlockSpec((W_A, tn), lambda i, j: (0, j)),
                  pl.BlockSpec((W_B, tn), lambda i, j: (0, j)),
                  pl.BlockSpec((W_C, tn), lambda i, j: (0, j))],
        out_specs=pl.BlockSpec((tm, tn), lambda i, j: (i, j)),
        out_shape=jax.ShapeDtypeStruct((m, n), jnp.bfloat16),
        scratch_shapes=[pltpu.VMEM((tm, ktot), jnp.bfloat16)],
        compiler_params=pltpu.CompilerParams(
            dimension_semantics=("parallel", "arbitrary"), vmem_limit_bytes=VMEM_LIMIT),
        name="branch_mix",
    )(o_a, o_b, o_c, merge, merge, merge, w_a, w_b, w_c)


REF_COLS = tuple(zip(('q_a', 'kv_a', 'g_a', 'q_b', 'kv_b', 'q_i', 'w_i', 'k_i', 'qkv_c', 'a_c', 'b_c', 'z_c', 'merge'),
                     IN_SPLITS))
SMALL_COLS = ('k_i', 'g_a', 'w_i', 'a_c', 'b_c')
WIDE_COLS = ('q_a', 'kv_a', 'q_b', 'kv_b', 'q_i', 'qkv_c', 'z_c', 'merge')


def _layout():
    width = dict(REF_COLS)
    assert all(width[n] % LANE == 0 for n in WIDE_COLS) and sum(width[n] for n in SMALL_COLS) <= LANE
    off, pos = {}, 0
    for n in WIDE_COLS:
        off[n] = pos
        pos += width[n]
    off['small'] = pos
    small, spos = {}, 0
    for n in SMALL_COLS:
        small[n] = (spos, width[n])
        spos += width[n]
    return off, small, pos + LANE


PROJ_OFF, SMALL_OFF, PROJ_WIDTH = _layout()
ROPE_COLS = PROJ_OFF['qkv_c']
ROPE_TM = 256


def permute_w_in(w):
    ref_off, pos = {}, 0
    for n, wd in REF_COLS:
        ref_off[n] = (pos, wd)
        pos += wd
    take = lambda n: w[..., ref_off[n][0]:ref_off[n][0] + ref_off[n][1]]
    used = sum(ref_off[n][1] for n in SMALL_COLS)
    pad = jnp.zeros(w.shape[:-1] + (LANE - used,), w.dtype)
    return jnp.concatenate([take(n) for n in WIDE_COLS] + [take(n) for n in SMALL_COLS] + [pad], axis=-1).astype(jnp.bfloat16)


def small_col(hproj, name):
    lo, wd = SMALL_OFF[name]
    return hproj[..., PROJ_OFF['small'] + lo:PROJ_OFF['small'] + lo + wd]


def _rope_tables(pos, rows, head_dim=HEAD_DIM):
    half = head_dim // 2
    inv = ROPE_THETA ** (-jnp.arange(half, dtype=jnp.float32) / half)
    ang = pos.astype(jnp.float32)[:, None] * inv
    cos, sin = jnp.cos(ang), jnp.sin(ang)
    reps = LANE // head_dim
    pad = ((0, rows - pos.shape[0]), (0, 0))
    return (jnp.pad(jnp.tile(jnp.concatenate([cos, cos], axis=-1), (1, reps)), pad),
            jnp.pad(jnp.tile(jnp.concatenate([-sin, sin], axis=-1), (1, reps)), pad))


def _rope_split_kernel(h_ref, small_ref, c128_ref, s128_ref, c64_ref, s64_ref,
                       qa_ref, qb_ref, qi_ref, nsa_ref, win_ref, dsa_ref, idx_ref,
                       kslc_ref, vslc_ref, kwin_ref, vwin_ref, kdsa_ref, vdsa_ref, kidx_ref):
    d, bf = HEAD_DIM, jnp.bfloat16
    c128, s128, c64, s64 = c128_ref[...], s128_ref[...], c64_ref[...], s64_ref[...]
    first_half = (lax.broadcasted_iota(jnp.int32, (1, LANE), 1) & (IDX_DIM - 1)) < IDX_DIM // 2

    def rope128(x):
        return x * c128 + pltpu.roll(x, d // 2, 1) * s128

    def rope64(x):
        rot = jnp.where(first_half, pltpu.roll(x, LANE - IDX_DIM // 2, 1), pltpu.roll(x, IDX_DIM // 2, 1))
        return x * c64 + rot * s64

    col = lambda name, i: h_ref[:, PROJ_OFF[name] + i * d:PROJ_OFF[name] + (i + 1) * d]
    for i in range(NSA_HEADS):
        qa_ref[:, i * d:(i + 1) * d] = (rope128(col('q_a', i)) * (d ** -0.5)).astype(bf)
    for i in range(DSA_HEADS):
        qb_ref[:, i * d:(i + 1) * d] = (rope128(col('q_b', i)) * (d ** -0.5)).astype(bf)
    k_slc, v_slc, k_win, v_win = rope128(col('kv_a', 2)), col('kv_a', 3), rope128(col('kv_a', 4)), col('kv_a', 5)
    for i, part in enumerate((col('kv_a', 0), col('kv_a', 1), k_slc, v_slc)):
        nsa_ref[:, i * d:(i + 1) * d] = part
    win_ref[:, 0:d], win_ref[:, d:2 * d] = k_win, v_win
    kslc_ref[...], vslc_ref[...], kwin_ref[...], vwin_ref[...] = (k_slc.astype(bf), v_slc.astype(bf),
                                                                   k_win.astype(bf), v_win.astype(bf))
    k_dsa, v_dsa = rope128(col('kv_b', 0)), col('kv_b', 1)
    dsa_ref[:, 0:d], dsa_ref[:, d:2 * d] = k_dsa, v_dsa
    kdsa_ref[...], vdsa_ref[...] = k_dsa.astype(bf), v_dsa.astype(bf)
    for i in range(IDX_HEADS * IDX_DIM // LANE):
        pair = rope64(col('q_i', i)).astype(bf)
        qi_ref[2 * i] = pair[:, :IDX_DIM]
        qi_ref[2 * i + 1] = pair[:, IDX_DIM:]
    assert SMALL_OFF['k_i'][0] == 0
    k_idx = rope64(small_ref[...])[:, :IDX_DIM]
    idx_ref[...] = k_idx
    kidx_ref[...] = k_idx.astype(bf)


def rope_split(hproj, pos):
    bg, tg, _ = hproj.shape
    d, bf, f32 = HEAD_DIM, jnp.bfloat16, jnp.float32
    tm = min(ROPE_TM, tg)
    assert tg % tm == 0
    c128, s128 = _rope_tables(pos, tg, d)
    c64, s64 = _rope_tables(pos, tg, IDX_DIM)
    rows = lambda w: pl.BlockSpec((None, tm, w), lambda b, i: (b, i, 0))
    table = pl.BlockSpec((tm, LANE), lambda b, i: (i, 0))
    shape = lambda w, dt: jax.ShapeDtypeStruct((bg, tg, w), dt)
    outs = [(W_A, bf), (W_B, bf), None, (NSA_ROWS * d, f32), (2 * d, f32), (2 * d, f32), (IDX_DIM, f32),
            (d, bf), (d, bf), (d, bf), (d, bf), (d, bf), (d, bf), (IDX_DIM, bf)]
    out_specs = [pl.BlockSpec((None, IDX_HEADS, tm, IDX_DIM), lambda b, i: (b, 0, i, 0)) if o is None else rows(o[0])
                 for o in outs]
    out_shape = [jax.ShapeDtypeStruct((bg, IDX_HEADS, tg, IDX_DIM), bf) if o is None else shape(*o) for o in outs]
    return pl.pallas_call(
        _rope_split_kernel,
        grid=(bg, tg // tm),
        in_specs=[pl.BlockSpec((None, tm, ROPE_COLS), lambda b, i: (b, i, 0)),
                  pl.BlockSpec((None, tm, LANE), lambda b, i: (b, i, PROJ_OFF['small'] // LANE)),
                  table, table, table, table],
        out_specs=out_specs, out_shape=out_shape,
        compiler_params=pltpu.CompilerParams(
            dimension_semantics=("parallel", "parallel"), vmem_limit_bytes=VMEM_LIMIT),
        name="rope_split",
    )(hproj, hproj, c128, s128, c64, s64)


_NT = (((1,), (1,)), ((), ()))


def _softmax_tiles(qh, k_ref, v_ref, lo, hi, tk, mask_fn, m_ref, l_ref, acc_ref, nh, tq):
    m_ref[...] = jnp.full(m_ref.shape, NEG, jnp.float32)
    l_ref[...] = jnp.zeros(l_ref.shape, jnp.float32)
    acc_ref[...] = jnp.zeros(acc_ref.shape, jnp.float32)

    def body(kt, carry):
        off = pl.multiple_of(kt * tk, tk)
        k = k_ref[0, pl.ds(off, tk), :]
        v = v_ref[0, pl.ds(off, tk), :]
        s = lax.dot_general(qh, k, _NT, preferred_element_type=jnp.float32).reshape(nh, tq, tk)
        maskf = mask_fn(kt)
        bias = (maskf - 1.0) * (-NEG)
        sm = s + bias[None]
        m_old = m_ref[...]
        m_new = jnp.maximum(m_old, jnp.max(sm, axis=-1, keepdims=True))
        p = jnp.exp(sm - m_new) * maskf[None]
        alpha = jnp.exp(m_old - m_new)
        l_ref[...] = alpha * l_ref[...] + jnp.sum(p, axis=-1, keepdims=True)
        pv = jnp.dot(p.reshape(nh * tq, tk).astype(jnp.bfloat16), v, preferred_element_type=jnp.float32)
        acc_ref[...] = alpha * acc_ref[...] + pv.reshape(nh, tq, pv.shape[-1])
        m_ref[...] = m_new
        return carry

    lax.fori_loop(lo, hi, body, 0)
    return acc_ref[...] / jnp.maximum(l_ref[...], 1e-30)


def _split3_bf16(x):
    hi = x.astype(jnp.bfloat16)
    r1 = x - hi.astype(jnp.float32)
    mid = r1.astype(jnp.bfloat16)
    lo = (r1 - mid.astype(jnp.float32)).astype(jnp.bfloat16)
    return hi, mid, lo


def _select_blocks(psum, share, qpos_r, n_blk, tq):
    nbp, tqp = share.shape[0], psum.shape[0]
    imp_t = sum(lax.dot_general(share, part, _NT, preferred_element_type=jnp.float32)
                for part in _split3_bf16(psum))
    blk = lax.broadcasted_iota(jnp.int32, (nbp, tqp), 0)
    forced = (blk == jnp.right_shift(qpos_r, 6)) | (blk == 0)
    future = blk * SEL_BLOCK > qpos_r
    imp_t = jnp.where(forced, FORCED, jnp.where(future, -1.0, imp_t))
    imp_t = jnp.where(blk < n_blk, imp_t, -2.0)
    rank = jnp.zeros((nbp, tqp), jnp.float32)
    for i in range(n_blk):
        row = imp_t[i:i + 1, :]
        beats = (row > imp_t) | ((row == imp_t) & (blk > i))
        rank = rank + beats.astype(jnp.float32)
    sel_t = (rank < float(min(SEL_COUNT, n_blk))).astype(jnp.bfloat16)
    eye = (lax.broadcasted_iota(jnp.int32, (tq, tqp), 0) == lax.broadcasted_iota(jnp.int32, (tq, tqp), 1))
    return lax.dot_general(eye.astype(jnp.bfloat16), sel_t, _NT,
                           preferred_element_type=jnp.float32).astype(jnp.bfloat16)


def _softmax_flat(qh, tiles, nh, tq):
    scores = []
    for k, _, maskf in tiles:
        s = lax.dot_general(qh, k, _NT, preferred_element_type=jnp.float32).reshape(nh, tq, k.shape[0])
        scores.append(s + ((maskf - 1.0) * (-NEG))[None])
    m = scores[0]
    for s in scores[1:]:
        m = jnp.maximum(m, s)
    m = jnp.max(m, axis=-1, keepdims=True)
    lsum, acc = None, None
    for s, (_, v, maskf) in zip(scores, tiles):
        p = jnp.exp(s - m) * maskf[None]
        pv = jnp.dot(p.reshape(nh * tq, p.shape[-1]).astype(jnp.bfloat16), v, preferred_element_type=jnp.float32)
        lsum = p if lsum is None else lsum + p
        acc = pv if acc is None else acc + pv
    l = jnp.sum(lsum, axis=-1, keepdims=True)
    return acc.reshape(nh, tq, acc.shape[-1]) / jnp.maximum(l, 1e-30)


NSA_TQ = 128
NSA_TK_SLC = 512
NSA_TK_WIN = 128


def _nsa_prompt_kernel(q_ref, g_ref, kc_ref, vc_ref, ks_ref, vs_ref, kw_ref, vw_ref, share_ref, o_ref,
                       m_ref, l_ref, acc_ref, *, n_cmp, n_blk, n_keys):
    nh, tq, d = NSA_HEADS, NSA_TQ, HEAD_DIM
    nbp = share_ref.shape[0]
    ncp = share_ref.shape[1]
    start = pl.program_id(1) * tq
    qpos_c = start + lax.broadcasted_iota(jnp.int32, (tq, 1), 0)
    qpos_r = start + lax.broadcasted_iota(jnp.int32, (1, tq), 1)

    qh = jnp.concatenate([q_ref[0, :, h * d:(h + 1) * d] for h in range(nh)], axis=0)

    c_r = lax.broadcasted_iota(jnp.int32, (1, ncp), 1)
    mask_c = ((c_r * CMP_STRIDE + (CMP_LEN - 1) <= qpos_c) & (c_r < n_cmp)).astype(jnp.float32)
    s = lax.dot_general(qh, kc_ref[0], _NT, preferred_element_type=jnp.float32).reshape(nh, tq, ncp)
    sm = s + ((mask_c - 1.0) * (-NEG))[None]
    e = jnp.exp(sm - jnp.max(sm, axis=-1, keepdims=True)) * mask_c[None]
    p_cmp = e / jnp.maximum(jnp.sum(e, axis=-1, keepdims=True), 1e-30)
    o_cmp = jnp.dot(p_cmp.reshape(nh * tq, ncp).astype(jnp.bfloat16), vc_ref[0],
                    preferred_element_type=jnp.float32)

    sel = _select_blocks(jnp.sum(p_cmp, axis=0), share_ref[...], qpos_r, n_blk, tq)

    def slc_mask(kt):
        kpos = kt * NSA_TK_SLC + lax.broadcasted_iota(jnp.int32, (1, NSA_TK_SLC), 1)
        kblk = kt * (NSA_TK_SLC // SEL_BLOCK) + jnp.right_shift(
            lax.broadcasted_iota(jnp.int32, (nbp, NSA_TK_SLC), 1), 6)
        expand = (kblk == lax.broadcasted_iota(jnp.int32, (nbp, NSA_TK_SLC), 0)).astype(jnp.bfloat16)
        chosen = jnp.dot(sel, expand, preferred_element_type=jnp.float32)
        return chosen * (kpos <= qpos_c).astype(jnp.float32)

    hi_slc = jnp.minimum((start + tq - 1) // NSA_TK_SLC + 1, n_keys // NSA_TK_SLC)
    o_slc = _softmax_tiles(qh, ks_ref, vs_ref, 0, hi_slc, NSA_TK_SLC, slc_mask, m_ref, l_ref, acc_ref, nh, tq)

    def win_mask(kt):
        kpos = kt * NSA_TK_WIN + lax.broadcasted_iota(jnp.int32, (1, NSA_TK_WIN), 1)
        dlt = qpos_c - kpos
        return ((dlt >= 0) & (dlt < WINDOW)).astype(jnp.float32)

    lo_win = jnp.maximum(start - (WINDOW - 1), 0) // NSA_TK_WIN
    hi_win = (start + tq - 1) // NSA_TK_WIN + 1
    o_win = _softmax_tiles(qh, kw_ref, vw_ref, lo_win, hi_win, NSA_TK_WIN, win_mask, m_ref, l_ref, acc_ref, nh, tq)

    gate = 1.0 / (1.0 + jnp.exp(-g_ref[0]))
    o_cmp = o_cmp.reshape(nh, tq, d)
    for h in range(nh):
        o_ref[0, :, h * d:(h + 1) * d] = (gate[:, 3 * h:3 * h + 1] * o_cmp[h]
                                          + gate[:, 3 * h + 1:3 * h + 2] * o_slc[h]
                                          + gate[:, 3 * h + 2:3 * h + 3] * o_win[h])


def _share_matrix_t(n_cmp, n_blk, ncp, nbp):
    c0 = np.arange(n_cmp)[None, :] * CMP_STRIDE
    j0 = np.arange(n_blk)[:, None] * SEL_BLOCK
    share = np.clip(np.minimum(c0 + CMP_LEN, j0 + SEL_BLOCK) - np.maximum(c0, j0), 0, None) / CMP_LEN
    out = np.zeros((nbp, ncp), np.float32)
    out[:n_blk, :n_cmp] = share
    return jnp.asarray(out, jnp.bfloat16)


def nsa_prompt_attention(q, gates, k_cmp, v_cmp, k_slc, v_slc, k_win, v_win):
    b, t, _ = q.shape
    d = HEAD_DIM
    n_cmp = k_cmp.shape[1]
    n_blk = -(-t // SEL_BLOCK)
    ncp = -(-n_cmp // LANE) * LANE
    nbp = -(-n_blk // 16) * 16
    assert t % NSA_TK_SLC == 0 and t % NSA_TQ == 0
    bf = jnp.bfloat16
    cpad = ((0, 0), (0, ncp - n_cmp), (0, 0))
    kc, vc = jnp.pad(k_cmp, cpad).astype(bf), jnp.pad(v_cmp, cpad).astype(bf)
    whole = lambda n: pl.BlockSpec((1, n, d), lambda bi, qi: (bi, 0, 0))
    return pl.pallas_call(
        functools.partial(_nsa_prompt_kernel, n_cmp=n_cmp, n_blk=n_blk, n_keys=t),
        grid=(b, t // NSA_TQ),
        in_specs=[pl.BlockSpec((1, NSA_TQ, NSA_HEADS * d), lambda bi, qi: (bi, qi, 0)),
                  pl.BlockSpec((1, NSA_TQ, 3 * NSA_HEADS), lambda bi, qi: (bi, qi, 0)),
                  whole(ncp), whole(ncp), whole(t), whole(t), whole(t), whole(t),
                  pl.BlockSpec((nbp, ncp), lambda bi, qi: (0, 0))],
        out_specs=pl.BlockSpec((1, NSA_TQ, NSA_HEADS * d), lambda bi, qi: (bi, qi, 0)),
        out_shape=jax.ShapeDtypeStruct((b, t, NSA_HEADS * d), jnp.float32),
        scratch_shapes=[pltpu.VMEM((NSA_HEADS, NSA_TQ, 1), jnp.float32),
                        pltpu.VMEM((NSA_HEADS, NSA_TQ, 1), jnp.float32),
                        pltpu.VMEM((NSA_HEADS, NSA_TQ, d), jnp.float32)],
        compiler_params=pltpu.CompilerParams(
            dimension_semantics=("parallel", "arbitrary"), vmem_limit_bytes=VMEM_LIMIT),
        name="nsa_prompt",
    )(q, gates, kc, vc, k_slc.astype(bf), v_slc.astype(bf), k_win.astype(bf), v_win.astype(bf),
      _share_matrix_t(n_cmp, n_blk, ncp, nbp))


NSA_ROWS = 4


def _gelu_tanh(x):
    return 0.5 * x * (1.0 + jnp.tanh(math.sqrt(2.0 / math.pi) * (x + 0.044715 * (x * x * x))))


def _pad_rows(x, rows):
    return jnp.concatenate([x, jnp.zeros((rows - x.shape[0], x.shape[1]), x.dtype)], axis=0)


def _gated_store(o_ref, g_ref, branches, nh, d):
    gate = 1.0 / (1.0 + jnp.exp(-g_ref[...]))
    for h in range(nh):
        o_ref[:, h * d:(h + 1) * d] = sum(gate[:, 3 * h + i:3 * h + i + 1] * br[h] for i, br in enumerate(branches))


def _nsa_sample_kernel(pt_ref, q_ref, g_ref, new_ref, winc_ref, winn_ref, pe_ref, w1_ref, w2_ref, cos_ref, sin_ref,
                       share_ref, *rest, past, n_cmp, n_blk):
    del pt_ref
    pages, o_ref = rest[:-1], rest[-1]
    nh, d, pg = NSA_HEADS, HEAD_DIM, PAGE_SIZE
    tq = q_ref.shape[0]
    nbp, ncp = share_ref.shape
    bf = jnp.bfloat16
    seg_per_page = pg // CMP_STRIDE
    assert len(pages) * seg_per_page == ncp == LANE and CMP_LEN == 2 * CMP_STRIDE
    qpos_c = past + lax.broadcasted_iota(jnp.int32, (tq, 1), 0)
    qpos_r = past + lax.broadcasted_iota(jnp.int32, (1, LANE), 1)
    lane = lax.broadcasted_iota(jnp.int32, (1, LANE), 1)

    def compress(kv):
        first = jnp.zeros((ncp, CMP_HID), jnp.float32)
        second = jnp.zeros((ncp, CMP_HID), jnp.float32)

        def seg_rows(r):
            return jnp.concatenate([p[pl.ds(NSA_ROWS * r + kv, seg_per_page, stride=NSA_ROWS * CMP_STRIDE), :]
                                    for p in pages], axis=0)

        for r in range(0, CMP_STRIDE, 2):
            ya, yb = seg_rows(r), seg_rows(r + 1)
            for acc_off, name in ((0, 'first'), (CMP_STRIDE, 'second')):
                lhs = jnp.concatenate([ya + pe_ref[kv, acc_off + r:acc_off + r + 1, :],
                                       yb + pe_ref[kv, acc_off + r + 1:acc_off + r + 2, :]], axis=1).astype(bf)
                w = w1_ref[kv, acc_off + r:acc_off + r + 2].reshape(2 * d, CMP_HID)
                y = jnp.dot(lhs, w, preferred_element_type=jnp.float32)
                if name == 'first':
                    first = first + y
                else:
                    second = second + y
        hid = first + pltpu.roll(second, ncp - 1, 0)
        return jnp.dot(_gelu_tanh(hid).astype(bf), w2_ref[kv], preferred_element_type=jnp.float32)

    kc = compress(0)
    kc = (kc * cos_ref[...] + pltpu.roll(kc, d // 2, 1) * sin_ref[...]).astype(bf)
    vc = compress(1).astype(bf)

    q = q_ref[...].astype(jnp.float32)
    qh = jnp.concatenate([q[:, h * d:(h + 1) * d] for h in range(nh)], axis=0).astype(bf)

    mask_c = ((lane * CMP_STRIDE + (CMP_LEN - 1) <= qpos_c) & (lane < n_cmp)).astype(jnp.float32)
    s = lax.dot_general(qh, kc, _NT, preferred_element_type=jnp.float32).reshape(nh, tq, ncp)
    sm = s + ((mask_c - 1.0) * (-NEG))[None]
    e = jnp.exp(sm - jnp.max(sm, axis=-1, keepdims=True)) * mask_c[None]
    p_cmp = e / jnp.maximum(jnp.sum(e, axis=-1, keepdims=True), 1e-30)
    o_cmp = jnp.dot(p_cmp.reshape(nh * tq, ncp).astype(bf), vc, preferred_element_type=jnp.float32).reshape(nh, tq, d)

    sel = _select_blocks(_pad_rows(jnp.sum(p_cmp, axis=0), LANE), share_ref[...], qpos_r, n_blk, LANE)

    def slc_mask(key0):
        kblk = key0 // SEL_BLOCK + jnp.right_shift(lax.broadcasted_iota(jnp.int32, (nbp, LANE), 1), 6)
        expand = (kblk == lax.broadcasted_iota(jnp.int32, (nbp, LANE), 0)).astype(bf)
        chosen = jnp.dot(sel, expand, preferred_element_type=jnp.float32)[:tq]
        return chosen * (key0 + lane <= qpos_c).astype(jnp.float32)

    def component(ref, comp, n_comp, row0, rows):
        return ref[pl.ds(n_comp * row0 + comp, rows, stride=n_comp), :]

    tiles = [(component(p, 2, NSA_ROWS, 0, pg).astype(bf), component(p, 3, NSA_ROWS, 0, pg).astype(bf), slc_mask(j * pg))
             for j, p in enumerate(pages)]
    tiles.append((_pad_rows(component(new_ref, 2, NSA_ROWS, 0, tq), LANE).astype(bf),
                  _pad_rows(component(new_ref, 3, NSA_ROWS, 0, tq), LANE).astype(bf), slc_mask(past)))
    o_slc = _softmax_flat(qh, tiles, nh, tq)

    def win_mask(key0):
        dlt = qpos_c - (key0 + lane)
        return ((dlt >= 0) & (dlt < WINDOW)).astype(jnp.float32)

    n_wc = winc_ref.shape[0] // 2
    tiles = [(component(winc_ref, 0, 2, j * LANE, LANE).astype(bf), component(winc_ref, 1, 2, j * LANE, LANE).astype(bf),
              win_mask(past - n_wc + j * LANE)) for j in range(n_wc // LANE)]
    tiles.append((_pad_rows(component(winn_ref, 0, 2, 0, tq), LANE).astype(bf),
                  _pad_rows(component(winn_ref, 1, 2, 0, tq), LANE).astype(bf), win_mask(past)))
    o_win = _softmax_flat(qh, tiles, nh, tq)

    _gated_store(o_ref, g_ref, (o_cmp, o_slc, o_win), nh, d)


def nsa_sample_attention(layer, q, gates, nsa_new, win_new, pool, win_cache, page_table, pe, w1, w2):
    b, t, _ = q.shape
    wlen = win_cache.shape[2]
    nsa_new = nsa_new.reshape(b, t * NSA_ROWS, HEAD_DIM)
    win_new = win_new.reshape(b, t * 2, HEAD_DIM)
    pool = pool.reshape(pool.shape[0], pool.shape[1], PAGE_SIZE * NSA_ROWS, HEAD_DIM)
    win_cache = win_cache.reshape(win_cache.shape[0], b, wlen * 2, HEAD_DIM)
    d, n_pages = HEAD_DIM, page_table.shape[1]
    past = n_pages * PAGE_SIZE
    n_keys = past + t
    n_cmp = (n_keys - CMP_LEN) // CMP_STRIDE + 1
    n_blk = -(-n_keys // SEL_BLOCK)
    ncp = -(-n_cmp // LANE) * LANE
    nbp = -(-n_blk // 16) * 16
    assert (n_cmp + 1) * CMP_STRIDE <= past, "compressed tokens must come from cached rows only"
    cos, sin = _rope_tables(jnp.arange(n_cmp) * CMP_STRIDE + CMP_LEN - 1, ncp)
    bf = jnp.bfloat16
    per_seq = lambda *tail: pl.BlockSpec((None,) + tail, lambda i, pt: (i,) + (0,) * len(tail))
    const = lambda *shape: pl.BlockSpec(shape, lambda i, pt: (0,) * len(shape))
    page = lambda j: pl.BlockSpec((None, None, PAGE_SIZE * NSA_ROWS, d), lambda i, pt: (layer, pt[i, j], 0, 0))
    grid_spec = pltpu.PrefetchScalarGridSpec(
        num_scalar_prefetch=1, grid=(b,),
        in_specs=[per_seq(t, NSA_HEADS * d), per_seq(t, 3 * NSA_HEADS), per_seq(t * NSA_ROWS, d),
                  pl.BlockSpec((None, None, wlen * 2, d), lambda i, pt: (layer, i, 0, 0)),
                  per_seq(t * 2, d),
                  const(2, CMP_LEN, d), const(2, CMP_LEN, d, CMP_HID), const(2, CMP_HID, d),
                  const(ncp, d), const(ncp, d), const(nbp, ncp)] + [page(j) for j in range(n_pages)],
        out_specs=per_seq(t, NSA_HEADS * d))
    return pl.pallas_call(
        functools.partial(_nsa_sample_kernel, past=past, n_cmp=n_cmp, n_blk=n_blk),
        grid_spec=grid_spec,
        out_shape=jax.ShapeDtypeStruct((b, t, NSA_HEADS * d), jnp.float32),
        compiler_params=pltpu.CompilerParams(dimension_semantics=("parallel",), vmem_limit_bytes=VMEM_LIMIT),
        name="nsa_sample",
    )(page_table, q, gates, nsa_new, win_cache, win_new, pe, w1.reshape(2, CMP_LEN, d, CMP_HID).astype(bf),
      w2.astype(bf), cos, sin, _share_matrix_t(n_cmp, n_blk, ncp, nbp), *([pool] * n_pages))


DSA_TQ = 128
DSA_TK = 256
INT_MIN = -2 ** 31


def _dsa_prompt_kernel(q_ref, qi_ref, wt_ref, k_ref, v_ref, ki_ref, tri_ref, o_ref,
                       key_ref, sel_ref, m_ref, l_ref, acc_ref, *, n_keys, topk):
    nh, tq, d, tk = DSA_HEADS, DSA_TQ, HEAD_DIM, DSA_TK
    start = pl.program_id(1) * tq
    qpos_r = start + lax.broadcasted_iota(jnp.int32, (1, tq), 1)
    n_kt = jnp.minimum((start + tq - 1) // tk + 1, n_keys // tk)

    wt = wt_ref[0] * (IDX_HEADS ** -0.5 * IDX_DIM ** -0.5)

    def score_body(kt, carry):
        off = pl.multiple_of(kt * tk, tk)
        ki = ki_ref[0, pl.ds(off, tk), :]
        acc = jnp.zeros((tk, tq), jnp.float32)
        for h in range(IDX_HEADS):
            dots = lax.dot_general(ki, qi_ref[0, h], _NT, preferred_element_type=jnp.float32)
            acc = acc + jnp.maximum(dots, 0.0) * wt[h:h + 1, :]
        kpos_c = off + lax.broadcasted_iota(jnp.int32, (tk, 1), 0)
        score = jnp.where(kpos_c <= qpos_r, acc, NEG)
        bits = pltpu.bitcast(score, jnp.int32)
        key_ref[kt] = bits ^ (jnp.right_shift(bits, 31) & 0x7FFFFFFF)
        return carry

    lax.fori_loop(0, n_kt, score_body, 0)

    def count(pred):
        def body(kt, part):
            hit = pred(key_ref[kt]).astype(jnp.int32)
            return part + jnp.sum(hit.reshape(tk // 8, 8, tq), axis=0)
        part = lax.fori_loop(0, n_kt, body, jnp.zeros((8, tq), jnp.int32))
        return jnp.sum(part, axis=0, keepdims=True)

    thr = jnp.where(count(lambda kk: kk >= 0) >= topk, 0, INT_MIN).astype(jnp.int32)

    def bit_body(i, t):
        cand = t | jnp.left_shift(jnp.int32(1), 30 - i)
        return jnp.where(count(lambda kk: kk >= cand) >= topk, cand, t)

    thr = lax.fori_loop(0, 31, bit_body, thr)
    room = (topk - count(lambda kk: kk > thr)).astype(jnp.float32)

    def sel_body(kt, seen):
        kk = key_ref[kt]
        eq = (kk == thr).astype(jnp.float32)
        prefix = jnp.dot(tri_ref[...], eq.astype(jnp.bfloat16), preferred_element_type=jnp.float32)
        keep = (kk > thr) | ((eq > 0.0) & (prefix + seen <= room))
        kpos_c = kt * tk + lax.broadcasted_iota(jnp.int32, (tk, 1), 0)
        sel_ref[kt] = (keep & (kpos_c <= qpos_r)).astype(jnp.bfloat16)
        return seen + jnp.sum(eq, axis=0, keepdims=True)

    lax.fori_loop(0, n_kt, sel_body, jnp.zeros((1, tq), jnp.float32))

    eye = (lax.broadcasted_iota(jnp.int32, (tq, tq), 0)
           == lax.broadcasted_iota(jnp.int32, (tq, tq), 1)).astype(jnp.bfloat16)

    def sel_mask(kt):
        return lax.dot_general(eye, sel_ref[kt], _NT, preferred_element_type=jnp.float32)

    qh = jnp.concatenate([q_ref[0, :, h * d:(h + 1) * d] for h in range(nh)], axis=0)
    o = _softmax_tiles(qh, k_ref, v_ref, 0, n_kt, tk, sel_mask, m_ref, l_ref, acc_ref, nh, tq)
    for h in range(nh):
        o_ref[0, :, h * d:(h + 1) * d] = o[h]


def dsa_prompt_attention(q, q_idx, w_idx, k, v, k_idx):
    b, t, _ = q.shape
    d = HEAD_DIM
    topk = min(DSA_TOPK, t // 4)
    assert t % DSA_TK == 0 and t % DSA_TQ == 0
    bf = jnp.bfloat16
    tri = jnp.asarray(np.tril(np.ones((DSA_TK, DSA_TK), np.float32)), bf)
    whole = lambda n, w: pl.BlockSpec((1, n, w), lambda bi, qi: (bi, 0, 0))
    return pl.pallas_call(
        functools.partial(_dsa_prompt_kernel, n_keys=t, topk=topk),
        grid=(b, t // DSA_TQ),
        in_specs=[pl.BlockSpec((1, DSA_TQ, DSA_HEADS * d), lambda bi, qi: (bi, qi, 0)),
                  pl.BlockSpec((1, IDX_HEADS, DSA_TQ, IDX_DIM), lambda bi, qi: (bi, 0, qi, 0)),
                  pl.BlockSpec((1, IDX_HEADS, DSA_TQ), lambda bi, qi: (bi, 0, qi)),
                  whole(t, d), whole(t, d), whole(t, IDX_DIM),
                  pl.BlockSpec((DSA_TK, DSA_TK), lambda bi, qi: (0, 0))],
        out_specs=pl.BlockSpec((1, DSA_TQ, DSA_HEADS * d), lambda bi, qi: (bi, qi, 0)),
        out_shape=jax.ShapeDtypeStruct((b, t, DSA_HEADS * d), jnp.float32),
        scratch_shapes=[pltpu.VMEM((t // DSA_TK, DSA_TK, DSA_TQ), jnp.int32),
                        pltpu.VMEM((t // DSA_TK, DSA_TK, DSA_TQ), bf),
                        pltpu.VMEM((DSA_HEADS, DSA_TQ, 1), jnp.float32),
                        pltpu.VMEM((DSA_HEADS, DSA_TQ, 1), jnp.float32),
                        pltpu.VMEM((DSA_HEADS, DSA_TQ, d), jnp.float32)],
        compiler_params=pltpu.CompilerParams(
            dimension_semantics=("parallel", "arbitrary"), vmem_limit_bytes=VMEM_LIMIT),
        name="dsa_prompt",
    )(q, q_idx.astype(bf), jnp.swapaxes(w_idx, 1, 2),
      k.astype(bf), v.astype(bf), k_idx.astype(bf), tri)


def _dsa_sample_kernel(pt_ref, q_ref, qi_ref, w_ref, new_ref, inew_ref, tri_ref, *rest, past, topk):
    del pt_ref
    n_pages = (len(rest) - 1) // 2
    kv_pages, idx_pages, o_ref = rest[:n_pages], rest[n_pages:2 * n_pages], rest[-1]
    nh, d, pg = DSA_HEADS, HEAD_DIM, PAGE_SIZE
    tq = q_ref.shape[0]
    bf = jnp.bfloat16
    qpos_c = past + lax.broadcasted_iota(jnp.int32, (tq, 1), 0)
    lane = lax.broadcasted_iota(jnp.int32, (1, LANE), 1)

    qi = qi_ref[...]
    wb = jnp.broadcast_to(w_ref[...] * (IDX_HEADS ** -0.5 * IDX_DIM ** -0.5), (IDX_HEADS * tq, LANE))

    def key_tile(ki, key0):
        dots = lax.dot_general(qi, ki, _NT, preferred_element_type=jnp.float32)
        score = jnp.sum((jnp.maximum(dots, 0.0) * wb).reshape(IDX_HEADS, tq, LANE), axis=0)
        score = jnp.where(key0 + lane <= qpos_c, score, NEG)
        bits = pltpu.bitcast(score, jnp.int32)
        return bits ^ (jnp.right_shift(bits, 31) & 0x7FFFFFFF)

    keys = [key_tile(p[...].astype(bf), j * pg) for j, p in enumerate(idx_pages)]
    keys.append(key_tile(_pad_rows(inew_ref[...], LANE).astype(bf), past))

    def count(pred):
        hits = pred(keys[0]).astype(jnp.int32)
        for kk in keys[1:]:
            hits = hits + pred(kk).astype(jnp.int32)
        return jnp.sum(hits, axis=-1, keepdims=True)

    thr = jnp.where(count(lambda kk: kk >= 0) >= topk, 0, INT_MIN).astype(jnp.int32)

    def bit_body(i, t):
        cand = t | jnp.left_shift(jnp.int32(1), 30 - i)
        return jnp.where(count(lambda kk: kk >= cand) >= topk, cand, t)

    thr = lax.fori_loop(0, 31, bit_body, thr)
    room = (topk - count(lambda kk: kk > thr)).astype(jnp.float32)
    eqs = [(kk == thr).astype(jnp.float32) for kk in keys]
    totals = [jnp.sum(e, axis=-1, keepdims=True) for e in eqs]
    seen = jnp.zeros((tq, 1), jnp.float32)
    masks = []
    for j, (kk, e) in enumerate(zip(keys, eqs)):
        prefix = jnp.dot(e, tri_ref[...], preferred_element_type=jnp.float32)
        keep = (kk > thr) | ((e > 0.0) & (prefix + seen <= room))
        key0 = j * pg if j < n_pages else past
        masks.append((keep & (key0 + lane <= qpos_c)).astype(jnp.float32))
        seen = seen + totals[j]

    def component(ref, comp, rows):
        return ref[pl.ds(comp, rows, stride=2), :]

    tiles = [(component(p, 0, pg).astype(bf), component(p, 1, pg).astype(bf), masks[j]) for j, p in enumerate(kv_pages)]
    tiles.append((_pad_rows(component(new_ref, 0, tq), LANE).astype(bf),
                  _pad_rows(component(new_ref, 1, tq), LANE).astype(bf), masks[n_pages]))
    q = q_ref[...].astype(jnp.float32)
    qh = jnp.concatenate([q[:, h * d:(h + 1) * d] for h in range(nh)], axis=0).astype(bf)
    o = _softmax_flat(qh, tiles, nh, tq)
    for h in range(nh):
        o_ref[:, h * d:(h + 1) * d] = o[h]


def dsa_sample_attention(layer, q, q_idx, w_idx, dsa_new, idx_new, pool, idx_pool, page_table):
    b, t, _ = q.shape
    d, n_pages = HEAD_DIM, page_table.shape[1]
    past = n_pages * PAGE_SIZE
    topk = min(DSA_TOPK, (past + t) // 4)
    pool = pool.reshape(pool.shape[0], pool.shape[1], PAGE_SIZE * 2, d)
    tri = jnp.asarray(np.triu(np.ones((LANE, LANE), np.float32)))
    per_seq = lambda *tail: pl.BlockSpec((None,) + tail, lambda i, pt: (i,) + (0,) * len(tail))
    kv_page = lambda j: pl.BlockSpec((None, None, PAGE_SIZE * 2, d), lambda i, pt: (layer, pt[i, j], 0, 0))
    idx_page = lambda j: pl.BlockSpec((None, None, PAGE_SIZE, IDX_DIM), lambda i, pt: (layer, pt[i, j], 0, 0))
    grid_spec = pltpu.PrefetchScalarGridSpec(
        num_scalar_prefetch=1, grid=(b,),
        in_specs=[per_seq(t, DSA_HEADS * d), per_seq(IDX_HEADS * t, IDX_DIM), per_seq(IDX_HEADS * t, 1),
                  per_seq(t * 2, d), per_seq(t, IDX_DIM), pl.BlockSpec((LANE, LANE), lambda i, pt: (0, 0))]
                 + [kv_page(j) for j in range(n_pages)] + [idx_page(j) for j in range(n_pages)],
        out_specs=per_seq(t, DSA_HEADS * d))
    return pl.pallas_call(
        functools.partial(_dsa_sample_kernel, past=past, topk=topk),
        grid_spec=grid_spec,
        out_shape=jax.ShapeDtypeStruct((b, t, DSA_HEADS * d), jnp.float32),
        compiler_params=pltpu.CompilerParams(dimension_semantics=("parallel",), vmem_limit_bytes=VMEM_LIMIT),
        name="dsa_sample",
    )(page_table, q, q_idx.reshape(b, IDX_HEADS * t, IDX_DIM).astype(jnp.bfloat16),
      jnp.swapaxes(w_idx, 1, 2).reshape(b, IDX_HEADS * t, 1), dsa_new.reshape(b, t * 2, d), idx_new, tri,
      *([pool] * n_pages), *([idx_pool] * n_pages))


GDN_GROUP = 256
GDN_SUB = 16
GDN_HB = 2
CONV_PAD = 8


def _hp_dot(a, b):
    bf = jnp.bfloat16
    ah, bh = a.astype(bf), b.astype(bf)
    al, bl = (a - ah.astype(jnp.float32)).astype(bf), (b - bh.astype(jnp.float32)).astype(bf)
    dot = functools.partial(jnp.dot, preferred_element_type=jnp.float32)
    return dot(ah, bh) + dot(ah, bl) + dot(al, bh)


def _unit_lower_inverse(a, row, col):
    assert GDN_SUB == 16 and GDN_CHUNK == 64
    eye = (row == col).astype(jnp.float32)
    sub = jnp.right_shift(row, 4) == jnp.right_shift(col, 4)
    a16 = jnp.where(sub, a, 0.0)
    t16 = eye - a16
    power = a16
    for _ in range(3):
        power = _hp_dot(power, power)
        t16 = t16 + _hp_dot(t16, power)
    b = _hp_dot(t16, a - a16)
    imb = eye - b
    return _hp_dot(imb + _hp_dot(imb, _hp_dot(b, b)), t16)


def _gdn_prompt_kernel(xq_ref, xk_ref, xv_ref, pq_ref, pk_ref, pv_ref, a_ref, b_ref, z_ref, cwq_ref, cwk_ref, cwv_ref,
                       alog_ref, dtb_ref, ng_ref, s0_ref, o_ref, s_ref):
    f32, bf = jnp.float32, jnp.bfloat16
    d, g, c = HEAD_DIM, GDN_GROUP, GDN_CHUNK
    n_groups = o_ref.shape[0] // g
    row = lax.broadcasted_iota(jnp.int32, (g, g), 0)
    col = lax.broadcasted_iota(jnp.int32, (g, g), 1)
    same = jnp.right_shift(row, 6) == jnp.right_shift(col, 6)
    incl = same & (row >= col)
    strict = same & (row > col)
    eye = (row == col).astype(f32)
    tril_b, same_b = incl.astype(bf), same.astype(bf)
    s_ref[...] = s0_ref[...]

    def group_body(gi, carry):
        r0 = pl.multiple_of(gi * g, g)
        gate = -jnp.exp(alog_ref[...]) * (jnp.maximum(a_ref[pl.ds(r0, g), :] + dtb_ref[...], 0.0)
                                          + jnp.log(1.0 + jnp.exp(-jnp.abs(a_ref[pl.ds(r0, g), :] + dtb_ref[...]))))
        parts = _split3_bf16(gate)
        gcum_all = sum(jnp.dot(tril_b, p, preferred_element_type=f32) for p in parts)
        gtot_all = sum(jnp.dot(same_b, p, preferred_element_type=f32) for p in parts)
        beta_all = 1.0 / (1.0 + jnp.exp(-b_ref[pl.ds(r0, g), :]))
        for h in range(GDN_HB):
            lanes = slice(h * d, (h + 1) * d)

            def conv(x_ref, prev_ref, w_ref):
                hist = x_ref[pl.ds(pl.multiple_of(jnp.maximum(r0 - CONV_PAD, 0), CONV_PAD), CONV_PAD), lanes]
                hist = jnp.where(gi == 0, prev_ref[:, lanes], hist)
                win = jnp.concatenate([hist, x_ref[pl.ds(r0, g), lanes]], axis=0)
                y = sum(pltpu.roll(win, g + CONV_PAD - (CONV_PAD - GDN_CONV + 1 + i), 0)[:g] * w_ref[i:i + 1, lanes]
                        for i in range(GDN_CONV))
                return y / (1.0 + jnp.exp(-y))

            q, k, v = conv(xq_ref, pq_ref, cwq_ref), conv(xk_ref, pk_ref, cwk_ref), conv(xv_ref, pv_ref, cwv_ref)
            q = q * lax.rsqrt(jnp.sum(q * q, axis=-1, keepdims=True) + EPS) * (d ** -0.5)
            k = k * lax.rsqrt(jnp.sum(k * k, axis=-1, keepdims=True) + EPS)
            gc, gl, beta = gcum_all[:, h:h + 1], gtot_all[:, h:h + 1], beta_all[:, h:h + 1]
            g_i = jnp.broadcast_to(gc, (g, g))
            g_j = jnp.sum(g_i * eye, axis=0, keepdims=True)
            decay = jnp.where(incl, jnp.exp(jnp.where(incl, g_i - g_j, 0.0)), 0.0)
            kb = k * beta
            k16, eg = k.astype(bf), jnp.exp(gc)
            a_mat = jnp.where(strict, lax.dot_general(kb.astype(bf), k16, _NT, preferred_element_type=f32) * decay, 0.0)
            rhs = jnp.concatenate([v * beta, kb * eg], axis=1)
            sol = _hp_dot(_unit_lower_inverse(a_mat, row, col), rhs)
            u, w = sol[:, :d], sol[:, d:]
            qk = (lax.dot_general(q.astype(bf), k16, _NT, preferred_element_type=f32) * decay).astype(bf)
            q_dec, k_dec, w16 = (q * eg).astype(bf), (k * jnp.exp(gl - gc)).astype(bf), w.astype(bf)
            chunk_decay = jnp.exp(gl)
            for ci in range(g // c):
                rows = slice(ci * c, (ci + 1) * c)
                state = s_ref[h]
                s16 = state.astype(bf)
                new_v = u[rows] - jnp.dot(w16[rows], s16, preferred_element_type=f32)
                nv16 = new_v.astype(bf)
                pieces = [jnp.zeros((n, d), bf) for n in (ci * c,) if n] + [nv16] + [
                    jnp.zeros((n, d), bf) for n in (g - (ci + 1) * c,) if n]
                placed = jnp.concatenate(pieces, axis=0)
                out = (jnp.dot(q_dec[rows], s16, preferred_element_type=f32)
                       + jnp.dot(qk[rows], placed, preferred_element_type=f32))
                s_ref[h] = state * chunk_decay[ci * c:ci * c + 1, :] + lax.dot_general(
                    k_dec[rows], nv16, (((0,), (0,)), ((), ())), preferred_element_type=f32)
                zz = z_ref[pl.ds(r0 + ci * c, c), lanes]
                out = out * lax.rsqrt(jnp.mean(out * out, axis=-1, keepdims=True) + EPS) * ng_ref[...]
                o_ref[pl.ds(r0 + ci * c, c), lanes] = out * (zz / (1.0 + jnp.exp(-zz)))
        return carry

    lax.fori_loop(0, n_groups, group_body, 0)


def gdn_prompt(src, qkv_off, z_off, a, beta_logit, conv_prev, s0, conv_w, a_log, dt_bias, norm_g):
    b, t, _ = src.shape
    d, hb = HEAD_DIM, GDN_HB
    ng = GDN_HEADS // hb
    assert t % GDN_GROUP == 0 and GDN_HEADS % hb == 0 and qkv_off % (hb * d) == 0 and z_off % (hb * d) == 0
    qb, zb = qkv_off // (hb * d), z_off // (hb * d)
    prev = jnp.pad(conv_prev.astype(src.dtype), ((0, 0), (CONV_PAD - GDN_CONV + 1, 0), (0, 0)))
    by_group = lambda x: jnp.swapaxes(x.reshape(b, t, ng, hb), 1, 2)
    cols = lambda part: pl.BlockSpec((None, t, hb * d), lambda bi, gi: (bi, 0, qb + part * ng + gi))
    hist = lambda part: pl.BlockSpec((None, CONV_PAD, hb * d), lambda bi, gi: (bi, 0, part * ng + gi))
    cw = lambda part: pl.BlockSpec((GDN_CONV, hb * d), lambda bi, gi: (0, part * ng + gi))
    tok = pl.BlockSpec((None, None, t, hb), lambda bi, gi: (bi, gi, 0, 0))
    head_const = pl.BlockSpec((None, 1, hb), lambda bi, gi: (gi, 0, 0))
    state = pl.BlockSpec((None, hb, d, d), lambda bi, gi: (bi, gi, 0, 0))
    seq = pl.BlockSpec((None, t, hb * d), lambda bi, gi: (bi, 0, gi))
    return pl.pallas_call(
        _gdn_prompt_kernel,
        grid=(b, ng),
        in_specs=[cols(0), cols(1), cols(2), hist(0), hist(1), hist(2), tok, tok,
                  pl.BlockSpec((None, t, hb * d), lambda bi, gi: (bi, 0, zb + gi)),
                  cw(0), cw(1), cw(2), head_const, head_const,
                  pl.BlockSpec((1, d), lambda bi, gi: (0, 0)), state],
        out_specs=[seq, state],
        out_shape=[jax.ShapeDtypeStruct((b, t, W_C), jnp.float32),
                   jax.ShapeDtypeStruct((b, GDN_HEADS, d, d), jnp.float32)],
        compiler_params=pltpu.CompilerParams(
            dimension_semantics=("parallel", "arbitrary"), vmem_limit_bytes=VMEM_LIMIT),
        name="gdn_prompt",
    )(src, src, src, prev, prev, prev, by_group(a), by_group(beta_logit), src, conv_w, conv_w, conv_w,
      a_log.reshape(ng, 1, hb).astype(jnp.float32), dt_bias.reshape(ng, 1, hb).astype(jnp.float32),
      norm_g.reshape(1, d).astype(jnp.float32), s0)


def rms_norm(x, g):
    xf = x.astype(jnp.float32)
    y = xf * lax.rsqrt(jnp.mean(xf * xf, axis=-1, keepdims=True) + EPS)
    return (y * g.astype(jnp.float32)).astype(x.dtype)


def l2norm(x):
    xf = x.astype(jnp.float32)
    return xf * lax.rsqrt(jnp.sum(xf * xf, axis=-1, keepdims=True) + EPS)


def rope(x, pos):
    half = x.shape[-1] // 2
    inv = ROPE_THETA ** (-jnp.arange(half, dtype=jnp.float32) / half)
    ang = pos.astype(jnp.float32)[:, None] * inv
    cos, sin = jnp.cos(ang)[:, None, :], jnp.sin(ang)[:, None, :]
    xf = x.astype(jnp.float32)
    x1, x2 = xf[..., :half], xf[..., half:]
    return jnp.concatenate([x1 * cos - x2 * sin, x2 * cos + x1 * sin], axis=-1).astype(x.dtype)


def masked_softmax(s, mask):
    s = jnp.where(mask, s.astype(jnp.float32), NEG)
    e = jnp.where(mask, jnp.exp(s - jnp.max(s, axis=-1, keepdims=True)), 0.0)
    return e / jnp.maximum(jnp.sum(e, axis=-1, keepdims=True), 1e-30)


def split_cols(h):
    return jnp.split(h[..., :sum(IN_SPLITS)], [int(o) for o in np.cumsum(IN_SPLITS)[:-1]], axis=-1)


def gather_pages(pool, page_table):
    rows = pool[page_table]
    return rows.reshape(rows.shape[0], -1, *rows.shape[3:])


def over_query_blocks(fn, *qs):
    b, t = qs[0].shape[:2]
    if t <= Q_BLOCK:
        return fn(0, *qs)
    nb = t // Q_BLOCK
    blocks = tuple(q.reshape(b, nb, Q_BLOCK, *q.shape[2:]).swapaxes(0, 1) for q in qs)
    out = lax.map(lambda a: fn(a[0] * Q_BLOCK, *a[1]), (jnp.arange(nb), blocks))
    out = out.swapaxes(0, 1)
    return out.reshape(b, t, *out.shape[3:])


def nsa_compress(rows, pe, w1, w2):
    b, n_keys, d = rows.shape
    r = CMP_LEN // CMP_STRIDE
    n_cmp = (n_keys - CMP_LEN) // CMP_STRIDE + 1
    seg = rows[:, :(n_cmp + r - 1) * CMP_STRIDE].reshape(b, n_cmp + r - 1, CMP_STRIDE * d)
    w1r = w1.reshape(r, CMP_STRIDE * d, CMP_HID)
    h = pe.reshape(-1) @ w1 + sum(seg[:, i:i + n_cmp] @ w1r[i] for i in range(r))
    return jax.nn.gelu(h) @ w2


def nsa_attention(q, gates, k_cmp, v_cmp, k_slc, v_slc, k_win, v_win, past, win_pos0, banded):
    b, n_keys, d = k_slc.shape
    scale = d ** -0.5
    n_cmp = k_cmp.shape[1]
    cmp_end = jnp.arange(n_cmp) * CMP_STRIDE + CMP_LEN - 1
    n_blk = -(-n_keys // SEL_BLOCK)
    n_sel = min(SEL_COUNT, n_blk)
    pad = ((0, 0), (0, n_blk * SEL_BLOCK - n_keys), (0, 0))
    k_blk = jnp.pad(k_slc, pad).reshape(b, n_blk, SEL_BLOCK, d)
    v_blk = jnp.pad(v_slc, pad).reshape(b, n_blk, SEL_BLOCK, d)
    c0 = jnp.arange(n_cmp)[:, None] * CMP_STRIDE
    j0 = jnp.arange(n_blk)[None, :] * SEL_BLOCK
    share = jnp.clip(jnp.minimum(c0 + CMP_LEN, j0 + SEL_BLOCK) - jnp.maximum(c0, j0), 0, None).astype(jnp.float32) / CMP_LEN
    blk_id = jnp.arange(n_blk)
    if banded:
        wpad = ((0, 0), (WINDOW, 0), (0, 0))
        k_win, v_win = jnp.pad(k_win, wpad), jnp.pad(v_win, wpad)

    def block(start, qb, gb):
        nq = qb.shape[1]
        qp = past + start + jnp.arange(nq)
        s = jnp.einsum('bqhd,bcd->bqhc', qb, k_cmp) * scale
        p_cmp = masked_softmax(s, (cmp_end[None, :] <= qp[:, None])[None, :, None, :])
        o_cmp = jnp.einsum('bqhc,bcd->bqhd', p_cmp.astype(v_cmp.dtype), v_cmp)
        imp = jnp.einsum('bqhc,cj->bqj', p_cmp, share)
        forced = (blk_id[None, :] == qp[:, None] // SEL_BLOCK) | (blk_id[None, :] == 0)
        future = blk_id[None, :] * SEL_BLOCK > qp[:, None]
        imp = jnp.where(forced[None], FORCED, jnp.where(future[None], -1.0, imp))
        _, sel = lax.top_k(imp, n_sel)
        ks = jax.vmap(lambda kb, i: kb[i])(k_blk, sel).reshape(b, nq, n_sel * SEL_BLOCK, d)
        vs = jax.vmap(lambda vb, i: vb[i])(v_blk, sel).reshape(b, nq, n_sel * SEL_BLOCK, d)
        kpos = (sel[..., None] * SEL_BLOCK + jnp.arange(SEL_BLOCK)).reshape(b, nq, n_sel * SEL_BLOCK)
        s = jnp.einsum('bqhd,bqkd->bqhk', qb, ks) * scale
        p = masked_softmax(s, (kpos <= qp[None, :, None])[:, :, None, :])
        o_slc = jnp.einsum('bqhk,bqkd->bqhd', p.astype(vs.dtype), vs)
        if banded:
            n_w = WINDOW + nq
            kw = lax.dynamic_slice_in_dim(k_win, start, n_w, axis=1)
            vw = lax.dynamic_slice_in_dim(v_win, start, n_w, axis=1)
            kp = past + start - WINDOW + jnp.arange(n_w)
        else:
            kw, vw = k_win, v_win
            kp = win_pos0 + jnp.arange(k_win.shape[1])
        dlt = qp[:, None] - kp[None, :]
        s = jnp.einsum('bqhd,bkd->bqhk', qb, kw) * scale
        p = masked_softmax(s, ((dlt >= 0) & (dlt < WINDOW) & (kp[None, :] >= 0))[None, :, None, :])
        o_win = jnp.einsum('bqhk,bkd->bqhd', p.astype(vw.dtype), vw)
        g = jax.nn.sigmoid(gb.astype(jnp.float32))
        o = g[..., 0:1] * o_cmp + g[..., 1:2] * o_slc + g[..., 2:3] * o_win
        return o.astype(qb.dtype)

    return over_query_blocks(block, q, gates)


def dsa_attention(q, q_idx, w_idx, k, v, k_idx, past):
    b, n_keys, d = k.shape
    topk = min(DSA_TOPK, n_keys // 4)
    kpos = jnp.arange(n_keys)

    def block(start, qb, qib, wb):
        qp = past + start + jnp.arange(qb.shape[1])
        causal = kpos[None, :] <= qp[:, None]
        dots = jnp.einsum('bqhd,bsd->bqhs', qib, k_idx).astype(jnp.float32) * IDX_DIM ** -0.5
        score = jnp.einsum('bqh,bqhs->bqs', wb.astype(jnp.float32) * IDX_HEADS ** -0.5, jax.nn.relu(dots))
        score = jnp.where(causal[None], score, NEG)
        _, sel = lax.top_k(score, topk)
        ks = jax.vmap(lambda kk, i: kk[i])(k, sel)
        vs = jax.vmap(lambda vv, i: vv[i])(v, sel)
        s = jnp.einsum('bqhd,bqkd->bqhk', qb, ks) * d ** -0.5
        p = masked_softmax(s, (sel <= qp[None, :, None])[:, :, None, :])
        return jnp.einsum('bqhk,bqkd->bqhd', p.astype(vs.dtype), vs)

    return over_query_blocks(block, q, q_idx, w_idx)


def gated_delta_chunked(q, k, v, g, beta, s0):
    f32 = jnp.float32
    b, t, h, dk = k.shape
    dv = v.shape[-1]
    c = min(GDN_CHUNK, t)
    n = -(-t // c)
    pad = n * c - t

    def to_chunks(a):
        a = jnp.pad(a.astype(f32), [(0, 0), (0, pad)] + [(0, 0)] * (a.ndim - 2))
        a = a.reshape(b, n, c, *a.shape[2:])
        return jnp.moveaxis(a, (1, 3), (0, 2))

    qc, kc, vc, gc, bc = (to_chunks(a) for a in (q, k, v, g, beta))
    gcum = jnp.cumsum(gc, axis=-1)
    pos = jnp.arange(c)
    strict = pos[:, None] > pos[None, :]
    incl = pos[:, None] >= pos[None, :]
    diff = gcum[..., :, None] - gcum[..., None, :]
    decay = jnp.where(incl, jnp.exp(jnp.where(incl, diff, 0.0)), 0.0)
    k_beta = kc * bc[..., None]
    a_mat = jnp.where(strict, jnp.einsum('nbhid,nbhjd->nbhij', k_beta, kc) * decay, 0.0)
    rhs = jnp.concatenate([vc * bc[..., None], k_beta * jnp.exp(gcum)[..., None]], axis=-1)
    sol = lax.linalg.triangular_solve(a_mat + jnp.eye(c, dtype=f32), rhs,
                                      left_side=True, lower=True, unit_diagonal=True)
    u, w = sol[..., :dv], sol[..., dv:]
    qk = jnp.einsum('nbhid,nbhjd->nbhij', qc, kc) * decay
    q_dec = qc * jnp.exp(gcum)[..., None]
    k_dec = kc * jnp.exp(gcum[..., -1:] - gcum)[..., None]
    chunk_decay = jnp.exp(gcum[..., -1])

    def step(state, xs):
        u_i, w_i, qd_i, qk_i, kd_i, cd_i = xs
        new_v = u_i - jnp.einsum('bhck,bhkv->bhcv', w_i, state)
        o_i = jnp.einsum('bhck,bhkv->bhcv', qd_i, state) + jnp.einsum('bhij,bhjv->bhiv', qk_i, new_v)
        state = state * cd_i[..., None, None] + jnp.einsum('bhck,bhcv->bhkv', kd_i, new_v)
        return state, o_i

    s_final, o = lax.scan(step, s0.astype(f32), (u, w, q_dec, qk, k_dec, chunk_decay))
    o = jnp.moveaxis(o, (0, 2), (1, 3)).reshape(b, n * c, h, dv)[:, :t]
    return o, s_final


def gdn_mixer(qkv, a, beta_logit, z, conv_prev, s0, conv_w, a_log, dt_bias, norm_g):
    b, t, _ = qkv.shape
    xp = jnp.concatenate([conv_prev.astype(qkv.dtype), qkv], axis=1)
    conv = jax.nn.silu(sum(xp[:, i:i + t] * conv_w[i] for i in range(GDN_CONV)))
    q, k, v = (cc.reshape(b, t, GDN_HEADS, HEAD_DIM) for cc in jnp.split(conv, 3, axis=-1))
    q = l2norm(q) * HEAD_DIM ** -0.5
    k = l2norm(k)
    g = -jnp.exp(a_log.astype(jnp.float32)) * jax.nn.softplus(a.astype(jnp.float32) + dt_bias.astype(jnp.float32))
    beta = jax.nn.sigmoid(beta_logit.astype(jnp.float32))
    o, s_new = gated_delta_chunked(q, k, v, g, beta, s0)
    o = rms_norm(o, norm_g) * jax.nn.silu(z.reshape(b, t, GDN_HEADS, HEAD_DIM).astype(jnp.float32))
    return o.reshape(b, t, W_C).astype(qkv.dtype), s_new, xp[:, t:]


def mem_attention(x, gain, mem_kv, w_q, w_o):
    b, t, _ = x.shape
    q = dense(x, w_q, gain=gain).reshape(b, t, MEM_HEADS, HEAD_DIM)
    s = jnp.einsum('bqhd,bmhd->bhqm', q, mem_kv[:, :, 0]) * HEAD_DIM ** -0.5
    p = jax.nn.softmax(s.astype(jnp.float32), axis=-1).astype(x.dtype)
    o = jnp.einsum('bhqm,bmhd->bqhd', p, mem_kv[:, :, 1])
    return dense(o.reshape(b, t, MEM_HEADS * HEAD_DIM), w_o, resid=x)


def run_group(x, mem, cache, p):
    prompt = cache is None
    b, t, _ = x.shape
    past = 0 if prompt else cache['page_table'].shape[1] * cache['nsa_kv'].shape[2]
    names = ('nsa_kv', 'dsa_kv', 'idx_k', 'win_kv', 'gdn', 'conv') + (('mem_kv',) if prompt else ())
    out = {nm: [] for nm in names}
    bg, tg = (b, t) if prompt else (1, b * t)
    row_pos = past + jnp.arange(tg) % t
    for l in range(DEPTH):
        hproj = dense(x, p['w_in'][l], gain=p['norm_mix_g'][l])
        (q_a, q_b, q_i, nsa_new, win_new, dsa_new, idx_new,
         k_slc, v_slc, k_win, v_win, k_dsa, v_dsa, k_idx) = rope_split(hproj.reshape(bg, tg, PROJ_WIDTH), row_pos)
        nsa_new = nsa_new.reshape(b, t, NSA_ROWS, HEAD_DIM)
        win_new = win_new.reshape(b, t, 2, HEAD_DIM)
        dsa_new = dsa_new.reshape(b, t, 2, HEAD_DIM)
        idx_new = idx_new.reshape(b, t, IDX_DIM)
        g_a, w_i, a_c, b_c = (small_col(hproj, nm) for nm in ('g_a', 'w_i', 'a_c', 'b_c'))
        qkv_off, z_off = PROJ_OFF['qkv_c'], PROJ_OFF['z_c']
        qkv_c = hproj[..., qkv_off:qkv_off + 3 * W_C]
        if prompt:
            conv_prev = jnp.zeros((b, GDN_CONV - 1, 3 * W_C), x.dtype)
            s0 = jnp.zeros((b, GDN_HEADS, HEAD_DIM, HEAD_DIM), jnp.float32)
            mem_kv = dense(mem, p['w_mem_kv'][l], gain=p['mem_norm_g'][l]).reshape(b, -1, 2, MEM_HEADS, HEAD_DIM)
            k_cmp = nsa_compress(nsa_new[:, :, 0], p['nsa_cmp_pe'][l, 0], p['nsa_cmp_w1'][l, 0], p['nsa_cmp_w2'][l, 0])
            v_cmp = nsa_compress(nsa_new[:, :, 1], p['nsa_cmp_pe'][l, 1], p['nsa_cmp_w1'][l, 1], p['nsa_cmp_w2'][l, 1])
            cmp_end = jnp.arange(k_cmp.shape[1]) * CMP_STRIDE + CMP_LEN - 1
            k_cmp = rope(k_cmp[:, :, None], cmp_end)[:, :, 0]
            o_a = nsa_prompt_attention(q_a, g_a, k_cmp, v_cmp, k_slc, v_slc, k_win, v_win)
            o_b = dsa_prompt_attention(q_b, q_i, w_i, k_dsa, v_dsa, k_idx)
            win_out = win_new[:, -min(WINDOW, t):]
            o_c, s_new = gdn_prompt(hproj, qkv_off, z_off, a_c, b_c, conv_prev, s0, p['gdn_conv_w'][l],
                                    p['gdn_a_log'][l], p['gdn_dt_bias'][l], p['gdn_norm_g'][l])
            conv_new = jnp.concatenate([conv_prev, qkv_c[:, -(GDN_CONV - 1):]], axis=1)[:, -(GDN_CONV - 1):]
        else:
            pt = cache['page_table']
            conv_prev, s0 = cache['conv'][l], cache['gdn'][l]
            mem_kv = cache['mem_kv'][l]
            q_i = jnp.swapaxes(q_i.reshape(IDX_HEADS, b, t, IDX_DIM), 0, 1)
            o_a = nsa_sample_attention(l, q_a.reshape(b, t, W_A), g_a, nsa_new, win_new, cache['nsa_kv'], cache['win_kv'],
                                       pt, p['nsa_cmp_pe'][l], p['nsa_cmp_w1'][l], p['nsa_cmp_w2'][l])
            o_b = dsa_sample_attention(l, q_b.reshape(b, t, W_B), q_i, w_i, dsa_new, idx_new, cache['dsa_kv'],
                                       cache['idx_k'], pt)
            win_out = jnp.concatenate([cache['win_kv'][l], win_new], axis=1)[:, -cache['win_kv'].shape[2]:]
            o_c, s_new, conv_new = gdn_mixer(qkv_c, a_c, b_c, hproj[..., z_off:z_off + W_C], conv_prev, s0,
                                             p['gdn_conv_w'][l], p['gdn_a_log'][l], p['gdn_dt_bias'][l],
                                             p['gdn_norm_g'][l])
        mix = branch_mix(o_a.reshape(b * t, W_A), o_b.reshape(b * t, W_B), o_c.reshape(b * t, W_C),
                         hproj.reshape(b * t, PROJ_WIDTH), PROJ_OFF['merge'],
                         p['w_branch_a'][l], p['w_branch_b'][l], p['w_branch_c'][l])
        x = dense(mix, p['w_mix_out'][l], resid=x).reshape(b, t, D_MODEL)
        x = mem_attention(x, p['norm_mem_g'][l], mem_kv, p['w_mem_q'][l], p['w_mem_o'][l])
        x = dense(ffn_act(x, p['norm_ffn_g'][l], p['w_ffn_gate'][l], p['w_ffn_up'][l]), p['w_ffn_down'][l], resid=x)
        out['nsa_kv'].append(nsa_new)
        out['dsa_kv'].append(dsa_new)
        out['idx_k'].append(idx_new)
        out['win_kv'].append(win_out)
        out['gdn'].append(s_new)
        out['conv'].append(conv_new)
        if prompt:
            out['mem_kv'].append(mem_kv)
    y = rms_norm(x, p['norm_final_g'])
    return y, {nm: jnp.stack(v) for nm, v in out.items()}


def kernel(x_prompt, x_sample, mem_prompt, cache_nsa_kv, cache_dsa_kv, cache_dsa_idx_k, cache_win_kv, cache_mem_kv, state_gdn, state_conv, page_table, norm_mix_g, w_in, nsa_cmp_pe, nsa_cmp_w1, nsa_cmp_w2, gdn_conv_w, gdn_a_log, gdn_dt_bias, gdn_norm_g, w_branch_a, w_branch_b, w_branch_c, w_mix_out, norm_mem_g, mem_norm_g, w_mem_q, w_mem_kv, w_mem_o, norm_ffn_g, w_ffn_gate, w_ffn_up, w_ffn_down, norm_final_g):
    p = dict(norm_mix_g=norm_mix_g, w_in=w_in, nsa_cmp_pe=nsa_cmp_pe, nsa_cmp_w1=nsa_cmp_w1,
             nsa_cmp_w2=nsa_cmp_w2, gdn_conv_w=gdn_conv_w, gdn_a_log=gdn_a_log, gdn_dt_bias=gdn_dt_bias,
             gdn_norm_g=gdn_norm_g, w_branch_a=w_branch_a, w_branch_b=w_branch_b, w_branch_c=w_branch_c,
             w_mix_out=w_mix_out, norm_mem_g=norm_mem_g, mem_norm_g=mem_norm_g, w_mem_q=w_mem_q,
             w_mem_kv=w_mem_kv, w_mem_o=w_mem_o, norm_ffn_g=norm_ffn_g, w_ffn_gate=w_ffn_gate,
             w_ffn_up=w_ffn_up, w_ffn_down=w_ffn_down, norm_final_g=norm_final_g)
    for nm in ('w_branch_a', 'w_branch_b', 'w_branch_c', 'w_mix_out', 'w_mem_q', 'w_mem_kv', 'w_mem_o',
               'w_ffn_gate', 'w_ffn_up', 'w_ffn_down'):
        p[nm] = p[nm].astype(jnp.bfloat16)
    p['w_in'] = permute_w_in(w_in)
    cache = dict(nsa_kv=cache_nsa_kv, dsa_kv=cache_dsa_kv, idx_k=cache_dsa_idx_k, win_kv=cache_win_kv,
                 mem_kv=cache_mem_kv, gdn=state_gdn, conv=state_conv, page_table=page_table)
    y_prompt, sp = run_group(x_prompt, mem_prompt, None, p)
    y_sample, ss = run_group(x_sample, None, cache, p)
    return (y_prompt, y_sample,
            sp['nsa_kv'], sp['dsa_kv'], sp['idx_k'], sp['win_kv'], sp['gdn'], sp['conv'], sp['mem_kv'],
            ss['nsa_kv'], ss['dsa_kv'], ss['idx_k'], ss['win_kv'], ss['gdn'], ss['conv'])
```

```python
import functools
import math

import jax
import jax.numpy as jnp
import numpy as np
from jax import lax
from jax.experimental import pallas as pl
from jax.experimental.pallas import tpu as pltpu

D_MODEL = 2048
DEPTH = 2
PAGE_SIZE = 128
HEAD_DIM = 128
NSA_HEADS = D_MODEL // (4 * HEAD_DIM)
DSA_HEADS = D_MODEL // (4 * HEAD_DIM)
GDN_HEADS = D_MODEL // (2 * HEAD_DIM)
W_A = NSA_HEADS * HEAD_DIM
W_B = DSA_HEADS * HEAD_DIM
W_C = GDN_HEADS * HEAD_DIM
CMP_LEN = 32
CMP_STRIDE = 16
CMP_HID = 2 * HEAD_DIM
SEL_BLOCK = 64
SEL_COUNT = 16
WINDOW = 512
IDX_HEADS = 16
IDX_DIM = 64
DSA_TOPK = 256
GDN_CONV = 4
GDN_CHUNK = 64
MEM_HEADS = 4
ROPE_THETA = 10000.0
Q_BLOCK = 128
EPS = 1e-6
NEG = -1e30
FORCED = 1e9
IN_SPLITS = (W_A, 6 * HEAD_DIM, 3 * NSA_HEADS,
             W_B, 2 * HEAD_DIM, IDX_HEADS * IDX_DIM, IDX_HEADS, IDX_DIM,
             3 * W_C, GDN_HEADS, GDN_HEADS, W_C,
             3 * D_MODEL)

LANE = 128
VMEM_LIMIT = 48 * 1024 * 1024


DENSE_VMEM_BUDGET = 40 * 1024 * 1024
NORM_ROWS = 128


def _divisor_tiles(n, cap):
    return [t for t in range(min(cap, n), 0, -LANE) if t % LANE == 0 and n % t == 0]


def _dense_tiles(m, n, per_row_bytes, per_col_bytes, per_out_bytes):
    for tm in (1024, 512, 256, 128):
        if m % tm:
            continue
        for tn in _divisor_tiles(n, 1024):
            if tm * per_row_bytes + tn * per_col_bytes + tm * tn * per_out_bytes <= DENSE_VMEM_BUDGET:
                return tm, tn
    raise ValueError("no dense tile fits VMEM")


def _stage_rows(x_ref, g_ref, xn_ref):
    def body(r, carry):
        rows = pl.ds(pl.multiple_of(r * NORM_ROWS, NORM_ROWS), NORM_ROWS)
        xf = x_ref[rows, :].astype(jnp.float32)
        if g_ref is not None:
            xf = xf * lax.rsqrt(jnp.mean(xf * xf, axis=-1, keepdims=True) + EPS) * g_ref[...]
        xn_ref[rows, :] = xf.astype(jnp.bfloat16)
        return carry
    lax.fori_loop(0, x_ref.shape[0] // NORM_ROWS, body, 0)


def _dense_kernel(*refs, has_gain, has_resid):
    it = iter(refs)
    x_ref = next(it)
    g_ref = next(it) if has_gain else None
    w_ref = next(it)
    r_ref = next(it) if has_resid else None
    o_ref = next(it)
    xn_ref = next(it)

    @pl.when(pl.program_id(1) == 0)
    def _():
        _stage_rows(x_ref, g_ref, xn_ref)

    acc = jnp.dot(xn_ref[...], w_ref[...], preferred_element_type=jnp.float32)
    if has_resid:
        acc = acc + r_ref[...]
    o_ref[...] = acc.astype(o_ref.dtype)


def dense(x, w, gain=None, resid=None, out_dtype=jnp.float32):
    lead, kdim, n = x.shape[:-1], x.shape[-1], w.shape[-1]
    x2 = x.reshape(-1, kdim)
    m = x2.shape[0]
    ob = jnp.dtype(out_dtype).itemsize
    tm, tn = _dense_tiles(m, n, per_row_bytes=kdim * (2 * x2.dtype.itemsize + 2), per_col_bytes=kdim * 2 * 2,
                          per_out_bytes=2 * ob + (8 if resid is not None else 0))
    args, specs = [x2], [pl.BlockSpec((tm, kdim), lambda i, j: (i, 0))]
    if gain is not None:
        args.append(gain.reshape(1, kdim).astype(jnp.float32))
        specs.append(pl.BlockSpec((1, kdim), lambda i, j: (0, 0)))
    args.append(w)
    specs.append(pl.BlockSpec((kdim, tn), lambda i, j: (0, j)))
    if resid is not None:
        args.append(resid.reshape(m, n))
        specs.append(pl.BlockSpec((tm, tn), lambda i, j: (i, j)))
    out = pl.pallas_call(
        functools.partial(_dense_kernel, has_gain=gain is not None, has_resid=resid is not None),
        grid=(m // tm, n // tn),
        in_specs=specs,
        out_specs=pl.BlockSpec((tm, tn), lambda i, j: (i, j)),
        out_shape=jax.ShapeDtypeStruct((m, n), out_dtype),
        scratch_shapes=[pltpu.VMEM((tm, kdim), jnp.bfloat16)],
        compiler_params=pltpu.CompilerParams(
            dimension_semantics=("parallel", "arbitrary"), vmem_limit_bytes=VMEM_LIMIT),
        name="dense",
    )(*args)
    return out.reshape(*lead, n)


def _ffn_act_kernel(x_ref, g_ref, wg_ref, wu_ref, o_ref, xn_ref):
    @pl.when(pl.program_id(1) == 0)
    def _():
        _stage_rows(x_ref, g_ref, xn_ref)

    xn = xn_ref[...]
    a = jnp.dot(xn, wg_ref[...], preferred_element_type=jnp.float32)
    u = jnp.dot(xn, wu_ref[...], preferred_element_type=jnp.float32)
    o_ref[...] = (a / (1.0 + jnp.exp(-a)) * u).astype(o_ref.dtype)


def ffn_act(x, gain, w_gate, w_up):
    lead, kdim, n = x.shape[:-1], x.shape[-1], w_gate.shape[-1]
    x2 = x.reshape(-1, kdim)
    m = x2.shape[0]
    tm, tn = _dense_tiles(m, n, per_row_bytes=kdim * (2 * 4 + 2), per_col_bytes=2 * kdim * 2 * 2, per_out_bytes=2 * 2)
    out = pl.pallas_call(
        _ffn_act_kernel,
        grid=(m // tm, n // tn),
        in_specs=[pl.BlockSpec((tm, kdim), lambda i, j: (i, 0)),
                  pl.BlockSpec((1, kdim), lambda i, j: (0, 0)),
                  pl.BlockSpec((kdim, tn), lambda i, j: (0, j)),
                  pl.BlockSpec((kdim, tn), lambda i, j: (0, j))],
        out_specs=pl.BlockSpec((tm, tn), lambda i, j: (i, j)),
        out_shape=jax.ShapeDtypeStruct((m, n), jnp.bfloat16),
        scratch_shapes=[pltpu.VMEM((tm, kdim), jnp.bfloat16)],
        compiler_params=pltpu.CompilerParams(
            dimension_semantics=("parallel", "arbitrary"), vmem_limit_bytes=VMEM_LIMIT),
        name="ffn_act",
    )(x2, gain.reshape(1, kdim).astype(jnp.float32), w_gate, w_up)
    return out.reshape(*lead, n)


def _branch_mix_kernel(oa_ref, ob_ref, oc_ref, ma_ref, mb_ref, mc_ref, wa_ref, wb_ref, wc_ref, o_ref, xn_ref):
    @pl.when(pl.program_id(1) == 0)
    def _():
        off = 0
        for src in (oa_ref, ob_ref, oc_ref):
            xn_ref[:, off:off + src.shape[1]] = src[...].astype(jnp.bfloat16)
            off += src.shape[1]

    def gated(m_ref, w_ref, lo, hi):
        y = jnp.dot(xn_ref[:, lo:hi], w_ref[...], preferred_element_type=jnp.float32)
        return y / (1.0 + jnp.exp(-m_ref[...]))

    o_ref[...] = (gated(ma_ref, wa_ref, 0, W_A) + gated(mb_ref, wb_ref, W_A, W_A + W_B)
                  + gated(mc_ref, wc_ref, W_A + W_B, W_A + W_B + W_C)).astype(o_ref.dtype)


def branch_mix(o_a, o_b, o_c, merge, merge_off, w_a, w_b, w_c):
    m, n = merge.shape[0], w_a.shape[-1]
    ktot = W_A + W_B + W_C
    tm, tn = _dense_tiles(m, n, per_row_bytes=ktot * (2 * 4 + 2), per_col_bytes=ktot * 2 * 2, per_out_bytes=2 * 2 + 3 * 8)
    nb = n // tn
    assert merge_off % tn == 0
    mb = merge_off // tn
    row = lambda width: pl.BlockSpec((tm, width), lambda i, j: (i, 0))
    return pl.pallas_call(
        _branch_mix_kernel,
        grid=(m // tm, nb),
        in_specs=[row(W_A), row(W_B), row(W_C),
                  pl.BlockSpec((tm, tn), lambda i, j: (i, mb + j)),
                  pl.BlockSpec((tm, tn), lambda i, j: (i, mb + j + nb)),
                  pl.BlockSpec((tm, tn), lambda i, j: (i, mb + j + 2 * nb)),
                  pl.BlockSpec((W_A, tn), lambda i, j: (0, j)),
                  pl.BlockSpec((W_B, tn), lambda i, j: (0, j)),
                  pl.BlockSpec((W_C, tn), lambda i, j: (0, j))],
        out_specs=pl.BlockSpec((tm, tn), lambda i, j: (i, j)),
        out_shape=jax.ShapeDtypeStruct((m, n), jnp.bfloat16),
        scratch_shapes=[pltpu.VMEM((tm, ktot), jnp.bfloat16)],
        compiler_params=pltpu.CompilerParams(
            dimension_semantics=("parallel", "arbitrary"), vmem_limit_bytes=VMEM_LIMIT),
        name="branch_mix",
    )(o_a, o_b, o_c, merge, merge, merge, w_a, w_b, w_c)


REF_COLS = tuple(zip(('q_a', 'kv_a', 'g_a', 'q_b', 'kv_b', 'q_i', 'w_i', 'k_i', 'qkv_c', 'a_c', 'b_c', 'z_c', 'merge'),
                     IN_SPLITS))
SMALL_COLS = ('k_i', 'g_a', 'w_i', 'a_c', 'b_c')
WIDE_COLS = ('q_a', 'kv_a', 'q_b', 'kv_b', 'q_i', 'qkv_c', 'z_c', 'merge')


def _layout():
    width = dict(REF_COLS)
    assert all(width[n] % LANE == 0 for n in WIDE_COLS) and sum(width[n] for n in SMALL_COLS) <= LANE
    off, pos = {}, 0
    for n in WIDE_COLS:
        off[n] = pos
        pos += width[n]
    off['small'] = pos
    small, spos = {}, 0
    for n in SMALL_COLS:
        small[n] = (spos, width[n])
        spos += width[n]
    return off, small, pos + LANE


PROJ_OFF, SMALL_OFF, PROJ_WIDTH = _layout()
ROPE_COLS = PROJ_OFF['qkv_c']
ROPE_TM = 256


def permute_w_in(w):
    ref_off, pos = {}, 0
    for n, wd in REF_COLS:
        ref_off[n] = (pos, wd)
        pos += wd
    take = lambda n: w[..., ref_off[n][0]:ref_off[n][0] + ref_off[n][1]]
    used = sum(ref_off[n][1] for n in SMALL_COLS)
    pad = jnp.zeros(w.shape[:-1] + (LANE - used,), w.dtype)
    return jnp.concatenate([take(n) for n in WIDE_COLS] + [take(n) for n in SMALL_COLS] + [pad], axis=-1).astype(jnp.bfloat16)


def small_col(hproj, name):
    lo, wd = SMALL_OFF[name]
    return hproj[..., PROJ_OFF['small'] + lo:PROJ_OFF['small'] + lo + wd]


def _rope_tables(pos, rows, head_dim=HEAD_DIM):
    half = head_dim // 2
    inv = ROPE_THETA ** (-jnp.arange(half, dtype=jnp.float32) / half)
    ang = pos.astype(jnp.float32)[:, None] * inv
    cos, sin = jnp.cos(ang), jnp.sin(ang)
    reps = LANE // head_dim
    pad = ((0, rows - pos.shape[0]), (0, 0))
    return (jnp.pad(jnp.tile(jnp.concatenate([cos, cos], axis=-1), (1, reps)), pad),
            jnp.pad(jnp.tile(jnp.concatenate([-sin, sin], axis=-1), (1, reps)), pad))


def _rope_split_kernel(h_ref, small_ref, c128_ref, s128_ref, c64_ref, s64_ref,
                       qa_ref, qb_ref, qi_ref, nsa_ref, win_ref, dsa_ref, idx_ref,
                       kslc_ref, vslc_ref, kwin_ref, vwin_ref, kdsa_ref, vdsa_ref, kidx_ref):
    d, bf = HEAD_DIM, jnp.bfloat16
    c128, s128, c64, s64 = c128_ref[...], s128_ref[...], c64_ref[...], s64_ref[...]
    first_half = (lax.broadcasted_iota(jnp.int32, (1, LANE), 1) & (IDX_DIM - 1)) < IDX_DIM // 2

    def rope128(x):
        return x * c128 + pltpu.roll(x, d // 2, 1) * s128

    def rope64(x):
        rot = jnp.where(first_half, pltpu.roll(x, LANE - IDX_DIM // 2, 1), pltpu.roll(x, IDX_DIM // 2, 1))
        return x * c64 + rot * s64

    col = lambda name, i: h_ref[:, PROJ_OFF[name] + i * d:PROJ_OFF[name] + (i + 1) * d]
    for i in range(NSA_HEADS):
        qa_ref[:, i * d:(i + 1) * d] = (rope128(col('q_a', i)) * (d ** -0.5)).astype(bf)
    for i in range(DSA_HEADS):
        qb_ref[:, i * d:(i + 1) * d] = (rope128(col('q_b', i)) * (d ** -0.5)).astype(bf)
    k_slc, v_slc, k_win, v_win = rope128(col('kv_a', 2)), col('kv_a', 3), rope128(col('kv_a', 4)), col('kv_a', 5)
    for i, part in enumerate((col('kv_a', 0), col('kv_a', 1), k_slc, v_slc)):
        nsa_ref[:, i * d:(i + 1) * d] = part
    win_ref[:, 0:d], win_ref[:, d:2 * d] = k_win, v_win
    kslc_ref[...], vslc_ref[...], kwin_ref[...], vwin_ref[...] = (k_slc.astype(bf), v_slc.astype(bf),
                                                                   k_win.astype(bf), v_win.astype(bf))
    k_dsa, v_dsa = rope128(col('kv_b', 0)), col('kv_b', 1)
    dsa_ref[:, 0:d], dsa_ref[:, d:2 * d] = k_dsa, v_dsa
    kdsa_ref[...], vdsa_ref[...] = k_dsa.astype(bf), v_dsa.astype(bf)
    for i in range(IDX_HEADS * IDX_DIM // LANE):
        pair = rope64(col('q_i', i)).astype(bf)
        qi_ref[2 * i] = pair[:, :IDX_DIM]
        qi_ref[2 * i + 1] = pair[:, IDX_DIM:]
    assert SMALL_OFF['k_i'][0] == 0
    k_idx = rope64(small_ref[...])[:, :IDX_DIM]
    idx_ref[...] = k_idx
    kidx_ref[...] = k_idx.astype(bf)


def rope_split(hproj, pos):
    bg, tg, _ = hproj.shape
    d, bf, f32 = HEAD_DIM, jnp.bfloat16, jnp.float32
    tm = min(ROPE_TM, tg)
    assert tg % tm == 0
    c128, s128 = _rope_tables(pos, tg, d)
    c64, s64 = _rope_tables(pos, tg, IDX_DIM)
    rows = lambda w: pl.BlockSpec((None, tm, w), lambda b, i: (b, i, 0))
    table = pl.BlockSpec((tm, LANE), lambda b, i: (i, 0))
    shape = lambda w, dt: jax.ShapeDtypeStruct((bg, tg, w), dt)
    outs = [(W_A, bf), (W_B, bf), None, (NSA_ROWS * d, f32), (2 * d, f32), (2 * d, f32), (IDX_DIM, f32),
            (d, bf), (d, bf), (d, bf), (d, bf), (d, bf), (d, bf), (IDX_DIM, bf)]
    out_specs = [pl.BlockSpec((None, IDX_HEADS, tm, IDX_DIM), lambda b, i: (b, 0, i, 0)) if o is None else rows(o[0])
                 for o in outs]
    out_shape = [jax.ShapeDtypeStruct((bg, IDX_HEADS, tg, IDX_DIM), bf) if o is None else shape(*o) for o in outs]
    return pl.pallas_call(
        _rope_split_kernel,
        grid=(bg, tg // tm),
        in_specs=[pl.BlockSpec((None, tm, ROPE_COLS), lambda b, i: (b, i, 0)),
                  pl.BlockSpec((None, tm, LANE), lambda b, i: (b, i, PROJ_OFF['small'] // LANE)),
                  table, table, table, table],
        out_specs=out_specs, out_shape=out_shape,
        compiler_params=pltpu.CompilerParams(
            dimension_semantics=("parallel", "parallel"), vmem_limit_bytes=VMEM_LIMIT),
        name="rope_split",
    )(hproj, hproj, c128, s128, c64, s64)


_NT = (((1,), (1,)), ((), ()))


def _softmax_tiles(qh, k_ref, v_ref, lo, hi, tk, mask_fn, m_ref, l_ref, acc_ref, nh, tq):
    assert qh.shape[-1] == LANE and tk % LANE == 0
    m_ref[...] = jnp.full(m_ref.shape, NEG, jnp.float32)
    l_ref[...] = jnp.zeros(l_ref.shape, jnp.float32)
    acc_ref[...] = jnp.zeros(acc_ref.shape, jnp.float32)

    def body(kt, carry):
        off = pl.multiple_of(kt * tk, tk)
        k = k_ref[0, pl.ds(off, tk), :]
        v = v_ref[0, pl.ds(off, tk), :]
        bias = (mask_fn(kt) - 1.0) * (-NEG)
        heads = range(nh)
        chunks = []
        for h in heads:
            s = lax.dot_general(qh[h * tq:(h + 1) * tq], k, _NT, preferred_element_type=jnp.float32) + bias
            chunks.append([s[:, c * LANE:(c + 1) * LANE] for c in range(tk // LANE)])
        m_old = [m_ref[h] for h in heads]
        m_new = [jnp.maximum(m_old[h], jnp.max(functools.reduce(jnp.maximum, chunks[h]), axis=-1, keepdims=True))
                 for h in heads]
        ps = [[jnp.exp(ch - m_new[h]) for ch in chunks[h]] for h in heads]
        alpha = [jnp.exp(m_old[h] - m_new[h]) for h in heads]
        pv = [jnp.dot(jnp.concatenate(ps[h], axis=-1).astype(jnp.bfloat16), v, preferred_element_type=jnp.float32)
              for h in heads]
        for h in heads:
            l_ref[h] = alpha[h] * l_ref[h] + functools.reduce(jnp.add, ps[h])
            acc_ref[h] = alpha[h] * acc_ref[h] + pv[h]
            m_ref[h] = m_new[h]
        return carry

    lax.fori_loop(lo, hi, body, 0)
    return acc_ref[...] / jnp.maximum(jnp.sum(l_ref[...], axis=-1, keepdims=True), 1e-30)


def _split3_bf16(x):
    hi = x.astype(jnp.bfloat16)
    r1 = x - hi.astype(jnp.float32)
    mid = r1.astype(jnp.bfloat16)
    lo = (r1 - mid.astype(jnp.float32)).astype(jnp.bfloat16)
    return hi, mid, lo


def _select_blocks(psum, share, qpos_r, n_blk, tq):
    nbp, tqp = share.shape[0], psum.shape[0]
    imp_t = sum(lax.dot_general(share, part, _NT, preferred_element_type=jnp.float32)
                for part in _split3_bf16(psum))
    blk = lax.broadcasted_iota(jnp.int32, (nbp, tqp), 0)
    forced = (blk == jnp.right_shift(qpos_r, 6)) | (blk == 0)
    future = blk * SEL_BLOCK > qpos_r
    imp_t = jnp.where(forced, FORCED, jnp.where(future, -1.0, imp_t))
    imp_t = jnp.where(blk < n_blk, imp_t, -2.0)
    rank = jnp.zeros((nbp, tqp), jnp.float32)
    for i in range(n_blk):
        row = imp_t[i:i + 1, :]
        beats = (row > imp_t) | ((row == imp_t) & (blk > i))
        rank = rank + beats.astype(jnp.float32)
    sel_t = (rank < float(min(SEL_COUNT, n_blk))).astype(jnp.bfloat16)
    eye = (lax.broadcasted_iota(jnp.int32, (tq, tqp), 0) == lax.broadcasted_iota(jnp.int32, (tq, tqp), 1))
    return lax.dot_general(eye.astype(jnp.bfloat16), sel_t, _NT,
                           preferred_element_type=jnp.float32).astype(jnp.bfloat16)


def _softmax_flat(qh, tiles, nh, tq):
    scores = []
    for k, _, maskf in tiles:
        s = lax.dot_general(qh, k, _NT, preferred_element_type=jnp.float32).reshape(nh, tq, k.shape[0])
        scores.append(s + ((maskf - 1.0) * (-NEG))[None])
    m = scores[0]
    for s in scores[1:]:
        m = jnp.maximum(m, s)
    m = jnp.max(m, axis=-1, keepdims=True)
    lsum, acc = None, None
    for s, (_, v, maskf) in zip(scores, tiles):
        p = jnp.exp(s - m) * maskf[None]
        pv = jnp.dot(p.reshape(nh * tq, p.shape[-1]).astype(jnp.bfloat16), v, preferred_element_type=jnp.float32)
        lsum = p if lsum is None else lsum + p
        acc = pv if acc is None else acc + pv
    l = jnp.sum(lsum, axis=-1, keepdims=True)
    return acc.reshape(nh, tq, acc.shape[-1]) / jnp.maximum(l, 1e-30)


NSA_TQ = 128
NSA_TK_SLC = 512
NSA_TK_WIN = 128


def _nsa_prompt_kernel(q_ref, g_ref, kc_ref, vc_ref, ks_ref, vs_ref, kw_ref, vw_ref, share_ref, o_ref,
                       m_ref, l_ref, acc_ref, *, n_cmp, n_blk, n_keys):
    nh, tq, d = NSA_HEADS, NSA_TQ, HEAD_DIM
    nbp = share_ref.shape[0]
    ncp = share_ref.shape[1]
    start = pl.program_id(1) * tq
    qpos_c = start + lax.broadcasted_iota(jnp.int32, (tq, 1), 0)
    qpos_r = start + lax.broadcasted_iota(jnp.int32, (1, tq), 1)

    qh = jnp.concatenate([q_ref[0, :, h * d:(h + 1) * d] for h in range(nh)], axis=0)

    c_r = lax.broadcasted_iota(jnp.int32, (1, ncp), 1)
    mask_c = ((c_r * CMP_STRIDE + (CMP_LEN - 1) <= qpos_c) & (c_r < n_cmp)).astype(jnp.float32)
    s = lax.dot_general(qh, kc_ref[0], _NT, preferred_element_type=jnp.float32).reshape(nh, tq, ncp)
    sm = s + ((mask_c - 1.0) * (-NEG))[None]
    e = jnp.exp(sm - jnp.max(sm, axis=-1, keepdims=True)) * mask_c[None]
    p_cmp = e / jnp.maximum(jnp.sum(e, axis=-1, keepdims=True), 1e-30)
    o_cmp = jnp.dot(p_cmp.reshape(nh * tq, ncp).astype(jnp.bfloat16), vc_ref[0],
                    preferred_element_type=jnp.float32)

    sel = _select_blocks(jnp.sum(p_cmp, axis=0), share_ref[...], qpos_r, n_blk, tq)

    def slc_mask(kt):
        kpos = kt * NSA_TK_SLC + lax.broadcasted_iota(jnp.int32, (1, NSA_TK_SLC), 1)
        kblk = kt * (NSA_TK_SLC // SEL_BLOCK) + jnp.right_shift(
            lax.broadcasted_iota(jnp.int32, (nbp, NSA_TK_SLC), 1), 6)
        expand = (kblk == lax.broadcasted_iota(jnp.int32, (nbp, NSA_TK_SLC), 0)).astype(jnp.bfloat16)
        chosen = jnp.dot(sel, expand, preferred_element_type=jnp.float32)
        return chosen * (kpos <= qpos_c).astype(jnp.float32)

    hi_slc = jnp.minimum((start + tq - 1) // NSA_TK_SLC + 1, n_keys // NSA_TK_SLC)
    o_slc = _softmax_tiles(qh, ks_ref, vs_ref, 0, hi_slc, NSA_TK_SLC, slc_mask, m_ref, l_ref, acc_ref, nh, tq)

    tiles = []
    for i in range(WINDOW // NSA_TK_WIN + 1):
        kt = pl.program_id(1) * (tq // NSA_TK_WIN) - WINDOW // NSA_TK_WIN + i
        off = pl.multiple_of(jnp.maximum(kt, 0) * NSA_TK_WIN, NSA_TK_WIN)
        kpos = kt * NSA_TK_WIN + lax.broadcasted_iota(jnp.int32, (1, NSA_TK_WIN), 1)
        dlt = qpos_c - kpos
        maskf = ((dlt >= 0) & (dlt < WINDOW) & (kpos >= 0)).astype(jnp.float32)
        tiles.append((kw_ref[0, pl.ds(off, NSA_TK_WIN), :], vw_ref[0, pl.ds(off, NSA_TK_WIN), :], maskf))
    o_win = _softmax_flat(qh, tiles, nh, tq)

    gate = 1.0 / (1.0 + jnp.exp(-g_ref[0]))
    o_cmp = o_cmp.reshape(nh, tq, d)
    for h in range(nh):
        o_ref[0, :, h * d:(h + 1) * d] = (gate[:, 3 * h:3 * h + 1] * o_cmp[h]
                                          + gate[:, 3 * h + 1:3 * h + 2] * o_slc[h]
                                          + gate[:, 3 * h + 2:3 * h + 3] * o_win[h])


def _share_matrix_t(n_cmp, n_blk, ncp, nbp):
    c0 = np.arange(n_cmp)[None, :] * CMP_STRIDE
    j0 = np.arange(n_blk)[:, None] * SEL_BLOCK
    share = np.clip(np.minimum(c0 + CMP_LEN, j0 + SEL_BLOCK) - np.maximum(c0, j0), 0, None) / CMP_LEN
    out = np.zeros((nbp, ncp), np.float32)
    out[:n_blk, :n_cmp] = share
    return jnp.asarray(out, jnp.bfloat16)


def nsa_prompt_attention(q, gates, k_cmp, v_cmp, k_slc, v_slc, k_win, v_win):
    b, t, _ = q.shape
    d = HEAD_DIM
    n_cmp = k_cmp.shape[1]
    n_blk = -(-t // SEL_BLOCK)
    ncp = -(-n_cmp // LANE) * LANE
    nbp = -(-n_blk // 16) * 16
    assert t % NSA_TK_SLC == 0 and t % NSA_TQ == 0
    bf = jnp.bfloat16
    cpad = ((0, 0), (0, ncp - n_cmp), (0, 0))
    kc, vc = jnp.pad(k_cmp, cpad).astype(bf), jnp.pad(v_cmp, cpad).astype(bf)
    whole = lambda n: pl.BlockSpec((1, n, d), lambda bi, qi: (bi, 0, 0))
    return pl.pallas_call(
        functools.partial(_nsa_prompt_kernel, n_cmp=n_cmp, n_blk=n_blk, n_keys=t),
        grid=(b, t // NSA_TQ),
        in_specs=[pl.BlockSpec((1, NSA_TQ, NSA_HEADS * d), lambda bi, qi: (bi, qi, 0)),
                  pl.BlockSpec((1, NSA_TQ, 3 * NSA_HEADS), lambda bi, qi: (bi, qi, 0)),
                  whole(ncp), whole(ncp), whole(t), whole(t), whole(t), whole(t),
                  pl.BlockSpec((nbp, ncp), lambda bi, qi: (0, 0))],
        out_specs=pl.BlockSpec((1, NSA_TQ, NSA_HEADS * d), lambda bi, qi: (bi, qi, 0)),
        out_shape=jax.ShapeDtypeStruct((b, t, NSA_HEADS * d), jnp.float32),
        scratch_shapes=[pltpu.VMEM((NSA_HEADS, NSA_TQ, LANE), jnp.float32),
                        pltpu.VMEM((NSA_HEADS, NSA_TQ, LANE), jnp.float32),
                        pltpu.VMEM((NSA_HEADS, NSA_TQ, d), jnp.float32)],
        compiler_params=pltpu.CompilerParams(
            dimension_semantics=("parallel", "arbitrary"), vmem_limit_bytes=VMEM_LIMIT),
        name="nsa_prompt",
    )(q, gates, kc, vc, k_slc.astype(bf), v_slc.astype(bf), k_win.astype(bf), v_win.astype(bf),
      _share_matrix_t(n_cmp, n_blk, ncp, nbp))


NSA_ROWS = 4


def _gelu_tanh(x):
    return 0.5 * x * (1.0 + jnp.tanh(math.sqrt(2.0 / math.pi) * (x + 0.044715 * (x * x * x))))


def _pad_rows(x, rows):
    if rows == x.shape[0]:
        return x
    return jnp.concatenate([x, jnp.zeros((rows - x.shape[0], x.shape[1]), x.dtype)], axis=0)


def _gated_store(o_ref, g_ref, branches, nh, d):
    gate = 1.0 / (1.0 + jnp.exp(-g_ref[...]))
    for h in range(nh):
        o_ref[:, h * d:(h + 1) * d] = sum(gate[:, 3 * h + i:3 * h + i + 1] * br[h] for i, br in enumerate(branches))


def _nsa_sample_kernel(pt_ref, q_ref, g_ref, new_ref, winc_ref, winn_ref, pe_ref, w1_ref, w2_ref, cos_ref, sin_ref,
                       share_ref, *rest, past, n_cmp, n_blk):
    del pt_ref
    pages, o_ref = rest[:-1], rest[-1]
    nh, d, pg = NSA_HEADS, HEAD_DIM, PAGE_SIZE
    tq = q_ref.shape[0]
    nbp, ncp = share_ref.shape
    bf = jnp.bfloat16
    seg_per_page = pg // CMP_STRIDE
    assert len(pages) * seg_per_page == ncp == LANE and CMP_LEN == 2 * CMP_STRIDE
    qpos_c = past + lax.broadcasted_iota(jnp.int32, (tq, 1), 0)
    qpos_r = past + lax.broadcasted_iota(jnp.int32, (1, LANE), 1)
    lane = lax.broadcasted_iota(jnp.int32, (1, LANE), 1)

    def compress(kv):
        first = jnp.zeros((ncp, CMP_HID), jnp.float32)
        second = jnp.zeros((ncp, CMP_HID), jnp.float32)

        def seg_rows(r):
            return jnp.concatenate([p[pl.ds(NSA_ROWS * r + kv, seg_per_page, stride=NSA_ROWS * CMP_STRIDE), :]
                                    for p in pages], axis=0)

        for r in range(0, CMP_STRIDE, 2):
            ya, yb = seg_rows(r), seg_rows(r + 1)
            for acc_off, name in ((0, 'first'), (CMP_STRIDE, 'second')):
                lhs = jnp.concatenate([ya + pe_ref[kv, acc_off + r:acc_off + r + 1, :],
                                       yb + pe_ref[kv, acc_off + r + 1:acc_off + r + 2, :]], axis=1).astype(bf)
                w = w1_ref[kv, acc_off + r:acc_off + r + 2].reshape(2 * d, CMP_HID)
                y = jnp.dot(lhs, w, preferred_element_type=jnp.float32)
                if name == 'first':
                    first = first + y
                else:
                    second = second + y
        hid = first + pltpu.roll(second, ncp - 1, 0)
        return jnp.dot(_gelu_tanh(hid).astype(bf), w2_ref[kv], preferred_element_type=jnp.float32)

    kc = compress(0)
    kc = (kc * cos_ref[...] + pltpu.roll(kc, d // 2, 1) * sin_ref[...]).astype(bf)
    vc = compress(1).astype(bf)

    q = q_ref[...].astype(jnp.float32)
    qh = jnp.concatenate([q[:, h * d:(h + 1) * d] for h in range(nh)], axis=0).astype(bf)

    mask_c = ((lane * CMP_STRIDE + (CMP_LEN - 1) <= qpos_c) & (lane < n_cmp)).astype(jnp.float32)
    s = lax.dot_general(qh, kc, _NT, preferred_element_type=jnp.float32).reshape(nh, tq, ncp)
    sm = s + ((mask_c - 1.0) * (-NEG))[None]
    e = jnp.exp(sm - jnp.max(sm, axis=-1, keepdims=True)) * mask_c[None]
    p_cmp = e / jnp.maximum(jnp.sum(e, axis=-1, keepdims=True), 1e-30)
    o_cmp = jnp.dot(p_cmp.reshape(nh * tq, ncp).astype(bf), vc, preferred_element_type=jnp.float32).reshape(nh, tq, d)

    sel = _select_blocks(_pad_rows(jnp.sum(p_cmp, axis=0), LANE), share_ref[...], qpos_r, n_blk, LANE)

    def slc_mask(key0):
        kblk = key0 // SEL_BLOCK + jnp.right_shift(lax.broadcasted_iota(jnp.int32, (nbp, LANE), 1), 6)
        expand = (kblk == lax.broadcasted_iota(jnp.int32, (nbp, LANE), 0)).astype(bf)
        chosen = jnp.dot(sel, expand, preferred_element_type=jnp.float32)[:tq]
        return chosen * (key0 + lane <= qpos_c).astype(jnp.float32)

    def component(ref, comp, n_comp, row0, rows):
        return ref[pl.ds(n_comp * row0 + comp, rows, stride=n_comp), :]

    tiles = [(component(p, 2, NSA_ROWS, 0, pg).astype(bf), component(p, 3, NSA_ROWS, 0, pg).astype(bf), slc_mask(j * pg))
             for j, p in enumerate(pages)]
    tiles.append((_pad_rows(component(new_ref, 2, NSA_ROWS, 0, tq), LANE).astype(bf),
                  _pad_rows(component(new_ref, 3, NSA_ROWS, 0, tq), LANE).astype(bf), slc_mask(past)))
    o_slc = _softmax_flat(qh, tiles, nh, tq)

    def win_mask(key0):
        dlt = qpos_c - (key0 + lane)
        return ((dlt >= 0) & (dlt < WINDOW)).astype(jnp.float32)

    n_wc = winc_ref.shape[0] // 2
    tiles = [(component(winc_ref, 0, 2, j * LANE, LANE).astype(bf), component(winc_ref, 1, 2, j * LANE, LANE).astype(bf),
              win_mask(past - n_wc + j * LANE)) for j in range(n_wc // LANE)]
    tiles.append((_pad_rows(component(winn_ref, 0, 2, 0, tq), LANE).astype(bf),
                  _pad_rows(component(winn_ref, 1, 2, 0, tq), LANE).astype(bf), win_mask(past)))
    o_win = _softmax_flat(qh, tiles, nh, tq)

    _gated_store(o_ref, g_ref, (o_cmp, o_slc, o_win), nh, d)


def nsa_sample_attention(layer, q, gates, nsa_new, win_new, pool, win_cache, page_table, pe, w1, w2):
    b, t, _ = q.shape
    wlen = win_cache.shape[2]
    nsa_new = nsa_new.reshape(b, t * NSA_ROWS, HEAD_DIM)
    win_new = win_new.reshape(b, t * 2, HEAD_DIM)
    pool = pool.reshape(pool.shape[0], pool.shape[1], PAGE_SIZE * NSA_ROWS, HEAD_DIM)
    win_cache = win_cache.reshape(win_cache.shape[0], b, wlen * 2, HEAD_DIM)
    d, n_pages = HEAD_DIM, page_table.shape[1]
    past = n_pages * PAGE_SIZE
    n_keys = past + t
    n_cmp = (n_keys - CMP_LEN) // CMP_STRIDE + 1
    n_blk = -(-n_keys // SEL_BLOCK)
    ncp = -(-n_cmp // LANE) * LANE
    nbp = -(-n_blk // 16) * 16
    assert (n_cmp + 1) * CMP_STRIDE <= past, "compressed tokens must come from cached rows only"
    cos, sin = _rope_tables(jnp.arange(n_cmp) * CMP_STRIDE + CMP_LEN - 1, ncp)
    bf = jnp.bfloat16
    per_seq = lambda *tail: pl.BlockSpec((None,) + tail, lambda i, pt: (i,) + (0,) * len(tail))
    const = lambda *shape: pl.BlockSpec(shape, lambda i, pt: (0,) * len(shape))
    page = lambda j: pl.BlockSpec((None, None, PAGE_SIZE * NSA_ROWS, d), lambda i, pt: (layer, pt[i, j], 0, 0))
    grid_spec = pltpu.PrefetchScalarGridSpec(
        num_scalar_prefetch=1, grid=(b,),
        in_specs=[per_seq(t, NSA_HEADS * d), per_seq(t, 3 * NSA_HEADS), per_seq(t * NSA_ROWS, d),
                  pl.BlockSpec((None, None, wlen * 2, d), lambda i, pt: (layer, i, 0, 0)),
                  per_seq(t * 2, d),
                  const(2, CMP_LEN, d), const(2, CMP_LEN, d, CMP_HID), const(2, CMP_HID, d),
                  const(ncp, d), const(ncp, d), const(nbp, ncp)] + [page(j) for j in range(n_pages)],
        out_specs=per_seq(t, NSA_HEADS * d))
    return pl.pallas_call(
        functools.partial(_nsa_sample_kernel, past=past, n_cmp=n_cmp, n_blk=n_blk),
        grid_spec=grid_spec,
        out_shape=jax.ShapeDtypeStruct((b, t, NSA_HEADS * d), jnp.float32),
        compiler_params=pltpu.CompilerParams(dimension_semantics=("parallel",), vmem_limit_bytes=VMEM_LIMIT),
        name="nsa_sample",
    )(page_table, q, gates, nsa_new, win_cache, win_new, pe, w1.reshape(2, CMP_LEN, d, CMP_HID).astype(bf),
      w2.astype(bf), cos, sin, _share_matrix_t(n_cmp, n_blk, ncp, nbp), *([pool] * n_pages))


DSA_TQ = 128
DSA_TK = 256
INT_MIN = -2 ** 31


def _dsa_prompt_kernel(q_ref, qi_ref, wt_ref, k_ref, v_ref, ki_ref, tri_ref, o_ref,
                       key_ref, sel_ref, m_ref, l_ref, acc_ref, *, n_keys, topk):
    nh, tq, d, tk = DSA_HEADS, DSA_TQ, HEAD_DIM, DSA_TK
    start = pl.program_id(1) * tq
    qpos_r = start + lax.broadcasted_iota(jnp.int32, (1, tq), 1)
    n_kt = jnp.minimum((start + tq - 1) // tk + 1, n_keys // tk)

    wt = wt_ref[0] * (IDX_HEADS ** -0.5 * IDX_DIM ** -0.5)

    def score_body(kt, carry):
        off = pl.multiple_of(kt * tk, tk)
        ki = ki_ref[0, pl.ds(off, tk), :]
        acc = jnp.zeros((tk, tq), jnp.float32)
        for h in range(0, IDX_HEADS, 2):
            pair = qi_ref[0, h:h + 2].reshape(2 * tq, IDX_DIM)
            dots = lax.dot_general(ki, pair, _NT, preferred_element_type=jnp.float32)
            acc = (acc + jnp.maximum(dots[:, :tq], 0.0) * wt[h:h + 1, :]
                   + jnp.maximum(dots[:, tq:], 0.0) * wt[h + 1:h + 2, :])
        kpos_c = off + lax.broadcasted_iota(jnp.int32, (tk, 1), 0)
        score = jnp.where(kpos_c <= qpos_r, acc, NEG)
        bits = pltpu.bitcast(score, jnp.int32)
        key_ref[kt] = bits ^ (jnp.right_shift(bits, 31) & 0x7FFFFFFF)
        return carry

    lax.fori_loop(0, n_kt, score_body, 0)

    def count(pred):
        def body(kt, part):
            hit = pred(key_ref[kt]).astype(jnp.int32)
            return part + jnp.sum(hit.reshape(tk // 8, 8, tq), axis=0)
        part = lax.fori_loop(0, n_kt, body, jnp.zeros((8, tq), jnp.int32))
        return jnp.sum(part, axis=0, keepdims=True)

    thr = jnp.where(count(lambda kk: kk >= 0) >= topk, 0, INT_MIN).astype(jnp.int32)

    def bit_body(i, t):
        cand = t | jnp.left_shift(jnp.int32(1), 30 - i)
        return jnp.where(count(lambda kk: kk >= cand) >= topk, cand, t)

    thr = lax.fori_loop(0, 31, bit_body, thr)
    room = (topk - count(lambda kk: kk > thr)).astype(jnp.float32)

    def sel_body(kt, seen):
        kk = key_ref[kt]
        eq = (kk == thr).astype(jnp.float32)
        prefix = jnp.dot(tri_ref[...], eq.astype(jnp.bfloat16), preferred_element_type=jnp.float32)
        keep = (kk > thr) | ((eq > 0.0) & (prefix + seen <= room))
        kpos_c = kt * tk + lax.broadcasted_iota(jnp.int32, (tk, 1), 0)
        sel_ref[kt] = (keep & (kpos_c <= qpos_r)).astype(jnp.bfloat16)
        return seen + jnp.sum(eq, axis=0, keepdims=True)

    lax.fori_loop(0, n_kt, sel_body, jnp.zeros((1, tq), jnp.float32))

    eye = (lax.broadcasted_iota(jnp.int32, (tq, tq), 0)
           == lax.broadcasted_iota(jnp.int32, (tq, tq), 1)).astype(jnp.bfloat16)

    def sel_mask(kt):
        return lax.dot_general(eye, sel_ref[kt], _NT, preferred_element_type=jnp.float32)

    qh = jnp.concatenate([q_ref[0, :, h * d:(h + 1) * d] for h in range(nh)], axis=0)
    o = _softmax_tiles(qh, k_ref, v_ref, 0, n_kt, tk, sel_mask, m_ref, l_ref, acc_ref, nh, tq)
    for h in range(nh):
        o_ref[0, :, h * d:(h + 1) * d] = o[h]


def dsa_prompt_attention(q, q_idx, w_idx, k, v, k_idx):
    b, t, _ = q.shape
    d = HEAD_DIM
    topk = min(DSA_TOPK, t // 4)
    assert t % DSA_TK == 0 and t % DSA_TQ == 0
    bf = jnp.bfloat16
    tri = jnp.asarray(np.tril(np.ones((DSA_TK, DSA_TK), np.float32)), bf)
    whole = lambda n, w: pl.BlockSpec((1, n, w), lambda bi, qi: (bi, 0, 0))
    return pl.pallas_call(
        functools.partial(_dsa_prompt_kernel, n_keys=t, topk=topk),
        grid=(b, t // DSA_TQ),
        in_specs=[pl.BlockSpec((1, DSA_TQ, DSA_HEADS * d), lambda bi, qi: (bi, qi, 0)),
                  pl.BlockSpec((1, IDX_HEADS, DSA_TQ, IDX_DIM), lambda bi, qi: (bi, 0, qi, 0)),
                  pl.BlockSpec((1, IDX_HEADS, DSA_TQ), lambda bi, qi: (bi, 0, qi)),
                  whole(t, d), whole(t, d), whole(t, IDX_DIM),
                  pl.BlockSpec((DSA_TK, DSA_TK), lambda bi, qi: (0, 0))],
        out_specs=pl.BlockSpec((1, DSA_TQ, DSA_HEADS * d), lambda bi, qi: (bi, qi, 0)),
        out_shape=jax.ShapeDtypeStruct((b, t, DSA_HEADS * d), jnp.float32),
        scratch_shapes=[pltpu.VMEM((t // DSA_TK, DSA_TK, DSA_TQ), jnp.int32),
                        pltpu.VMEM((t // DSA_TK, DSA_TK, DSA_TQ), bf),
                        pltpu.VMEM((DSA_HEADS, DSA_TQ, LANE), jnp.float32),
                        pltpu.VMEM((DSA_HEADS, DSA_TQ, LANE), jnp.float32),
                        pltpu.VMEM((DSA_HEADS, DSA_TQ, d), jnp.float32)],
        compiler_params=pltpu.CompilerParams(
            dimension_semantics=("parallel", "arbitrary"), vmem_limit_bytes=VMEM_LIMIT),
        name="dsa_prompt",
    )(q, q_idx.astype(bf), jnp.swapaxes(w_idx, 1, 2),
      k.astype(bf), v.astype(bf), k_idx.astype(bf), tri)


def _dsa_sample_kernel(pt_ref, q_ref, qi_ref, w_ref, new_ref, inew_ref, tri_ref, *rest, past, topk):
    del pt_ref
    n_pages = (len(rest) - 1) // 2
    kv_pages, idx_pages, o_ref = rest[:n_pages], rest[n_pages:2 * n_pages], rest[-1]
    nh, d, pg = DSA_HEADS, HEAD_DIM, PAGE_SIZE
    tq = q_ref.shape[0]
    bf = jnp.bfloat16
    qpos_c = past + lax.broadcasted_iota(jnp.int32, (tq, 1), 0)
    lane = lax.broadcasted_iota(jnp.int32, (1, LANE), 1)

    qi = qi_ref[...]
    wb = jnp.broadcast_to(w_ref[...] * (IDX_HEADS ** -0.5 * IDX_DIM ** -0.5), (IDX_HEADS * tq, LANE))

    def key_tile(ki, key0):
        dots = lax.dot_general(qi, ki, _NT, preferred_element_type=jnp.float32)
        score = jnp.sum((jnp.maximum(dots, 0.0) * wb).reshape(IDX_HEADS, tq, LANE), axis=0)
        score = jnp.where(key0 + lane <= qpos_c, score, NEG)
        bits = pltpu.bitcast(score, jnp.int32)
        return bits ^ (jnp.right_shift(bits, 31) & 0x7FFFFFFF)

    keys = [key_tile(p[...].astype(bf), j * pg) for j, p in enumerate(idx_pages)]
    keys.append(key_tile(_pad_rows(inew_ref[...], LANE).astype(bf), past))

    def count(pred):
        hits = pred(keys[0]).astype(jnp.int32)
        for kk in keys[1:]:
            hits = hits + pred(kk).astype(jnp.int32)
        return jnp.sum(hits, axis=-1, keepdims=True)

    thr = jnp.where(count(lambda kk: kk >= 0) >= topk, 0, INT_MIN).astype(jnp.int32)

    def bit_body(i, t):
        cand = t | jnp.left_shift(jnp.int32(1), 30 - i)
        return jnp.where(count(lambda kk: kk >= cand) >= topk, cand, t)

    thr = lax.fori_loop(0, 31, bit_body, thr)
    room = (topk - count(lambda kk: kk > thr)).astype(jnp.float32)
    eqs = [(kk == thr).astype(jnp.float32) for kk in keys]
    totals = [jnp.sum(e, axis=-1, keepdims=True) for e in eqs]
    seen = jnp.zeros((tq, 1), jnp.float32)
    masks = []
    for j, (kk, e) in enumerate(zip(keys, eqs)):
        prefix = jnp.dot(e, tri_ref[...], preferred_element_type=jnp.float32)
        keep = (kk > thr) | ((e > 0.0) & (prefix + seen <= room))
        key0 = j * pg if j < n_pages else past
        masks.append((keep & (key0 + lane <= qpos_c)).astype(jnp.float32))
        seen = seen + totals[j]

    def component(ref, comp, rows):
        return ref[pl.ds(comp, rows, stride=2), :]

    tiles = [(component(p, 0, pg).astype(bf), component(p, 1, pg).astype(bf), masks[j]) for j, p in enumerate(kv_pages)]
    tiles.append((_pad_rows(component(new_ref, 0, tq), LANE).astype(bf),
                  _pad_rows(component(new_ref, 1, tq), LANE).astype(bf), masks[n_pages]))
    q = q_ref[...].astype(jnp.float32)
    qh = jnp.concatenate([q[:, h * d:(h + 1) * d] for h in range(nh)], axis=0).astype(bf)
    o = _softmax_flat(qh, tiles, nh, tq)
    for h in range(nh):
        o_ref[:, h * d:(h + 1) * d] = o[h]


def dsa_sample_attention(layer, q, q_idx, w_idx, dsa_new, idx_new, pool, idx_pool, page_table):
    b, t, _ = q.shape
    d, n_pages = HEAD_DIM, page_table.shape[1]
    past = n_pages * PAGE_SIZE
    topk = min(DSA_TOPK, (past + t) // 4)
    pool = pool.reshape(pool.shape[0], pool.shape[1], PAGE_SIZE * 2, d)
    tri = jnp.asarray(np.triu(np.ones((LANE, LANE), np.float32)))
    per_seq = lambda *tail: pl.BlockSpec((None,) + tail, lambda i, pt: (i,) + (0,) * len(tail))
    kv_page = lambda j: pl.BlockSpec((None, None, PAGE_SIZE * 2, d), lambda i, pt: (layer, pt[i, j], 0, 0))
    idx_page = lambda j: pl.BlockSpec((None, None, PAGE_SIZE, IDX_DIM), lambda i, pt: (layer, pt[i, j], 0, 0))
    grid_spec = pltpu.PrefetchScalarGridSpec(
        num_scalar_prefetch=1, grid=(b,),
        in_specs=[per_seq(t, DSA_HEADS * d), per_seq(IDX_HEADS * t, IDX_DIM), per_seq(IDX_HEADS * t, 1),
                  per_seq(t * 2, d), per_seq(t, IDX_DIM), pl.BlockSpec((LANE, LANE), lambda i, pt: (0, 0))]
                 + [kv_page(j) for j in range(n_pages)] + [idx_page(j) for j in range(n_pages)],
        out_specs=per_seq(t, DSA_HEADS * d))
    return pl.pallas_call(
        functools.partial(_dsa_sample_kernel, past=past, topk=topk),
        grid_spec=grid_spec,
        out_shape=jax.ShapeDtypeStruct((b, t, DSA_HEADS * d), jnp.float32),
        compiler_params=pltpu.CompilerParams(dimension_semantics=("parallel",), vmem_limit_bytes=VMEM_LIMIT),
        name="dsa_sample",
    )(page_table, q, q_idx.reshape(b, IDX_HEADS * t, IDX_DIM).astype(jnp.bfloat16),
      jnp.swapaxes(w_idx, 1, 2).reshape(b, IDX_HEADS * t, 1), dsa_new.reshape(b, t * 2, d), idx_new, tri,
      *([pool] * n_pages), *([idx_pool] * n_pages))


GDN_GROUP = 256
GDN_SUB = 16
GDN_HB = 4
CONV_PAD = 8


def _hp_dot(a, b):
    bf = jnp.bfloat16
    ah, bh = a.astype(bf), b.astype(bf)
    al, bl = (a - ah.astype(jnp.float32)).astype(bf), (b - bh.astype(jnp.float32)).astype(bf)
    dot = functools.partial(jnp.dot, preferred_element_type=jnp.float32)
    return dot(ah, bh) + dot(ah, bl) + dot(al, bh)


def _unit_lower_inverse(a, row, col):
    assert GDN_SUB == 16 and GDN_CHUNK == 64
    n = range(len(a))
    eye = (row == col).astype(jnp.float32)
    sub = jnp.right_shift(row, 4) == jnp.right_shift(col, 4)
    a16 = [jnp.where(sub, a[i], 0.0) for i in n]
    t16 = [eye - a16[i] for i in n]
    power = a16
    for _ in range(3):
        power = [_hp_dot(power[i], power[i]) for i in n]
        t16 = [t16[i] + _hp_dot(t16[i], power[i]) for i in n]
    b = [_hp_dot(t16[i], a[i] - a16[i]) for i in n]
    b2 = [_hp_dot(b[i], b[i]) for i in n]
    imb = [eye - b[i] for i in n]
    left = [imb[i] + _hp_dot(imb[i], b2[i]) for i in n]
    return [_hp_dot(left[i], t16[i]) for i in n]


def _gdn_prompt_kernel(xq_ref, xk_ref, xv_ref, pq_ref, pk_ref, pv_ref, a_ref, b_ref, z_ref, cwq_ref, cwk_ref, cwv_ref,
                       alog_ref, dtb_ref, ng_ref, s0_ref, o_ref, s_ref, hist_ref):
    f32, bf = jnp.float32, jnp.bfloat16
    d, g, c = HEAD_DIM, GDN_GROUP, GDN_CHUNK
    heads = range(GDN_HB)
    lanes = [slice(h * d, (h + 1) * d) for h in heads]
    row = lax.broadcasted_iota(jnp.int32, (g, g), 0)
    col = lax.broadcasted_iota(jnp.int32, (g, g), 1)
    same = jnp.right_shift(row, 6) == jnp.right_shift(col, 6)
    incl = same & (row >= col)
    strict = same & (row > col)
    eye = (row == col).astype(f32)
    tril_b, same_b = incl.astype(bf), same.astype(bf)

    @pl.when(pl.program_id(2) == 0)
    def _():
        s_ref[...] = s0_ref[...]
        for part, p_ref in enumerate((pq_ref, pk_ref, pv_ref)):
            hist_ref[part] = p_ref[...]

    pre = a_ref[...] + dtb_ref[...]
    gate = -jnp.exp(alog_ref[...]) * (jnp.maximum(pre, 0.0) + jnp.log(1.0 + jnp.exp(-jnp.abs(pre))))
    parts = _split3_bf16(gate)
    gcum_all = sum(jnp.dot(tril_b, p, preferred_element_type=f32) for p in parts)
    gtot_all = sum(jnp.dot(same_b, p, preferred_element_type=f32) for p in parts)
    beta_all = 1.0 / (1.0 + jnp.exp(-b_ref[...]))

    def conv(part, x_ref, w_ref, h):
        win = jnp.concatenate([hist_ref[part, :, lanes[h]], x_ref[:, lanes[h]]], axis=0)
        y = sum(pltpu.roll(win, g + CONV_PAD - (CONV_PAD - GDN_CONV + 1 + i), 0)[:g] * w_ref[i:i + 1, lanes[h]]
                for i in range(GDN_CONV))
        return y / (1.0 + jnp.exp(-y))

    q = [conv(0, xq_ref, cwq_ref, h) for h in heads]
    k = [conv(1, xk_ref, cwk_ref, h) for h in heads]
    v = [conv(2, xv_ref, cwv_ref, h) for h in heads]
    for part, x_ref in enumerate((xq_ref, xk_ref, xv_ref)):
        hist_ref[part] = x_ref[g - CONV_PAD:g, :]
    q = [q[h] * lax.rsqrt(jnp.sum(q[h] * q[h], axis=-1, keepdims=True) + EPS) * (d ** -0.5) for h in heads]
    k = [k[h] * lax.rsqrt(jnp.sum(k[h] * k[h], axis=-1, keepdims=True) + EPS) for h in heads]
    gc = [gcum_all[:, h:h + 1] for h in heads]
    gl = [gtot_all[:, h:h + 1] for h in heads]
    beta = [beta_all[:, h:h + 1] for h in heads]
    g_i = [jnp.broadcast_to(gc[h], (g, g)) for h in heads]
    g_j = [jnp.sum(g_i[h] * eye, axis=0, keepdims=True) for h in heads]
    decay = [jnp.where(incl, jnp.exp(jnp.where(incl, g_i[h] - g_j[h], 0.0)), 0.0) for h in heads]
    kb = [k[h] * beta[h] for h in heads]
    k16 = [k[h].astype(bf) for h in heads]
    eg = [jnp.exp(gc[h]) for h in heads]
    a_mat = [jnp.where(strict, lax.dot_general(kb[h].astype(bf), k16[h], _NT, preferred_element_type=f32) * decay[h], 0.0)
             for h in heads]
    qk = [(lax.dot_general(q[h].astype(bf), k16[h], _NT, preferred_element_type=f32) * decay[h]).astype(bf)
          for h in heads]
    rhs = [jnp.concatenate([v[h] * beta[h], kb[h] * eg[h]], axis=1) for h in heads]
    inv = _unit_lower_inverse(a_mat, row, col)
    sol = [_hp_dot(inv[h], rhs[h]) for h in heads]
    u = [sol[h][:, :d] for h in heads]
    w16 = [sol[h][:, d:].astype(bf) for h in heads]
    q_dec = [(q[h] * eg[h]).astype(bf) for h in heads]
    k_dec = [(k[h] * jnp.exp(gl[h] - gc[h])).astype(bf) for h in heads]
    chunk_decay = [jnp.exp(gl[h]) for h in heads]
    for ci in range(g // c):
        rows = slice(ci * c, (ci + 1) * c)
        state = [s_ref[h] for h in heads]
        s16 = [state[h].astype(bf) for h in heads]
        nv16 = [(u[h][rows] - jnp.dot(w16[h][rows], s16[h], preferred_element_type=f32)).astype(bf) for h in heads]
        pad = lambda n: [jnp.zeros((n, d), bf)] if n else []
        placed = [jnp.concatenate(pad(ci * c) + [nv16[h]] + pad(g - (ci + 1) * c), axis=0) for h in heads]
        out = [jnp.dot(q_dec[h][rows], s16[h], preferred_element_type=f32)
               + jnp.dot(qk[h][rows], placed[h], preferred_element_type=f32) for h in heads]
        for h in heads:
            s_ref[h] = state[h] * chunk_decay[h][ci * c:ci * c + 1, :] + lax.dot_general(
                k_dec[h][rows], nv16[h], (((0,), (0,)), ((), ())), preferred_element_type=f32)
        for h in heads:
            zz = z_ref[rows, lanes[h]]
            normed = out[h] * lax.rsqrt(jnp.mean(out[h] * out[h], axis=-1, keepdims=True) + EPS) * ng_ref[...]
            o_ref[rows, lanes[h]] = normed * (zz / (1.0 + jnp.exp(-zz)))


def gdn_prompt(src, qkv_off, z_off, a, beta_logit, conv_prev, s0, conv_w, a_log, dt_bias, norm_g):
    b, t, _ = src.shape
    d, hb = HEAD_DIM, GDN_HB
    ng = GDN_HEADS // hb
    assert t % GDN_GROUP == 0 and GDN_HEADS % hb == 0 and qkv_off % (hb * d) == 0 and z_off % (hb * d) == 0
    qb, zb = qkv_off // (hb * d), z_off // (hb * d)
    prev = jnp.pad(conv_prev.astype(src.dtype), ((0, 0), (CONV_PAD - GDN_CONV + 1, 0), (0, 0)))
    by_group = lambda x: jnp.swapaxes(x.reshape(b, t, ng, hb), 1, 2)
    g = GDN_GROUP
    cols = lambda part: pl.BlockSpec((None, g, hb * d), lambda bi, gi, ti: (bi, ti, qb + part * ng + gi))
    hist = lambda part: pl.BlockSpec((None, CONV_PAD, hb * d), lambda bi, gi, ti: (bi, 0, part * ng + gi))
    cw = lambda part: pl.BlockSpec((GDN_CONV, hb * d), lambda bi, gi, ti: (0, part * ng + gi))
    tok = pl.BlockSpec((None, None, g, hb), lambda bi, gi, ti: (bi, gi, ti, 0))
    head_const = pl.BlockSpec((None, 1, hb), lambda bi, gi, ti: (gi, 0, 0))
    state = pl.BlockSpec((None, hb, d, d), lambda bi, gi, ti: (bi, gi, 0, 0))
    return pl.pallas_call(
        _gdn_prompt_kernel,
        grid=(b, ng, t // g),
        in_specs=[cols(0), cols(1), cols(2), hist(0), hist(1), hist(2), tok, tok,
                  pl.BlockSpec((None, g, hb * d), lambda bi, gi, ti: (bi, ti, zb + gi)),
                  cw(0), cw(1), cw(2), head_const, head_const,
                  pl.BlockSpec((1, d), lambda bi, gi, ti: (0, 0)), state],
        out_specs=[pl.BlockSpec((None, g, hb * d), lambda bi, gi, ti: (bi, ti, gi)), state],
        out_shape=[jax.ShapeDtypeStruct((b, t, W_C), jnp.float32),
                   jax.ShapeDtypeStruct((b, GDN_HEADS, d, d), jnp.float32)],
        scratch_shapes=[pltpu.VMEM((3, CONV_PAD, hb * d), jnp.float32)],
        compiler_params=pltpu.CompilerParams(
            dimension_semantics=("parallel", "parallel", "arbitrary"), vmem_limit_bytes=VMEM_LIMIT),
        name="gdn_prompt",
    )(src, src, src, prev, prev, prev, by_group(a), by_group(beta_logit), src, conv_w, conv_w, conv_w,
      a_log.reshape(ng, 1, hb).astype(jnp.float32), dt_bias.reshape(ng, 1, hb).astype(jnp.float32),
      norm_g.reshape(1, d).astype(jnp.float32), s0)


MEM_TQ = 256
MIN_MXU_ROWS = 16


def _mem_attn_kernel(q_ref, kv_ref, o_ref):
    nh, d, bf = MEM_HEADS, HEAD_DIM, jnp.bfloat16
    tq = q_ref.shape[0]
    m = kv_ref.shape[0] // (2 * nh)
    rows = max(tq, MIN_MXU_ROWS)
    for h in range(nh):
        k = kv_ref[pl.ds(h, m, stride=2 * nh), :].astype(bf)
        v = kv_ref[pl.ds(nh + h, m, stride=2 * nh), :].astype(bf)
        q = _pad_rows(q_ref[:, h * d:(h + 1) * d], rows).astype(bf)
        s = lax.dot_general(q, k, _NT, preferred_element_type=jnp.float32) * (d ** -0.5)
        e = jnp.exp(s - jnp.max(s, axis=-1, keepdims=True))
        p = e / jnp.sum(e, axis=-1, keepdims=True)
        o_ref[:, h * d:(h + 1) * d] = jnp.dot(p.astype(bf), v, preferred_element_type=jnp.float32)[:tq]


def mem_attention_core(q, mem_kv):
    b, t, w = q.shape
    m = mem_kv.shape[1]
    tq = min(MEM_TQ, t)
    assert t % tq == 0
    kv = mem_kv.reshape(b, m * 2 * MEM_HEADS, HEAD_DIM)
    return pl.pallas_call(
        _mem_attn_kernel,
        grid=(b, t // tq),
        in_specs=[pl.BlockSpec((None, tq, w), lambda bi, qi: (bi, qi, 0)),
                  pl.BlockSpec((None, m * 2 * MEM_HEADS, HEAD_DIM), lambda bi, qi: (bi, 0, 0))],
        out_specs=pl.BlockSpec((None, tq, w), lambda bi, qi: (bi, qi, 0)),
        out_shape=jax.ShapeDtypeStruct((b, t, w), jnp.float32),
        compiler_params=pltpu.CompilerParams(
            dimension_semantics=("parallel", "arbitrary"), vmem_limit_bytes=VMEM_LIMIT),
        name="mem_attn",
    )(q, kv)


def rms_norm(x, g):
    xf = x.astype(jnp.float32)
    y = xf * lax.rsqrt(jnp.mean(xf * xf, axis=-1, keepdims=True) + EPS)
    return (y * g.astype(jnp.float32)).astype(x.dtype)


def l2norm(x):
    xf = x.astype(jnp.float32)
    return xf * lax.rsqrt(jnp.sum(xf * xf, axis=-1, keepdims=True) + EPS)


def rope(x, pos):
    half = x.shape[-1] // 2
    inv = ROPE_THETA ** (-jnp.arange(half, dtype=jnp.float32) / half)
    ang = pos.astype(jnp.float32)[:, None] * inv
    cos, sin = jnp.cos(ang)[:, None, :], jnp.sin(ang)[:, None, :]
    xf = x.astype(jnp.float32)
    x1, x2 = xf[..., :half], xf[..., half:]
    return jnp.concatenate([x1 * cos - x2 * sin, x2 * cos + x1 * sin], axis=-1).astype(x.dtype)


def masked_softmax(s, mask):
    s = jnp.where(mask, s.astype(jnp.float32), NEG)
    e = jnp.where(mask, jnp.exp(s - jnp.max(s, axis=-1, keepdims=True)), 0.0)
    return e / jnp.maximum(jnp.sum(e, axis=-1, keepdims=True), 1e-30)


def split_cols(h):
    return jnp.split(h[..., :sum(IN_SPLITS)], [int(o) for o in np.cumsum(IN_SPLITS)[:-1]], axis=-1)


def gather_pages(pool, page_table):
    rows = pool[page_table]
    return rows.reshape(rows.shape[0], -1, *rows.shape[3:])


def over_query_blocks(fn, *qs):
    b, t = qs[0].shape[:2]
    if t <= Q_BLOCK:
        return fn(0, *qs)
    nb = t // Q_BLOCK
    blocks = tuple(q.reshape(b, nb, Q_BLOCK, *q.shape[2:]).swapaxes(0, 1) for q in qs)
    out = lax.map(lambda a: fn(a[0] * Q_BLOCK, *a[1]), (jnp.arange(nb), blocks))
    out = out.swapaxes(0, 1)
    return out.reshape(b, t, *out.shape[3:])


def nsa_compress(rows, pe, w1, w2):
    b, n_keys, d = rows.shape
    r = CMP_LEN // CMP_STRIDE
    n_cmp = (n_keys - CMP_LEN) // CMP_STRIDE + 1
    seg = rows[:, :(n_cmp + r - 1) * CMP_STRIDE].reshape(b, n_cmp + r - 1, CMP_STRIDE * d)
    w1r = w1.reshape(r, CMP_STRIDE * d, CMP_HID)
    h = pe.reshape(-1) @ w1 + sum(seg[:, i:i + n_cmp] @ w1r[i] for i in range(r))
    return jax.nn.gelu(h) @ w2


def nsa_attention(q, gates, k_cmp, v_cmp, k_slc, v_slc, k_win, v_win, past, win_pos0, banded):
    b, n_keys, d = k_slc.shape
    scale = d ** -0.5
    n_cmp = k_cmp.shape[1]
    cmp_end = jnp.arange(n_cmp) * CMP_STRIDE + CMP_LEN - 1
    n_blk = -(-n_keys // SEL_BLOCK)
    n_sel = min(SEL_COUNT, n_blk)
    pad = ((0, 0), (0, n_blk * SEL_BLOCK - n_keys), (0, 0))
    k_blk = jnp.pad(k_slc, pad).reshape(b, n_blk, SEL_BLOCK, d)
    v_blk = jnp.pad(v_slc, pad).reshape(b, n_blk, SEL_BLOCK, d)
    c0 = jnp.arange(n_cmp)[:, None] * CMP_STRIDE
    j0 = jnp.arange(n_blk)[None, :] * SEL_BLOCK
    share = jnp.clip(jnp.minimum(c0 + CMP_LEN, j0 + SEL_BLOCK) - jnp.maximum(c0, j0), 0, None).astype(jnp.float32) / CMP_LEN
    blk_id = jnp.arange(n_blk)
    if banded:
        wpad = ((0, 0), (WINDOW, 0), (0, 0))
        k_win, v_win = jnp.pad(k_win, wpad), jnp.pad(v_win, wpad)

    def block(start, qb, gb):
        nq = qb.shape[1]
        qp = past + start + jnp.arange(nq)
        s = jnp.einsum('bqhd,bcd->bqhc', qb, k_cmp) * scale
        p_cmp = masked_softmax(s, (cmp_end[None, :] <= qp[:, None])[None, :, None, :])
        o_cmp = jnp.einsum('bqhc,bcd->bqhd', p_cmp.astype(v_cmp.dtype), v_cmp)
        imp = jnp.einsum('bqhc,cj->bqj', p_cmp, share)
        forced = (blk_id[None, :] == qp[:, None] // SEL_BLOCK) | (blk_id[None, :] == 0)
        future = blk_id[None, :] * SEL_BLOCK > qp[:, None]
        imp = jnp.where(forced[None], FORCED, jnp.where(future[None], -1.0, imp))
        _, sel = lax.top_k(imp, n_sel)
        ks = jax.vmap(lambda kb, i: kb[i])(k_blk, sel).reshape(b, nq, n_sel * SEL_BLOCK, d)
        vs = jax.vmap(lambda vb, i: vb[i])(v_blk, sel).reshape(b, nq, n_sel * SEL_BLOCK, d)
        kpos = (sel[..., None] * SEL_BLOCK + jnp.arange(SEL_BLOCK)).reshape(b, nq, n_sel * SEL_BLOCK)
        s = jnp.einsum('bqhd,bqkd->bqhk', qb, ks) * scale
        p = masked_softmax(s, (kpos <= qp[None, :, None])[:, :, None, :])
        o_slc = jnp.einsum('bqhk,bqkd->bqhd', p.astype(vs.dtype), vs)
        if banded:
            n_w = WINDOW + nq
            kw = lax.dynamic_slice_in_dim(k_win, start, n_w, axis=1)
            vw = lax.dynamic_slice_in_dim(v_win, start, n_w, axis=1)
            kp = past + start - WINDOW + jnp.arange(n_w)
        else:
            kw, vw = k_win, v_win
            kp = win_pos0 + jnp.arange(k_win.shape[1])
        dlt = qp[:, None] - kp[None, :]
        s = jnp.einsum('bqhd,bkd->bqhk', qb, kw) * scale
        p = masked_softmax(s, ((dlt >= 0) & (dlt < WINDOW) & (kp[None, :] >= 0))[None, :, None, :])
        o_win = jnp.einsum('bqhk,bkd->bqhd', p.astype(vw.dtype), vw)
        g = jax.nn.sigmoid(gb.astype(jnp.float32))
        o = g[..., 0:1] * o_cmp + g[..., 1:2] * o_slc + g[..., 2:3] * o_win
        return o.astype(qb.dtype)

    return over_query_blocks(block, q, gates)


def dsa_attention(q, q_idx, w_idx, k, v, k_idx, past):
    b, n_keys, d = k.shape
    topk = min(DSA_TOPK, n_keys // 4)
    kpos = jnp.arange(n_keys)

    def block(start, qb, qib, wb):
        qp = past + start + jnp.arange(qb.shape[1])
        causal = kpos[None, :] <= qp[:, None]
        dots = jnp.einsum('bqhd,bsd->bqhs', qib, k_idx).astype(jnp.float32) * IDX_DIM ** -0.5
        score = jnp.einsum('bqh,bqhs->bqs', wb.astype(jnp.float32) * IDX_HEADS ** -0.5, jax.nn.relu(dots))
        score = jnp.where(causal[None], score, NEG)
        _, sel = lax.top_k(score, topk)
        ks = jax.vmap(lambda kk, i: kk[i])(k, sel)
        vs = jax.vmap(lambda vv, i: vv[i])(v, sel)
        s = jnp.einsum('bqhd,bqkd->bqhk', qb, ks) * d ** -0.5
        p = masked_softmax(s, (sel <= qp[None, :, None])[:, :, None, :])
        return jnp.einsum('bqhk,bqkd->bqhd', p.astype(vs.dtype), vs)

    return over_query_blocks(block, q, q_idx, w_idx)


def gated_delta_chunked(q, k, v, g, beta, s0):
    f32 = jnp.float32
    b, t, h, dk = k.shape
    dv = v.shape[-1]
    c = min(GDN_CHUNK, t)
    n = -(-t // c)
    pad = n * c - t

    def to_chunks(a):
        a = jnp.pad(a.astype(f32), [(0, 0), (0, pad)] + [(0, 0)] * (a.ndim - 2))
        a = a.reshape(b, n, c, *a.shape[2:])
        return jnp.moveaxis(a, (1, 3), (0, 2))

    qc, kc, vc, gc, bc = (to_chunks(a) for a in (q, k, v, g, beta))
    gcum = jnp.cumsum(gc, axis=-1)
    pos = jnp.arange(c)
    strict = pos[:, None] > pos[None, :]
    incl = pos[:, None] >= pos[None, :]
    diff = gcum[..., :, None] - gcum[..., None, :]
    decay = jnp.where(incl, jnp.exp(jnp.where(incl, diff, 0.0)), 0.0)
    k_beta = kc * bc[..., None]
    a_mat = jnp.where(strict, jnp.einsum('nbhid,nbhjd->nbhij', k_beta, kc) * decay, 0.0)
    rhs = jnp.concatenate([vc * bc[..., None], k_beta * jnp.exp(gcum)[..., None]], axis=-1)
    sol = lax.linalg.triangular_solve(a_mat + jnp.eye(c, dtype=f32), rhs,
                                      left_side=True, lower=True, unit_diagonal=True)
    u, w = sol[..., :dv], sol[..., dv:]
    qk = jnp.einsum('nbhid,nbhjd->nbhij', qc, kc) * decay
    q_dec = qc * jnp.exp(gcum)[..., None]
    k_dec = kc * jnp.exp(gcum[..., -1:] - gcum)[..., None]
    chunk_decay = jnp.exp(gcum[..., -1])

    def step(state, xs):
        u_i, w_i, qd_i, qk_i, kd_i, cd_i = xs
        new_v = u_i - jnp.einsum('bhck,bhkv->bhcv', w_i, state)
        o_i = jnp.einsum('bhck,bhkv->bhcv', qd_i, state) + jnp.einsum('bhij,bhjv->bhiv', qk_i, new_v)
        state = state * cd_i[..., None, None] + jnp.einsum('bhck,bhcv->bhkv', kd_i, new_v)
        return state, o_i

    s_final, o = lax.scan(step, s0.astype(f32), (u, w, q_dec, qk, k_dec, chunk_decay))
    o = jnp.moveaxis(o, (0, 2), (1, 3)).reshape(b, n * c, h, dv)[:, :t]
    return o, s_final


def gdn_mixer(qkv, a, beta_logit, z, conv_prev, s0, conv_w, a_log, dt_bias, norm_g):
    b, t, _ = qkv.shape
    xp = jnp.concatenate([conv_prev.astype(qkv.dtype), qkv], axis=1)
    conv = jax.nn.silu(sum(xp[:, i:i + t] * conv_w[i] for i in range(GDN_CONV)))
    q, k, v = (cc.reshape(b, t, GDN_HEADS, HEAD_DIM) for cc in jnp.split(conv, 3, axis=-1))
    q = l2norm(q) * HEAD_DIM ** -0.5
    k = l2norm(k)
    g = -jnp.exp(a_log.astype(jnp.float32)) * jax.nn.softplus(a.astype(jnp.float32) + dt_bias.astype(jnp.float32))
    beta = jax.nn.sigmoid(beta_logit.astype(jnp.float32))
    o, s_new = gated_delta_chunked(q, k, v, g, beta, s0)
    o = rms_norm(o, norm_g) * jax.nn.silu(z.reshape(b, t, GDN_HEADS, HEAD_DIM).astype(jnp.float32))
    return o.reshape(b, t, W_C).astype(qkv.dtype), s_new, xp[:, t:]


def mem_attention(x, gain, mem_kv, w_q, w_o):
    b, t, _ = x.shape
    o = mem_attention_core(dense(x, w_q, gain=gain), mem_kv)
    return dense(o, w_o, resid=x)


def run_group(x, mem, cache, p):
    prompt = cache is None
    b, t, _ = x.shape
    past = 0 if prompt else cache['page_table'].shape[1] * cache['nsa_kv'].shape[2]
    names = ('nsa_kv', 'dsa_kv', 'idx_k', 'win_kv', 'gdn', 'conv') + (('mem_kv',) if prompt else ())
    out = {nm: [] for nm in names}
    bg, tg = (b, t) if prompt else (1, b * t)
    row_pos = past + jnp.arange(tg) % t
    for l in range(DEPTH):
        hproj = dense(x, p['w_in'][l], gain=p['norm_mix_g'][l])
        (q_a, q_b, q_i, nsa_new, win_new, dsa_new, idx_new,
         k_slc, v_slc, k_win, v_win, k_dsa, v_dsa, k_idx) = rope_split(hproj.reshape(bg, tg, PROJ_WIDTH), row_pos)
        nsa_new = nsa_new.reshape(b, t, NSA_ROWS, HEAD_DIM)
        win_new = win_new.reshape(b, t, 2, HEAD_DIM)
        dsa_new = dsa_new.reshape(b, t, 2, HEAD_DIM)
        idx_new = idx_new.reshape(b, t, IDX_DIM)
        g_a, w_i, a_c, b_c = (small_col(hproj, nm) for nm in ('g_a', 'w_i', 'a_c', 'b_c'))
        qkv_off, z_off = PROJ_OFF['qkv_c'], PROJ_OFF['z_c']
        qkv_c = hproj[..., qkv_off:qkv_off + 3 * W_C]
        if prompt:
            conv_prev = jnp.zeros((b, GDN_CONV - 1, 3 * W_C), x.dtype)
            s0 = jnp.zeros((b, GDN_HEADS, HEAD_DIM, HEAD_DIM), jnp.float32)
            mem_kv = dense(mem, p['w_mem_kv'][l], gain=p['mem_norm_g'][l]).reshape(b, -1, 2, MEM_HEADS, HEAD_DIM)
            k_cmp = nsa_compress(nsa_new[:, :, 0], p['nsa_cmp_pe'][l, 0], p['nsa_cmp_w1'][l, 0], p['nsa_cmp_w2'][l, 0])
            v_cmp = nsa_compress(nsa_new[:, :, 1], p['nsa_cmp_pe'][l, 1], p['nsa_cmp_w1'][l, 1], p['nsa_cmp_w2'][l, 1])
            cmp_end = jnp.arange(k_cmp.shape[1]) * CMP_STRIDE + CMP_LEN - 1
            k_cmp = rope(k_cmp[:, :, None], cmp_end)[:, :, 0]
            o_a = nsa_prompt_attention(q_a, g_a, k_cmp, v_cmp, k_slc, v_slc, k_win, v_win)
            o_b = dsa_prompt_attention(q_b, q_i, w_i, k_dsa, v_dsa, k_idx)
            win_out = win_new[:, -min(WINDOW, t):]
            o_c, s_new = gdn_prompt(hproj, qkv_off, z_off, a_c, b_c, conv_prev, s0, p['gdn_conv_w'][l],
                                    p['gdn_a_log'][l], p['gdn_dt_bias'][l], p['gdn_norm_g'][l])
            conv_new = jnp.concatenate([conv_prev, qkv_c[:, -(GDN_CONV - 1):]], axis=1)[:, -(GDN_CONV - 1):]
        else:
            pt = cache['page_table']
            conv_prev, s0 = cache['conv'][l], cache['gdn'][l]
            mem_kv = cache['mem_kv'][l]
            q_i = jnp.swapaxes(q_i.reshape(IDX_HEADS, b, t, IDX_DIM), 0, 1)
            o_a = nsa_sample_attention(l, q_a.reshape(b, t, W_A), g_a, nsa_new, win_new, cache['nsa_kv'], cache['win_kv'],
                                       pt, p['nsa_cmp_pe'][l], p['nsa_cmp_w1'][l], p['nsa_cmp_w2'][l])
            o_b = dsa_sample_attention(l, q_b.reshape(b, t, W_B), q_i, w_i, dsa_new, idx_new, cache['dsa_kv'],
                                       cache['idx_k'], pt)
            win_out = win_new
            o_c, s_new, conv_new = gdn_mixer(qkv_c, a_c, b_c, hproj[..., z_off:z_off + W_C], conv_prev, s0,
                                             p['gdn_conv_w'][l], p['gdn_a_log'][l], p['gdn_dt_bias'][l],
                                             p['gdn_norm_g'][l])
        mix = branch_mix(o_a.reshape(b * t, W_A), o_b.reshape(b * t, W_B), o_c.reshape(b * t, W_C),
                         hproj.reshape(b * t, PROJ_WIDTH), PROJ_OFF['merge'],
                         p['w_branch_a'][l], p['w_branch_b'][l], p['w_branch_c'][l])
        x = dense(mix, p['w_mix_out'][l], resid=x).reshape(b, t, D_MODEL)
        x = mem_attention(x, p['norm_mem_g'][l], mem_kv, p['w_mem_q'][l], p['w_mem_o'][l])
        x = dense(ffn_act(x, p['norm_ffn_g'][l], p['w_ffn_gate'][l], p['w_ffn_up'][l]), p['w_ffn_down'][l], resid=x)
        out['nsa_kv'].append(nsa_new)
        out['dsa_kv'].append(dsa_new)
        out['idx_k'].append(idx_new)
        out['win_kv'].append(win_out)
        out['gdn'].append(s_new)
        out['conv'].append(conv_new)
        if prompt:
            out['mem_kv'].append(mem_kv)
    y = rms_norm(x, p['norm_final_g'])
    out = {nm: jnp.stack(v) for nm, v in out.items()}
    if not prompt:
        wlen = cache['win_kv'].shape[2]
        out['win_kv'] = jnp.concatenate([cache['win_kv'], out['win_kv']], axis=2)[:, :, -wlen:]
    return y, out


def kernel(x_prompt, x_sample, mem_prompt, cache_nsa_kv, cache_dsa_kv, cache_dsa_idx_k, cache_win_kv, cache_mem_kv, state_gdn, state_conv, page_table, norm_mix_g, w_in, nsa_cmp_pe, nsa_cmp_w1, nsa_cmp_w2, gdn_conv_w, gdn_a_log, gdn_dt_bias, gdn_norm_g, w_branch_a, w_branch_b, w_branch_c, w_mix_out, norm_mem_g, mem_norm_g, w_mem_q, w_mem_kv, w_mem_o, norm_ffn_g, w_ffn_gate, w_ffn_up, w_ffn_down, norm_final_g):
    p = dict(norm_mix_g=norm_mix_g, w_in=w_in, nsa_cmp_pe=nsa_cmp_pe, nsa_cmp_w1=nsa_cmp_w1,
             nsa_cmp_w2=nsa_cmp_w2, gdn_conv_w=gdn_conv_w, gdn_a_log=gdn_a_log, gdn_dt_bias=gdn_dt_bias,
             gdn_norm_g=gdn_norm_g, w_branch_a=w_branch_a, w_branch_b=w_branch_b, w_branch_c=w_branch_c,
             w_mix_out=w_mix_out, norm_mem_g=norm_mem_g, mem_norm_g=mem_norm_g, w_mem_q=w_mem_q,
             w_mem_kv=w_mem_kv, w_mem_o=w_mem_o, norm_ffn_g=norm_ffn_g, w_ffn_gate=w_ffn_gate,
             w_ffn_up=w_ffn_up, w_ffn_down=w_ffn_down, norm_final_g=norm_final_g)
    for nm in ('w_branch_a', 'w_branch_b', 'w_branch_c', 'w_mix_out', 'w_mem_q', 'w_mem_kv', 'w_mem_o',
               'w_ffn_gate', 'w_ffn_up', 'w_ffn_down'):
        p[nm] = p[nm].astype(jnp.bfloat16)
    p['w_in'] = permute_w_in(w_in)
    cache = dict(nsa_kv=cache_nsa_kv, dsa_kv=cache_dsa_kv, idx_k=cache_dsa_idx_k, win_kv=cache_win_kv,
                 mem_kv=cache_mem_kv, gdn=state_gdn, conv=state_conv, page_table=page_table)
    y_prompt, sp = run_group(x_prompt, mem_prompt, None, p)
    y_sample, ss = run_group(x_sample, None, cache, p)
    return (y_prompt, y_sample,
            sp['nsa_kv'], sp['dsa_kv'], sp['idx_k'], sp['win_kv'], sp['gdn'], sp['conv'], sp['mem_kv'],
            ss['nsa_kv'], ss['dsa_kv'], ss['idx_k'], ss['win_kv'], ss['gdn'], ss['conv'])
```

```python
import functools
import math

import jax
import jax.numpy as jnp
import numpy as np
from jax import lax
from jax.experimental import pallas as pl
from jax.experimental.pallas import tpu as pltpu

D_MODEL = 2048
DEPTH = 2
PAGE_SIZE = 128
HEAD_DIM = 128
NSA_HEADS = D_MODEL // (4 * HEAD_DIM)
DSA_HEADS = D_MODEL // (4 * HEAD_DIM)
GDN_HEADS = D_MODEL // (2 * HEAD_DIM)
W_A = NSA_HEADS * HEAD_DIM
W_B = DSA_HEADS * HEAD_DIM
W_C = GDN_HEADS * HEAD_DIM
CMP_LEN = 32
CMP_STRIDE = 16
CMP_HID = 2 * HEAD_DIM
SEL_BLOCK = 64
SEL_COUNT = 16
WINDOW = 512
IDX_HEADS = 16
IDX_DIM = 64
DSA_TOPK = 256
GDN_CONV = 4
GDN_CHUNK = 64
MEM_HEADS = 4
ROPE_THETA = 10000.0
Q_BLOCK = 128
EPS = 1e-6
NEG = -1e30
FORCED = 1e9
IN_SPLITS = (W_A, 6 * HEAD_DIM, 3 * NSA_HEADS,
             W_B, 2 * HEAD_DIM, IDX_HEADS * IDX_DIM, IDX_HEADS, IDX_DIM,
             3 * W_C, GDN_HEADS, GDN_HEADS, W_C,
             3 * D_MODEL)

LANE = 128
VMEM_LIMIT = 48 * 1024 * 1024


DENSE_VMEM_BUDGET = 40 * 1024 * 1024
NORM_ROWS = 128


def _divisor_tiles(n, cap):
    return [t for t in range(min(cap, n), 0, -LANE) if t % LANE == 0 and n % t == 0]


def _dense_tiles(m, n, per_row_bytes, per_col_bytes, per_out_bytes):
    for tm in (1024, 512, 256, 128):
        if m % tm:
            continue
        for tn in _divisor_tiles(n, 1024):
            if tm * per_row_bytes + tn * per_col_bytes + tm * tn * per_out_bytes <= DENSE_VMEM_BUDGET:
                return tm, tn
    raise ValueError("no dense tile fits VMEM")


def _stage_rows(x_ref, g_ref, xn_ref):
    def body(r, carry):
        rows = pl.ds(pl.multiple_of(r * NORM_ROWS, NORM_ROWS), NORM_ROWS)
        xf = x_ref[rows, :].astype(jnp.float32)
        if g_ref is not None:
            xf = xf * lax.rsqrt(jnp.mean(xf * xf, axis=-1, keepdims=True) + EPS) * g_ref[...]
        xn_ref[rows, :] = xf.astype(jnp.bfloat16)
        return carry
    lax.fori_loop(0, x_ref.shape[0] // NORM_ROWS, body, 0)


def _dense_kernel(*refs, has_gain, has_resid, staged):
    it = iter(refs)
    x_ref = next(it)
    g_ref = next(it) if has_gain else None
    w_ref = next(it)
    r_ref = next(it) if has_resid else None
    o_ref = next(it)
    xn_ref = next(it) if staged else x_ref

    if staged:
        @pl.when(pl.program_id(1) == 0)
        def _():
            _stage_rows(x_ref, g_ref, xn_ref)

    acc = jnp.dot(xn_ref[...], w_ref[...], preferred_element_type=jnp.float32)
    if has_resid:
        acc = acc + r_ref[...]
    o_ref[...] = acc.astype(o_ref.dtype)


def dense(x, w, gain=None, resid=None, out_dtype=jnp.float32):
    lead, kdim, n = x.shape[:-1], x.shape[-1], w.shape[-1]
    x2 = x.reshape(-1, kdim)
    m = x2.shape[0]
    ob = jnp.dtype(out_dtype).itemsize
    staged = gain is not None or x2.dtype != jnp.bfloat16
    tm, tn = _dense_tiles(m, n, per_row_bytes=kdim * (2 * x2.dtype.itemsize + (2 if staged else 0)),
                          per_col_bytes=kdim * 2 * 2, per_out_bytes=2 * ob + (8 if resid is not None else 0))
    args, specs = [x2], [pl.BlockSpec((tm, kdim), lambda i, j: (i, 0))]
    if gain is not None:
        args.append(gain.reshape(1, kdim).astype(jnp.float32))
        specs.append(pl.BlockSpec((1, kdim), lambda i, j: (0, 0)))
    args.append(w)
    specs.append(pl.BlockSpec((kdim, tn), lambda i, j: (0, j)))
    if resid is not None:
        args.append(resid.reshape(m, n))
        specs.append(pl.BlockSpec((tm, tn), lambda i, j: (i, j)))
    out = pl.pallas_call(
        functools.partial(_dense_kernel, has_gain=gain is not None, has_resid=resid is not None, staged=staged),
        grid=(m // tm, n // tn),
        in_specs=specs,
        out_specs=pl.BlockSpec((tm, tn), lambda i, j: (i, j)),
        out_shape=jax.ShapeDtypeStruct((m, n), out_dtype),
        scratch_shapes=[pltpu.VMEM((tm, kdim), jnp.bfloat16)] if staged else [],
        compiler_params=pltpu.CompilerParams(
            dimension_semantics=("parallel", "arbitrary"), vmem_limit_bytes=VMEM_LIMIT),
        name="dense",
    )(*args)
    return out.reshape(*lead, n)


def _ffn_act_kernel(x_ref, g_ref, wg_ref, wu_ref, o_ref, xn_ref):
    @pl.when(pl.program_id(1) == 0)
    def _():
        _stage_rows(x_ref, g_ref, xn_ref)

    xn = xn_ref[...]
    a = jnp.dot(xn, wg_ref[...], preferred_element_type=jnp.float32)
    u = jnp.dot(xn, wu_ref[...], preferred_element_type=jnp.float32)
    o_ref[...] = (a / (1.0 + jnp.exp(-a)) * u).astype(o_ref.dtype)


def ffn_act(x, gain, w_gate, w_up):
    lead, kdim, n = x.shape[:-1], x.shape[-1], w_gate.shape[-1]
    x2 = x.reshape(-1, kdim)
    m = x2.shape[0]
    tm, tn = _dense_tiles(m, n, per_row_bytes=kdim * (2 * 4 + 2), per_col_bytes=2 * kdim * 2 * 2, per_out_bytes=2 * 2)
    out = pl.pallas_call(
        _ffn_act_kernel,
        grid=(m // tm, n // tn),
        in_specs=[pl.BlockSpec((tm, kdim), lambda i, j: (i, 0)),
                  pl.BlockSpec((1, kdim), lambda i, j: (0, 0)),
                  pl.BlockSpec((kdim, tn), lambda i, j: (0, j)),
                  pl.BlockSpec((kdim, tn), lambda i, j: (0, j))],
        out_specs=pl.BlockSpec((tm, tn), lambda i, j: (i, j)),
        out_shape=jax.ShapeDtypeStruct((m, n), jnp.bfloat16),
        scratch_shapes=[pltpu.VMEM((tm, kdim), jnp.bfloat16)],
        compiler_params=pltpu.CompilerParams(
            dimension_semantics=("parallel", "arbitrary"), vmem_limit_bytes=VMEM_LIMIT),
        name="ffn_act",
    )(x2, gain.reshape(1, kdim).astype(jnp.float32), w_gate, w_up)
    return out.reshape(*lead, n)


def _branch_mix_kernel(oa_ref, ob_ref, oc_ref, ma_ref, mb_ref, mc_ref, wa_ref, wb_ref, wc_ref, o_ref, xn_ref):
    @pl.when(pl.program_id(1) == 0)
    def _():
        off = 0
        for src in (oa_ref, ob_ref, oc_ref):
            xn_ref[:, off:off + src.shape[1]] = src[...].astype(jnp.bfloat16)
            off += src.shape[1]

    def gated(m_ref, w_ref, lo, hi):
        y = jnp.dot(xn_ref[:, lo:hi], w_ref[...], preferred_element_type=jnp.float32)
        return y / (1.0 + jnp.exp(-m_ref[...]))

    o_ref[...] = (gated(ma_ref, wa_ref, 0, W_A) + gated(mb_ref, wb_ref, W_A, W_A + W_B)
                  + gated(mc_ref, wc_ref, W_A + W_B, W_A + W_B + W_C)).astype(o_ref.dtype)


def branch_mix(o_a, o_b, o_c, merge, merge_off, w_a, w_b, w_c):
    m, n = merge.shape[0], w_a.shape[-1]
    ktot = W_A + W_B + W_C
    tm, tn = _dense_tiles(m, n, per_row_bytes=ktot * (2 * 4 + 2), per_col_bytes=ktot * 2 * 2, per_out_bytes=2 * 2 + 3 * 8)
    nb = n // tn
    assert merge_off % tn == 0
    mb = merge_off // tn
    row = lambda width: pl.BlockSpec((tm, width), lambda i, j: (i, 0))
    return pl.pallas_call(
        _branch_mix_kernel,
        grid=(m // tm, nb),
        in_specs=[row(W_A), row(W_B), row(W_C),
                  pl.BlockSpec((tm, tn), lambda i, j: (i, mb + j)),
                  pl.BlockSpec((tm, tn), lambda i, j: (i, mb + j + nb)),
                  pl.BlockSpec((tm, tn), lambda i, j: (i, mb + j + 2 * nb)),
                  pl.BlockSpec((W_A, tn), lambda i, j: (0, j)),
                  pl.BlockSpec((W_B, tn), lambda i, j: (0, j)),
                  pl.BlockSpec((W_C, tn), lambda i, j: (0, j))],
        out_specs=pl.BlockSpec((tm, tn), lambda i, j: (i, j)),
        out_shape=jax.ShapeDtypeStruct((m, n), jnp.bfloat16),
        scratch_shapes=[pltpu.VMEM((tm, ktot), jnp.bfloat16)],
        compiler_params=pltpu.CompilerParams(
            dimension_semantics=("parallel", "arbitrary"), vmem_limit_bytes=VMEM_LIMIT),
        name="branch_mix",
    )(o_a, o_b, o_c, merge, merge, merge, w_a, w_b, w_c)


REF_COLS = tuple(zip(('q_a', 'kv_a', 'g_a', 'q_b', 'kv_b', 'q_i', 'w_i', 'k_i', 'qkv_c', 'a_c', 'b_c', 'z_c', 'merge'),
                     IN_SPLITS))
SMALL_COLS = ('k_i', 'g_a', 'w_i', 'a_c', 'b_c')
WIDE_COLS = ('q_a', 'kv_a', 'q_b', 'kv_b', 'q_i', 'qkv_c', 'z_c', 'merge')


def _layout():
    width = dict(REF_COLS)
    assert all(width[n] % LANE == 0 for n in WIDE_COLS) and sum(width[n] for n in SMALL_COLS) <= LANE
    off, pos = {}, 0
    for n in WIDE_COLS:
        off[n] = pos
        pos += width[n]
    off['small'] = pos
    small, spos = {}, 0
    for n in SMALL_COLS:
        small[n] = (spos, width[n])
        spos += width[n]
    return off, small, pos + LANE


PROJ_OFF, SMALL_OFF, PROJ_WIDTH = _layout()
ROPE_COLS = PROJ_OFF['qkv_c']
ROPE_TM = 256


PERMUTE_ROWS = 128


def _column_moves():
    ref_off, pos = {}, 0
    for n, wd in REF_COLS:
        ref_off[n] = pos
        pos += wd
    width = dict(REF_COLS)
    moves = [(ref_off[n], PROJ_OFF[n], width[n]) for n in WIDE_COLS]
    moves += [(ref_off[n], PROJ_OFF['small'] + SMALL_OFF[n][0], width[n]) for n in SMALL_COLS]
    return moves, PROJ_OFF['small'] + sum(width[n] for n in SMALL_COLS)


def _permute_kernel(w_ref, o_ref):
    moves, used = _column_moves()
    for src, dst, wd in moves:
        o_ref[:, dst:dst + wd] = w_ref[:, src:src + wd].astype(o_ref.dtype)
    o_ref[:, used:] = jnp.zeros((o_ref.shape[0], o_ref.shape[1] - used), o_ref.dtype)


def permute_w_in(w):
    nl, kdim, n_in = w.shape
    return pl.pallas_call(
        _permute_kernel,
        grid=(nl, kdim // PERMUTE_ROWS),
        in_specs=[pl.BlockSpec((None, PERMUTE_ROWS, n_in), lambda l, i: (l, i, 0))],
        out_specs=pl.BlockSpec((None, PERMUTE_ROWS, PROJ_WIDTH), lambda l, i: (l, i, 0)),
        out_shape=jax.ShapeDtypeStruct((nl, kdim, PROJ_WIDTH), jnp.bfloat16),
        compiler_params=pltpu.CompilerParams(
            dimension_semantics=("parallel", "parallel"), vmem_limit_bytes=VMEM_LIMIT),
        name="permute_w_in",
    )(w)


def small_col(hproj, name):
    lo, wd = SMALL_OFF[name]
    return hproj[..., PROJ_OFF['small'] + lo:PROJ_OFF['small'] + lo + wd]


def _rope_tables(pos, rows, head_dim=HEAD_DIM):
    half = head_dim // 2
    inv = ROPE_THETA ** (-jnp.arange(half, dtype=jnp.float32) / half)
    ang = pos.astype(jnp.float32)[:, None] * inv
    cos, sin = jnp.cos(ang), jnp.sin(ang)
    reps = LANE // head_dim
    pad = ((0, rows - pos.shape[0]), (0, 0))
    return (jnp.pad(jnp.tile(jnp.concatenate([cos, cos], axis=-1), (1, reps)), pad),
            jnp.pad(jnp.tile(jnp.concatenate([-sin, sin], axis=-1), (1, reps)), pad))


def _rope_split_kernel(h_ref, small_ref, c128_ref, s128_ref, c64_ref, s64_ref,
                       qa_ref, qb_ref, qi_ref, nsa_ref, win_ref, dsa_ref, idx_ref,
                       kslc_ref, vslc_ref, kwin_ref, vwin_ref, kdsa_ref, vdsa_ref, kidx_ref):
    d, bf = HEAD_DIM, jnp.bfloat16
    c128, s128, c64, s64 = c128_ref[...], s128_ref[...], c64_ref[...], s64_ref[...]
    first_half = (lax.broadcasted_iota(jnp.int32, (1, LANE), 1) & (IDX_DIM - 1)) < IDX_DIM // 2

    def rope128(x):
        return x * c128 + pltpu.roll(x, d // 2, 1) * s128

    def rope64(x):
        rot = jnp.where(first_half, pltpu.roll(x, LANE - IDX_DIM // 2, 1), pltpu.roll(x, IDX_DIM // 2, 1))
        return x * c64 + rot * s64

    col = lambda name, i: h_ref[:, PROJ_OFF[name] + i * d:PROJ_OFF[name] + (i + 1) * d]
    for i in range(NSA_HEADS):
        qa_ref[:, i * d:(i + 1) * d] = (rope128(col('q_a', i)) * (d ** -0.5)).astype(bf)
    for i in range(DSA_HEADS):
        qb_ref[:, i * d:(i + 1) * d] = (rope128(col('q_b', i)) * (d ** -0.5)).astype(bf)
    k_slc, v_slc, k_win, v_win = rope128(col('kv_a', 2)), col('kv_a', 3), rope128(col('kv_a', 4)), col('kv_a', 5)
    for i, part in enumerate((col('kv_a', 0), col('kv_a', 1), k_slc, v_slc)):
        nsa_ref[:, i * d:(i + 1) * d] = part
    win_ref[:, 0:d], win_ref[:, d:2 * d] = k_win, v_win
    kslc_ref[...], vslc_ref[...], kwin_ref[...], vwin_ref[...] = (k_slc.astype(bf), v_slc.astype(bf),
                                                                   k_win.astype(bf), v_win.astype(bf))
    k_dsa, v_dsa = rope128(col('kv_b', 0)), col('kv_b', 1)
    dsa_ref[:, 0:d], dsa_ref[:, d:2 * d] = k_dsa, v_dsa
    kdsa_ref[...], vdsa_ref[...] = k_dsa.astype(bf), v_dsa.astype(bf)
    for i in range(IDX_HEADS * IDX_DIM // LANE):
        pair = rope64(col('q_i', i)).astype(bf)
        qi_ref[2 * i] = pair[:, :IDX_DIM]
        qi_ref[2 * i + 1] = pair[:, IDX_DIM:]
    assert SMALL_OFF['k_i'][0] == 0
    k_idx = rope64(small_ref[...])[:, :IDX_DIM]
    idx_ref[...] = k_idx
    kidx_ref[...] = k_idx.astype(bf)


def rope_split(hproj, pos):
    bg, tg, _ = hproj.shape
    d, bf, f32 = HEAD_DIM, jnp.bfloat16, jnp.float32
    tm = min(ROPE_TM, tg)
    assert tg % tm == 0
    c128, s128 = _rope_tables(pos, tg, d)
    c64, s64 = _rope_tables(pos, tg, IDX_DIM)
    rows = lambda w: pl.BlockSpec((None, tm, w), lambda b, i: (b, i, 0))
    table = pl.BlockSpec((tm, LANE), lambda b, i: (i, 0))
    shape = lambda w, dt: jax.ShapeDtypeStruct((bg, tg, w), dt)
    outs = [(W_A, bf), (W_B, bf), None, (NSA_ROWS * d, f32), (2 * d, f32), (2 * d, f32), (IDX_DIM, f32),
            (d, bf), (d, bf), (d, bf), (d, bf), (d, bf), (d, bf), (IDX_DIM, bf)]
    out_specs = [pl.BlockSpec((None, IDX_HEADS, tm, IDX_DIM), lambda b, i: (b, 0, i, 0)) if o is None else rows(o[0])
                 for o in outs]
    out_shape = [jax.ShapeDtypeStruct((bg, IDX_HEADS, tg, IDX_DIM), bf) if o is None else shape(*o) for o in outs]
    return pl.pallas_call(
        _rope_split_kernel,
        grid=(bg, tg // tm),
        in_specs=[pl.BlockSpec((None, tm, ROPE_COLS), lambda b, i: (b, i, 0)),
                  pl.BlockSpec((None, tm, LANE), lambda b, i: (b, i, PROJ_OFF['small'] // LANE)),
                  table, table, table, table],
        out_specs=out_specs, out_shape=out_shape,
        compiler_params=pltpu.CompilerParams(
            dimension_semantics=("parallel", "parallel"), vmem_limit_bytes=VMEM_LIMIT),
        name="rope_split",
    )(hproj, hproj, c128, s128, c64, s64)


_NT = (((1,), (1,)), ((), ()))


def _softmax_tiles(qh, k_ref, v_ref, lo, hi, tk, mask_fn, m_ref, l_ref, acc_ref, nh, tq):
    assert qh.shape[-1] == LANE and tk % LANE == 0
    m_ref[...] = jnp.full(m_ref.shape, NEG, jnp.float32)
    l_ref[...] = jnp.zeros(l_ref.shape, jnp.float32)
    acc_ref[...] = jnp.zeros(acc_ref.shape, jnp.float32)

    def body(kt, carry):
        off = pl.multiple_of(kt * tk, tk)
        k = k_ref[0, pl.ds(off, tk), :]
        v = v_ref[0, pl.ds(off, tk), :]
        bias = (mask_fn(kt) - 1.0) * (-NEG)
        heads = range(nh)
        chunks = []
        for h in heads:
            s = lax.dot_general(qh[h * tq:(h + 1) * tq], k, _NT, preferred_element_type=jnp.float32) + bias
            chunks.append([s[:, c * LANE:(c + 1) * LANE] for c in range(tk // LANE)])
        m_old = [m_ref[h] for h in heads]
        m_new = [jnp.maximum(m_old[h], jnp.max(functools.reduce(jnp.maximum, chunks[h]), axis=-1, keepdims=True))
                 for h in heads]
        ps = [[jnp.exp(ch - m_new[h]) for ch in chunks[h]] for h in heads]
        alpha = [jnp.exp(m_old[h] - m_new[h]) for h in heads]
        pv = [jnp.dot(jnp.concatenate(ps[h], axis=-1).astype(jnp.bfloat16), v, preferred_element_type=jnp.float32)
              for h in heads]
        for h in heads:
            l_ref[h] = alpha[h] * l_ref[h] + functools.reduce(jnp.add, ps[h])
            acc_ref[h] = alpha[h] * acc_ref[h] + pv[h]
            m_ref[h] = m_new[h]
        return carry

    lax.fori_loop(lo, hi, body, 0)
    return acc_ref[...] / jnp.maximum(jnp.sum(l_ref[...], axis=-1, keepdims=True), 1e-30)


def _split3_bf16(x):
    hi = x.astype(jnp.bfloat16)
    r1 = x - hi.astype(jnp.float32)
    mid = r1.astype(jnp.bfloat16)
    lo = (r1 - mid.astype(jnp.float32)).astype(jnp.bfloat16)
    return hi, mid, lo


def _select_blocks(psum, share, qpos_r, n_blk, tq):
    nbp, tqp = share.shape[0], psum.shape[0]
    imp_t = sum(lax.dot_general(share, part, _NT, preferred_element_type=jnp.float32)
                for part in _split3_bf16(psum))
    blk = lax.broadcasted_iota(jnp.int32, (nbp, tqp), 0)
    forced = (blk == jnp.right_shift(qpos_r, 6)) | (blk == 0)
    future = blk * SEL_BLOCK > qpos_r
    imp_t = jnp.where(forced, FORCED, jnp.where(future, -1.0, imp_t))
    imp_t = jnp.where(blk < n_blk, imp_t, -2.0)
    rank = jnp.zeros((nbp, tqp), jnp.float32)
    for i in range(n_blk):
        row = imp_t[i:i + 1, :]
        beats = (row > imp_t) | ((row == imp_t) & (blk > i))
        rank = rank + beats.astype(jnp.float32)
    sel_t = (rank < float(min(SEL_COUNT, n_blk))).astype(jnp.bfloat16)
    eye = (lax.broadcasted_iota(jnp.int32, (tq, tqp), 0) == lax.broadcasted_iota(jnp.int32, (tq, tqp), 1))
    return lax.dot_general(eye.astype(jnp.bfloat16), sel_t, _NT,
                           preferred_element_type=jnp.float32).astype(jnp.bfloat16)


def _softmax_flat(qh, tiles, nh, tq):
    scores = []
    for k, _, maskf in tiles:
        s = lax.dot_general(qh, k, _NT, preferred_element_type=jnp.float32).reshape(nh, tq, k.shape[0])
        scores.append(s + ((maskf - 1.0) * (-NEG))[None])
    m = scores[0]
    for s in scores[1:]:
        m = jnp.maximum(m, s)
    m = jnp.max(m, axis=-1, keepdims=True)
    lsum, acc = None, None
    for s, (_, v, maskf) in zip(scores, tiles):
        p = jnp.exp(s - m) * maskf[None]
        pv = jnp.dot(p.reshape(nh * tq, p.shape[-1]).astype(jnp.bfloat16), v, preferred_element_type=jnp.float32)
        lsum = p if lsum is None else lsum + p
        acc = pv if acc is None else acc + pv
    l = jnp.sum(lsum, axis=-1, keepdims=True)
    return acc.reshape(nh, tq, acc.shape[-1]) / jnp.maximum(l, 1e-30)


NSA_TQ = 128
NSA_TK_SLC = 512
NSA_TK_WIN = 128


def _nsa_prompt_kernel(q_ref, g_ref, kc_ref, vc_ref, ks_ref, vs_ref, kw_ref, vw_ref, share_ref, o_ref,
                       m_ref, l_ref, acc_ref, *, n_cmp, n_blk, n_keys):
    nh, tq, d = NSA_HEADS, NSA_TQ, HEAD_DIM
    nbp = share_ref.shape[0]
    ncp = share_ref.shape[1]
    start = pl.program_id(1) * tq
    qpos_c = start + lax.broadcasted_iota(jnp.int32, (tq, 1), 0)
    qpos_r = start + lax.broadcasted_iota(jnp.int32, (1, tq), 1)

    qh = jnp.concatenate([q_ref[0, :, h * d:(h + 1) * d] for h in range(nh)], axis=0)

    c_r = lax.broadcasted_iota(jnp.int32, (1, ncp), 1)
    mask_c = ((c_r * CMP_STRIDE + (CMP_LEN - 1) <= qpos_c) & (c_r < n_cmp)).astype(jnp.float32)
    s = lax.dot_general(qh, kc_ref[0], _NT, preferred_element_type=jnp.float32).reshape(nh, tq, ncp)
    sm = s + ((mask_c - 1.0) * (-NEG))[None]
    e = jnp.exp(sm - jnp.max(sm, axis=-1, keepdims=True)) * mask_c[None]
    p_cmp = e / jnp.maximum(jnp.sum(e, axis=-1, keepdims=True), 1e-30)
    o_cmp = jnp.dot(p_cmp.reshape(nh * tq, ncp).astype(jnp.bfloat16), vc_ref[0],
                    preferred_element_type=jnp.float32)

    sel = _select_blocks(jnp.sum(p_cmp, axis=0), share_ref[...], qpos_r, n_blk, tq)

    def slc_mask(kt):
        kpos = kt * NSA_TK_SLC + lax.broadcasted_iota(jnp.int32, (1, NSA_TK_SLC), 1)
        kblk = kt * (NSA_TK_SLC // SEL_BLOCK) + jnp.right_shift(
            lax.broadcasted_iota(jnp.int32, (nbp, NSA_TK_SLC), 1), 6)
        expand = (kblk == lax.broadcasted_iota(jnp.int32, (nbp, NSA_TK_SLC), 0)).astype(jnp.bfloat16)
        chosen = jnp.dot(sel, expand, preferred_element_type=jnp.float32)
        return chosen * (kpos <= qpos_c).astype(jnp.float32)

    hi_slc = jnp.minimum((start + tq - 1) // NSA_TK_SLC + 1, n_keys // NSA_TK_SLC)
    o_slc = _softmax_tiles(qh, ks_ref, vs_ref, 0, hi_slc, NSA_TK_SLC, slc_mask, m_ref, l_ref, acc_ref, nh, tq)

    tiles = []
    for i in range(WINDOW // NSA_TK_WIN + 1):
        kt = pl.program_id(1) * (tq // NSA_TK_WIN) - WINDOW // NSA_TK_WIN + i
        off = pl.multiple_of(jnp.maximum(kt, 0) * NSA_TK_WIN, NSA_TK_WIN)
        kpos = kt * NSA_TK_WIN + lax.broadcasted_iota(jnp.int32, (1, NSA_TK_WIN), 1)
        dlt = qpos_c - kpos
        maskf = ((dlt >= 0) & (dlt < WINDOW) & (kpos >= 0)).astype(jnp.float32)
        tiles.append((kw_ref[0, pl.ds(off, NSA_TK_WIN), :], vw_ref[0, pl.ds(off, NSA_TK_WIN), :], maskf))
    o_win = _softmax_flat(qh, tiles, nh, tq)

    gate = 1.0 / (1.0 + jnp.exp(-g_ref[0]))
    o_cmp = o_cmp.reshape(nh, tq, d)
    for h in range(nh):
        o_ref[0, :, h * d:(h + 1) * d] = (gate[:, 3 * h:3 * h + 1] * o_cmp[h]
                                          + gate[:, 3 * h + 1:3 * h + 2] * o_slc[h]
                                          + gate[:, 3 * h + 2:3 * h + 3] * o_win[h])


def _share_matrix_t(n_cmp, n_blk, ncp, nbp):
    c0 = np.arange(n_cmp)[None, :] * CMP_STRIDE
    j0 = np.arange(n_blk)[:, None] * SEL_BLOCK
    share = np.clip(np.minimum(c0 + CMP_LEN, j0 + SEL_BLOCK) - np.maximum(c0, j0), 0, None) / CMP_LEN
    out = np.zeros((nbp, ncp), np.float32)
    out[:n_blk, :n_cmp] = share
    return jnp.asarray(out, jnp.bfloat16)


def nsa_prompt_attention(q, gates, k_cmp, v_cmp, n_cmp, k_slc, v_slc, k_win, v_win):
    b, t, _ = q.shape
    d = HEAD_DIM
    n_blk = -(-t // SEL_BLOCK)
    ncp = -(-n_cmp // LANE) * LANE
    nbp = -(-n_blk // 16) * 16
    assert t % NSA_TK_SLC == 0 and t % NSA_TQ == 0
    bf = jnp.bfloat16
    cpad = ((0, 0), (0, ncp - k_cmp.shape[1]), (0, 0))
    kc, vc = jnp.pad(k_cmp, cpad).astype(bf), jnp.pad(v_cmp, cpad).astype(bf)
    whole = lambda n: pl.BlockSpec((1, n, d), lambda bi, qi: (bi, 0, 0))
    return pl.pallas_call(
        functools.partial(_nsa_prompt_kernel, n_cmp=n_cmp, n_blk=n_blk, n_keys=t),
        grid=(b, t // NSA_TQ),
        in_specs=[pl.BlockSpec((1, NSA_TQ, NSA_HEADS * d), lambda bi, qi: (bi, qi, 0)),
                  pl.BlockSpec((1, NSA_TQ, 3 * NSA_HEADS), lambda bi, qi: (bi, qi, 0)),
                  whole(ncp), whole(ncp), whole(t), whole(t), whole(t), whole(t),
                  pl.BlockSpec((nbp, ncp), lambda bi, qi: (0, 0))],
        out_specs=pl.BlockSpec((1, NSA_TQ, NSA_HEADS * d), lambda bi, qi: (bi, qi, 0)),
        out_shape=jax.ShapeDtypeStruct((b, t, NSA_HEADS * d), jnp.float32),
        scratch_shapes=[pltpu.VMEM((NSA_HEADS, NSA_TQ, LANE), jnp.float32),
                        pltpu.VMEM((NSA_HEADS, NSA_TQ, LANE), jnp.float32),
                        pltpu.VMEM((NSA_HEADS, NSA_TQ, d), jnp.float32)],
        compiler_params=pltpu.CompilerParams(
            dimension_semantics=("parallel", "arbitrary"), vmem_limit_bytes=VMEM_LIMIT),
        name="nsa_prompt",
    )(q, gates, kc, vc, k_slc.astype(bf), v_slc.astype(bf), k_win.astype(bf), v_win.astype(bf),
      _share_matrix_t(n_cmp, n_blk, ncp, nbp))


NSA_ROWS = 4


def _gelu_tanh(x):
    return 0.5 * x * (1.0 + jnp.tanh(math.sqrt(2.0 / math.pi) * (x + 0.044715 * (x * x * x))))


def _pad_rows(x, rows):
    if rows == x.shape[0]:
        return x
    return jnp.concatenate([x, jnp.zeros((rows - x.shape[0], x.shape[1]), x.dtype)], axis=0)


def _compress_tokens(seg_rows, pe_ref, w1_ref, w2_ref, kv, ncp):
    d, bf = HEAD_DIM, jnp.bfloat16
    first = jnp.zeros((ncp, CMP_HID), jnp.float32)
    second = jnp.zeros((ncp, CMP_HID), jnp.float32)
    for r in range(0, CMP_STRIDE, 2):
        ya, yb = seg_rows(r), seg_rows(r + 1)
        for acc_off in (0, CMP_STRIDE):
            lhs = jnp.concatenate([ya + pe_ref[kv, acc_off + r:acc_off + r + 1, :],
                                   yb + pe_ref[kv, acc_off + r + 1:acc_off + r + 2, :]], axis=1).astype(bf)
            w = w1_ref[kv, acc_off + r:acc_off + r + 2].reshape(2 * d, CMP_HID)
            y = jnp.dot(lhs, w, preferred_element_type=jnp.float32)
            if acc_off == 0:
                first = first + y
            else:
                second = second + y
    hid = first + pltpu.roll(second, ncp - 1, 0)
    return jnp.dot(_gelu_tanh(hid).astype(bf), w2_ref[kv], preferred_element_type=jnp.float32)


def _compress_prompt_kernel(rows_ref, pe_ref, w1_ref, w2_ref, cos_ref, sin_ref, kc_ref, vc_ref):
    ncp, d = kc_ref.shape[0], HEAD_DIM

    def seg_rows(kv, r):
        return rows_ref[pl.ds(NSA_ROWS * r + kv, ncp, stride=NSA_ROWS * CMP_STRIDE), :]

    kc = _compress_tokens(functools.partial(seg_rows, 0), pe_ref, w1_ref, w2_ref, 0, ncp)
    kc_ref[...] = (kc * cos_ref[...] + pltpu.roll(kc, d // 2, 1) * sin_ref[...]).astype(kc_ref.dtype)
    vc_ref[...] = _compress_tokens(functools.partial(seg_rows, 1), pe_ref, w1_ref, w2_ref, 1, ncp).astype(vc_ref.dtype)


def nsa_compress_prompt(nsa_new, pe, w1, w2):
    b, t = nsa_new.shape[:2]
    d, bf = HEAD_DIM, jnp.bfloat16
    ncp = t // CMP_STRIDE
    n_cmp = (t - CMP_LEN) // CMP_STRIDE + 1
    assert ncp % 8 == 0 and n_cmp <= ncp
    cos, sin = _rope_tables(jnp.arange(n_cmp) * CMP_STRIDE + CMP_LEN - 1, ncp)
    const = lambda *shape: pl.BlockSpec(shape, lambda i: (0,) * len(shape))
    out = pl.BlockSpec((None, ncp, d), lambda i: (i, 0, 0))
    return pl.pallas_call(
        _compress_prompt_kernel,
        grid=(b,),
        in_specs=[pl.BlockSpec((None, t * NSA_ROWS, d), lambda i: (i, 0, 0)),
                  const(2, CMP_LEN, d), const(2, CMP_LEN, d, CMP_HID), const(2, CMP_HID, d), const(ncp, d), const(ncp, d)],
        out_specs=[out, out],
        out_shape=[jax.ShapeDtypeStruct((b, ncp, d), bf)] * 2,
        compiler_params=pltpu.CompilerParams(dimension_semantics=("parallel",), vmem_limit_bytes=VMEM_LIMIT),
        name="nsa_compress",
    )(nsa_new.reshape(b, t * NSA_ROWS, d), pe, w1.reshape(2, CMP_LEN, d, CMP_HID).astype(bf), w2.astype(bf), cos, sin)


def _gated_store(o_ref, g_ref, branches, nh, d):
    gate = 1.0 / (1.0 + jnp.exp(-g_ref[...]))
    for h in range(nh):
        o_ref[:, h * d:(h + 1) * d] = sum(gate[:, 3 * h + i:3 * h + i + 1] * br[h] for i, br in enumerate(branches))


def _nsa_sample_kernel(pt_ref, q_ref, g_ref, new_ref, winc_ref, winn_ref, pe_ref, w1_ref, w2_ref, cos_ref, sin_ref,
                       share_ref, *rest, past, n_cmp, n_blk):
    del pt_ref
    pages, o_ref = rest[:-1], rest[-1]
    nh, d, pg = NSA_HEADS, HEAD_DIM, PAGE_SIZE
    tq = q_ref.shape[0]
    nbp, ncp = share_ref.shape
    bf = jnp.bfloat16
    seg_per_page = pg // CMP_STRIDE
    assert len(pages) * seg_per_page == ncp == LANE and CMP_LEN == 2 * CMP_STRIDE
    qpos_c = past + lax.broadcasted_iota(jnp.int32, (tq, 1), 0)
    qpos_r = past + lax.broadcasted_iota(jnp.int32, (1, LANE), 1)
    lane = lax.broadcasted_iota(jnp.int32, (1, LANE), 1)

    def seg_rows(kv, r):
        return jnp.concatenate([p[pl.ds(NSA_ROWS * r + kv, seg_per_page, stride=NSA_ROWS * CMP_STRIDE), :]
                                for p in pages], axis=0)

    kc = _compress_tokens(functools.partial(seg_rows, 0), pe_ref, w1_ref, w2_ref, 0, ncp)
    kc = (kc * cos_ref[...] + pltpu.roll(kc, d // 2, 1) * sin_ref[...]).astype(bf)
    vc = _compress_tokens(functools.partial(seg_rows, 1), pe_ref, w1_ref, w2_ref, 1, ncp).astype(bf)

    q = q_ref[...].astype(jnp.float32)
    qh = jnp.concatenate([q[:, h * d:(h + 1) * d] for h in range(nh)], axis=0).astype(bf)

    mask_c = ((lane * CMP_STRIDE + (CMP_LEN - 1) <= qpos_c) & (lane < n_cmp)).astype(jnp.float32)
    s = lax.dot_general(qh, kc, _NT, preferred_element_type=jnp.float32).reshape(nh, tq, ncp)
    sm = s + ((mask_c - 1.0) * (-NEG))[None]
    e = jnp.exp(sm - jnp.max(sm, axis=-1, keepdims=True)) * mask_c[None]
    p_cmp = e / jnp.maximum(jnp.sum(e, axis=-1, keepdims=True), 1e-30)
    o_cmp = jnp.dot(p_cmp.reshape(nh * tq, ncp).astype(bf), vc, preferred_element_type=jnp.float32).reshape(nh, tq, d)

    sel = _select_blocks(_pad_rows(jnp.sum(p_cmp, axis=0), LANE), share_ref[...], qpos_r, n_blk, LANE)

    def slc_mask(key0):
        kblk = key0 // SEL_BLOCK + jnp.right_shift(lax.broadcasted_iota(jnp.int32, (nbp, LANE), 1), 6)
        expand = (kblk == lax.broadcasted_iota(jnp.int32, (nbp, LANE), 0)).astype(bf)
        chosen = jnp.dot(sel, expand, preferred_element_type=jnp.float32)[:tq]
        return chosen * (key0 + lane <= qpos_c).astype(jnp.float32)

    def component(ref, comp, n_comp, row0, rows):
        return ref[pl.ds(n_comp * row0 + comp, rows, stride=n_comp), :]

    tiles = [(component(p, 2, NSA_ROWS, 0, pg).astype(bf), component(p, 3, NSA_ROWS, 0, pg).astype(bf), slc_mask(j * pg))
             for j, p in enumerate(pages)]
    tiles.append((_pad_rows(component(new_ref, 2, NSA_ROWS, 0, tq), LANE).astype(bf),
                  _pad_rows(component(new_ref, 3, NSA_ROWS, 0, tq), LANE).astype(bf), slc_mask(past)))
    o_slc = _softmax_flat(qh, tiles, nh, tq)

    def win_mask(key0):
        dlt = qpos_c - (key0 + lane)
        return ((dlt >= 0) & (dlt < WINDOW)).astype(jnp.float32)

    n_wc = winc_ref.shape[0] // 2
    tiles = [(component(winc_ref, 0, 2, j * LANE, LANE).astype(bf), component(winc_ref, 1, 2, j * LANE, LANE).astype(bf),
              win_mask(past - n_wc + j * LANE)) for j in range(n_wc // LANE)]
    tiles.append((_pad_rows(component(winn_ref, 0, 2, 0, tq), LANE).astype(bf),
                  _pad_rows(component(winn_ref, 1, 2, 0, tq), LANE).astype(bf), win_mask(past)))
    o_win = _softmax_flat(qh, tiles, nh, tq)

    _gated_store(o_ref, g_ref, (o_cmp, o_slc, o_win), nh, d)


def nsa_sample_attention(layer, q, gates, nsa_new, win_new, pool, win_cache, page_table, pe, w1, w2):
    b, t, _ = q.shape
    wlen = win_cache.shape[2]
    nsa_new = nsa_new.reshape(b, t * NSA_ROWS, HEAD_DIM)
    win_new = win_new.reshape(b, t * 2, HEAD_DIM)
    pool = pool.reshape(pool.shape[0], pool.shape[1], PAGE_SIZE * NSA_ROWS, HEAD_DIM)
    win_cache = win_cache.reshape(win_cache.shape[0], b, wlen * 2, HEAD_DIM)
    d, n_pages = HEAD_DIM, page_table.shape[1]
    past = n_pages * PAGE_SIZE
    n_keys = past + t
    n_cmp = (n_keys - CMP_LEN) // CMP_STRIDE + 1
    n_blk = -(-n_keys // SEL_BLOCK)
    ncp = -(-n_cmp // LANE) * LANE
    nbp = -(-n_blk // 16) * 16
    assert (n_cmp + 1) * CMP_STRIDE <= past, "compressed tokens must come from cached rows only"
    cos, sin = _rope_tables(jnp.arange(n_cmp) * CMP_STRIDE + CMP_LEN - 1, ncp)
    bf = jnp.bfloat16
    per_seq = lambda *tail: pl.BlockSpec((None,) + tail, lambda i, pt: (i,) + (0,) * len(tail))
    const = lambda *shape: pl.BlockSpec(shape, lambda i, pt: (0,) * len(shape))
    page = lambda j: pl.BlockSpec((None, None, PAGE_SIZE * NSA_ROWS, d), lambda i, pt: (layer, pt[i, j], 0, 0))
    grid_spec = pltpu.PrefetchScalarGridSpec(
        num_scalar_prefetch=1, grid=(b,),
        in_specs=[per_seq(t, NSA_HEADS * d), per_seq(t, 3 * NSA_HEADS), per_seq(t * NSA_ROWS, d),
                  pl.BlockSpec((None, None, wlen * 2, d), lambda i, pt: (layer, i, 0, 0)),
                  per_seq(t * 2, d),
                  const(2, CMP_LEN, d), const(2, CMP_LEN, d, CMP_HID), const(2, CMP_HID, d),
                  const(ncp, d), const(ncp, d), const(nbp, ncp)] + [page(j) for j in range(n_pages)],
        out_specs=per_seq(t, NSA_HEADS * d))
    return pl.pallas_call(
        functools.partial(_nsa_sample_kernel, past=past, n_cmp=n_cmp, n_blk=n_blk),
        grid_spec=grid_spec,
        out_shape=jax.ShapeDtypeStruct((b, t, NSA_HEADS * d), jnp.float32),
        compiler_params=pltpu.CompilerParams(dimension_semantics=("parallel",), vmem_limit_bytes=VMEM_LIMIT),
        name="nsa_sample",
    )(page_table, q, gates, nsa_new, win_cache, win_new, pe, w1.reshape(2, CMP_LEN, d, CMP_HID).astype(bf),
      w2.astype(bf), cos, sin, _share_matrix_t(n_cmp, n_blk, ncp, nbp), *([pool] * n_pages))


DSA_TQ = 128
DSA_TK = 256
INT_MIN = -2 ** 31


def _dsa_prompt_kernel(q_ref, qi_ref, wt_ref, k_ref, v_ref, ki_ref, tri_ref, o_ref,
                       key_ref, sel_ref, m_ref, l_ref, acc_ref, *, n_keys, topk):
    nh, tq, d, tk = DSA_HEADS, DSA_TQ, HEAD_DIM, DSA_TK
    start = pl.program_id(1) * tq
    qpos_r = start + lax.broadcasted_iota(jnp.int32, (1, tq), 1)
    n_kt = jnp.minimum((start + tq - 1) // tk + 1, n_keys // tk)

    wt = wt_ref[0] * (IDX_HEADS ** -0.5 * IDX_DIM ** -0.5)

    def score_body(kt, carry):
        off = pl.multiple_of(kt * tk, tk)
        ki = ki_ref[0, pl.ds(off, tk), :]
        acc = jnp.zeros((tk, tq), jnp.float32)
        for h in range(0, IDX_HEADS, 2):
            pair = qi_ref[0, h:h + 2].reshape(2 * tq, IDX_DIM)
            dots = lax.dot_general(ki, pair, _NT, preferred_element_type=jnp.float32)
            acc = (acc + jnp.maximum(dots[:, :tq], 0.0) * wt[h:h + 1, :]
                   + jnp.maximum(dots[:, tq:], 0.0) * wt[h + 1:h + 2, :])
        kpos_c = off + lax.broadcasted_iota(jnp.int32, (tk, 1), 0)
        score = jnp.where(kpos_c <= qpos_r, acc, NEG)
        bits = pltpu.bitcast(score, jnp.int32)
        key_ref[kt] = bits ^ (jnp.right_shift(bits, 31) & 0x7FFFFFFF)
        return carry

    lax.fori_loop(0, n_kt, score_body, 0)

    def count(pred):
        def body(kt, part):
            hit = pred(key_ref[kt]).astype(jnp.int32)
            return part + jnp.sum(hit.reshape(tk // 8, 8, tq), axis=0)
        part = lax.fori_loop(0, n_kt, body, jnp.zeros((8, tq), jnp.int32))
        return jnp.sum(part, axis=0, keepdims=True)

    thr = jnp.where(count(lambda kk: kk >= 0) >= topk, 0, INT_MIN).astype(jnp.int32)

    def bit_body(i, t):
        cand = t | jnp.left_shift(jnp.int32(1), 30 - i)
        return jnp.where(count(lambda kk: kk >= cand) >= topk, cand, t)

    thr = lax.fori_loop(0, 31, bit_body, thr)
    room = (topk - count(lambda kk: kk > thr)).astype(jnp.float32)

    def sel_body(kt, seen):
        kk = key_ref[kt]
        eq = (kk == thr).astype(jnp.float32)
        prefix = jnp.dot(tri_ref[...], eq.astype(jnp.bfloat16), preferred_element_type=jnp.float32)
        keep = (kk > thr) | ((eq > 0.0) & (prefix + seen <= room))
        kpos_c = kt * tk + lax.broadcasted_iota(jnp.int32, (tk, 1), 0)
        sel_ref[kt] = (keep & (kpos_c <= qpos_r)).astype(jnp.bfloat16)
        return seen + jnp.sum(eq, axis=0, keepdims=True)

    lax.fori_loop(0, n_kt, sel_body, jnp.zeros((1, tq), jnp.float32))

    eye = (lax.broadcasted_iota(jnp.int32, (tq, tq), 0)
           == lax.broadcasted_iota(jnp.int32, (tq, tq), 1)).astype(jnp.bfloat16)

    def sel_mask(kt):
        return lax.dot_general(eye, sel_ref[kt], _NT, preferred_element_type=jnp.float32)

    qh = jnp.concatenate([q_ref[0, :, h * d:(h + 1) * d] for h in range(nh)], axis=0)
    o = _softmax_tiles(qh, k_ref, v_ref, 0, n_kt, tk, sel_mask, m_ref, l_ref, acc_ref, nh, tq)
    for h in range(nh):
        o_ref[0, :, h * d:(h + 1) * d] = o[h]


def dsa_prompt_attention(q, q_idx, w_idx, k, v, k_idx):
    b, t, _ = q.shape
    d = HEAD_DIM
    topk = min(DSA_TOPK, t // 4)
    assert t % DSA_TK == 0 and t % DSA_TQ == 0
    bf = jnp.bfloat16
    tri = jnp.asarray(np.tril(np.ones((DSA_TK, DSA_TK), np.float32)), bf)
    whole = lambda n, w: pl.BlockSpec((1, n, w), lambda bi, qi: (bi, 0, 0))
    return pl.pallas_call(
        functools.partial(_dsa_prompt_kernel, n_keys=t, topk=topk),
        grid=(b, t // DSA_TQ),
        in_specs=[pl.BlockSpec((1, DSA_TQ, DSA_HEADS * d), lambda bi, qi: (bi, qi, 0)),
                  pl.BlockSpec((1, IDX_HEADS, DSA_TQ, IDX_DIM), lambda bi, qi: (bi, 0, qi, 0)),
                  pl.BlockSpec((1, IDX_HEADS, DSA_TQ), lambda bi, qi: (bi, 0, qi)),
                  whole(t, d), whole(t, d), whole(t, IDX_DIM),
                  pl.BlockSpec((DSA_TK, DSA_TK), lambda bi, qi: (0, 0))],
        out_specs=pl.BlockSpec((1, DSA_TQ, DSA_HEADS * d), lambda bi, qi: (bi, qi, 0)),
        out_shape=jax.ShapeDtypeStruct((b, t, DSA_HEADS * d), jnp.float32),
        scratch_shapes=[pltpu.VMEM((t // DSA_TK, DSA_TK, DSA_TQ), jnp.int32),
                        pltpu.VMEM((t // DSA_TK, DSA_TK, DSA_TQ), bf),
                        pltpu.VMEM((DSA_HEADS, DSA_TQ, LANE), jnp.float32),
                        pltpu.VMEM((DSA_HEADS, DSA_TQ, LANE), jnp.float32),
                        pltpu.VMEM((DSA_HEADS, DSA_TQ, d), jnp.float32)],
        compiler_params=pltpu.CompilerParams(
            dimension_semantics=("parallel", "arbitrary"), vmem_limit_bytes=VMEM_LIMIT),
        name="dsa_prompt",
    )(q, q_idx.astype(bf), jnp.swapaxes(w_idx, 1, 2),
      k.astype(bf), v.astype(bf), k_idx.astype(bf), tri)


SEARCH_BITS = 3


def _dsa_sample_kernel(pt_ref, q_ref, qi_ref, w_ref, new_ref, inew_ref, tri_ref, *rest, past, topk):
    del pt_ref
    n_pages = (len(rest) - 1) // 2
    kv_pages, idx_pages, o_ref = rest[:n_pages], rest[n_pages:2 * n_pages], rest[-1]
    nh, d, pg = DSA_HEADS, HEAD_DIM, PAGE_SIZE
    tq = q_ref.shape[0]
    bf = jnp.bfloat16
    qpos_c = past + lax.broadcasted_iota(jnp.int32, (tq, 1), 0)
    lane = lax.broadcasted_iota(jnp.int32, (1, LANE), 1)

    qi = qi_ref[...]
    wb = jnp.broadcast_to(w_ref[...] * (IDX_HEADS ** -0.5 * IDX_DIM ** -0.5), (IDX_HEADS * tq, LANE))

    def key_tile(ki_t, key0):
        dots = jnp.dot(qi, ki_t, preferred_element_type=jnp.float32)
        score = jnp.sum((jnp.maximum(dots, 0.0) * wb).reshape(IDX_HEADS, tq, LANE), axis=0)
        score = jnp.where(key0 + lane <= qpos_c, score, NEG)
        bits = pltpu.bitcast(score, jnp.int32)
        return bits ^ (jnp.right_shift(bits, 31) & 0x7FFFFFFF)

    keys = [key_tile(p[...].astype(bf), j * pg) for j, p in enumerate(idx_pages)]
    keys.append(key_tile(inew_ref[...].astype(bf), past))

    def count(pred):
        hits = pred(keys[0]).astype(jnp.int32)
        for kk in keys[1:]:
            hits = hits + pred(kk).astype(jnp.int32)
        return jnp.sum(hits, axis=-1, keepdims=True)

    def at_least(cand):
        return (count(lambda kk: kk >= cand) >= topk).astype(jnp.int32)

    thr = jnp.where(at_least(0) > 0, 0, INT_MIN).astype(jnp.int32)
    n_steps, last_bits = divmod(31, SEARCH_BITS)

    def radix_body(i, t):
        shift = 31 - SEARCH_BITS * (i + 1)
        digit = sum(at_least(t | jnp.left_shift(jnp.int32(j), shift)) for j in range(1, 2 ** SEARCH_BITS))
        return t | jnp.left_shift(digit, shift)

    thr = lax.fori_loop(0, n_steps, radix_body, thr)
    if last_bits:
        thr = thr | sum(at_least(thr | j) for j in range(1, 2 ** last_bits))
    room = (topk - count(lambda kk: kk > thr)).astype(jnp.float32)
    eqs = [(kk == thr).astype(jnp.float32) for kk in keys]
    totals = [jnp.sum(e, axis=-1, keepdims=True) for e in eqs]
    seen = jnp.zeros((tq, 1), jnp.float32)
    masks = []
    for j, (kk, e) in enumerate(zip(keys, eqs)):
        prefix = jnp.dot(e, tri_ref[...], preferred_element_type=jnp.float32)
        keep = (kk > thr) | ((e > 0.0) & (prefix + seen <= room))
        key0 = j * pg if j < n_pages else past
        masks.append((keep & (key0 + lane <= qpos_c)).astype(jnp.float32))
        seen = seen + totals[j]

    def component(ref, comp, rows):
        return ref[pl.ds(comp, rows, stride=2), :]

    tiles = [(component(p, 0, pg).astype(bf), component(p, 1, pg).astype(bf), masks[j]) for j, p in enumerate(kv_pages)]
    tiles.append((_pad_rows(component(new_ref, 0, tq), LANE).astype(bf),
                  _pad_rows(component(new_ref, 1, tq), LANE).astype(bf), masks[n_pages]))
    q = q_ref[...].astype(jnp.float32)
    qh = jnp.concatenate([q[:, h * d:(h + 1) * d] for h in range(nh)], axis=0).astype(bf)
    o = _softmax_flat(qh, tiles, nh, tq)
    for h in range(nh):
        o_ref[:, h * d:(h + 1) * d] = o[h]


def dsa_sample_attention(layer, q, q_idx, w_idx, dsa_new, idx_new, pool, idx_pool, page_table):
    b, t, _ = q.shape
    d, n_pages = HEAD_DIM, page_table.shape[1]
    past = n_pages * PAGE_SIZE
    topk = min(DSA_TOPK, (past + t) // 4)
    pool = pool.reshape(pool.shape[0], pool.shape[1], PAGE_SIZE * 2, d)
    idx_pool_t = jnp.swapaxes(idx_pool, 2, 3)
    idx_new_t = jnp.pad(jnp.swapaxes(idx_new, 1, 2), ((0, 0), (0, 0), (0, LANE - t)))
    tri = jnp.asarray(np.triu(np.ones((LANE, LANE), np.float32)))
    per_seq = lambda *tail: pl.BlockSpec((None,) + tail, lambda i, pt: (i,) + (0,) * len(tail))
    kv_page = lambda j: pl.BlockSpec((None, None, PAGE_SIZE * 2, d), lambda i, pt: (layer, pt[i, j], 0, 0))
    idx_page = lambda j: pl.BlockSpec((None, None, IDX_DIM, PAGE_SIZE), lambda i, pt: (layer, pt[i, j], 0, 0))
    grid_spec = pltpu.PrefetchScalarGridSpec(
        num_scalar_prefetch=1, grid=(b,),
        in_specs=[per_seq(t, DSA_HEADS * d), per_seq(IDX_HEADS * t, IDX_DIM), per_seq(IDX_HEADS * t, 1),
                  per_seq(t * 2, d), per_seq(IDX_DIM, LANE), pl.BlockSpec((LANE, LANE), lambda i, pt: (0, 0))]
                 + [kv_page(j) for j in range(n_pages)] + [idx_page(j) for j in range(n_pages)],
        out_specs=per_seq(t, DSA_HEADS * d))
    return pl.pallas_call(
        functools.partial(_dsa_sample_kernel, past=past, topk=topk),
        grid_spec=grid_spec,
        out_shape=jax.ShapeDtypeStruct((b, t, DSA_HEADS * d), jnp.float32),
        compiler_params=pltpu.CompilerParams(dimension_semantics=("parallel",), vmem_limit_bytes=VMEM_LIMIT),
        name="dsa_sample",
    )(page_table, q, q_idx.reshape(b, IDX_HEADS * t, IDX_DIM).astype(jnp.bfloat16),
      jnp.swapaxes(w_idx, 1, 2).reshape(b, IDX_HEADS * t, 1), dsa_new.reshape(b, t * 2, d), idx_new_t, tri,
      *([pool] * n_pages), *([idx_pool_t] * n_pages))


GDN_GROUP = 256
GDN_SUB = 16
GDN_HB = 4
CONV_PAD = 8


def _hp_dot(a, b):
    bf = jnp.bfloat16
    ah, bh = a.astype(bf), b.astype(bf)
    al, bl = (a - ah.astype(jnp.float32)).astype(bf), (b - bh.astype(jnp.float32)).astype(bf)
    dot = functools.partial(jnp.dot, preferred_element_type=jnp.float32)
    return dot(ah, bh) + dot(ah, bl) + dot(al, bh)


def _unit_lower_inverse(a, row, col):
    assert GDN_SUB == 16 and GDN_CHUNK == 64
    n = range(len(a))
    eye = (row == col).astype(jnp.float32)
    sub = jnp.right_shift(row, 4) == jnp.right_shift(col, 4)
    a16 = [jnp.where(sub, a[i], 0.0) for i in n]
    t16 = [eye - a16[i] for i in n]
    power = a16
    for _ in range(3):
        power = [_hp_dot(power[i], power[i]) for i in n]
        t16 = [t16[i] + _hp_dot(t16[i], power[i]) for i in n]
    b = [_hp_dot(t16[i], a[i] - a16[i]) for i in n]
    b2 = [_hp_dot(b[i], b[i]) for i in n]
    imb = [eye - b[i] for i in n]
    left = [imb[i] + _hp_dot(imb[i], b2[i]) for i in n]
    return [_hp_dot(left[i], t16[i]) for i in n]


def _gdn_prompt_kernel(xq_ref, xk_ref, xv_ref, pq_ref, pk_ref, pv_ref, a_ref, b_ref, z_ref, cwq_ref, cwk_ref, cwv_ref,
                       alog_ref, dtb_ref, ng_ref, s0_ref, o_ref, s_ref, hist_ref):
    f32, bf = jnp.float32, jnp.bfloat16
    d, g, c = HEAD_DIM, GDN_GROUP, GDN_CHUNK
    heads = range(GDN_HB)
    lanes = [slice(h * d, (h + 1) * d) for h in heads]
    row = lax.broadcasted_iota(jnp.int32, (g, g), 0)
    col = lax.broadcasted_iota(jnp.int32, (g, g), 1)
    same = jnp.right_shift(row, 6) == jnp.right_shift(col, 6)
    incl = same & (row >= col)
    strict = same & (row > col)
    eye = (row == col).astype(f32)
    tril_b, same_b = incl.astype(bf), same.astype(bf)

    @pl.when(pl.program_id(2) == 0)
    def _():
        s_ref[...] = s0_ref[...]
        for part, p_ref in enumerate((pq_ref, pk_ref, pv_ref)):
            hist_ref[part] = p_ref[...]

    pre = a_ref[...] + dtb_ref[...]
    gate = -jnp.exp(alog_ref[...]) * (jnp.maximum(pre, 0.0) + jnp.log(1.0 + jnp.exp(-jnp.abs(pre))))
    parts = _split3_bf16(gate)
    gcum_all = sum(jnp.dot(tril_b, p, preferred_element_type=f32) for p in parts)
    gtot_all = sum(jnp.dot(same_b, p, preferred_element_type=f32) for p in parts)
    beta_all = 1.0 / (1.0 + jnp.exp(-b_ref[...]))

    def conv(part, x_ref, w_ref, h):
        win = jnp.concatenate([hist_ref[part, :, lanes[h]], x_ref[:, lanes[h]]], axis=0)
        y = sum(pltpu.roll(win, g + CONV_PAD - (CONV_PAD - GDN_CONV + 1 + i), 0)[:g] * w_ref[i:i + 1, lanes[h]]
                for i in range(GDN_CONV))
        return y / (1.0 + jnp.exp(-y))

    q = [conv(0, xq_ref, cwq_ref, h) for h in heads]
    k = [conv(1, xk_ref, cwk_ref, h) for h in heads]
    v = [conv(2, xv_ref, cwv_ref, h) for h in heads]
    for part, x_ref in enumerate((xq_ref, xk_ref, xv_ref)):
        hist_ref[part] = x_ref[g - CONV_PAD:g, :]
    q = [q[h] * lax.rsqrt(jnp.sum(q[h] * q[h], axis=-1, keepdims=True) + EPS) * (d ** -0.5) for h in heads]
    k = [k[h] * lax.rsqrt(jnp.sum(k[h] * k[h], axis=-1, keepdims=True) + EPS) for h in heads]
    gc = [gcum_all[:, h:h + 1] for h in heads]
    gl = [gtot_all[:, h:h + 1] for h in heads]
    beta = [beta_all[:, h:h + 1] for h in heads]
    g_i = [jnp.broadcast_to(gc[h], (g, g)) for h in heads]
    g_j = [jnp.sum(g_i[h] * eye, axis=0, keepdims=True) for h in heads]
    decay = [jnp.where(incl, jnp.exp(jnp.where(incl, g_i[h] - g_j[h], 0.0)), 0.0) for h in heads]
    kb = [k[h] * beta[h] for h in heads]
    k16 = [k[h].astype(bf) for h in heads]
    eg = [jnp.exp(gc[h]) for h in heads]
    a_mat = [jnp.where(strict, lax.dot_general(kb[h].astype(bf), k16[h], _NT, preferred_element_type=f32) * decay[h], 0.0)
             for h in heads]
    qk = [(lax.dot_general(q[h].astype(bf), k16[h], _NT, preferred_element_type=f32) * decay[h]).astype(bf)
          for h in heads]
    rhs = [jnp.concatenate([v[h] * beta[h], kb[h] * eg[h]], axis=1) for h in heads]
    inv = _unit_lower_inverse(a_mat, row, col)
    sol = [_hp_dot(inv[h], rhs[h]) for h in heads]
    u = [sol[h][:, :d] for h in heads]
    w16 = [sol[h][:, d:].astype(bf) for h in heads]
    q_dec = [(q[h] * eg[h]).astype(bf) for h in heads]
    k_dec = [(k[h] * jnp.exp(gl[h] - gc[h])).astype(bf) for h in heads]
    chunk_decay = [jnp.exp(gl[h]) for h in heads]
    for ci in range(g // c):
        rows = slice(ci * c, (ci + 1) * c)
        state = [s_ref[h] for h in heads]
        s16 = [state[h].astype(bf) for h in heads]
        nv16 = [(u[h][rows] - jnp.dot(w16[h][rows], s16[h], preferred_element_type=f32)).astype(bf) for h in heads]
        pad = lambda n: [jnp.zeros((n, d), bf)] if n else []
        placed = [jnp.concatenate(pad(ci * c) + [nv16[h]] + pad(g - (ci + 1) * c), axis=0) for h in heads]
        out = [jnp.dot(q_dec[h][rows], s16[h], preferred_element_type=f32)
               + jnp.dot(qk[h][rows], placed[h], preferred_element_type=f32) for h in heads]
        for h in heads:
            s_ref[h] = state[h] * chunk_decay[h][ci * c:ci * c + 1, :] + lax.dot_general(
                k_dec[h][rows], nv16[h], (((0,), (0,)), ((), ())), preferred_element_type=f32)
        for h in heads:
            zz = z_ref[rows, lanes[h]]
            normed = out[h] * lax.rsqrt(jnp.mean(out[h] * out[h], axis=-1, keepdims=True) + EPS) * ng_ref[...]
            o_ref[rows, lanes[h]] = normed * (zz / (1.0 + jnp.exp(-zz)))


def gdn_prompt(src, qkv_off, z_off, a, beta_logit, conv_prev, s0, conv_w, a_log, dt_bias, norm_g):
    b, t, _ = src.shape
    d, hb = HEAD_DIM, GDN_HB
    ng = GDN_HEADS // hb
    assert t % GDN_GROUP == 0 and GDN_HEADS % hb == 0 and qkv_off % (hb * d) == 0 and z_off % (hb * d) == 0
    qb, zb = qkv_off // (hb * d), z_off // (hb * d)
    prev = jnp.pad(conv_prev.astype(src.dtype), ((0, 0), (CONV_PAD - GDN_CONV + 1, 0), (0, 0)))
    by_group = lambda x: jnp.swapaxes(x.reshape(b, t, ng, hb), 1, 2)
    g = GDN_GROUP
    cols = lambda part: pl.BlockSpec((None, g, hb * d), lambda bi, gi, ti: (bi, ti, qb + part * ng + gi))
    hist = lambda part: pl.BlockSpec((None, CONV_PAD, hb * d), lambda bi, gi, ti: (bi, 0, part * ng + gi))
    cw = lambda part: pl.BlockSpec((GDN_CONV, hb * d), lambda bi, gi, ti: (0, part * ng + gi))
    tok = pl.BlockSpec((None, None, g, hb), lambda bi, gi, ti: (bi, gi, ti, 0))
    head_const = pl.BlockSpec((None, 1, hb), lambda bi, gi, ti: (gi, 0, 0))
    state = pl.BlockSpec((None, hb, d, d), lambda bi, gi, ti: (bi, gi, 0, 0))
    return pl.pallas_call(
        _gdn_prompt_kernel,
        grid=(b, ng, t // g),
        in_specs=[cols(0), cols(1), cols(2), hist(0), hist(1), hist(2), tok, tok,
                  pl.BlockSpec((None, g, hb * d), lambda bi, gi, ti: (bi, ti, zb + gi)),
                  cw(0), cw(1), cw(2), head_const, head_const,
                  pl.BlockSpec((1, d), lambda bi, gi, ti: (0, 0)), state],
        out_specs=[pl.BlockSpec((None, g, hb * d), lambda bi, gi, ti: (bi, ti, gi)), state],
        out_shape=[jax.ShapeDtypeStruct((b, t, W_C), jnp.float32),
                   jax.ShapeDtypeStruct((b, GDN_HEADS, d, d), jnp.float32)],
        scratch_shapes=[pltpu.VMEM((3, CONV_PAD, hb * d), jnp.float32)],
        compiler_params=pltpu.CompilerParams(
            dimension_semantics=("parallel", "parallel", "arbitrary"), vmem_limit_bytes=VMEM_LIMIT),
        name="gdn_prompt",
    )(src, src, src, prev, prev, prev, by_group(a), by_group(beta_logit), src, conv_w, conv_w, conv_w,
      a_log.reshape(ng, 1, hb).astype(jnp.float32), dt_bias.reshape(ng, 1, hb).astype(jnp.float32),
      norm_g.reshape(1, d).astype(jnp.float32), s0)


MEM_TQ = 256
MIN_MXU_ROWS = 16


def _mem_attn_kernel(q_ref, kv_ref, o_ref):
    nh, d, bf = MEM_HEADS, HEAD_DIM, jnp.bfloat16
    tq = q_ref.shape[0]
    m = kv_ref.shape[0] // (2 * nh)
    rows = max(tq, MIN_MXU_ROWS)
    for h in range(nh):
        k = kv_ref[pl.ds(h, m, stride=2 * nh), :].astype(bf)
        v = kv_ref[pl.ds(nh + h, m, stride=2 * nh), :].astype(bf)
        q = _pad_rows(q_ref[:, h * d:(h + 1) * d], rows).astype(bf)
        s = lax.dot_general(q, k, _NT, preferred_element_type=jnp.float32) * (d ** -0.5)
        e = jnp.exp(s - jnp.max(s, axis=-1, keepdims=True))
        p = e / jnp.sum(e, axis=-1, keepdims=True)
        o_ref[:, h * d:(h + 1) * d] = jnp.dot(p.astype(bf), v, preferred_element_type=jnp.float32)[:tq]


def mem_attention_core(q, mem_kv):
    b, t, w = q.shape
    m = mem_kv.shape[1]
    tq = min(MEM_TQ, t)
    assert t % tq == 0
    kv = mem_kv.reshape(b, m * 2 * MEM_HEADS, HEAD_DIM)
    return pl.pallas_call(
        _mem_attn_kernel,
        grid=(b, t // tq),
        in_specs=[pl.BlockSpec((None, tq, w), lambda bi, qi: (bi, qi, 0)),
                  pl.BlockSpec((None, m * 2 * MEM_HEADS, HEAD_DIM), lambda bi, qi: (bi, 0, 0))],
        out_specs=pl.BlockSpec((None, tq, w), lambda bi, qi: (bi, qi, 0)),
        out_shape=jax.ShapeDtypeStruct((b, t, w), jnp.float32),
        compiler_params=pltpu.CompilerParams(
            dimension_semantics=("parallel", "arbitrary"), vmem_limit_bytes=VMEM_LIMIT),
        name="mem_attn",
    )(q, kv)


def rms_norm(x, g):
    xf = x.astype(jnp.float32)
    y = xf * lax.rsqrt(jnp.mean(xf * xf, axis=-1, keepdims=True) + EPS)
    return (y * g.astype(jnp.float32)).astype(x.dtype)


def l2norm(x):
    xf = x.astype(jnp.float32)
    return xf * lax.rsqrt(jnp.sum(xf * xf, axis=-1, keepdims=True) + EPS)


def rope(x, pos):
    half = x.shape[-1] // 2
    inv = ROPE_THETA ** (-jnp.arange(half, dtype=jnp.float32) / half)
    ang = pos.astype(jnp.float32)[:, None] * inv
    cos, sin = jnp.cos(ang)[:, None, :], jnp.sin(ang)[:, None, :]
    xf = x.astype(jnp.float32)
    x1, x2 = xf[..., :half], xf[..., half:]
    return jnp.concatenate([x1 * cos - x2 * sin, x2 * cos + x1 * sin], axis=-1).astype(x.dtype)


def masked_softmax(s, mask):
    s = jnp.where(mask, s.astype(jnp.float32), NEG)
    e = jnp.where(mask, jnp.exp(s - jnp.max(s, axis=-1, keepdims=True)), 0.0)
    return e / jnp.maximum(jnp.sum(e, axis=-1, keepdims=True), 1e-30)


def split_cols(h):
    return jnp.split(h[..., :sum(IN_SPLITS)], [int(o) for o in np.cumsum(IN_SPLITS)[:-1]], axis=-1)


def gather_pages(pool, page_table):
    rows = pool[page_table]
    return rows.reshape(rows.shape[0], -1, *rows.shape[3:])


def over_query_blocks(fn, *qs):
    b, t = qs[0].shape[:2]
    if t <= Q_BLOCK:
        return fn(0, *qs)
    nb = t // Q_BLOCK
    blocks = tuple(q.reshape(b, nb, Q_BLOCK, *q.shape[2:]).swapaxes(0, 1) for q in qs)
    out = lax.map(lambda a: fn(a[0] * Q_BLOCK, *a[1]), (jnp.arange(nb), blocks))
    out = out.swapaxes(0, 1)
    return out.reshape(b, t, *out.shape[3:])


def nsa_compress(rows, pe, w1, w2):
    b, n_keys, d = rows.shape
    r = CMP_LEN // CMP_STRIDE
    n_cmp = (n_keys - CMP_LEN) // CMP_STRIDE + 1
    seg = rows[:, :(n_cmp + r - 1) * CMP_STRIDE].reshape(b, n_cmp + r - 1, CMP_STRIDE * d)
    w1r = w1.reshape(r, CMP_STRIDE * d, CMP_HID)
    h = pe.reshape(-1) @ w1 + sum(seg[:, i:i + n_cmp] @ w1r[i] for i in range(r))
    return jax.nn.gelu(h) @ w2


def nsa_attention(q, gates, k_cmp, v_cmp, k_slc, v_slc, k_win, v_win, past, win_pos0, banded):
    b, n_keys, d = k_slc.shape
    scale = d ** -0.5
    n_cmp = k_cmp.shape[1]
    cmp_end = jnp.arange(n_cmp) * CMP_STRIDE + CMP_LEN - 1
    n_blk = -(-n_keys // SEL_BLOCK)
    n_sel = min(SEL_COUNT, n_blk)
    pad = ((0, 0), (0, n_blk * SEL_BLOCK - n_keys), (0, 0))
    k_blk = jnp.pad(k_slc, pad).reshape(b, n_blk, SEL_BLOCK, d)
    v_blk = jnp.pad(v_slc, pad).reshape(b, n_blk, SEL_BLOCK, d)
    c0 = jnp.arange(n_cmp)[:, None] * CMP_STRIDE
    j0 = jnp.arange(n_blk)[None, :] * SEL_BLOCK
    share = jnp.clip(jnp.minimum(c0 + CMP_LEN, j0 + SEL_BLOCK) - jnp.maximum(c0, j0), 0, None).astype(jnp.float32) / CMP_LEN
    blk_id = jnp.arange(n_blk)
    if banded:
        wpad = ((0, 0), (WINDOW, 0), (0, 0))
        k_win, v_win = jnp.pad(k_win, wpad), jnp.pad(v_win, wpad)

    def block(start, qb, gb):
        nq = qb.shape[1]
        qp = past + start + jnp.arange(nq)
        s = jnp.einsum('bqhd,bcd->bqhc', qb, k_cmp) * scale
        p_cmp = masked_softmax(s, (cmp_end[None, :] <= qp[:, None])[None, :, None, :])
        o_cmp = jnp.einsum('bqhc,bcd->bqhd', p_cmp.astype(v_cmp.dtype), v_cmp)
        imp = jnp.einsum('bqhc,cj->bqj', p_cmp, share)
        forced = (blk_id[None, :] == qp[:, None] // SEL_BLOCK) | (blk_id[None, :] == 0)
        future = blk_id[None, :] * SEL_BLOCK > qp[:, None]
        imp = jnp.where(forced[None], FORCED, jnp.where(future[None], -1.0, imp))
        _, sel = lax.top_k(imp, n_sel)
        ks = jax.vmap(lambda kb, i: kb[i])(k_blk, sel).reshape(b, nq, n_sel * SEL_BLOCK, d)
        vs = jax.vmap(lambda vb, i: vb[i])(v_blk, sel).reshape(b, nq, n_sel * SEL_BLOCK, d)
        kpos = (sel[..., None] * SEL_BLOCK + jnp.arange(SEL_BLOCK)).reshape(b, nq, n_sel * SEL_BLOCK)
        s = jnp.einsum('bqhd,bqkd->bqhk', qb, ks) * scale
        p = masked_softmax(s, (kpos <= qp[None, :, None])[:, :, None, :])
        o_slc = jnp.einsum('bqhk,bqkd->bqhd', p.astype(vs.dtype), vs)
        if banded:
            n_w = WINDOW + nq
            kw = lax.dynamic_slice_in_dim(k_win, start, n_w, axis=1)
            vw = lax.dynamic_slice_in_dim(v_win, start, n_w, axis=1)
            kp = past + start - WINDOW + jnp.arange(n_w)
        else:
            kw, vw = k_win, v_win
            kp = win_pos0 + jnp.arange(k_win.shape[1])
        dlt = qp[:, None] - kp[None, :]
        s = jnp.einsum('bqhd,bkd->bqhk', qb, kw) * scale
        p = masked_softmax(s, ((dlt >= 0) & (dlt < WINDOW) & (kp[None, :] >= 0))[None, :, None, :])
        o_win = jnp.einsum('bqhk,bkd->bqhd', p.astype(vw.dtype), vw)
        g = jax.nn.sigmoid(gb.astype(jnp.float32))
        o = g[..., 0:1] * o_cmp + g[..., 1:2] * o_slc + g[..., 2:3] * o_win
        return o.astype(qb.dtype)

    return over_query_blocks(block, q, gates)


def dsa_attention(q, q_idx, w_idx, k, v, k_idx, past):
    b, n_keys, d = k.shape
    topk = min(DSA_TOPK, n_keys // 4)
    kpos = jnp.arange(n_keys)

    def block(start, qb, qib, wb):
        qp = past + start + jnp.arange(qb.shape[1])
        causal = kpos[None, :] <= qp[:, None]
        dots = jnp.einsum('bqhd,bsd->bqhs', qib, k_idx).astype(jnp.float32) * IDX_DIM ** -0.5
        score = jnp.einsum('bqh,bqhs->bqs', wb.astype(jnp.float32) * IDX_HEADS ** -0.5, jax.nn.relu(dots))
        score = jnp.where(causal[None], score, NEG)
        _, sel = lax.top_k(score, topk)
        ks = jax.vmap(lambda kk, i: kk[i])(k, sel)
        vs = jax.vmap(lambda vv, i: vv[i])(v, sel)
        s = jnp.einsum('bqhd,bqkd->bqhk', qb, ks) * d ** -0.5
        p = masked_softmax(s, (sel <= qp[None, :, None])[:, :, None, :])
        return jnp.einsum('bqhk,bqkd->bqhd', p.astype(vs.dtype), vs)

    return over_query_blocks(block, q, q_idx, w_idx)


def gated_delta_chunked(q, k, v, g, beta, s0):
    f32 = jnp.float32
    b, t, h, dk = k.shape
    dv = v.shape[-1]
    c = min(GDN_CHUNK, t)
    n = -(-t // c)
    pad = n * c - t

    def to_chunks(a):
        a = jnp.pad(a.astype(f32), [(0, 0), (0, pad)] + [(0, 0)] * (a.ndim - 2))
        a = a.reshape(b, n, c, *a.shape[2:])
        return jnp.moveaxis(a, (1, 3), (0, 2))

    qc, kc, vc, gc, bc = (to_chunks(a) for a in (q, k, v, g, beta))
    gcum = jnp.cumsum(gc, axis=-1)
    pos = jnp.arange(c)
    strict = pos[:, None] > pos[None, :]
    incl = pos[:, None] >= pos[None, :]
    diff = gcum[..., :, None] - gcum[..., None, :]
    decay = jnp.where(incl, jnp.exp(jnp.where(incl, diff, 0.0)), 0.0)
    k_beta = kc * bc[..., None]
    a_mat = jnp.where(strict, jnp.einsum('nbhid,nbhjd->nbhij', k_beta, kc) * decay, 0.0)
    rhs = jnp.concatenate([vc * bc[..., None], k_beta * jnp.exp(gcum)[..., None]], axis=-1)
    sol = lax.linalg.triangular_solve(a_mat + jnp.eye(c, dtype=f32), rhs,
                                      left_side=True, lower=True, unit_diagonal=True)
    u, w = sol[..., :dv], sol[..., dv:]
    qk = jnp.einsum('nbhid,nbhjd->nbhij', qc, kc) * decay
    q_dec = qc * jnp.exp(gcum)[..., None]
    k_dec = kc * jnp.exp(gcum[..., -1:] - gcum)[..., None]
    chunk_decay = jnp.exp(gcum[..., -1])

    def step(state, xs):
        u_i, w_i, qd_i, qk_i, kd_i, cd_i = xs
        new_v = u_i - jnp.einsum('bhck,bhkv->bhcv', w_i, state)
        o_i = jnp.einsum('bhck,bhkv->bhcv', qd_i, state) + jnp.einsum('bhij,bhjv->bhiv', qk_i, new_v)
        state = state * cd_i[..., None, None] + jnp.einsum('bhck,bhcv->bhkv', kd_i, new_v)
        return state, o_i

    s_final, o = lax.scan(step, s0.astype(f32), (u, w, q_dec, qk, k_dec, chunk_decay))
    o = jnp.moveaxis(o, (0, 2), (1, 3)).reshape(b, n * c, h, dv)[:, :t]
    return o, s_final


def gdn_mixer(qkv, a, beta_logit, z, conv_prev, s0, conv_w, a_log, dt_bias, norm_g):
    b, t, _ = qkv.shape
    xp = jnp.concatenate([conv_prev.astype(qkv.dtype), qkv], axis=1)
    conv = jax.nn.silu(sum(xp[:, i:i + t] * conv_w[i] for i in range(GDN_CONV)))
    q, k, v = (cc.reshape(b, t, GDN_HEADS, HEAD_DIM) for cc in jnp.split(conv, 3, axis=-1))
    q = l2norm(q) * HEAD_DIM ** -0.5
    k = l2norm(k)
    g = -jnp.exp(a_log.astype(jnp.float32)) * jax.nn.softplus(a.astype(jnp.float32) + dt_bias.astype(jnp.float32))
    beta = jax.nn.sigmoid(beta_logit.astype(jnp.float32))
    o, s_new = gated_delta_chunked(q, k, v, g, beta, s0)
    o = rms_norm(o, norm_g) * jax.nn.silu(z.reshape(b, t, GDN_HEADS, HEAD_DIM).astype(jnp.float32))
    return o.reshape(b, t, W_C).astype(qkv.dtype), s_new, xp[:, t:]


def mem_attention(x, gain, mem_kv, w_q, w_o):
    b, t, _ = x.shape
    o = mem_attention_core(dense(x, w_q, gain=gain), mem_kv)
    return dense(o, w_o, resid=x)


def run_group(x, mem, cache, p):
    prompt = cache is None
    b, t, _ = x.shape
    past = 0 if prompt else cache['page_table'].shape[1] * cache['nsa_kv'].shape[2]
    names = ('nsa_kv', 'dsa_kv', 'idx_k', 'win_kv', 'gdn', 'conv') + (('mem_kv',) if prompt else ())
    out = {nm: [] for nm in names}
    bg, tg = (b, t) if prompt else (1, b * t)
    row_pos = past + jnp.arange(tg) % t
    for l in range(DEPTH):
        hproj = dense(x, p['w_in'][l], gain=p['norm_mix_g'][l])
        (q_a, q_b, q_i, nsa_new, win_new, dsa_new, idx_new,
         k_slc, v_slc, k_win, v_win, k_dsa, v_dsa, k_idx) = rope_split(hproj.reshape(bg, tg, PROJ_WIDTH), row_pos)
        nsa_new = nsa_new.reshape(b, t, NSA_ROWS, HEAD_DIM)
        win_new = win_new.reshape(b, t, 2, HEAD_DIM)
        dsa_new = dsa_new.reshape(b, t, 2, HEAD_DIM)
        idx_new = idx_new.reshape(b, t, IDX_DIM)
        g_a, w_i, a_c, b_c = (small_col(hproj, nm) for nm in ('g_a', 'w_i', 'a_c', 'b_c'))
        qkv_off, z_off = PROJ_OFF['qkv_c'], PROJ_OFF['z_c']
        qkv_c = hproj[..., qkv_off:qkv_off + 3 * W_C]
        if prompt:
            conv_prev = jnp.zeros((b, GDN_CONV - 1, 3 * W_C), x.dtype)
            s0 = jnp.zeros((b, GDN_HEADS, HEAD_DIM, HEAD_DIM), jnp.float32)
            mem_kv = dense(mem, p['w_mem_kv'][l], gain=p['mem_norm_g'][l]).reshape(b, -1, 2, MEM_HEADS, HEAD_DIM)
            k_cmp, v_cmp = nsa_compress_prompt(nsa_new, p['nsa_cmp_pe'][l], p['nsa_cmp_w1'][l], p['nsa_cmp_w2'][l])
            o_a = nsa_prompt_attention(q_a, g_a, k_cmp, v_cmp, (t - CMP_LEN) // CMP_STRIDE + 1,
                                       k_slc, v_slc, k_win, v_win)
            o_b = dsa_prompt_attention(q_b, q_i, w_i, k_dsa, v_dsa, k_idx)
            win_out = win_new[:, -min(WINDOW, t):]
            o_c, s_new = gdn_prompt(hproj, qkv_off, z_off, a_c, b_c, conv_prev, s0, p['gdn_conv_w'][l],
                                    p['gdn_a_log'][l], p['gdn_dt_bias'][l], p['gdn_norm_g'][l])
            conv_new = jnp.concatenate([conv_prev, qkv_c[:, -(GDN_CONV - 1):]], axis=1)[:, -(GDN_CONV - 1):]
        else:
            pt = cache['page_table']
            conv_prev, s0 = cache['conv'][l], cache['gdn'][l]
            mem_kv = cache['mem_kv'][l]
            q_i = jnp.swapaxes(q_i.reshape(IDX_HEADS, b, t, IDX_DIM), 0, 1)
            o_a = nsa_sample_attention(l, q_a.reshape(b, t, W_A), g_a, nsa_new, win_new, cache['nsa_kv'], cache['win_kv'],
                                       pt, p['nsa_cmp_pe'][l], p['nsa_cmp_w1'][l], p['nsa_cmp_w2'][l])
            o_b = dsa_sample_attention(l, q_b.reshape(b, t, W_B), q_i, w_i, dsa_new, idx_new, cache['dsa_kv'],
                                       cache['idx_k'], pt)
            win_out = win_new
            o_c, s_new, conv_new = gdn_mixer(qkv_c, a_c, b_c, hproj[..., z_off:z_off + W_C], conv_prev, s0,
                                             p['gdn_conv_w'][l], p['gdn_a_log'][l], p['gdn_dt_bias'][l],
                                             p['gdn_norm_g'][l])
        mix = branch_mix(o_a.reshape(b * t, W_A), o_b.reshape(b * t, W_B), o_c.reshape(b * t, W_C),
                         hproj.reshape(b * t, PROJ_WIDTH), PROJ_OFF['merge'],
                         p['w_branch_a'][l], p['w_branch_b'][l], p['w_branch_c'][l])
        x = dense(mix, p['w_mix_out'][l], resid=x).reshape(b, t, D_MODEL)
        x = mem_attention(x, p['norm_mem_g'][l], mem_kv, p['w_mem_q'][l], p['w_mem_o'][l])
        x = dense(ffn_act(x, p['norm_ffn_g'][l], p['w_ffn_gate'][l], p['w_ffn_up'][l]), p['w_ffn_down'][l], resid=x)
        out['nsa_kv'].append(nsa_new)
        out['dsa_kv'].append(dsa_new)
        out['idx_k'].append(idx_new)
        out['win_kv'].append(win_out)
        out['gdn'].append(s_new)
        out['conv'].append(conv_new)
        if prompt:
            out['mem_kv'].append(mem_kv)
    y = rms_norm(x, p['norm_final_g'])
    out = {nm: jnp.stack(v) for nm, v in out.items()}
    if not prompt:
        wlen = cache['win_kv'].shape[2]
        out['win_kv'] = jnp.concatenate([cache['win_kv'], out['win_kv']], axis=2)[:, :, -wlen:]
    return y, out


def kernel(x_prompt, x_sample, mem_prompt, cache_nsa_kv, cache_dsa_kv, cache_dsa_idx_k, cache_win_kv, cache_mem_kv, state_gdn, state_conv, page_table, norm_mix_g, w_in, nsa_cmp_pe, nsa_cmp_w1, nsa_cmp_w2, gdn_conv_w, gdn_a_log, gdn_dt_bias, gdn_norm_g, w_branch_a, w_branch_b, w_branch_c, w_mix_out, norm_mem_g, mem_norm_g, w_mem_q, w_mem_kv, w_mem_o, norm_ffn_g, w_ffn_gate, w_ffn_up, w_ffn_down, norm_final_g):
    p = dict(norm_mix_g=norm_mix_g, w_in=w_in, nsa_cmp_pe=nsa_cmp_pe, nsa_cmp_w1=nsa_cmp_w1,
             nsa_cmp_w2=nsa_cmp_w2, gdn_conv_w=gdn_conv_w, gdn_a_log=gdn_a_log, gdn_dt_bias=gdn_dt_bias,
             gdn_norm_g=gdn_norm_g, w_branch_a=w_branch_a, w_branch_b=w_branch_b, w_branch_c=w_branch_c,
             w_mix_out=w_mix_out, norm_mem_g=norm_mem_g, mem_norm_g=mem_norm_g, w_mem_q=w_mem_q,
             w_mem_kv=w_mem_kv, w_mem_o=w_mem_o, norm_ffn_g=norm_ffn_g, w_ffn_gate=w_ffn_gate,
             w_ffn_up=w_ffn_up, w_ffn_down=w_ffn_down, norm_final_g=norm_final_g)
    for nm in ('w_branch_a', 'w_branch_b', 'w_branch_c', 'w_mix_out', 'w_mem_q', 'w_mem_kv', 'w_mem_o',
               'w_ffn_gate', 'w_ffn_up', 'w_ffn_down'):
        p[nm] = p[nm].astype(jnp.bfloat16)
    p['w_in'] = permute_w_in(w_in)
    cache = dict(nsa_kv=cache_nsa_kv, dsa_kv=cache_dsa_kv, idx_k=cache_dsa_idx_k, win_kv=cache_win_kv,
                 mem_kv=cache_mem_kv, gdn=state_gdn, conv=state_conv, page_table=page_table)
    y_prompt, sp = run_group(x_prompt, mem_prompt, None, p)
    y_sample, ss = run_group(x_sample, None, cache, p)
    return (y_prompt, y_sample,
            sp['nsa_kv'], sp['dsa_kv'], sp['idx_k'], sp['win_kv'], sp['gdn'], sp['conv'], sp['mem_kv'],
            ss['nsa_kv'], ss['dsa_kv'], ss['idx_k'], ss['win_kv'], ss['gdn'], ss['conv'])
```

```python
import functools
import math

import jax
import jax.numpy as jnp
import numpy as np
from jax import lax
from jax.experimental import pallas as pl
from jax.experimental.pallas import tpu as pltpu

D_MODEL = 2048
DEPTH = 2
PAGE_SIZE = 128
HEAD_DIM = 128
NSA_HEADS = D_MODEL // (4 * HEAD_DIM)
DSA_HEADS = D_MODEL // (4 * HEAD_DIM)
GDN_HEADS = D_MODEL // (2 * HEAD_DIM)
W_A = NSA_HEADS * HEAD_DIM
W_B = DSA_HEADS * HEAD_DIM
W_C = GDN_HEADS * HEAD_DIM
CMP_LEN = 32
CMP_STRIDE = 16
CMP_HID = 2 * HEAD_DIM
SEL_BLOCK = 64
SEL_COUNT = 16
WINDOW = 512
IDX_HEADS = 16
IDX_DIM = 64
DSA_TOPK = 256
GDN_CONV = 4
GDN_CHUNK = 64
MEM_HEADS = 4
ROPE_THETA = 10000.0
Q_BLOCK = 128
EPS = 1e-6
NEG = -1e30
FORCED = 1e9
IN_SPLITS = (W_A, 6 * HEAD_DIM, 3 * NSA_HEADS,
             W_B, 2 * HEAD_DIM, IDX_HEADS * IDX_DIM, IDX_HEADS, IDX_DIM,
             3 * W_C, GDN_HEADS, GDN_HEADS, W_C,
             3 * D_MODEL)

LANE = 128
VMEM_LIMIT = 48 * 1024 * 1024


DENSE_VMEM_BUDGET = 40 * 1024 * 1024
NORM_ROWS = 128


def _divisor_tiles(n, cap):
    return [t for t in range(min(cap, n), 0, -LANE) if t % LANE == 0 and n % t == 0]


def _dense_tiles(m, n, per_row_bytes, per_col_bytes, per_out_bytes):
    for tm in (1024, 512, 256, 128):
        if m % tm:
            continue
        for tn in _divisor_tiles(n, 1024):
            if tm * per_row_bytes + tn * per_col_bytes + tm * tn * per_out_bytes <= DENSE_VMEM_BUDGET:
                return tm, tn
    raise ValueError("no dense tile fits VMEM")


def _stage_rows(x_ref, g_ref, xn_ref):
    def body(r, carry):
        rows = pl.ds(pl.multiple_of(r * NORM_ROWS, NORM_ROWS), NORM_ROWS)
        xf = x_ref[rows, :].astype(jnp.float32)
        if g_ref is not None:
            xf = xf * lax.rsqrt(jnp.mean(xf * xf, axis=-1, keepdims=True) + EPS) * g_ref[...]
        xn_ref[rows, :] = xf.astype(jnp.bfloat16)
        return carry
    lax.fori_loop(0, x_ref.shape[0] // NORM_ROWS, body, 0)


def _dense_kernel(*refs, has_gain, has_resid, staged, w_rows):
    it = iter(refs)
    x_ref = next(it)
    g_ref = next(it) if has_gain else None
    w_ref = next(it)
    r_ref = next(it) if has_resid else None
    o_ref = next(it)
    xn_ref = next(it) if staged else x_ref

    if staged:
        @pl.when(pl.program_id(1) == 0)
        def _():
            _stage_rows(x_ref, g_ref, xn_ref)

    if w_rows:
        acc = lax.dot_general(xn_ref[...], w_ref[...], (((1,), (1,)), ((), ())), preferred_element_type=jnp.float32)
    else:
        acc = jnp.dot(xn_ref[...], w_ref[...], preferred_element_type=jnp.float32)
    if has_resid:
        acc = acc + r_ref[...]
    o_ref[...] = acc.astype(o_ref.dtype)


def dense(x, w, gain=None, resid=None, out_dtype=jnp.float32, w_rows=False):
    lead, kdim, n = x.shape[:-1], x.shape[-1], w.shape[0 if w_rows else -1]
    x2 = x.reshape(-1, kdim)
    m = x2.shape[0]
    ob = jnp.dtype(out_dtype).itemsize
    staged = gain is not None or x2.dtype != jnp.bfloat16
    tm, tn = _dense_tiles(m, n, per_row_bytes=kdim * (2 * x2.dtype.itemsize + (2 if staged else 0)),
                          per_col_bytes=kdim * 2 * 2, per_out_bytes=2 * ob + (8 if resid is not None else 0))
    args, specs = [x2], [pl.BlockSpec((tm, kdim), lambda i, j: (i, 0))]
    if gain is not None:
        args.append(gain.reshape(1, kdim).astype(jnp.float32))
        specs.append(pl.BlockSpec((1, kdim), lambda i, j: (0, 0)))
    args.append(w)
    specs.append(pl.BlockSpec((tn, kdim), lambda i, j: (j, 0)) if w_rows
                 else pl.BlockSpec((kdim, tn), lambda i, j: (0, j)))
    if resid is not None:
        args.append(resid.reshape(m, n))
        specs.append(pl.BlockSpec((tm, tn), lambda i, j: (i, j)))
    out = pl.pallas_call(
        functools.partial(_dense_kernel, has_gain=gain is not None, has_resid=resid is not None, staged=staged,
                          w_rows=w_rows),
        grid=(m // tm, n // tn),
        in_specs=specs,
        out_specs=pl.BlockSpec((tm, tn), lambda i, j: (i, j)),
        out_shape=jax.ShapeDtypeStruct((m, n), out_dtype),
        scratch_shapes=[pltpu.VMEM((tm, kdim), jnp.bfloat16)] if staged else [],
        compiler_params=pltpu.CompilerParams(
            dimension_semantics=("parallel", "arbitrary"), vmem_limit_bytes=VMEM_LIMIT),
        name="dense",
    )(*args)
    return out.reshape(*lead, n)


def _ffn_act_kernel(x_ref, g_ref, wg_ref, wu_ref, o_ref, xn_ref):
    @pl.when(pl.program_id(1) == 0)
    def _():
        _stage_rows(x_ref, g_ref, xn_ref)

    xn = xn_ref[...]
    a = jnp.dot(xn, wg_ref[...], preferred_element_type=jnp.float32)
    u = jnp.dot(xn, wu_ref[...], preferred_element_type=jnp.float32)
    o_ref[...] = (a / (1.0 + jnp.exp(-a)) * u).astype(o_ref.dtype)


def ffn_act(x, gain, w_gate, w_up):
    lead, kdim, n = x.shape[:-1], x.shape[-1], w_gate.shape[-1]
    x2 = x.reshape(-1, kdim)
    m = x2.shape[0]
    tm, tn = _dense_tiles(m, n, per_row_bytes=kdim * (2 * 4 + 2), per_col_bytes=2 * kdim * 2 * 2, per_out_bytes=2 * 2)
    out = pl.pallas_call(
        _ffn_act_kernel,
        grid=(m // tm, n // tn),
        in_specs=[pl.BlockSpec((tm, kdim), lambda i, j: (i, 0)),
                  pl.BlockSpec((1, kdim), lambda i, j: (0, 0)),
                  pl.BlockSpec((kdim, tn), lambda i, j: (0, j)),
                  pl.BlockSpec((kdim, tn), lambda i, j: (0, j))],
        out_specs=pl.BlockSpec((tm, tn), lambda i, j: (i, j)),
        out_shape=jax.ShapeDtypeStruct((m, n), jnp.bfloat16),
        scratch_shapes=[pltpu.VMEM((tm, kdim), jnp.bfloat16)],
        compiler_params=pltpu.CompilerParams(
            dimension_semantics=("parallel", "arbitrary"), vmem_limit_bytes=VMEM_LIMIT),
        name="ffn_act",
    )(x2, gain.reshape(1, kdim).astype(jnp.float32), w_gate, w_up)
    return out.reshape(*lead, n)


def _branch_mix_kernel(oa_ref, ob_ref, oc_ref, ma_ref, mb_ref, mc_ref, wa_ref, wb_ref, wc_ref, o_ref, xn_ref):
    @pl.when(pl.program_id(1) == 0)
    def _():
        off = 0
        for src in (oa_ref, ob_ref, oc_ref):
            xn_ref[:, off:off + src.shape[1]] = src[...].astype(jnp.bfloat16)
            off += src.shape[1]

    def gated(m_ref, w_ref, lo, hi):
        y = jnp.dot(xn_ref[:, lo:hi], w_ref[...], preferred_element_type=jnp.float32)
        return y / (1.0 + jnp.exp(-m_ref[...]))

    o_ref[...] = (gated(ma_ref, wa_ref, 0, W_A) + gated(mb_ref, wb_ref, W_A, W_A + W_B)
                  + gated(mc_ref, wc_ref, W_A + W_B, W_A + W_B + W_C)).astype(o_ref.dtype)


def branch_mix(o_a, o_b, o_c, merge, merge_off, w_a, w_b, w_c):
    m, n = merge.shape[0], w_a.shape[-1]
    ktot = W_A + W_B + W_C
    tm, tn = _dense_tiles(m, n, per_row_bytes=ktot * (2 * 4 + 2), per_col_bytes=ktot * 2 * 2, per_out_bytes=2 * 2 + 3 * 8)
    nb = n // tn
    assert merge_off % tn == 0
    mb = merge_off // tn
    row = lambda width: pl.BlockSpec((tm, width), lambda i, j: (i, 0))
    return pl.pallas_call(
        _branch_mix_kernel,
        grid=(m // tm, nb),
        in_specs=[row(W_A), row(W_B), row(W_C),
                  pl.BlockSpec((tm, tn), lambda i, j: (i, mb + j)),
                  pl.BlockSpec((tm, tn), lambda i, j: (i, mb + j + nb)),
                  pl.BlockSpec((tm, tn), lambda i, j: (i, mb + j + 2 * nb)),
                  pl.BlockSpec((W_A, tn), lambda i, j: (0, j)),
                  pl.BlockSpec((W_B, tn), lambda i, j: (0, j)),
                  pl.BlockSpec((W_C, tn), lambda i, j: (0, j))],
        out_specs=pl.BlockSpec((tm, tn), lambda i, j: (i, j)),
        out_shape=jax.ShapeDtypeStruct((m, n), jnp.bfloat16),
        scratch_shapes=[pltpu.VMEM((tm, ktot), jnp.bfloat16)],
        compiler_params=pltpu.CompilerParams(
            dimension_semantics=("parallel", "arbitrary"), vmem_limit_bytes=VMEM_LIMIT),
        name="branch_mix",
    )(o_a, o_b, o_c, merge, merge, merge, w_a, w_b, w_c)


REF_COLS = tuple(zip(('q_a', 'kv_a', 'g_a', 'q_b', 'kv_b', 'q_i', 'w_i', 'k_i', 'qkv_c', 'a_c', 'b_c', 'z_c', 'merge'),
                     IN_SPLITS))
SMALL_COLS = ('k_i', 'g_a', 'w_i', 'a_c', 'b_c')
WIDE_COLS = ('q_a', 'kv_a', 'q_b', 'kv_b', 'q_i', 'qkv_c', 'z_c', 'merge')


def _layout():
    width = dict(REF_COLS)
    assert all(width[n] % LANE == 0 for n in WIDE_COLS) and sum(width[n] for n in SMALL_COLS) <= LANE
    off, pos = {}, 0
    for n in WIDE_COLS:
        off[n] = pos
        pos += width[n]
    off['small'] = pos
    small, spos = {}, 0
    for n in SMALL_COLS:
        small[n] = (spos, width[n])
        spos += width[n]
    return off, small, pos + LANE


PROJ_OFF, SMALL_OFF, PROJ_WIDTH = _layout()
ROPE_COLS = PROJ_OFF['qkv_c']
ROPE_TM = 256


def _column_moves():
    ref_off, pos = {}, 0
    for n, wd in REF_COLS:
        ref_off[n] = pos
        pos += wd
    width = dict(REF_COLS)
    moves = [(ref_off[n], PROJ_OFF[n], width[n]) for n in WIDE_COLS]
    moves += [(ref_off[n], PROJ_OFF['small'] + SMALL_OFF[n][0], width[n]) for n in SMALL_COLS]
    return moves, PROJ_OFF['small'] + sum(width[n] for n in SMALL_COLS)


def permute_w_in(w):
    moves, used = _column_moves()
    cols = jnp.transpose(w, (2, 0, 1))
    pad = jnp.zeros((PROJ_WIDTH - used, w.shape[1]), jnp.bfloat16)
    order = sorted(moves, key=lambda mv: mv[1])
    return [jnp.concatenate([cols[src:src + wd, l].astype(jnp.bfloat16) for src, _, wd in order] + [pad], axis=0)
            for l in range(w.shape[0])]


def small_col(hproj, name):
    lo, wd = SMALL_OFF[name]
    return hproj[..., PROJ_OFF['small'] + lo:PROJ_OFF['small'] + lo + wd]


def _rope_tables(pos, rows, head_dim=HEAD_DIM):
    half = head_dim // 2
    inv = ROPE_THETA ** (-jnp.arange(half, dtype=jnp.float32) / half)
    ang = pos.astype(jnp.float32)[:, None] * inv
    cos, sin = jnp.cos(ang), jnp.sin(ang)
    reps = LANE // head_dim
    pad = ((0, rows - pos.shape[0]), (0, 0))
    return (jnp.pad(jnp.tile(jnp.concatenate([cos, cos], axis=-1), (1, reps)), pad),
            jnp.pad(jnp.tile(jnp.concatenate([-sin, sin], axis=-1), (1, reps)), pad))


def _rope_split_kernel(h_ref, small_ref, c128_ref, s128_ref, c64_ref, s64_ref,
                       qa_ref, qb_ref, qi_ref, nsa_ref, win_ref, dsa_ref, idx_ref,
                       kslc_ref, vslc_ref, kwin_ref, vwin_ref, kdsa_ref, vdsa_ref, kidx_ref):
    d, bf = HEAD_DIM, jnp.bfloat16
    c128, s128, c64, s64 = c128_ref[...], s128_ref[...], c64_ref[...], s64_ref[...]
    first_half = (lax.broadcasted_iota(jnp.int32, (1, LANE), 1) & (IDX_DIM - 1)) < IDX_DIM // 2

    def rope128(x):
        return x * c128 + pltpu.roll(x, d // 2, 1) * s128

    def rope64(x):
        rot = jnp.where(first_half, pltpu.roll(x, LANE - IDX_DIM // 2, 1), pltpu.roll(x, IDX_DIM // 2, 1))
        return x * c64 + rot * s64

    col = lambda name, i: h_ref[:, PROJ_OFF[name] + i * d:PROJ_OFF[name] + (i + 1) * d]
    for i in range(NSA_HEADS):
        qa_ref[:, i * d:(i + 1) * d] = (rope128(col('q_a', i)) * (d ** -0.5)).astype(bf)
    for i in range(DSA_HEADS):
        qb_ref[:, i * d:(i + 1) * d] = (rope128(col('q_b', i)) * (d ** -0.5)).astype(bf)
    k_slc, v_slc, k_win, v_win = rope128(col('kv_a', 2)), col('kv_a', 3), rope128(col('kv_a', 4)), col('kv_a', 5)
    for i, part in enumerate((col('kv_a', 0), col('kv_a', 1), k_slc, v_slc)):
        nsa_ref[:, i * d:(i + 1) * d] = part
    win_ref[:, 0:d], win_ref[:, d:2 * d] = k_win, v_win
    kslc_ref[...], vslc_ref[...], kwin_ref[...], vwin_ref[...] = (k_slc.astype(bf), v_slc.astype(bf),
                                                                   k_win.astype(bf), v_win.astype(bf))
    k_dsa, v_dsa = rope128(col('kv_b', 0)), col('kv_b', 1)
    dsa_ref[:, 0:d], dsa_ref[:, d:2 * d] = k_dsa, v_dsa
    kdsa_ref[...], vdsa_ref[...] = k_dsa.astype(bf), v_dsa.astype(bf)
    for i in range(IDX_HEADS * IDX_DIM // LANE):
        pair = rope64(col('q_i', i)).astype(bf)
        qi_ref[2 * i] = pair[:, :IDX_DIM]
        qi_ref[2 * i + 1] = pair[:, IDX_DIM:]
    assert SMALL_OFF['k_i'][0] == 0
    k_idx = rope64(small_ref[...])[:, :IDX_DIM]
    idx_ref[...] = k_idx
    kidx_ref[...] = k_idx.astype(bf)


def rope_split(hproj, pos):
    bg, tg, _ = hproj.shape
    d, bf, f32 = HEAD_DIM, jnp.bfloat16, jnp.float32
    tm = min(ROPE_TM, tg)
    assert tg % tm == 0
    c128, s128 = _rope_tables(pos, tg, d)
    c64, s64 = _rope_tables(pos, tg, IDX_DIM)
    rows = lambda w: pl.BlockSpec((None, tm, w), lambda b, i: (b, i, 0))
    table = pl.BlockSpec((tm, LANE), lambda b, i: (i, 0))
    shape = lambda w, dt: jax.ShapeDtypeStruct((bg, tg, w), dt)
    outs = [(W_A, bf), (W_B, bf), None, (NSA_ROWS * d, f32), (2 * d, f32), (2 * d, f32), (IDX_DIM, f32),
            (d, bf), (d, bf), (d, bf), (d, bf), (d, bf), (d, bf), (IDX_DIM, bf)]
    out_specs = [pl.BlockSpec((None, IDX_HEADS, tm, IDX_DIM), lambda b, i: (b, 0, i, 0)) if o is None else rows(o[0])
                 for o in outs]
    out_shape = [jax.ShapeDtypeStruct((bg, IDX_HEADS, tg, IDX_DIM), bf) if o is None else shape(*o) for o in outs]
    return pl.pallas_call(
        _rope_split_kernel,
        grid=(bg, tg // tm),
        in_specs=[pl.BlockSpec((None, tm, ROPE_COLS), lambda b, i: (b, i, 0)),
                  pl.BlockSpec((None, tm, LANE), lambda b, i: (b, i, PROJ_OFF['small'] // LANE)),
                  table, table, table, table],
        out_specs=out_specs, out_shape=out_shape,
        compiler_params=pltpu.CompilerParams(
            dimension_semantics=("parallel", "parallel"), vmem_limit_bytes=VMEM_LIMIT),
        name="rope_split",
    )(hproj, hproj, c128, s128, c64, s64)


_NT = (((1,), (1,)), ((), ()))


def _softmax_tiles(qh, k_ref, v_ref, lo, hi, tk, mask_fn, m_ref, l_ref, acc_ref, nh, tq):
    assert qh.shape[-1] == LANE and tk % LANE == 0
    m_ref[...] = jnp.full(m_ref.shape, NEG, jnp.float32)
    l_ref[...] = jnp.zeros(l_ref.shape, jnp.float32)
    acc_ref[...] = jnp.zeros(acc_ref.shape, jnp.float32)

    def body(kt, carry):
        off = pl.multiple_of(kt * tk, tk)
        k = k_ref[0, pl.ds(off, tk), :]
        v = v_ref[0, pl.ds(off, tk), :]
        bias = (mask_fn(kt) - 1.0) * (-NEG)
        heads = range(nh)
        chunks = []
        for h in heads:
            s = lax.dot_general(qh[h * tq:(h + 1) * tq], k, _NT, preferred_element_type=jnp.float32) + bias
            chunks.append([s[:, c * LANE:(c + 1) * LANE] for c in range(tk // LANE)])
        m_old = [m_ref[h] for h in heads]
        m_new = [jnp.maximum(m_old[h], jnp.max(functools.reduce(jnp.maximum, chunks[h]), axis=-1, keepdims=True))
                 for h in heads]
        ps = [[jnp.exp(ch - m_new[h]) for ch in chunks[h]] for h in heads]
        alpha = [jnp.exp(m_old[h] - m_new[h]) for h in heads]
        pv = [jnp.dot(jnp.concatenate(ps[h], axis=-1).astype(jnp.bfloat16), v, preferred_element_type=jnp.float32)
              for h in heads]
        for h in heads:
            l_ref[h] = alpha[h] * l_ref[h] + functools.reduce(jnp.add, ps[h])
            acc_ref[h] = alpha[h] * acc_ref[h] + pv[h]
            m_ref[h] = m_new[h]
        return carry

    lax.fori_loop(lo, hi, body, 0)
    return acc_ref[...] / jnp.maximum(jnp.sum(l_ref[...], axis=-1, keepdims=True), 1e-30)


def _split3_bf16(x):
    hi = x.astype(jnp.bfloat16)
    r1 = x - hi.astype(jnp.float32)
    mid = r1.astype(jnp.bfloat16)
    lo = (r1 - mid.astype(jnp.float32)).astype(jnp.bfloat16)
    return hi, mid, lo


def _select_blocks(psum, share, qpos_r, n_blk, tq):
    nbp, tqp = share.shape[0], psum.shape[0]
    imp_t = sum(lax.dot_general(share, part, _NT, preferred_element_type=jnp.float32)
                for part in _split3_bf16(psum))
    blk = lax.broadcasted_iota(jnp.int32, (nbp, tqp), 0)
    forced = (blk == jnp.right_shift(qpos_r, 6)) | (blk == 0)
    future = blk * SEL_BLOCK > qpos_r
    imp_t = jnp.where(forced, FORCED, jnp.where(future, -1.0, imp_t))
    imp_t = jnp.where(blk < n_blk, imp_t, -2.0)
    rank = jnp.zeros((nbp, tqp), jnp.float32)
    for i in range(n_blk):
        row = imp_t[i:i + 1, :]
        beats = (row > imp_t) | ((row == imp_t) & (blk > i))
        rank = rank + beats.astype(jnp.float32)
    sel_t = (rank < float(min(SEL_COUNT, n_blk))).astype(jnp.bfloat16)
    eye = (lax.broadcasted_iota(jnp.int32, (tq, tqp), 0) == lax.broadcasted_iota(jnp.int32, (tq, tqp), 1))
    return lax.dot_general(eye.astype(jnp.bfloat16), sel_t, _NT,
                           preferred_element_type=jnp.float32).astype(jnp.bfloat16)


def _softmax_flat(qh, tiles, nh, tq):
    scores = []
    for k, _, maskf in tiles:
        s = lax.dot_general(qh, k, _NT, preferred_element_type=jnp.float32).reshape(nh, tq, k.shape[0])
        scores.append(s + ((maskf - 1.0) * (-NEG))[None])
    m = scores[0]
    for s in scores[1:]:
        m = jnp.maximum(m, s)
    m = jnp.max(m, axis=-1, keepdims=True)
    lsum, acc = None, None
    for s, (_, v, maskf) in zip(scores, tiles):
        p = jnp.exp(s - m) * maskf[None]
        pv = jnp.dot(p.reshape(nh * tq, p.shape[-1]).astype(jnp.bfloat16), v, preferred_element_type=jnp.float32)
        lsum = p if lsum is None else lsum + p
        acc = pv if acc is None else acc + pv
    l = jnp.sum(lsum, axis=-1, keepdims=True)
    return acc.reshape(nh, tq, acc.shape[-1]) / jnp.maximum(l, 1e-30)


NSA_TQ = 128
NSA_TK_SLC = 512
NSA_TK_WIN = 128


def _nsa_prompt_kernel(q_ref, g_ref, kc_ref, vc_ref, ks_ref, vs_ref, kw_ref, vw_ref, share_ref, o_ref,
                       m_ref, l_ref, acc_ref, *, n_cmp, n_blk, n_keys):
    nh, tq, d = NSA_HEADS, NSA_TQ, HEAD_DIM
    nbp = share_ref.shape[0]
    ncp = share_ref.shape[1]
    start = pl.program_id(1) * tq
    qpos_c = start + lax.broadcasted_iota(jnp.int32, (tq, 1), 0)
    qpos_r = start + lax.broadcasted_iota(jnp.int32, (1, tq), 1)

    qh = jnp.concatenate([q_ref[0, :, h * d:(h + 1) * d] for h in range(nh)], axis=0)

    c_r = lax.broadcasted_iota(jnp.int32, (1, ncp), 1)
    mask_c = ((c_r * CMP_STRIDE + (CMP_LEN - 1) <= qpos_c) & (c_r < n_cmp)).astype(jnp.float32)
    s = lax.dot_general(qh, kc_ref[0], _NT, preferred_element_type=jnp.float32).reshape(nh, tq, ncp)
    sm = s + ((mask_c - 1.0) * (-NEG))[None]
    e = jnp.exp(sm - jnp.max(sm, axis=-1, keepdims=True)) * mask_c[None]
    p_cmp = e / jnp.maximum(jnp.sum(e, axis=-1, keepdims=True), 1e-30)
    o_cmp = jnp.dot(p_cmp.reshape(nh * tq, ncp).astype(jnp.bfloat16), vc_ref[0],
                    preferred_element_type=jnp.float32)

    sel = _select_blocks(jnp.sum(p_cmp, axis=0), share_ref[...], qpos_r, n_blk, tq)

    def slc_mask(kt):
        kpos = kt * NSA_TK_SLC + lax.broadcasted_iota(jnp.int32, (1, NSA_TK_SLC), 1)
        kblk = kt * (NSA_TK_SLC // SEL_BLOCK) + jnp.right_shift(
            lax.broadcasted_iota(jnp.int32, (nbp, NSA_TK_SLC), 1), 6)
        expand = (kblk == lax.broadcasted_iota(jnp.int32, (nbp, NSA_TK_SLC), 0)).astype(jnp.bfloat16)
        chosen = jnp.dot(sel, expand, preferred_element_type=jnp.float32)
        return chosen * (kpos <= qpos_c).astype(jnp.float32)

    hi_slc = jnp.minimum((start + tq - 1) // NSA_TK_SLC + 1, n_keys // NSA_TK_SLC)
    o_slc = _softmax_tiles(qh, ks_ref, vs_ref, 0, hi_slc, NSA_TK_SLC, slc_mask, m_ref, l_ref, acc_ref, nh, tq)

    tiles = []
    for i in range(WINDOW // NSA_TK_WIN + 1):
        kt = pl.program_id(1) * (tq // NSA_TK_WIN) - WINDOW // NSA_TK_WIN + i
        off = pl.multiple_of(jnp.maximum(kt, 0) * NSA_TK_WIN, NSA_TK_WIN)
        kpos = kt * NSA_TK_WIN + lax.broadcasted_iota(jnp.int32, (1, NSA_TK_WIN), 1)
        dlt = qpos_c - kpos
        maskf = ((dlt >= 0) & (dlt < WINDOW) & (kpos >= 0)).astype(jnp.float32)
        tiles.append((kw_ref[0, pl.ds(off, NSA_TK_WIN), :], vw_ref[0, pl.ds(off, NSA_TK_WIN), :], maskf))
    o_win = _softmax_flat(qh, tiles, nh, tq)

    gate = 1.0 / (1.0 + jnp.exp(-g_ref[0]))
    o_cmp = o_cmp.reshape(nh, tq, d)
    for h in range(nh):
        o_ref[0, :, h * d:(h + 1) * d] = (gate[:, 3 * h:3 * h + 1] * o_cmp[h]
                                          + gate[:, 3 * h + 1:3 * h + 2] * o_slc[h]
                                          + gate[:, 3 * h + 2:3 * h + 3] * o_win[h])


def _share_matrix_t(n_cmp, n_blk, ncp, nbp):
    c0 = np.arange(n_cmp)[None, :] * CMP_STRIDE
    j0 = np.arange(n_blk)[:, None] * SEL_BLOCK
    share = np.clip(np.minimum(c0 + CMP_LEN, j0 + SEL_BLOCK) - np.maximum(c0, j0), 0, None) / CMP_LEN
    out = np.zeros((nbp, ncp), np.float32)
    out[:n_blk, :n_cmp] = share
    return jnp.asarray(out, jnp.bfloat16)


def nsa_prompt_attention(q, gates, k_cmp, v_cmp, n_cmp, k_slc, v_slc, k_win, v_win):
    b, t, _ = q.shape
    d = HEAD_DIM
    n_blk = -(-t // SEL_BLOCK)
    ncp = -(-n_cmp // LANE) * LANE
    nbp = -(-n_blk // 16) * 16
    assert t % NSA_TK_SLC == 0 and t % NSA_TQ == 0
    bf = jnp.bfloat16
    cpad = ((0, 0), (0, ncp - k_cmp.shape[1]), (0, 0))
    kc, vc = jnp.pad(k_cmp, cpad).astype(bf), jnp.pad(v_cmp, cpad).astype(bf)
    whole = lambda n: pl.BlockSpec((1, n, d), lambda bi, qi: (bi, 0, 0))
    return pl.pallas_call(
        functools.partial(_nsa_prompt_kernel, n_cmp=n_cmp, n_blk=n_blk, n_keys=t),
        grid=(b, t // NSA_TQ),
        in_specs=[pl.BlockSpec((1, NSA_TQ, NSA_HEADS * d), lambda bi, qi: (bi, qi, 0)),
                  pl.BlockSpec((1, NSA_TQ, 3 * NSA_HEADS), lambda bi, qi: (bi, qi, 0)),
                  whole(ncp), whole(ncp), whole(t), whole(t), whole(t), whole(t),
                  pl.BlockSpec((nbp, ncp), lambda bi, qi: (0, 0))],
        out_specs=pl.BlockSpec((1, NSA_TQ, NSA_HEADS * d), lambda bi, qi: (bi, qi, 0)),
        out_shape=jax.ShapeDtypeStruct((b, t, NSA_HEADS * d), jnp.float32),
        scratch_shapes=[pltpu.VMEM((NSA_HEADS, NSA_TQ, LANE), jnp.float32),
                        pltpu.VMEM((NSA_HEADS, NSA_TQ, LANE), jnp.float32),
                        pltpu.VMEM((NSA_HEADS, NSA_TQ, d), jnp.float32)],
        compiler_params=pltpu.CompilerParams(
            dimension_semantics=("parallel", "arbitrary"), vmem_limit_bytes=VMEM_LIMIT),
        name="nsa_prompt",
    )(q, gates, kc, vc, k_slc.astype(bf), v_slc.astype(bf), k_win.astype(bf), v_win.astype(bf),
      _share_matrix_t(n_cmp, n_blk, ncp, nbp))


NSA_ROWS = 4
NSA_SEQ = 2


def _gelu_tanh(x):
    return 0.5 * x * (1.0 + jnp.tanh(math.sqrt(2.0 / math.pi) * (x + 0.044715 * (x * x * x))))


def _pad_rows(x, rows):
    if rows == x.shape[0]:
        return x
    return jnp.concatenate([x, jnp.zeros((rows - x.shape[0], x.shape[1]), x.dtype)], axis=0)


def _compress_tokens(seg_rows, pe_ref, w1_ref, w2_ref, kv, ncp):
    d, bf = HEAD_DIM, jnp.bfloat16
    first = jnp.zeros((ncp, CMP_HID), jnp.float32)
    second = jnp.zeros((ncp, CMP_HID), jnp.float32)
    for r in range(0, CMP_STRIDE, 2):
        ya, yb = seg_rows(r), seg_rows(r + 1)
        for acc_off in (0, CMP_STRIDE):
            lhs = jnp.concatenate([ya + pe_ref[kv, acc_off + r:acc_off + r + 1, :],
                                   yb + pe_ref[kv, acc_off + r + 1:acc_off + r + 2, :]], axis=1).astype(bf)
            w = w1_ref[kv, acc_off + r:acc_off + r + 2].reshape(2 * d, CMP_HID)
            y = jnp.dot(lhs, w, preferred_element_type=jnp.float32)
            if acc_off == 0:
                first = first + y
            else:
                second = second + y
    hid = first + pltpu.roll(second, ncp - 1, 0)
    return jnp.dot(_gelu_tanh(hid).astype(bf), w2_ref[kv], preferred_element_type=jnp.float32)


def _compress_prompt_kernel(rows_ref, pe_ref, w1_ref, w2_ref, cos_ref, sin_ref, kc_ref, vc_ref):
    ncp, d = kc_ref.shape[0], HEAD_DIM

    def seg_rows(kv, r):
        return rows_ref[pl.ds(NSA_ROWS * r + kv, ncp, stride=NSA_ROWS * CMP_STRIDE), :]

    kc = _compress_tokens(functools.partial(seg_rows, 0), pe_ref, w1_ref, w2_ref, 0, ncp)
    kc_ref[...] = (kc * cos_ref[...] + pltpu.roll(kc, d // 2, 1) * sin_ref[...]).astype(kc_ref.dtype)
    vc_ref[...] = _compress_tokens(functools.partial(seg_rows, 1), pe_ref, w1_ref, w2_ref, 1, ncp).astype(vc_ref.dtype)


def nsa_compress_prompt(nsa_new, pe, w1, w2):
    b, t = nsa_new.shape[:2]
    d, bf = HEAD_DIM, jnp.bfloat16
    ncp = t // CMP_STRIDE
    n_cmp = (t - CMP_LEN) // CMP_STRIDE + 1
    assert ncp % 8 == 0 and n_cmp <= ncp
    cos, sin = _rope_tables(jnp.arange(n_cmp) * CMP_STRIDE + CMP_LEN - 1, ncp)
    const = lambda *shape: pl.BlockSpec(shape, lambda i: (0,) * len(shape))
    out = pl.BlockSpec((None, ncp, d), lambda i: (i, 0, 0))
    return pl.pallas_call(
        _compress_prompt_kernel,
        grid=(b,),
        in_specs=[pl.BlockSpec((None, t * NSA_ROWS, d), lambda i: (i, 0, 0)),
                  const(2, CMP_LEN, d), const(2, CMP_LEN, d, CMP_HID), const(2, CMP_HID, d), const(ncp, d), const(ncp, d)],
        out_specs=[out, out],
        out_shape=[jax.ShapeDtypeStruct((b, ncp, d), bf)] * 2,
        compiler_params=pltpu.CompilerParams(dimension_semantics=("parallel",), vmem_limit_bytes=VMEM_LIMIT),
        name="nsa_compress",
    )(nsa_new.reshape(b, t * NSA_ROWS, d), pe, w1.reshape(2, CMP_LEN, d, CMP_HID).astype(bf), w2.astype(bf), cos, sin)


def _gated_store(o_ref, g_ref, branches, nh, d):
    gate = 1.0 / (1.0 + jnp.exp(-g_ref[...]))
    for h in range(nh):
        o_ref[:, h * d:(h + 1) * d] = sum(gate[:, 3 * h + i:3 * h + i + 1] * br[h] for i, br in enumerate(branches))


def _nsa_sample_kernel(pt_ref, q_ref, g_ref, new_ref, winc_ref, winn_ref, pe_ref, w1_ref, w2_ref, cos_ref, sin_ref,
                       share_ref, *rest, past, n_cmp, n_blk):
    del pt_ref
    all_pages, o_ref = rest[:-1], rest[-1]
    n_seq = q_ref.shape[0]
    n_pages = len(all_pages) // n_seq
    ncp = share_ref.shape[1]
    seg_per_page = PAGE_SIZE // CMP_STRIDE
    assert n_pages * seg_per_page == ncp == LANE and CMP_LEN == 2 * CMP_STRIDE

    def seg_rows(kv, r):
        return jnp.concatenate([p[pl.ds(NSA_ROWS * r + kv, seg_per_page, stride=NSA_ROWS * CMP_STRIDE), :]
                                for p in all_pages], axis=0)

    kc_all = _compress_tokens(functools.partial(seg_rows, 0), pe_ref, w1_ref, w2_ref, 0, n_seq * ncp)
    vc_all = _compress_tokens(functools.partial(seg_rows, 1), pe_ref, w1_ref, w2_ref, 1, n_seq * ncp)
    for s in range(n_seq):
        _nsa_sample_one(q_ref.at[s], g_ref.at[s], new_ref.at[s], winc_ref.at[s], winn_ref.at[s], cos_ref, sin_ref,
                        share_ref, all_pages[s * n_pages:(s + 1) * n_pages], o_ref.at[s],
                        kc_all[s * ncp:(s + 1) * ncp], vc_all[s * ncp:(s + 1) * ncp], past, n_cmp, n_blk)


def _nsa_sample_one(q_ref, g_ref, new_ref, winc_ref, winn_ref, cos_ref, sin_ref, share_ref, pages, o_ref,
                    kc, vc, past, n_cmp, n_blk):
    nh, d, pg = NSA_HEADS, HEAD_DIM, PAGE_SIZE
    tq = q_ref.shape[0]
    nbp, ncp = share_ref.shape
    bf = jnp.bfloat16
    qpos_c = past + lax.broadcasted_iota(jnp.int32, (tq, 1), 0)
    qpos_r = past + lax.broadcasted_iota(jnp.int32, (1, LANE), 1)
    lane = lax.broadcasted_iota(jnp.int32, (1, LANE), 1)
    kc = (kc * cos_ref[...] + pltpu.roll(kc, d // 2, 1) * sin_ref[...]).astype(bf)
    vc = vc.astype(bf)

    q = q_ref[...].astype(jnp.float32)
    qh = jnp.concatenate([q[:, h * d:(h + 1) * d] for h in range(nh)], axis=0).astype(bf)

    mask_c = ((lane * CMP_STRIDE + (CMP_LEN - 1) <= qpos_c) & (lane < n_cmp)).astype(jnp.float32)
    s = lax.dot_general(qh, kc, _NT, preferred_element_type=jnp.float32).reshape(nh, tq, ncp)
    sm = s + ((mask_c - 1.0) * (-NEG))[None]
    e = jnp.exp(sm - jnp.max(sm, axis=-1, keepdims=True)) * mask_c[None]
    p_cmp = e / jnp.maximum(jnp.sum(e, axis=-1, keepdims=True), 1e-30)
    o_cmp = jnp.dot(p_cmp.reshape(nh * tq, ncp).astype(bf), vc, preferred_element_type=jnp.float32).reshape(nh, tq, d)

    sel = _select_blocks(_pad_rows(jnp.sum(p_cmp, axis=0), LANE), share_ref[...], qpos_r, n_blk, LANE)

    def slc_mask(key0):
        kblk = key0 // SEL_BLOCK + jnp.right_shift(lax.broadcasted_iota(jnp.int32, (nbp, LANE), 1), 6)
        expand = (kblk == lax.broadcasted_iota(jnp.int32, (nbp, LANE), 0)).astype(bf)
        chosen = jnp.dot(sel, expand, preferred_element_type=jnp.float32)[:tq]
        return chosen * (key0 + lane <= qpos_c).astype(jnp.float32)

    def component(ref, comp, n_comp, row0, rows):
        return ref[pl.ds(n_comp * row0 + comp, rows, stride=n_comp), :]

    tiles = [(component(p, 2, NSA_ROWS, 0, pg).astype(bf), component(p, 3, NSA_ROWS, 0, pg).astype(bf), slc_mask(j * pg))
             for j, p in enumerate(pages)]
    tiles.append((_pad_rows(component(new_ref, 2, NSA_ROWS, 0, tq), LANE).astype(bf),
                  _pad_rows(component(new_ref, 3, NSA_ROWS, 0, tq), LANE).astype(bf), slc_mask(past)))
    o_slc = _softmax_flat(qh, tiles, nh, tq)

    def win_mask(key0):
        dlt = qpos_c - (key0 + lane)
        return ((dlt >= 0) & (dlt < WINDOW)).astype(jnp.float32)

    n_wc = winc_ref.shape[0] // 2
    tiles = [(component(winc_ref, 0, 2, j * LANE, LANE).astype(bf), component(winc_ref, 1, 2, j * LANE, LANE).astype(bf),
              win_mask(past - n_wc + j * LANE)) for j in range(n_wc // LANE)]
    tiles.append((_pad_rows(component(winn_ref, 0, 2, 0, tq), LANE).astype(bf),
                  _pad_rows(component(winn_ref, 1, 2, 0, tq), LANE).astype(bf), win_mask(past)))
    o_win = _softmax_flat(qh, tiles, nh, tq)

    _gated_store(o_ref, g_ref, (o_cmp, o_slc, o_win), nh, d)


def nsa_sample_attention(layer, q, gates, nsa_new, win_new, pool, win_cache, page_table, pe, w1, w2):
    b, t, _ = q.shape
    wlen = win_cache.shape[2]
    nsa_new = nsa_new.reshape(b, t * NSA_ROWS, HEAD_DIM)
    win_new = win_new.reshape(b, t * 2, HEAD_DIM)
    pool = pool.reshape(pool.shape[0], pool.shape[1], PAGE_SIZE * NSA_ROWS, HEAD_DIM)
    win_cache = win_cache.reshape(win_cache.shape[0], b, wlen * 2, HEAD_DIM)
    d, n_pages = HEAD_DIM, page_table.shape[1]
    past = n_pages * PAGE_SIZE
    n_keys = past + t
    n_cmp = (n_keys - CMP_LEN) // CMP_STRIDE + 1
    n_blk = -(-n_keys // SEL_BLOCK)
    ncp = -(-n_cmp // LANE) * LANE
    nbp = -(-n_blk // 16) * 16
    assert (n_cmp + 1) * CMP_STRIDE <= past, "compressed tokens must come from cached rows only"
    cos, sin = _rope_tables(jnp.arange(n_cmp) * CMP_STRIDE + CMP_LEN - 1, ncp)
    bf = jnp.bfloat16
    ns = NSA_SEQ
    assert b % ns == 0
    per_seq = lambda *tail: pl.BlockSpec((ns,) + tail, lambda i, pt: (i,) + (0,) * len(tail))
    const = lambda *shape: pl.BlockSpec(shape, lambda i, pt: (0,) * len(shape))
    page = lambda s, j: pl.BlockSpec((None, None, PAGE_SIZE * NSA_ROWS, d),
                                     lambda i, pt: (layer, pt[i * ns + s, j], 0, 0))
    grid_spec = pltpu.PrefetchScalarGridSpec(
        num_scalar_prefetch=1, grid=(b // ns,),
        in_specs=[per_seq(t, NSA_HEADS * d), per_seq(t, 3 * NSA_HEADS), per_seq(t * NSA_ROWS, d),
                  pl.BlockSpec((None, ns, wlen * 2, d), lambda i, pt: (layer, i, 0, 0)),
                  per_seq(t * 2, d),
                  const(2, CMP_LEN, d), const(2, CMP_LEN, d, CMP_HID), const(2, CMP_HID, d),
                  const(ncp, d), const(ncp, d), const(nbp, ncp)]
                 + [page(s, j) for s in range(ns) for j in range(n_pages)],
        out_specs=per_seq(t, NSA_HEADS * d))
    return pl.pallas_call(
        functools.partial(_nsa_sample_kernel, past=past, n_cmp=n_cmp, n_blk=n_blk),
        grid_spec=grid_spec,
        out_shape=jax.ShapeDtypeStruct((b, t, NSA_HEADS * d), jnp.float32),
        compiler_params=pltpu.CompilerParams(dimension_semantics=("parallel",), vmem_limit_bytes=VMEM_LIMIT),
        name="nsa_sample",
    )(page_table, q, gates, nsa_new, win_cache, win_new, pe, w1.reshape(2, CMP_LEN, d, CMP_HID).astype(bf),
      w2.astype(bf), cos, sin, _share_matrix_t(n_cmp, n_blk, ncp, nbp), *([pool] * (ns * n_pages)))


DSA_TQ = 128
DSA_TK = 256
INT_MIN = -2 ** 31


def _dsa_prompt_kernel(q_ref, qi_ref, wt_ref, k_ref, v_ref, ki_ref, tri_ref, o_ref,
                       key_ref, sel_ref, m_ref, l_ref, acc_ref, *, n_keys, topk):
    nh, tq, d, tk = DSA_HEADS, DSA_TQ, HEAD_DIM, DSA_TK
    start = pl.program_id(1) * tq
    qpos_r = start + lax.broadcasted_iota(jnp.int32, (1, tq), 1)
    n_kt = jnp.minimum((start + tq - 1) // tk + 1, n_keys // tk)

    wt = wt_ref[0] * (IDX_HEADS ** -0.5 * IDX_DIM ** -0.5)

    def score_body(kt, carry):
        off = pl.multiple_of(kt * tk, tk)
        ki = ki_ref[0, pl.ds(off, tk), :]
        acc = jnp.zeros((tk, tq), jnp.float32)
        for h in range(0, IDX_HEADS, 2):
            pair = qi_ref[0, h:h + 2].reshape(2 * tq, IDX_DIM)
            dots = lax.dot_general(ki, pair, _NT, preferred_element_type=jnp.float32)
            acc = (acc + jnp.maximum(dots[:, :tq], 0.0) * wt[h:h + 1, :]
                   + jnp.maximum(dots[:, tq:], 0.0) * wt[h + 1:h + 2, :])
        kpos_c = off + lax.broadcasted_iota(jnp.int32, (tk, 1), 0)
        score = jnp.where(kpos_c <= qpos_r, acc, NEG)
        bits = pltpu.bitcast(score, jnp.int32)
        key_ref[kt] = bits ^ (jnp.right_shift(bits, 31) & 0x7FFFFFFF)
        return carry

    lax.fori_loop(0, n_kt, score_body, 0)

    def count(pred):
        def body(kt, part):
            hit = pred(key_ref[kt]).astype(jnp.int32)
            return part + jnp.sum(hit.reshape(tk // 8, 8, tq), axis=0)
        part = lax.fori_loop(0, n_kt, body, jnp.zeros((8, tq), jnp.int32))
        return jnp.sum(part, axis=0, keepdims=True)

    thr = jnp.where(count(lambda kk: kk >= 0) >= topk, 0, INT_MIN).astype(jnp.int32)

    def bit_body(i, t):
        cand = t | jnp.left_shift(jnp.int32(1), 30 - i)
        return jnp.where(count(lambda kk: kk >= cand) >= topk, cand, t)

    thr = lax.fori_loop(0, 31, bit_body, thr)
    room = (topk - count(lambda kk: kk > thr)).astype(jnp.float32)

    def sel_body(kt, seen):
        kk = key_ref[kt]
        eq = (kk == thr).astype(jnp.float32)
        prefix = jnp.dot(tri_ref[...], eq.astype(jnp.bfloat16), preferred_element_type=jnp.float32)
        keep = (kk > thr) | ((eq > 0.0) & (prefix + seen <= room))
        kpos_c = kt * tk + lax.broadcasted_iota(jnp.int32, (tk, 1), 0)
        sel_ref[kt] = (keep & (kpos_c <= qpos_r)).astype(jnp.bfloat16)
        return seen + jnp.sum(eq, axis=0, keepdims=True)

    lax.fori_loop(0, n_kt, sel_body, jnp.zeros((1, tq), jnp.float32))

    eye = (lax.broadcasted_iota(jnp.int32, (tq, tq), 0)
           == lax.broadcasted_iota(jnp.int32, (tq, tq), 1)).astype(jnp.bfloat16)

    def sel_mask(kt):
        return lax.dot_general(eye, sel_ref[kt], _NT, preferred_element_type=jnp.float32)

    qh = jnp.concatenate([q_ref[0, :, h * d:(h + 1) * d] for h in range(nh)], axis=0)
    o = _softmax_tiles(qh, k_ref, v_ref, 0, n_kt, tk, sel_mask, m_ref, l_ref, acc_ref, nh, tq)
    for h in range(nh):
        o_ref[0, :, h * d:(h + 1) * d] = o[h]


def dsa_prompt_attention(q, q_idx, w_idx, k, v, k_idx):
    b, t, _ = q.shape
    d = HEAD_DIM
    topk = min(DSA_TOPK, t // 4)
    assert t % DSA_TK == 0 and t % DSA_TQ == 0
    bf = jnp.bfloat16
    tri = jnp.asarray(np.tril(np.ones((DSA_TK, DSA_TK), np.float32)), bf)
    whole = lambda n, w: pl.BlockSpec((1, n, w), lambda bi, qi: (bi, 0, 0))
    return pl.pallas_call(
        functools.partial(_dsa_prompt_kernel, n_keys=t, topk=topk),
        grid=(b, t // DSA_TQ),
        in_specs=[pl.BlockSpec((1, DSA_TQ, DSA_HEADS * d), lambda bi, qi: (bi, qi, 0)),
                  pl.BlockSpec((1, IDX_HEADS, DSA_TQ, IDX_DIM), lambda bi, qi: (bi, 0, qi, 0)),
                  pl.BlockSpec((1, IDX_HEADS, DSA_TQ), lambda bi, qi: (bi, 0, qi)),
                  whole(t, d), whole(t, d), whole(t, IDX_DIM),
                  pl.BlockSpec((DSA_TK, DSA_TK), lambda bi, qi: (0, 0))],
        out_specs=pl.BlockSpec((1, DSA_TQ, DSA_HEADS * d), lambda bi, qi: (bi, qi, 0)),
        out_shape=jax.ShapeDtypeStruct((b, t, DSA_HEADS * d), jnp.float32),
        scratch_shapes=[pltpu.VMEM((t // DSA_TK, DSA_TK, DSA_TQ), jnp.int32),
                        pltpu.VMEM((t // DSA_TK, DSA_TK, DSA_TQ), bf),
                        pltpu.VMEM((DSA_HEADS, DSA_TQ, LANE), jnp.float32),
                        pltpu.VMEM((DSA_HEADS, DSA_TQ, LANE), jnp.float32),
                        pltpu.VMEM((DSA_HEADS, DSA_TQ, d), jnp.float32)],
        compiler_params=pltpu.CompilerParams(
            dimension_semantics=("parallel", "arbitrary"), vmem_limit_bytes=VMEM_LIMIT),
        name="dsa_prompt",
    )(q, q_idx.astype(bf), jnp.swapaxes(w_idx, 1, 2),
      k.astype(bf), v.astype(bf), k_idx.astype(bf), tri)


SEARCH_BITS = 3


def _dsa_sample_kernel(pt_ref, q_ref, qi_ref, w_ref, new_ref, inew_ref, tri_ref, *rest, past, topk):
    del pt_ref
    n_pages = (len(rest) - 1) // 2
    kv_pages, idx_pages, o_ref = rest[:n_pages], rest[n_pages:2 * n_pages], rest[-1]
    nh, d, pg = DSA_HEADS, HEAD_DIM, PAGE_SIZE
    tq = q_ref.shape[0]
    bf = jnp.bfloat16
    qpos_c = past + lax.broadcasted_iota(jnp.int32, (tq, 1), 0)
    lane = lax.broadcasted_iota(jnp.int32, (1, LANE), 1)

    qi = qi_ref[...]
    wb = jnp.broadcast_to(w_ref[...] * (IDX_HEADS ** -0.5 * IDX_DIM ** -0.5), (IDX_HEADS * tq, LANE))

    def key_tile(ki_t, key0):
        dots = jnp.dot(qi, ki_t, preferred_element_type=jnp.float32)
        score = jnp.sum((jnp.maximum(dots, 0.0) * wb).reshape(IDX_HEADS, tq, LANE), axis=0)
        score = jnp.where(key0 + lane <= qpos_c, score, NEG)
        bits = pltpu.bitcast(score, jnp.int32)
        return bits ^ (jnp.right_shift(bits, 31) & 0x7FFFFFFF)

    keys = [key_tile(p[...].astype(bf), j * pg) for j, p in enumerate(idx_pages)]
    keys.append(key_tile(inew_ref[...].astype(bf), past))

    def count(pred):
        hits = pred(keys[0]).astype(jnp.int32)
        for kk in keys[1:]:
            hits = hits + pred(kk).astype(jnp.int32)
        return jnp.sum(hits, axis=-1, keepdims=True)

    def at_least(cand):
        return (count(lambda kk: kk >= cand) >= topk).astype(jnp.int32)

    thr = jnp.where(at_least(0) > 0, 0, INT_MIN).astype(jnp.int32)
    n_steps, last_bits = divmod(31, SEARCH_BITS)

    def radix_body(i, t):
        shift = 31 - SEARCH_BITS * (i + 1)
        digit = sum(at_least(t | jnp.left_shift(jnp.int32(j), shift)) for j in range(1, 2 ** SEARCH_BITS))
        return t | jnp.left_shift(digit, shift)

    thr = lax.fori_loop(0, n_steps, radix_body, thr)
    if last_bits:
        thr = thr | sum(at_least(thr | j) for j in range(1, 2 ** last_bits))
    room = (topk - count(lambda kk: kk > thr)).astype(jnp.float32)
    eqs = [(kk == thr).astype(jnp.float32) for kk in keys]
    totals = [jnp.sum(e, axis=-1, keepdims=True) for e in eqs]
    stacked = jnp.concatenate(eqs + [jnp.zeros((-len(eqs) * tq % MIN_MXU_ROWS, LANE), jnp.float32)] * (
        1 if len(eqs) * tq % MIN_MXU_ROWS else 0), axis=0).astype(bf)
    prefixes = jnp.dot(stacked, tri_ref[...], preferred_element_type=jnp.float32)
    seen = jnp.zeros((tq, 1), jnp.float32)
    masks = []
    for j, (kk, e) in enumerate(zip(keys, eqs)):
        prefix = prefixes[j * tq:(j + 1) * tq]
        keep = (kk > thr) | ((e > 0.0) & (prefix + seen <= room))
        key0 = j * pg if j < n_pages else past
        masks.append((keep & (key0 + lane <= qpos_c)).astype(jnp.float32))
        seen = seen + totals[j]

    def component(ref, comp, rows):
        return ref[pl.ds(comp, rows, stride=2), :]

    tiles = [(component(p, 0, pg).astype(bf), component(p, 1, pg).astype(bf), masks[j]) for j, p in enumerate(kv_pages)]
    tiles.append((_pad_rows(component(new_ref, 0, tq), LANE).astype(bf),
                  _pad_rows(component(new_ref, 1, tq), LANE).astype(bf), masks[n_pages]))
    q = q_ref[...].astype(jnp.float32)
    qh = jnp.concatenate([q[:, h * d:(h + 1) * d] for h in range(nh)], axis=0).astype(bf)
    o = _softmax_flat(qh, tiles, nh, tq)
    for h in range(nh):
        o_ref[:, h * d:(h + 1) * d] = o[h]


def dsa_sample_attention(layer, q, q_idx, w_idx, dsa_new, idx_new, pool, idx_pool, page_table):
    b, t, _ = q.shape
    d, n_pages = HEAD_DIM, page_table.shape[1]
    past = n_pages * PAGE_SIZE
    topk = min(DSA_TOPK, (past + t) // 4)
    pool = pool.reshape(pool.shape[0], pool.shape[1], PAGE_SIZE * 2, d)
    idx_pool_t = jnp.swapaxes(idx_pool, 2, 3)
    idx_new_t = jnp.pad(jnp.swapaxes(idx_new, 1, 2), ((0, 0), (0, 0), (0, LANE - t)))
    tri = jnp.asarray(np.triu(np.ones((LANE, LANE), np.float32)), jnp.bfloat16)
    per_seq = lambda *tail: pl.BlockSpec((None,) + tail, lambda i, pt: (i,) + (0,) * len(tail))
    kv_page = lambda j: pl.BlockSpec((None, None, PAGE_SIZE * 2, d), lambda i, pt: (layer, pt[i, j], 0, 0))
    idx_page = lambda j: pl.BlockSpec((None, None, IDX_DIM, PAGE_SIZE), lambda i, pt: (layer, pt[i, j], 0, 0))
    grid_spec = pltpu.PrefetchScalarGridSpec(
        num_scalar_prefetch=1, grid=(b,),
        in_specs=[per_seq(t, DSA_HEADS * d), per_seq(IDX_HEADS * t, IDX_DIM), per_seq(IDX_HEADS * t, 1),
                  per_seq(t * 2, d), per_seq(IDX_DIM, LANE), pl.BlockSpec((LANE, LANE), lambda i, pt: (0, 0))]
                 + [kv_page(j) for j in range(n_pages)] + [idx_page(j) for j in range(n_pages)],
        out_specs=per_seq(t, DSA_HEADS * d))
    return pl.pallas_call(
        functools.partial(_dsa_sample_kernel, past=past, topk=topk),
        grid_spec=grid_spec,
        out_shape=jax.ShapeDtypeStruct((b, t, DSA_HEADS * d), jnp.float32),
        compiler_params=pltpu.CompilerParams(dimension_semantics=("parallel",), vmem_limit_bytes=VMEM_LIMIT),
        name="dsa_sample",
    )(page_table, q, q_idx.reshape(b, IDX_HEADS * t, IDX_DIM).astype(jnp.bfloat16),
      jnp.swapaxes(w_idx, 1, 2).reshape(b, IDX_HEADS * t, 1), dsa_new.reshape(b, t * 2, d), idx_new_t, tri,
      *([pool] * n_pages), *([idx_pool_t] * n_pages))


GDN_GROUP = 256
GDN_SUB = 16
GDN_HB = 4
CONV_PAD = 8


def _hp_dot(a, b):
    bf = jnp.bfloat16
    ah, bh = a.astype(bf), b.astype(bf)
    al, bl = (a - ah.astype(jnp.float32)).astype(bf), (b - bh.astype(jnp.float32)).astype(bf)
    dot = functools.partial(jnp.dot, preferred_element_type=jnp.float32)
    return dot(ah, bh) + dot(ah, bl) + dot(al, bh)


def _unit_lower_inverse(a, row, col):
    assert GDN_SUB == 16 and GDN_CHUNK == 64
    n = range(len(a))
    eye = (row == col).astype(jnp.float32)
    sub = jnp.right_shift(row, 4) == jnp.right_shift(col, 4)
    a16 = [jnp.where(sub, a[i], 0.0) for i in n]
    t16 = [eye - a16[i] for i in n]
    power = a16
    for _ in range(3):
        power = [_hp_dot(power[i], power[i]) for i in n]
        t16 = [t16[i] + _hp_dot(t16[i], power[i]) for i in n]
    b = [_hp_dot(t16[i], a[i] - a16[i]) for i in n]
    b2 = [_hp_dot(b[i], b[i]) for i in n]
    imb = [eye - b[i] for i in n]
    left = [imb[i] + _hp_dot(imb[i], b2[i]) for i in n]
    return [_hp_dot(left[i], t16[i]) for i in n]


def _gdn_prompt_kernel(xq_ref, xk_ref, xv_ref, pq_ref, pk_ref, pv_ref, a_ref, b_ref, z_ref, cwq_ref, cwk_ref, cwv_ref,
                       alog_ref, dtb_ref, ng_ref, s0_ref, o_ref, s_ref, hist_ref):
    f32, bf = jnp.float32, jnp.bfloat16
    d, g, c = HEAD_DIM, GDN_GROUP, GDN_CHUNK
    heads = range(GDN_HB)
    lanes = [slice(h * d, (h + 1) * d) for h in heads]
    row = lax.broadcasted_iota(jnp.int32, (g, g), 0)
    col = lax.broadcasted_iota(jnp.int32, (g, g), 1)
    same = jnp.right_shift(row, 6) == jnp.right_shift(col, 6)
    incl = same & (row >= col)
    strict = same & (row > col)
    eye = (row == col).astype(f32)
    tril_b, same_b = incl.astype(bf), same.astype(bf)

    @pl.when(pl.program_id(2) == 0)
    def _():
        s_ref[...] = s0_ref[...]
        for part, p_ref in enumerate((pq_ref, pk_ref, pv_ref)):
            hist_ref[part] = p_ref[...]

    pre = a_ref[...] + dtb_ref[...]
    gate = -jnp.exp(alog_ref[...]) * (jnp.maximum(pre, 0.0) + jnp.log(1.0 + jnp.exp(-jnp.abs(pre))))
    parts = _split3_bf16(gate)
    gcum_all = sum(jnp.dot(tril_b, p, preferred_element_type=f32) for p in parts)
    gtot_all = sum(jnp.dot(same_b, p, preferred_element_type=f32) for p in parts)
    beta_all = 1.0 / (1.0 + jnp.exp(-b_ref[...]))

    def conv(part, x_ref, w_ref, h):
        win = jnp.concatenate([hist_ref[part, :, lanes[h]], x_ref[:, lanes[h]]], axis=0)
        y = sum(pltpu.roll(win, g + CONV_PAD - (CONV_PAD - GDN_CONV + 1 + i), 0)[:g] * w_ref[i:i + 1, lanes[h]]
                for i in range(GDN_CONV))
        return y / (1.0 + jnp.exp(-y))

    q = [conv(0, xq_ref, cwq_ref, h) for h in heads]
    k = [conv(1, xk_ref, cwk_ref, h) for h in heads]
    v = [conv(2, xv_ref, cwv_ref, h) for h in heads]
    for part, x_ref in enumerate((xq_ref, xk_ref, xv_ref)):
        hist_ref[part] = x_ref[g - CONV_PAD:g, :]
    q = [q[h] * lax.rsqrt(jnp.sum(q[h] * q[h], axis=-1, keepdims=True) + EPS) * (d ** -0.5) for h in heads]
    k = [k[h] * lax.rsqrt(jnp.sum(k[h] * k[h], axis=-1, keepdims=True) + EPS) for h in heads]
    gc = [gcum_all[:, h:h + 1] for h in heads]
    gl = [gtot_all[:, h:h + 1] for h in heads]
    beta = [beta_all[:, h:h + 1] for h in heads]
    g_i = [jnp.broadcast_to(gc[h], (g, g)) for h in heads]
    g_j = [jnp.sum(g_i[h] * eye, axis=0, keepdims=True) for h in heads]
    decay = [jnp.where(incl, jnp.exp(jnp.where(incl, g_i[h] - g_j[h], 0.0)), 0.0) for h in heads]
    kb = [k[h] * beta[h] for h in heads]
    k16 = [k[h].astype(bf) for h in heads]
    eg = [jnp.exp(gc[h]) for h in heads]
    a_mat = [jnp.where(strict, lax.dot_general(kb[h].astype(bf), k16[h], _NT, preferred_element_type=f32) * decay[h], 0.0)
             for h in heads]
    qk = [(lax.dot_general(q[h].astype(bf), k16[h], _NT, preferred_element_type=f32) * decay[h]).astype(bf)
          for h in heads]
    rhs = [jnp.concatenate([v[h] * beta[h], kb[h] * eg[h]], axis=1) for h in heads]
    inv = _unit_lower_inverse(a_mat, row, col)
    sol = [_hp_dot(inv[h], rhs[h]) for h in heads]
    u = [sol[h][:, :d] for h in heads]
    w16 = [sol[h][:, d:].astype(bf) for h in heads]
    q_dec = [(q[h] * eg[h]).astype(bf) for h in heads]
    k_dec = [(k[h] * jnp.exp(gl[h] - gc[h])).astype(bf) for h in heads]
    chunk_decay = [jnp.exp(gl[h]) for h in heads]
    for ci in range(g // c):
        rows = slice(ci * c, (ci + 1) * c)
        state = [s_ref[h] for h in heads]
        s16 = [state[h].astype(bf) for h in heads]
        nv16 = [(u[h][rows] - jnp.dot(w16[h][rows], s16[h], preferred_element_type=f32)).astype(bf) for h in heads]
        pad = lambda n: [jnp.zeros((n, d), bf)] if n else []
        placed = [jnp.concatenate(pad(ci * c) + [nv16[h]] + pad(g - (ci + 1) * c), axis=0) for h in heads]
        out = [jnp.dot(q_dec[h][rows], s16[h], preferred_element_type=f32)
               + jnp.dot(qk[h][rows], placed[h], preferred_element_type=f32) for h in heads]
        for h in heads:
            s_ref[h] = state[h] * chunk_decay[h][ci * c:ci * c + 1, :] + lax.dot_general(
                k_dec[h][rows], nv16[h], (((0,), (0,)), ((), ())), preferred_element_type=f32)
        for h in heads:
            zz = z_ref[rows, lanes[h]]
            normed = out[h] * lax.rsqrt(jnp.mean(out[h] * out[h], axis=-1, keepdims=True) + EPS) * ng_ref[...]
            o_ref[rows, lanes[h]] = normed * (zz / (1.0 + jnp.exp(-zz)))


def gdn_prompt(src, qkv_off, z_off, a, beta_logit, conv_prev, s0, conv_w, a_log, dt_bias, norm_g):
    b, t, _ = src.shape
    d, hb = HEAD_DIM, GDN_HB
    ng = GDN_HEADS // hb
    assert t % GDN_GROUP == 0 and GDN_HEADS % hb == 0 and qkv_off % (hb * d) == 0 and z_off % (hb * d) == 0
    qb, zb = qkv_off // (hb * d), z_off // (hb * d)
    prev = jnp.pad(conv_prev.astype(src.dtype), ((0, 0), (CONV_PAD - GDN_CONV + 1, 0), (0, 0)))
    by_group = lambda x: jnp.swapaxes(x.reshape(b, t, ng, hb), 1, 2)
    g = GDN_GROUP
    cols = lambda part: pl.BlockSpec((None, g, hb * d), lambda bi, gi, ti: (bi, ti, qb + part * ng + gi))
    hist = lambda part: pl.BlockSpec((None, CONV_PAD, hb * d), lambda bi, gi, ti: (bi, 0, part * ng + gi))
    cw = lambda part: pl.BlockSpec((GDN_CONV, hb * d), lambda bi, gi, ti: (0, part * ng + gi))
    tok = pl.BlockSpec((None, None, g, hb), lambda bi, gi, ti: (bi, gi, ti, 0))
    head_const = pl.BlockSpec((None, 1, hb), lambda bi, gi, ti: (gi, 0, 0))
    state = pl.BlockSpec((None, hb, d, d), lambda bi, gi, ti: (bi, gi, 0, 0))
    return pl.pallas_call(
        _gdn_prompt_kernel,
        grid=(b, ng, t // g),
        in_specs=[cols(0), cols(1), cols(2), hist(0), hist(1), hist(2), tok, tok,
                  pl.BlockSpec((None, g, hb * d), lambda bi, gi, ti: (bi, ti, zb + gi)),
                  cw(0), cw(1), cw(2), head_const, head_const,
                  pl.BlockSpec((1, d), lambda bi, gi, ti: (0, 0)), state],
        out_specs=[pl.BlockSpec((None, g, hb * d), lambda bi, gi, ti: (bi, ti, gi)), state],
        out_shape=[jax.ShapeDtypeStruct((b, t, W_C), jnp.float32),
                   jax.ShapeDtypeStruct((b, GDN_HEADS, d, d), jnp.float32)],
        scratch_shapes=[pltpu.VMEM((3, CONV_PAD, hb * d), jnp.float32)],
        compiler_params=pltpu.CompilerParams(
            dimension_semantics=("parallel", "parallel", "arbitrary"), vmem_limit_bytes=VMEM_LIMIT),
        name="gdn_prompt",
    )(src, src, src, prev, prev, prev, by_group(a), by_group(beta_logit), src, conv_w, conv_w, conv_w,
      a_log.reshape(ng, 1, hb).astype(jnp.float32), dt_bias.reshape(ng, 1, hb).astype(jnp.float32),
      norm_g.reshape(1, d).astype(jnp.float32), s0)


MEM_TQ = 256
MIN_MXU_ROWS = 16


def _mem_attn_kernel(q_ref, kv_ref, o_ref):
    nh, d, bf = MEM_HEADS, HEAD_DIM, jnp.bfloat16
    tq = q_ref.shape[0]
    m = kv_ref.shape[0] // (2 * nh)
    rows = max(tq, MIN_MXU_ROWS)
    for h in range(nh):
        k = kv_ref[pl.ds(h, m, stride=2 * nh), :].astype(bf)
        v = kv_ref[pl.ds(nh + h, m, stride=2 * nh), :].astype(bf)
        q = _pad_rows(q_ref[:, h * d:(h + 1) * d], rows).astype(bf)
        s = lax.dot_general(q, k, _NT, preferred_element_type=jnp.float32) * (d ** -0.5)
        e = jnp.exp(s - jnp.max(s, axis=-1, keepdims=True))
        p = e / jnp.sum(e, axis=-1, keepdims=True)
        o_ref[:, h * d:(h + 1) * d] = jnp.dot(p.astype(bf), v, preferred_element_type=jnp.float32)[:tq]


def mem_attention_core(q, mem_kv, layer):
    b, t, w = q.shape
    m = mem_kv.shape[2]
    tq = min(MEM_TQ, t)
    assert t % tq == 0
    kv = mem_kv.reshape(mem_kv.shape[0], b, m * 2 * MEM_HEADS, HEAD_DIM)
    return pl.pallas_call(
        _mem_attn_kernel,
        grid=(b, t // tq),
        in_specs=[pl.BlockSpec((None, tq, w), lambda bi, qi: (bi, qi, 0)),
                  pl.BlockSpec((None, None, m * 2 * MEM_HEADS, HEAD_DIM), lambda bi, qi: (layer, bi, 0, 0))],
        out_specs=pl.BlockSpec((None, tq, w), lambda bi, qi: (bi, qi, 0)),
        out_shape=jax.ShapeDtypeStruct((b, t, w), jnp.float32),
        compiler_params=pltpu.CompilerParams(
            dimension_semantics=("parallel", "arbitrary"), vmem_limit_bytes=VMEM_LIMIT),
        name="mem_attn",
    )(q, kv)


def rms_norm(x, g):
    xf = x.astype(jnp.float32)
    y = xf * lax.rsqrt(jnp.mean(xf * xf, axis=-1, keepdims=True) + EPS)
    return (y * g.astype(jnp.float32)).astype(x.dtype)


def l2norm(x):
    xf = x.astype(jnp.float32)
    return xf * lax.rsqrt(jnp.sum(xf * xf, axis=-1, keepdims=True) + EPS)


def rope(x, pos):
    half = x.shape[-1] // 2
    inv = ROPE_THETA ** (-jnp.arange(half, dtype=jnp.float32) / half)
    ang = pos.astype(jnp.float32)[:, None] * inv
    cos, sin = jnp.cos(ang)[:, None, :], jnp.sin(ang)[:, None, :]
    xf = x.astype(jnp.float32)
    x1, x2 = xf[..., :half], xf[..., half:]
    return jnp.concatenate([x1 * cos - x2 * sin, x2 * cos + x1 * sin], axis=-1).astype(x.dtype)


def masked_softmax(s, mask):
    s = jnp.where(mask, s.astype(jnp.float32), NEG)
    e = jnp.where(mask, jnp.exp(s - jnp.max(s, axis=-1, keepdims=True)), 0.0)
    return e / jnp.maximum(jnp.sum(e, axis=-1, keepdims=True), 1e-30)


def split_cols(h):
    return jnp.split(h[..., :sum(IN_SPLITS)], [int(o) for o in np.cumsum(IN_SPLITS)[:-1]], axis=-1)


def gather_pages(pool, page_table):
    rows = pool[page_table]
    return rows.reshape(rows.shape[0], -1, *rows.shape[3:])


def over_query_blocks(fn, *qs):
    b, t = qs[0].shape[:2]
    if t <= Q_BLOCK:
        return fn(0, *qs)
    nb = t // Q_BLOCK
    blocks = tuple(q.reshape(b, nb, Q_BLOCK, *q.shape[2:]).swapaxes(0, 1) for q in qs)
    out = lax.map(lambda a: fn(a[0] * Q_BLOCK, *a[1]), (jnp.arange(nb), blocks))
    out = out.swapaxes(0, 1)
    return out.reshape(b, t, *out.shape[3:])


def nsa_compress(rows, pe, w1, w2):
    b, n_keys, d = rows.shape
    r = CMP_LEN // CMP_STRIDE
    n_cmp = (n_keys - CMP_LEN) // CMP_STRIDE + 1
    seg = rows[:, :(n_cmp + r - 1) * CMP_STRIDE].reshape(b, n_cmp + r - 1, CMP_STRIDE * d)
    w1r = w1.reshape(r, CMP_STRIDE * d, CMP_HID)
    h = pe.reshape(-1) @ w1 + sum(seg[:, i:i + n_cmp] @ w1r[i] for i in range(r))
    return jax.nn.gelu(h) @ w2


def nsa_attention(q, gates, k_cmp, v_cmp, k_slc, v_slc, k_win, v_win, past, win_pos0, banded):
    b, n_keys, d = k_slc.shape
    scale = d ** -0.5
    n_cmp = k_cmp.shape[1]
    cmp_end = jnp.arange(n_cmp) * CMP_STRIDE + CMP_LEN - 1
    n_blk = -(-n_keys // SEL_BLOCK)
    n_sel = min(SEL_COUNT, n_blk)
    pad = ((0, 0), (0, n_blk * SEL_BLOCK - n_keys), (0, 0))
    k_blk = jnp.pad(k_slc, pad).reshape(b, n_blk, SEL_BLOCK, d)
    v_blk = jnp.pad(v_slc, pad).reshape(b, n_blk, SEL_BLOCK, d)
    c0 = jnp.arange(n_cmp)[:, None] * CMP_STRIDE
    j0 = jnp.arange(n_blk)[None, :] * SEL_BLOCK
    share = jnp.clip(jnp.minimum(c0 + CMP_LEN, j0 + SEL_BLOCK) - jnp.maximum(c0, j0), 0, None).astype(jnp.float32) / CMP_LEN
    blk_id = jnp.arange(n_blk)
    if banded:
        wpad = ((0, 0), (WINDOW, 0), (0, 0))
        k_win, v_win = jnp.pad(k_win, wpad), jnp.pad(v_win, wpad)

    def block(start, qb, gb):
        nq = qb.shape[1]
        qp = past + start + jnp.arange(nq)
        s = jnp.einsum('bqhd,bcd->bqhc', qb, k_cmp) * scale
        p_cmp = masked_softmax(s, (cmp_end[None, :] <= qp[:, None])[None, :, None, :])
        o_cmp = jnp.einsum('bqhc,bcd->bqhd', p_cmp.astype(v_cmp.dtype), v_cmp)
        imp = jnp.einsum('bqhc,cj->bqj', p_cmp, share)
        forced = (blk_id[None, :] == qp[:, None] // SEL_BLOCK) | (blk_id[None, :] == 0)
        future = blk_id[None, :] * SEL_BLOCK > qp[:, None]
        imp = jnp.where(forced[None], FORCED, jnp.where(future[None], -1.0, imp))
        _, sel = lax.top_k(imp, n_sel)
        ks = jax.vmap(lambda kb, i: kb[i])(k_blk, sel).reshape(b, nq, n_sel * SEL_BLOCK, d)
        vs = jax.vmap(lambda vb, i: vb[i])(v_blk, sel).reshape(b, nq, n_sel * SEL_BLOCK, d)
        kpos = (sel[..., None] * SEL_BLOCK + jnp.arange(SEL_BLOCK)).reshape(b, nq, n_sel * SEL_BLOCK)
        s = jnp.einsum('bqhd,bqkd->bqhk', qb, ks) * scale
        p = masked_softmax(s, (kpos <= qp[None, :, None])[:, :, None, :])
        o_slc = jnp.einsum('bqhk,bqkd->bqhd', p.astype(vs.dtype), vs)
        if banded:
            n_w = WINDOW + nq
            kw = lax.dynamic_slice_in_dim(k_win, start, n_w, axis=1)
            vw = lax.dynamic_slice_in_dim(v_win, start, n_w, axis=1)
            kp = past + start - WINDOW + jnp.arange(n_w)
        else:
            kw, vw = k_win, v_win
            kp = win_pos0 + jnp.arange(k_win.shape[1])
        dlt = qp[:, None] - kp[None, :]
        s = jnp.einsum('bqhd,bkd->bqhk', qb, kw) * scale
        p = masked_softmax(s, ((dlt >= 0) & (dlt < WINDOW) & (kp[None, :] >= 0))[None, :, None, :])
        o_win = jnp.einsum('bqhk,bkd->bqhd', p.astype(vw.dtype), vw)
        g = jax.nn.sigmoid(gb.astype(jnp.float32))
        o = g[..., 0:1] * o_cmp + g[..., 1:2] * o_slc + g[..., 2:3] * o_win
        return o.astype(qb.dtype)

    return over_query_blocks(block, q, gates)


def dsa_attention(q, q_idx, w_idx, k, v, k_idx, past):
    b, n_keys, d = k.shape
    topk = min(DSA_TOPK, n_keys // 4)
    kpos = jnp.arange(n_keys)

    def block(start, qb, qib, wb):
        qp = past + start + jnp.arange(qb.shape[1])
        causal = kpos[None, :] <= qp[:, None]
        dots = jnp.einsum('bqhd,bsd->bqhs', qib, k_idx).astype(jnp.float32) * IDX_DIM ** -0.5
        score = jnp.einsum('bqh,bqhs->bqs', wb.astype(jnp.float32) * IDX_HEADS ** -0.5, jax.nn.relu(dots))
        score = jnp.where(causal[None], score, NEG)
        _, sel = lax.top_k(score, topk)
        ks = jax.vmap(lambda kk, i: kk[i])(k, sel)
        vs = jax.vmap(lambda vv, i: vv[i])(v, sel)
        s = jnp.einsum('bqhd,bqkd->bqhk', qb, ks) * d ** -0.5
        p = masked_softmax(s, (sel <= qp[None, :, None])[:, :, None, :])
        return jnp.einsum('bqhk,bqkd->bqhd', p.astype(vs.dtype), vs)

    return over_query_blocks(block, q, q_idx, w_idx)


def gated_delta_chunked(q, k, v, g, beta, s0):
    f32 = jnp.float32
    b, t, h, dk = k.shape
    dv = v.shape[-1]
    c = min(GDN_CHUNK, t)
    n = -(-t // c)
    pad = n * c - t

    def to_chunks(a):
        a = jnp.pad(a.astype(f32), [(0, 0), (0, pad)] + [(0, 0)] * (a.ndim - 2))
        a = a.reshape(b, n, c, *a.shape[2:])
        return jnp.moveaxis(a, (1, 3), (0, 2))

    qc, kc, vc, gc, bc = (to_chunks(a) for a in (q, k, v, g, beta))
    gcum = jnp.cumsum(gc, axis=-1)
    pos = jnp.arange(c)
    strict = pos[:, None] > pos[None, :]
    incl = pos[:, None] >= pos[None, :]
    diff = gcum[..., :, None] - gcum[..., None, :]
    decay = jnp.where(incl, jnp.exp(jnp.where(incl, diff, 0.0)), 0.0)
    k_beta = kc * bc[..., None]
    a_mat = jnp.where(strict, jnp.einsum('nbhid,nbhjd->nbhij', k_beta, kc) * decay, 0.0)
    rhs = jnp.concatenate([vc * bc[..., None], k_beta * jnp.exp(gcum)[..., None]], axis=-1)
    sol = lax.linalg.triangular_solve(a_mat + jnp.eye(c, dtype=f32), rhs,
                                      left_side=True, lower=True, unit_diagonal=True)
    u, w = sol[..., :dv], sol[..., dv:]
    qk = jnp.einsum('nbhid,nbhjd->nbhij', qc, kc) * decay
    q_dec = qc * jnp.exp(gcum)[..., None]
    k_dec = kc * jnp.exp(gcum[..., -1:] - gcum)[..., None]
    chunk_decay = jnp.exp(gcum[..., -1])

    def step(state, xs):
        u_i, w_i, qd_i, qk_i, kd_i, cd_i = xs
        new_v = u_i - jnp.einsum('bhck,bhkv->bhcv', w_i, state)
        o_i = jnp.einsum('bhck,bhkv->bhcv', qd_i, state) + jnp.einsum('bhij,bhjv->bhiv', qk_i, new_v)
        state = state * cd_i[..., None, None] + jnp.einsum('bhck,bhcv->bhkv', kd_i, new_v)
        return state, o_i

    s_final, o = lax.scan(step, s0.astype(f32), (u, w, q_dec, qk, k_dec, chunk_decay))
    o = jnp.moveaxis(o, (0, 2), (1, 3)).reshape(b, n * c, h, dv)[:, :t]
    return o, s_final


def gdn_mixer(qkv, a, beta_logit, z, conv_prev, s0, conv_w, a_log, dt_bias, norm_g):
    b, t, _ = qkv.shape
    xp = jnp.concatenate([conv_prev.astype(qkv.dtype), qkv], axis=1)
    conv = jax.nn.silu(sum(xp[:, i:i + t] * conv_w[i] for i in range(GDN_CONV)))
    q, k, v = (cc.reshape(b, t, GDN_HEADS, HEAD_DIM) for cc in jnp.split(conv, 3, axis=-1))
    q = l2norm(q) * HEAD_DIM ** -0.5
    k = l2norm(k)
    g = -jnp.exp(a_log.astype(jnp.float32)) * jax.nn.softplus(a.astype(jnp.float32) + dt_bias.astype(jnp.float32))
    beta = jax.nn.sigmoid(beta_logit.astype(jnp.float32))
    o, s_new = gated_delta_chunked(q, k, v, g, beta, s0)
    o = rms_norm(o, norm_g) * jax.nn.silu(z.reshape(b, t, GDN_HEADS, HEAD_DIM).astype(jnp.float32))
    return o.reshape(b, t, W_C).astype(qkv.dtype), s_new, xp[:, t:]


def mem_attention(x, gain, mem_kv, layer, w_q, w_o):
    o = mem_attention_core(dense(x, w_q, gain=gain), mem_kv, layer)
    return dense(o, w_o, resid=x)


def run_group(x, mem, cache, p):
    prompt = cache is None
    b, t, _ = x.shape
    past = 0 if prompt else cache['page_table'].shape[1] * cache['nsa_kv'].shape[2]
    names = ('nsa_kv', 'dsa_kv', 'idx_k', 'win_kv', 'gdn', 'conv') + (('mem_kv',) if prompt else ())
    out = {nm: [] for nm in names}
    bg, tg = (b, t) if prompt else (1, b * t)
    row_pos = past + jnp.arange(tg) % t
    for l in range(DEPTH):
        hproj = dense(x, p['w_in'][l], gain=p['norm_mix_g'][l], w_rows=True)
        (q_a, q_b, q_i, nsa_new, win_new, dsa_new, idx_new,
         k_slc, v_slc, k_win, v_win, k_dsa, v_dsa, k_idx) = rope_split(hproj.reshape(bg, tg, PROJ_WIDTH), row_pos)
        nsa_new = nsa_new.reshape(b, t, NSA_ROWS, HEAD_DIM)
        win_new = win_new.reshape(b, t, 2, HEAD_DIM)
        dsa_new = dsa_new.reshape(b, t, 2, HEAD_DIM)
        idx_new = idx_new.reshape(b, t, IDX_DIM)
        g_a, w_i, a_c, b_c = (small_col(hproj, nm) for nm in ('g_a', 'w_i', 'a_c', 'b_c'))
        qkv_off, z_off = PROJ_OFF['qkv_c'], PROJ_OFF['z_c']
        qkv_c = hproj[..., qkv_off:qkv_off + 3 * W_C]
        if prompt:
            conv_prev = jnp.zeros((b, GDN_CONV - 1, 3 * W_C), x.dtype)
            s0 = jnp.zeros((b, GDN_HEADS, HEAD_DIM, HEAD_DIM), jnp.float32)
            mem_kv = dense(mem, p['w_mem_kv'][l], gain=p['mem_norm_g'][l]).reshape(b, -1, 2, MEM_HEADS, HEAD_DIM)
            k_cmp, v_cmp = nsa_compress_prompt(nsa_new, p['nsa_cmp_pe'][l], p['nsa_cmp_w1'][l], p['nsa_cmp_w2'][l])
            o_a = nsa_prompt_attention(q_a, g_a, k_cmp, v_cmp, (t - CMP_LEN) // CMP_STRIDE + 1,
                                       k_slc, v_slc, k_win, v_win)
            o_b = dsa_prompt_attention(q_b, q_i, w_i, k_dsa, v_dsa, k_idx)
            win_out = win_new[:, -min(WINDOW, t):]
            o_c, s_new = gdn_prompt(hproj, qkv_off, z_off, a_c, b_c, conv_prev, s0, p['gdn_conv_w'][l],
                                    p['gdn_a_log'][l], p['gdn_dt_bias'][l], p['gdn_norm_g'][l])
            conv_new = jnp.concatenate([conv_prev, qkv_c[:, -(GDN_CONV - 1):]], axis=1)[:, -(GDN_CONV - 1):]
        else:
            pt = cache['page_table']
            conv_prev, s0 = cache['conv'][l], cache['gdn'][l]
            mem_kv = None
            q_i = jnp.swapaxes(q_i.reshape(IDX_HEADS, b, t, IDX_DIM), 0, 1)
            o_a = nsa_sample_attention(l, q_a.reshape(b, t, W_A), g_a, nsa_new, win_new, cache['nsa_kv'], cache['win_kv'],
                                       pt, p['nsa_cmp_pe'][l], p['nsa_cmp_w1'][l], p['nsa_cmp_w2'][l])
            o_b = dsa_sample_attention(l, q_b.reshape(b, t, W_B), q_i, w_i, dsa_new, idx_new, cache['dsa_kv'],
                                       cache['idx_k'], pt)
            win_out = win_new
            o_c, s_new, conv_new = gdn_mixer(qkv_c, a_c, b_c, hproj[..., z_off:z_off + W_C], conv_prev, s0,
                                             p['gdn_conv_w'][l], p['gdn_a_log'][l], p['gdn_dt_bias'][l],
                                             p['gdn_norm_g'][l])
        mix = branch_mix(o_a.reshape(b * t, W_A), o_b.reshape(b * t, W_B), o_c.reshape(b * t, W_C),
                         hproj.reshape(b * t, PROJ_WIDTH), PROJ_OFF['merge'],
                         p['w_branch_a'][l], p['w_branch_b'][l], p['w_branch_c'][l])
        x = dense(mix, p['w_mix_out'][l], resid=x).reshape(b, t, D_MODEL)
        mem_src, mem_layer = (mem_kv[None], 0) if prompt else (cache['mem_kv'], l)
        x = mem_attention(x, p['norm_mem_g'][l], mem_src, mem_layer, p['w_mem_q'][l], p['w_mem_o'][l])
        x = dense(ffn_act(x, p['norm_ffn_g'][l], p['w_ffn_gate'][l], p['w_ffn_up'][l]), p['w_ffn_down'][l], resid=x)
        out['nsa_kv'].append(nsa_new)
        out['dsa_kv'].append(dsa_new)
        out['idx_k'].append(idx_new)
        out['win_kv'].append(win_out)
        out['gdn'].append(s_new)
        out['conv'].append(conv_new)
        if prompt:
            out['mem_kv'].append(mem_kv)
    y = rms_norm(x, p['norm_final_g'])
    out = {nm: jnp.stack(v) for nm, v in out.items()}
    if not prompt:
        wlen = cache['win_kv'].shape[2]
        out['win_kv'] = jnp.concatenate([cache['win_kv'], out['win_kv']], axis=2)[:, :, -wlen:]
    return y, out


def kernel(x_prompt, x_sample, mem_prompt, cache_nsa_kv, cache_dsa_kv, cache_dsa_idx_k, cache_win_kv, cache_mem_kv, state_gdn, state_conv, page_table, norm_mix_g, w_in, nsa_cmp_pe, nsa_cmp_w1, nsa_cmp_w2, gdn_conv_w, gdn_a_log, gdn_dt_bias, gdn_norm_g, w_branch_a, w_branch_b, w_branch_c, w_mix_out, norm_mem_g, mem_norm_g, w_mem_q, w_mem_kv, w_mem_o, norm_ffn_g, w_ffn_gate, w_ffn_up, w_ffn_down, norm_final_g):
    p = dict(norm_mix_g=norm_mix_g, w_in=w_in, nsa_cmp_pe=nsa_cmp_pe, nsa_cmp_w1=nsa_cmp_w1,
             nsa_cmp_w2=nsa_cmp_w2, gdn_conv_w=gdn_conv_w, gdn_a_log=gdn_a_log, gdn_dt_bias=gdn_dt_bias,
             gdn_norm_g=gdn_norm_g, w_branch_a=w_branch_a, w_branch_b=w_branch_b, w_branch_c=w_branch_c,
             w_mix_out=w_mix_out, norm_mem_g=norm_mem_g, mem_norm_g=mem_norm_g, w_mem_q=w_mem_q,
             w_mem_kv=w_mem_kv, w_mem_o=w_mem_o, norm_ffn_g=norm_ffn_g, w_ffn_gate=w_ffn_gate,
             w_ffn_up=w_ffn_up, w_ffn_down=w_ffn_down, norm_final_g=norm_final_g)
    for nm in ('w_branch_a', 'w_branch_b', 'w_branch_c', 'w_mix_out', 'w_mem_q', 'w_mem_kv', 'w_mem_o',
               'w_ffn_gate', 'w_ffn_up', 'w_ffn_down'):
        p[nm] = p[nm].astype(jnp.bfloat16)
    p['w_in'] = permute_w_in(w_in)
    cache = dict(nsa_kv=cache_nsa_kv, dsa_kv=cache_dsa_kv, idx_k=cache_dsa_idx_k, win_kv=cache_win_kv,
                 mem_kv=cache_mem_kv, gdn=state_gdn, conv=state_conv, page_table=page_table)
    y_prompt, sp = run_group(x_prompt, mem_prompt, None, p)
    y_sample, ss = run_group(x_sample, None, cache, p)
    return (y_prompt, y_sample,
            sp['nsa_kv'], sp['dsa_kv'], sp['idx_k'], sp['win_kv'], sp['gdn'], sp['conv'], sp['mem_kv'],
            ss['nsa_kv'], ss['dsa_kv'], ss['idx_k'], ss['win_kv'], ss['gdn'], ss['conv'])
```

```python
import functools
import math

import jax
import jax.numpy as jnp
import numpy as np
from jax import lax
from jax.experimental import pallas as pl
from jax.experimental.pallas import tpu as pltpu

D_MODEL = 2048
DEPTH = 2
PAGE_SIZE = 128
HEAD_DIM = 128
NSA_HEADS = D_MODEL // (4 * HEAD_DIM)
DSA_HEADS = D_MODEL // (4 * HEAD_DIM)
GDN_HEADS = D_MODEL // (2 * HEAD_DIM)
W_A = NSA_HEADS * HEAD_DIM
W_B = DSA_HEADS * HEAD_DIM
W_C = GDN_HEADS * HEAD_DIM
CMP_LEN = 32
CMP_STRIDE = 16
CMP_HID = 2 * HEAD_DIM
SEL_BLOCK = 64
SEL_COUNT = 16
WINDOW = 512
IDX_HEADS = 16
IDX_DIM = 64
DSA_TOPK = 256
GDN_CONV = 4
GDN_CHUNK = 64
MEM_HEADS = 4
ROPE_THETA = 10000.0
Q_BLOCK = 128
EPS = 1e-6
NEG = -1e30
FORCED = 1e9
IN_SPLITS = (W_A, 6 * HEAD_DIM, 3 * NSA_HEADS,
             W_B, 2 * HEAD_DIM, IDX_HEADS * IDX_DIM, IDX_HEADS, IDX_DIM,
             3 * W_C, GDN_HEADS, GDN_HEADS, W_C,
             3 * D_MODEL)

LANE = 128
VMEM_LIMIT = 48 * 1024 * 1024


DENSE_VMEM_BUDGET = 40 * 1024 * 1024
NORM_ROWS = 128


def _divisor_tiles(n, cap):
    return [t for t in range(min(cap, n), 0, -LANE) if t % LANE == 0 and n % t == 0]


def _dense_tiles(m, n, per_row_bytes, per_col_bytes, per_out_bytes):
    for tm in (1024, 512, 256, 128):
        if m % tm:
            continue
        for tn in _divisor_tiles(n, 1024):
            if tm * per_row_bytes + tn * per_col_bytes + tm * tn * per_out_bytes <= DENSE_VMEM_BUDGET:
                return tm, tn
    raise ValueError("no dense tile fits VMEM")


def _stage_rows(x_ref, g_ref, xn_ref):
    def body(r, carry):
        rows = pl.ds(pl.multiple_of(r * NORM_ROWS, NORM_ROWS), NORM_ROWS)
        xf = x_ref[rows, :].astype(jnp.float32)
        if g_ref is not None:
            xf = xf * lax.rsqrt(jnp.mean(xf * xf, axis=-1, keepdims=True) + EPS) * g_ref[...]
        xn_ref[rows, :] = xf.astype(jnp.bfloat16)
        return carry
    lax.fori_loop(0, x_ref.shape[0] // NORM_ROWS, body, 0)


def _dense_kernel(*refs, has_gain, has_resid, staged):
    it = iter(refs)
    x_ref = next(it)
    g_ref = next(it) if has_gain else None
    w_ref = next(it)
    r_ref = next(it) if has_resid else None
    o_ref = next(it)
    xn_ref = next(it) if staged else x_ref

    if staged:
        @pl.when(pl.program_id(1) == 0)
        def _():
            _stage_rows(x_ref, g_ref, xn_ref)

    acc = jnp.dot(xn_ref[...], w_ref[...], preferred_element_type=jnp.float32)
    if has_resid:
        acc = acc + r_ref[...]
    o_ref[...] = acc.astype(o_ref.dtype)


def dense(x, w, gain=None, resid=None, out_dtype=jnp.float32):
    lead, kdim, n = x.shape[:-1], x.shape[-1], w.shape[-1]
    x2 = x.reshape(-1, kdim)
    m = x2.shape[0]
    ob = jnp.dtype(out_dtype).itemsize
    staged = gain is not None or x2.dtype != jnp.bfloat16
    tm, tn = _dense_tiles(m, n, per_row_bytes=kdim * (2 * x2.dtype.itemsize + (2 if staged else 0)),
                          per_col_bytes=kdim * 2 * 2, per_out_bytes=2 * ob + (8 if resid is not None else 0))
    args, specs = [x2], [pl.BlockSpec((tm, kdim), lambda i, j: (i, 0))]
    if gain is not None:
        args.append(gain.reshape(1, kdim).astype(jnp.float32))
        specs.append(pl.BlockSpec((1, kdim), lambda i, j: (0, 0)))
    args.append(w)
    specs.append(pl.BlockSpec((kdim, tn), lambda i, j: (0, j)))
    if resid is not None:
        args.append(resid.reshape(m, n))
        specs.append(pl.BlockSpec((tm, tn), lambda i, j: (i, j)))
    out = pl.pallas_call(
        functools.partial(_dense_kernel, has_gain=gain is not None, has_resid=resid is not None, staged=staged),
        grid=(m // tm, n // tn),
        in_specs=specs,
        out_specs=pl.BlockSpec((tm, tn), lambda i, j: (i, j)),
        out_shape=jax.ShapeDtypeStruct((m, n), out_dtype),
        scratch_shapes=[pltpu.VMEM((tm, kdim), jnp.bfloat16)] if staged else [],
        compiler_params=pltpu.CompilerParams(
            dimension_semantics=("parallel", "arbitrary"), vmem_limit_bytes=VMEM_LIMIT),
        name="dense",
    )(*args)
    return out.reshape(*lead, n)


def _ffn_act_kernel(x_ref, g_ref, wg_ref, wu_ref, o_ref, xn_ref):
    @pl.when(pl.program_id(1) == 0)
    def _():
        _stage_rows(x_ref, g_ref, xn_ref)

    xn = xn_ref[...]
    a = jnp.dot(xn, wg_ref[...], preferred_element_type=jnp.float32)
    u = jnp.dot(xn, wu_ref[...], preferred_element_type=jnp.float32)
    o_ref[...] = (a / (1.0 + jnp.exp(-a)) * u).astype(o_ref.dtype)


def ffn_act(x, gain, w_gate, w_up):
    lead, kdim, n = x.shape[:-1], x.shape[-1], w_gate.shape[-1]
    x2 = x.reshape(-1, kdim)
    m = x2.shape[0]
    tm, tn = _dense_tiles(m, n, per_row_bytes=kdim * (2 * 4 + 2), per_col_bytes=2 * kdim * 2 * 2, per_out_bytes=2 * 2)
    out = pl.pallas_call(
        _ffn_act_kernel,
        grid=(m // tm, n // tn),
        in_specs=[pl.BlockSpec((tm, kdim), lambda i, j: (i, 0)),
                  pl.BlockSpec((1, kdim), lambda i, j: (0, 0)),
                  pl.BlockSpec((kdim, tn), lambda i, j: (0, j)),
                  pl.BlockSpec((kdim, tn), lambda i, j: (0, j))],
        out_specs=pl.BlockSpec((tm, tn), lambda i, j: (i, j)),
        out_shape=jax.ShapeDtypeStruct((m, n), jnp.bfloat16),
        scratch_shapes=[pltpu.VMEM((tm, kdim), jnp.bfloat16)],
        compiler_params=pltpu.CompilerParams(
            dimension_semantics=("parallel", "arbitrary"), vmem_limit_bytes=VMEM_LIMIT),
        name="ffn_act",
    )(x2, gain.reshape(1, kdim).astype(jnp.float32), w_gate, w_up)
    return out.reshape(*lead, n)


def _branch_mix_kernel(oa_ref, ob_ref, oc_ref, ma_ref, mb_ref, mc_ref, wa_ref, wb_ref, wc_ref, o_ref, xn_ref):
    @pl.when(pl.program_id(1) == 0)
    def _():
        off = 0
        for src in (oa_ref, ob_ref, oc_ref):
            xn_ref[:, off:off + src.shape[1]] = src[...].astype(jnp.bfloat16)
            off += src.shape[1]

    def gated(m_ref, w_ref, lo, hi):
        y = jnp.dot(xn_ref[:, lo:hi], w_ref[...], preferred_element_type=jnp.float32)
        return y / (1.0 + jnp.exp(-m_ref[...]))

    o_ref[...] = (gated(ma_ref, wa_ref, 0, W_A) + gated(mb_ref, wb_ref, W_A, W_A + W_B)
                  + gated(mc_ref, wc_ref, W_A + W_B, W_A + W_B + W_C)).astype(o_ref.dtype)


def branch_mix(o_a, o_b, o_c, merge, merge_off, w_a, w_b, w_c):
    m, n = merge.shape[0], w_a.shape[-1]
    ktot = W_A + W_B + W_C
    tm, tn = _dense_tiles(m, n, per_row_bytes=ktot * (2 * 4 + 2), per_col_bytes=ktot * 2 * 2, per_out_bytes=2 * 2 + 3 * 8)
    nb = n // tn
    assert merge_off % tn == 0
    mb = merge_off // tn
    row = lambda width: pl.BlockSpec((tm, width), lambda i, j: (i, 0))
    return pl.pallas_call(
        _branch_mix_kernel,
        grid=(m // tm, nb),
        in_specs=[row(W_A), row(W_B), row(W_C),
                  pl.BlockSpec((tm, tn), lambda i, j: (i, mb + j)),
                  pl.BlockSpec((tm, tn), lambda i, j: (i, mb + j + nb)),
                  pl.BlockSpec((tm, tn), lambda i, j: (i, mb + j + 2 * nb)),
                  pl.BlockSpec((W_A, tn), lambda i, j: (0, j)),
                  pl.BlockSpec((W_B, tn), lambda i, j: (0, j)),
                  pl.BlockSpec((W_C, tn), lambda i, j: (0, j))],
        out_specs=pl.BlockSpec((tm, tn), lambda i, j: (i, j)),
        out_shape=jax.ShapeDtypeStruct((m, n), jnp.bfloat16),
        scratch_shapes=[pltpu.VMEM((tm, ktot), jnp.bfloat16)],
        compiler_params=pltpu.CompilerParams(
            dimension_semantics=("parallel", "arbitrary"), vmem_limit_bytes=VMEM_LIMIT),
        name="branch_mix",
    )(o_a, o_b, o_c, merge, merge, merge, w_a, w_b, w_c)


REF_COLS = tuple(zip(('q_a', 'kv_a', 'g_a', 'q_b', 'kv_b', 'q_i', 'w_i', 'k_i', 'qkv_c', 'a_c', 'b_c', 'z_c', 'merge'),
                     IN_SPLITS))
SMALL_COLS = ('k_i', 'g_a', 'w_i', 'a_c', 'b_c')
WIDE_COLS = ('q_a', 'kv_a', 'q_b', 'kv_b', 'q_i', 'qkv_c', 'z_c', 'merge')


def _layout():
    width = dict(REF_COLS)
    assert all(width[n] % LANE == 0 for n in WIDE_COLS) and sum(width[n] for n in SMALL_COLS) <= LANE
    off, pos = {}, 0
    for n in WIDE_COLS:
        off[n] = pos
        pos += width[n]
    off['small'] = pos
    small, spos = {}, 0
    for n in SMALL_COLS:
        small[n] = (spos, width[n])
        spos += width[n]
    return off, small, pos + LANE


PROJ_OFF, SMALL_OFF, PROJ_WIDTH = _layout()
ROPE_COLS = PROJ_OFF['qkv_c']
ROPE_TM = 256


def _column_moves():
    ref_off, pos = {}, 0
    for n, wd in REF_COLS:
        ref_off[n] = pos
        pos += wd
    width = dict(REF_COLS)
    moves = [(ref_off[n], PROJ_OFF[n], width[n]) for n in WIDE_COLS]
    moves += [(ref_off[n], PROJ_OFF['small'] + SMALL_OFF[n][0], width[n]) for n in SMALL_COLS]
    return moves, PROJ_OFF['small'] + sum(width[n] for n in SMALL_COLS)


PERMUTE_ROWS = 128


def _permute_kernel(w_ref, o_ref):
    moves, used = _column_moves()
    for src, dst, wd in moves:
        o_ref[:, dst:dst + wd] = w_ref[:, src:src + wd].astype(o_ref.dtype)
    o_ref[:, used:] = jnp.zeros((o_ref.shape[0], o_ref.shape[1] - used), o_ref.dtype)


def permute_w_in(w):
    nl, kdim, n_in = w.shape
    return pl.pallas_call(
        _permute_kernel,
        grid=(nl, kdim // PERMUTE_ROWS),
        in_specs=[pl.BlockSpec((None, PERMUTE_ROWS, n_in), lambda l, i: (l, i, 0))],
        out_specs=pl.BlockSpec((None, PERMUTE_ROWS, PROJ_WIDTH), lambda l, i: (l, i, 0)),
        out_shape=jax.ShapeDtypeStruct((nl, kdim, PROJ_WIDTH), jnp.bfloat16),
        compiler_params=pltpu.CompilerParams(
            dimension_semantics=("parallel", "parallel"), vmem_limit_bytes=VMEM_LIMIT),
        name="permute_w_in",
    )(w)


def small_col(hproj, name):
    lo, wd = SMALL_OFF[name]
    return hproj[..., PROJ_OFF['small'] + lo:PROJ_OFF['small'] + lo + wd]


def _rope_tables(pos, rows, head_dim=HEAD_DIM):
    half = head_dim // 2
    inv = ROPE_THETA ** (-jnp.arange(half, dtype=jnp.float32) / half)
    ang = pos.astype(jnp.float32)[:, None] * inv
    cos, sin = jnp.cos(ang), jnp.sin(ang)
    reps = LANE // head_dim
    pad = ((0, rows - pos.shape[0]), (0, 0))
    return (jnp.pad(jnp.tile(jnp.concatenate([cos, cos], axis=-1), (1, reps)), pad),
            jnp.pad(jnp.tile(jnp.concatenate([-sin, sin], axis=-1), (1, reps)), pad))


def _rope_split_kernel(h_ref, small_ref, c128_ref, s128_ref, c64_ref, s64_ref,
                       qa_ref, qb_ref, qi_ref, nsa_ref, win_ref, dsa_ref, idx_ref,
                       kslc_ref, vslc_ref, kwin_ref, vwin_ref, kdsa_ref, vdsa_ref, kidx_ref):
    d, bf = HEAD_DIM, jnp.bfloat16
    c128, s128, c64, s64 = c128_ref[...], s128_ref[...], c64_ref[...], s64_ref[...]
    first_half = (lax.broadcasted_iota(jnp.int32, (1, LANE), 1) & (IDX_DIM - 1)) < IDX_DIM // 2

    def rope128(x):
        return x * c128 + pltpu.roll(x, d // 2, 1) * s128

    def rope64(x):
        rot = jnp.where(first_half, pltpu.roll(x, LANE - IDX_DIM // 2, 1), pltpu.roll(x, IDX_DIM // 2, 1))
        return x * c64 + rot * s64

    col = lambda name, i: h_ref[:, PROJ_OFF[name] + i * d:PROJ_OFF[name] + (i + 1) * d]
    for i in range(NSA_HEADS):
        qa_ref[:, i * d:(i + 1) * d] = (rope128(col('q_a', i)) * (d ** -0.5)).astype(bf)
    for i in range(DSA_HEADS):
        qb_ref[:, i * d:(i + 1) * d] = (rope128(col('q_b', i)) * (d ** -0.5)).astype(bf)
    k_slc, v_slc, k_win, v_win = rope128(col('kv_a', 2)), col('kv_a', 3), rope128(col('kv_a', 4)), col('kv_a', 5)
    for i, part in enumerate((col('kv_a', 0), col('kv_a', 1), k_slc, v_slc)):
        nsa_ref[:, i * d:(i + 1) * d] = part
    win_ref[:, 0:d], win_ref[:, d:2 * d] = k_win, v_win
    kslc_ref[...], vslc_ref[...], kwin_ref[...], vwin_ref[...] = (k_slc.astype(bf), v_slc.astype(bf),
                                                                   k_win.astype(bf), v_win.astype(bf))
    k_dsa, v_dsa = rope128(col('kv_b', 0)), col('kv_b', 1)
    dsa_ref[:, 0:d], dsa_ref[:, d:2 * d] = k_dsa, v_dsa
    kdsa_ref[...], vdsa_ref[...] = k_dsa.astype(bf), v_dsa.astype(bf)
    for i in range(IDX_HEADS * IDX_DIM // LANE):
        pair = rope64(col('q_i', i)).astype(bf)
        qi_ref[2 * i] = pair[:, :IDX_DIM]
        qi_ref[2 * i + 1] = pair[:, IDX_DIM:]
    assert SMALL_OFF['k_i'][0] == 0
    k_idx = rope64(small_ref[...])[:, :IDX_DIM]
    idx_ref[...] = k_idx
    kidx_ref[...] = k_idx.astype(bf)


def rope_split(hproj, pos):
    bg, tg, _ = hproj.shape
    d, bf, f32 = HEAD_DIM, jnp.bfloat16, jnp.float32
    tm = min(ROPE_TM, tg)
    assert tg % tm == 0
    c128, s128 = _rope_tables(pos, tg, d)
    c64, s64 = _rope_tables(pos, tg, IDX_DIM)
    rows = lambda w: pl.BlockSpec((None, tm, w), lambda b, i: (b, i, 0))
    table = pl.BlockSpec((tm, LANE), lambda b, i: (i, 0))
    shape = lambda w, dt: jax.ShapeDtypeStruct((bg, tg, w), dt)
    outs = [(W_A, bf), (W_B, bf), None, (NSA_ROWS * d, f32), (2 * d, f32), (2 * d, f32), (IDX_DIM, f32),
            (d, bf), (d, bf), (d, bf), (d, bf), (d, bf), (d, bf), (IDX_DIM, bf)]
    out_specs = [pl.BlockSpec((None, IDX_HEADS, tm, IDX_DIM), lambda b, i: (b, 0, i, 0)) if o is None else rows(o[0])
                 for o in outs]
    out_shape = [jax.ShapeDtypeStruct((bg, IDX_HEADS, tg, IDX_DIM), bf) if o is None else shape(*o) for o in outs]
    return pl.pallas_call(
        _rope_split_kernel,
        grid=(bg, tg // tm),
        in_specs=[pl.BlockSpec((None, tm, ROPE_COLS), lambda b, i: (b, i, 0)),
                  pl.BlockSpec((None, tm, LANE), lambda b, i: (b, i, PROJ_OFF['small'] // LANE)),
                  table, table, table, table],
        out_specs=out_specs, out_shape=out_shape,
        compiler_params=pltpu.CompilerParams(
            dimension_semantics=("parallel", "parallel"), vmem_limit_bytes=VMEM_LIMIT),
        name="rope_split",
    )(hproj, hproj, c128, s128, c64, s64)


_NT = (((1,), (1,)), ((), ()))


def _softmax_tiles(qh, k_ref, v_ref, lo, hi, tk, mask_fn, m_ref, l_ref, acc_ref, nh, tq):
    assert qh.shape[-1] == LANE and tk % LANE == 0
    m_ref[...] = jnp.full(m_ref.shape, NEG, jnp.float32)
    l_ref[...] = jnp.zeros(l_ref.shape, jnp.float32)
    acc_ref[...] = jnp.zeros(acc_ref.shape, jnp.float32)

    def body(kt, carry):
        off = pl.multiple_of(kt * tk, tk)
        k = k_ref[0, pl.ds(off, tk), :]
        v = v_ref[0, pl.ds(off, tk), :]
        bias = (mask_fn(kt) - 1.0) * (-NEG)
        heads = range(nh)
        chunks = []
        for h in heads:
            s = lax.dot_general(qh[h * tq:(h + 1) * tq], k, _NT, preferred_element_type=jnp.float32) + bias
            chunks.append([s[:, c * LANE:(c + 1) * LANE] for c in range(tk // LANE)])
        m_old = [m_ref[h] for h in heads]
        m_new = [jnp.maximum(m_old[h], jnp.max(functools.reduce(jnp.maximum, chunks[h]), axis=-1, keepdims=True))
                 for h in heads]
        ps = [[jnp.exp(ch - m_new[h]) for ch in chunks[h]] for h in heads]
        alpha = [jnp.exp(m_old[h] - m_new[h]) for h in heads]
        pv = [jnp.dot(jnp.concatenate(ps[h], axis=-1).astype(jnp.bfloat16), v, preferred_element_type=jnp.float32)
              for h in heads]
        for h in heads:
            l_ref[h] = alpha[h] * l_ref[h] + functools.reduce(jnp.add, ps[h])
            acc_ref[h] = alpha[h] * acc_ref[h] + pv[h]
            m_ref[h] = m_new[h]
        return carry

    lax.fori_loop(lo, hi, body, 0)
    return acc_ref[...] / jnp.maximum(jnp.sum(l_ref[...], axis=-1, keepdims=True), 1e-30)


def _split3_bf16(x):
    hi = x.astype(jnp.bfloat16)
    r1 = x - hi.astype(jnp.float32)
    mid = r1.astype(jnp.bfloat16)
    lo = (r1 - mid.astype(jnp.float32)).astype(jnp.bfloat16)
    return hi, mid, lo


def _select_blocks(psum, share, qpos_r, n_blk, tq):
    nbp, tqp = share.shape[0], psum.shape[0]
    imp_t = sum(lax.dot_general(share, part, _NT, preferred_element_type=jnp.float32)
                for part in _split3_bf16(psum))
    blk = lax.broadcasted_iota(jnp.int32, (nbp, tqp), 0)
    forced = (blk == jnp.right_shift(qpos_r, 6)) | (blk == 0)
    future = blk * SEL_BLOCK > qpos_r
    imp_t = jnp.where(forced, FORCED, jnp.where(future, -1.0, imp_t))
    imp_t = jnp.where(blk < n_blk, imp_t, -2.0)
    rank = jnp.zeros((nbp, tqp), jnp.float32)
    for i in range(n_blk):
        row = imp_t[i:i + 1, :]
        beats = (row > imp_t) | ((row == imp_t) & (blk > i))
        rank = rank + beats.astype(jnp.float32)
    sel_t = (rank < float(min(SEL_COUNT, n_blk))).astype(jnp.bfloat16)
    eye = (lax.broadcasted_iota(jnp.int32, (tq, tqp), 0) == lax.broadcasted_iota(jnp.int32, (tq, tqp), 1))
    return lax.dot_general(eye.astype(jnp.bfloat16), sel_t, _NT,
                           preferred_element_type=jnp.float32).astype(jnp.bfloat16)


def _softmax_flat(qh, tiles, nh, tq):
    scores = []
    for k, _, maskf in tiles:
        s = lax.dot_general(qh, k, _NT, preferred_element_type=jnp.float32).reshape(nh, tq, k.shape[0])
        scores.append(s + ((maskf - 1.0) * (-NEG))[None])
    m = scores[0]
    for s in scores[1:]:
        m = jnp.maximum(m, s)
    m = jnp.max(m, axis=-1, keepdims=True)
    lsum, acc = None, None
    for s, (_, v, maskf) in zip(scores, tiles):
        p = jnp.exp(s - m) * maskf[None]
        pv = jnp.dot(p.reshape(nh * tq, p.shape[-1]).astype(jnp.bfloat16), v, preferred_element_type=jnp.float32)
        lsum = p if lsum is None else lsum + p
        acc = pv if acc is None else acc + pv
    l = jnp.sum(lsum, axis=-1, keepdims=True)
    return acc.reshape(nh, tq, acc.shape[-1]) / jnp.maximum(l, 1e-30)


NSA_TQ = 128
NSA_TK_SLC = 512
NSA_TK_WIN = 128


def _nsa_prompt_kernel(q_ref, g_ref, kc_ref, vc_ref, ks_ref, vs_ref, kw_ref, vw_ref, share_ref, o_ref,
                       m_ref, l_ref, acc_ref, *, n_cmp, n_blk, n_keys):
    nh, tq, d = NSA_HEADS, NSA_TQ, HEAD_DIM
    nbp = share_ref.shape[0]
    ncp = share_ref.shape[1]
    start = pl.program_id(1) * tq
    qpos_c = start + lax.broadcasted_iota(jnp.int32, (tq, 1), 0)
    qpos_r = start + lax.broadcasted_iota(jnp.int32, (1, tq), 1)

    qh = jnp.concatenate([q_ref[0, :, h * d:(h + 1) * d] for h in range(nh)], axis=0)

    c_r = lax.broadcasted_iota(jnp.int32, (1, ncp), 1)
    mask_c = ((c_r * CMP_STRIDE + (CMP_LEN - 1) <= qpos_c) & (c_r < n_cmp)).astype(jnp.float32)
    s = lax.dot_general(qh, kc_ref[0], _NT, preferred_element_type=jnp.float32).reshape(nh, tq, ncp)
    sm = s + ((mask_c - 1.0) * (-NEG))[None]
    e = jnp.exp(sm - jnp.max(sm, axis=-1, keepdims=True)) * mask_c[None]
    p_cmp = e / jnp.maximum(jnp.sum(e, axis=-1, keepdims=True), 1e-30)
    o_cmp = jnp.dot(p_cmp.reshape(nh * tq, ncp).astype(jnp.bfloat16), vc_ref[0],
                    preferred_element_type=jnp.float32)

    sel = _select_blocks(jnp.sum(p_cmp, axis=0), share_ref[...], qpos_r, n_blk, tq)

    def slc_mask(kt):
        kpos = kt * NSA_TK_SLC + lax.broadcasted_iota(jnp.int32, (1, NSA_TK_SLC), 1)
        kblk = kt * (NSA_TK_SLC // SEL_BLOCK) + jnp.right_shift(
            lax.broadcasted_iota(jnp.int32, (nbp, NSA_TK_SLC), 1), 6)
        expand = (kblk == lax.broadcasted_iota(jnp.int32, (nbp, NSA_TK_SLC), 0)).astype(jnp.bfloat16)
        chosen = jnp.dot(sel, expand, preferred_element_type=jnp.float32)
        return chosen * (kpos <= qpos_c).astype(jnp.float32)

    hi_slc = jnp.minimum((start + tq - 1) // NSA_TK_SLC + 1, n_keys // NSA_TK_SLC)
    o_slc = _softmax_tiles(qh, ks_ref, vs_ref, 0, hi_slc, NSA_TK_SLC, slc_mask, m_ref, l_ref, acc_ref, nh, tq)

    tiles = []
    for i in range(WINDOW // NSA_TK_WIN + 1):
        kt = pl.program_id(1) * (tq // NSA_TK_WIN) - WINDOW // NSA_TK_WIN + i
        off = pl.multiple_of(jnp.maximum(kt, 0) * NSA_TK_WIN, NSA_TK_WIN)
        kpos = kt * NSA_TK_WIN + lax.broadcasted_iota(jnp.int32, (1, NSA_TK_WIN), 1)
        dlt = qpos_c - kpos
        maskf = ((dlt >= 0) & (dlt < WINDOW) & (kpos >= 0)).astype(jnp.float32)
        tiles.append((kw_ref[0, pl.ds(off, NSA_TK_WIN), :], vw_ref[0, pl.ds(off, NSA_TK_WIN), :], maskf))
    o_win = _softmax_flat(qh, tiles, nh, tq)

    gate = 1.0 / (1.0 + jnp.exp(-g_ref[0]))
    o_cmp = o_cmp.reshape(nh, tq, d)
    for h in range(nh):
        o_ref[0, :, h * d:(h + 1) * d] = (gate[:, 3 * h:3 * h + 1] * o_cmp[h]
                                          + gate[:, 3 * h + 1:3 * h + 2] * o_slc[h]
                                          + gate[:, 3 * h + 2:3 * h + 3] * o_win[h])


def _share_matrix_t(n_cmp, n_blk, ncp, nbp):
    c0 = np.arange(n_cmp)[None, :] * CMP_STRIDE
    j0 = np.arange(n_blk)[:, None] * SEL_BLOCK
    share = np.clip(np.minimum(c0 + CMP_LEN, j0 + SEL_BLOCK) - np.maximum(c0, j0), 0, None) / CMP_LEN
    out = np.zeros((nbp, ncp), np.float32)
    out[:n_blk, :n_cmp] = share
    return jnp.asarray(out, jnp.bfloat16)


def nsa_prompt_attention(q, gates, k_cmp, v_cmp, n_cmp, k_slc, v_slc, k_win, v_win):
    b, t, _ = q.shape
    d = HEAD_DIM
    n_blk = -(-t // SEL_BLOCK)
    ncp = -(-n_cmp // LANE) * LANE
    nbp = -(-n_blk // 16) * 16
    assert t % NSA_TK_SLC == 0 and t % NSA_TQ == 0
    bf = jnp.bfloat16
    cpad = ((0, 0), (0, ncp - k_cmp.shape[1]), (0, 0))
    kc, vc = jnp.pad(k_cmp, cpad).astype(bf), jnp.pad(v_cmp, cpad).astype(bf)
    whole = lambda n: pl.BlockSpec((1, n, d), lambda bi, qi: (bi, 0, 0))
    return pl.pallas_call(
        functools.partial(_nsa_prompt_kernel, n_cmp=n_cmp, n_blk=n_blk, n_keys=t),
        grid=(b, t // NSA_TQ),
        in_specs=[pl.BlockSpec((1, NSA_TQ, NSA_HEADS * d), lambda bi, qi: (bi, qi, 0)),
                  pl.BlockSpec((1, NSA_TQ, 3 * NSA_HEADS), lambda bi, qi: (bi, qi, 0)),
                  whole(ncp), whole(ncp), whole(t), whole(t), whole(t), whole(t),
                  pl.BlockSpec((nbp, ncp), lambda bi, qi: (0, 0))],
        out_specs=pl.BlockSpec((1, NSA_TQ, NSA_HEADS * d), lambda bi, qi: (bi, qi, 0)),
        out_shape=jax.ShapeDtypeStruct((b, t, NSA_HEADS * d), jnp.float32),
        scratch_shapes=[pltpu.VMEM((NSA_HEADS, NSA_TQ, LANE), jnp.float32),
                        pltpu.VMEM((NSA_HEADS, NSA_TQ, LANE), jnp.float32),
                        pltpu.VMEM((NSA_HEADS, NSA_TQ, d), jnp.float32)],
        compiler_params=pltpu.CompilerParams(
            dimension_semantics=("parallel", "arbitrary"), vmem_limit_bytes=VMEM_LIMIT),
        name="nsa_prompt",
    )(q, gates, kc, vc, k_slc.astype(bf), v_slc.astype(bf), k_win.astype(bf), v_win.astype(bf),
      _share_matrix_t(n_cmp, n_blk, ncp, nbp))


NSA_ROWS = 4
NSA_SEQ = 2


def _gelu_tanh(x):
    return 0.5 * x * (1.0 + jnp.tanh(math.sqrt(2.0 / math.pi) * (x + 0.044715 * (x * x * x))))


def _pad_rows(x, rows):
    if rows == x.shape[0]:
        return x
    return jnp.concatenate([x, jnp.zeros((rows - x.shape[0], x.shape[1]), x.dtype)], axis=0)


def _compress_tokens(seg_rows, pe_ref, w1_ref, w2_ref, kv, ncp):
    d, bf = HEAD_DIM, jnp.bfloat16
    first = jnp.zeros((ncp, CMP_HID), jnp.float32)
    second = jnp.zeros((ncp, CMP_HID), jnp.float32)
    for r in range(0, CMP_STRIDE, 2):
        ya, yb = seg_rows(r), seg_rows(r + 1)
        for acc_off in (0, CMP_STRIDE):
            lhs = jnp.concatenate([ya + pe_ref[kv, acc_off + r:acc_off + r + 1, :],
                                   yb + pe_ref[kv, acc_off + r + 1:acc_off + r + 2, :]], axis=1).astype(bf)
            w = w1_ref[kv, acc_off + r:acc_off + r + 2].reshape(2 * d, CMP_HID)
            y = jnp.dot(lhs, w, preferred_element_type=jnp.float32)
            if acc_off == 0:
                first = first + y
            else:
                second = second + y
    hid = first + pltpu.roll(second, ncp - 1, 0)
    return jnp.dot(_gelu_tanh(hid).astype(bf), w2_ref[kv], preferred_element_type=jnp.float32)


def _compress_prompt_kernel(rows_ref, pe_ref, w1_ref, w2_ref, cos_ref, sin_ref, kc_ref, vc_ref):
    ncp, d = kc_ref.shape[0], HEAD_DIM

    def seg_rows(kv, r):
        return rows_ref[pl.ds(NSA_ROWS * r + kv, ncp, stride=NSA_ROWS * CMP_STRIDE), :]

    kc = _compress_tokens(functools.partial(seg_rows, 0), pe_ref, w1_ref, w2_ref, 0, ncp)
    kc_ref[...] = (kc * cos_ref[...] + pltpu.roll(kc, d // 2, 1) * sin_ref[...]).astype(kc_ref.dtype)
    vc_ref[...] = _compress_tokens(functools.partial(seg_rows, 1), pe_ref, w1_ref, w2_ref, 1, ncp).astype(vc_ref.dtype)


def nsa_compress_prompt(nsa_new, pe, w1, w2):
    b, t = nsa_new.shape[:2]
    d, bf = HEAD_DIM, jnp.bfloat16
    ncp = t // CMP_STRIDE
    n_cmp = (t - CMP_LEN) // CMP_STRIDE + 1
    assert ncp % 8 == 0 and n_cmp <= ncp
    cos, sin = _rope_tables(jnp.arange(n_cmp) * CMP_STRIDE + CMP_LEN - 1, ncp)
    const = lambda *shape: pl.BlockSpec(shape, lambda i: (0,) * len(shape))
    out = pl.BlockSpec((None, ncp, d), lambda i: (i, 0, 0))
    return pl.pallas_call(
        _compress_prompt_kernel,
        grid=(b,),
        in_specs=[pl.BlockSpec((None, t * NSA_ROWS, d), lambda i: (i, 0, 0)),
                  const(2, CMP_LEN, d), const(2, CMP_LEN, d, CMP_HID), const(2, CMP_HID, d), const(ncp, d), const(ncp, d)],
        out_specs=[out, out],
        out_shape=[jax.ShapeDtypeStruct((b, ncp, d), bf)] * 2,
        compiler_params=pltpu.CompilerParams(dimension_semantics=("parallel",), vmem_limit_bytes=VMEM_LIMIT),
        name="nsa_compress",
    )(nsa_new.reshape(b, t * NSA_ROWS, d), pe, w1.reshape(2, CMP_LEN, d, CMP_HID).astype(bf), w2.astype(bf), cos, sin)


def _gated_store(o_ref, g_ref, branches, nh, d):
    gate = 1.0 / (1.0 + jnp.exp(-g_ref[...]))
    for h in range(nh):
        o_ref[:, h * d:(h + 1) * d] = sum(gate[:, 3 * h + i:3 * h + i + 1] * br[h] for i, br in enumerate(branches))


def _nsa_sample_kernel(pt_ref, q_ref, g_ref, new_ref, winc_ref, winn_ref, pe_ref, w1_ref, w2_ref, cos_ref, sin_ref,
                       share_ref, *rest, past, n_cmp, n_blk):
    del pt_ref
    all_pages, o_ref = rest[:-1], rest[-1]
    n_seq = q_ref.shape[0]
    n_pages = len(all_pages) // n_seq
    ncp = share_ref.shape[1]
    seg_per_page = PAGE_SIZE // CMP_STRIDE
    assert n_pages * seg_per_page == ncp == LANE and CMP_LEN == 2 * CMP_STRIDE

    def seg_rows(kv, r):
        return jnp.concatenate([p[pl.ds(NSA_ROWS * r + kv, seg_per_page, stride=NSA_ROWS * CMP_STRIDE), :]
                                for p in all_pages], axis=0)

    kc_all = _compress_tokens(functools.partial(seg_rows, 0), pe_ref, w1_ref, w2_ref, 0, n_seq * ncp)
    vc_all = _compress_tokens(functools.partial(seg_rows, 1), pe_ref, w1_ref, w2_ref, 1, n_seq * ncp)
    for s in range(n_seq):
        _nsa_sample_one(q_ref.at[s], g_ref.at[s], new_ref.at[s], winc_ref.at[s], winn_ref.at[s], cos_ref, sin_ref,
                        share_ref, all_pages[s * n_pages:(s + 1) * n_pages], o_ref.at[s],
                        kc_all[s * ncp:(s + 1) * ncp], vc_all[s * ncp:(s + 1) * ncp], past, n_cmp, n_blk)


def _nsa_sample_one(q_ref, g_ref, new_ref, winc_ref, winn_ref, cos_ref, sin_ref, share_ref, pages, o_ref,
                    kc, vc, past, n_cmp, n_blk):
    nh, d, pg = NSA_HEADS, HEAD_DIM, PAGE_SIZE
    tq = q_ref.shape[0]
    nbp, ncp = share_ref.shape
    bf = jnp.bfloat16
    qpos_c = past + lax.broadcasted_iota(jnp.int32, (tq, 1), 0)
    qpos_r = past + lax.broadcasted_iota(jnp.int32, (1, LANE), 1)
    lane = lax.broadcasted_iota(jnp.int32, (1, LANE), 1)
    kc = (kc * cos_ref[...] + pltpu.roll(kc, d // 2, 1) * sin_ref[...]).astype(bf)
    vc = vc.astype(bf)

    q = q_ref[...].astype(jnp.float32)
    qh = jnp.concatenate([q[:, h * d:(h + 1) * d] for h in range(nh)], axis=0).astype(bf)

    mask_c = ((lane * CMP_STRIDE + (CMP_LEN - 1) <= qpos_c) & (lane < n_cmp)).astype(jnp.float32)
    s = lax.dot_general(qh, kc, _NT, preferred_element_type=jnp.float32).reshape(nh, tq, ncp)
    sm = s + ((mask_c - 1.0) * (-NEG))[None]
    e = jnp.exp(sm - jnp.max(sm, axis=-1, keepdims=True)) * mask_c[None]
    p_cmp = e / jnp.maximum(jnp.sum(e, axis=-1, keepdims=True), 1e-30)
    o_cmp = jnp.dot(p_cmp.reshape(nh * tq, ncp).astype(bf), vc, preferred_element_type=jnp.float32).reshape(nh, tq, d)

    sel = _select_blocks(_pad_rows(jnp.sum(p_cmp, axis=0), LANE), share_ref[...], qpos_r, n_blk, LANE)

    def slc_mask(key0):
        kblk = key0 // SEL_BLOCK + jnp.right_shift(lax.broadcasted_iota(jnp.int32, (nbp, LANE), 1), 6)
        expand = (kblk == lax.broadcasted_iota(jnp.int32, (nbp, LANE), 0)).astype(bf)
        chosen = jnp.dot(sel, expand, preferred_element_type=jnp.float32)[:tq]
        return chosen * (key0 + lane <= qpos_c).astype(jnp.float32)

    def component(ref, comp, n_comp, row0, rows):
        return ref[pl.ds(n_comp * row0 + comp, rows, stride=n_comp), :]

    tiles = [(component(p, 2, NSA_ROWS, 0, pg).astype(bf), component(p, 3, NSA_ROWS, 0, pg).astype(bf), slc_mask(j * pg))
             for j, p in enumerate(pages)]
    tiles.append((_pad_rows(component(new_ref, 2, NSA_ROWS, 0, tq), LANE).astype(bf),
                  _pad_rows(component(new_ref, 3, NSA_ROWS, 0, tq), LANE).astype(bf), slc_mask(past)))
    o_slc = _softmax_flat(qh, tiles, nh, tq)

    def win_mask(key0):
        dlt = qpos_c - (key0 + lane)
        return ((dlt >= 0) & (dlt < WINDOW)).astype(jnp.float32)

    n_wc = winc_ref.shape[0] // 2
    tiles = [(component(winc_ref, 0, 2, j * LANE, LANE).astype(bf), component(winc_ref, 1, 2, j * LANE, LANE).astype(bf),
              win_mask(past - n_wc + j * LANE)) for j in range(n_wc // LANE)]
    tiles.append((_pad_rows(component(winn_ref, 0, 2, 0, tq), LANE).astype(bf),
                  _pad_rows(component(winn_ref, 1, 2, 0, tq), LANE).astype(bf), win_mask(past)))
    o_win = _softmax_flat(qh, tiles, nh, tq)

    _gated_store(o_ref, g_ref, (o_cmp, o_slc, o_win), nh, d)


def nsa_sample_attention(layer, q, gates, nsa_new, win_new, pool, win_cache, page_table, pe, w1, w2):
    b, t, _ = q.shape
    wlen = win_cache.shape[2]
    nsa_new = nsa_new.reshape(b, t * NSA_ROWS, HEAD_DIM)
    win_new = win_new.reshape(b, t * 2, HEAD_DIM)
    pool = pool.reshape(pool.shape[0], pool.shape[1], PAGE_SIZE * NSA_ROWS, HEAD_DIM)
    win_cache = win_cache.reshape(win_cache.shape[0], b, wlen * 2, HEAD_DIM)
    d, n_pages = HEAD_DIM, page_table.shape[1]
    past = n_pages * PAGE_SIZE
    n_keys = past + t
    n_cmp = (n_keys - CMP_LEN) // CMP_STRIDE + 1
    n_blk = -(-n_keys // SEL_BLOCK)
    ncp = -(-n_cmp // LANE) * LANE
    nbp = -(-n_blk // 16) * 16
    assert (n_cmp + 1) * CMP_STRIDE <= past, "compressed tokens must come from cached rows only"
    cos, sin = _rope_tables(jnp.arange(n_cmp) * CMP_STRIDE + CMP_LEN - 1, ncp)
    bf = jnp.bfloat16
    ns = NSA_SEQ
    assert b % ns == 0
    per_seq = lambda *tail: pl.BlockSpec((ns,) + tail, lambda i, pt: (i,) + (0,) * len(tail))
    const = lambda *shape: pl.BlockSpec(shape, lambda i, pt: (0,) * len(shape))
    page = lambda s, j: pl.BlockSpec((None, None, PAGE_SIZE * NSA_ROWS, d),
                                     lambda i, pt: (layer, pt[i * ns + s, j], 0, 0))
    grid_spec = pltpu.PrefetchScalarGridSpec(
        num_scalar_prefetch=1, grid=(b // ns,),
        in_specs=[per_seq(t, NSA_HEADS * d), per_seq(t, 3 * NSA_HEADS), per_seq(t * NSA_ROWS, d),
                  pl.BlockSpec((None, ns, wlen * 2, d), lambda i, pt: (layer, i, 0, 0)),
                  per_seq(t * 2, d),
                  const(2, CMP_LEN, d), const(2, CMP_LEN, d, CMP_HID), const(2, CMP_HID, d),
                  const(ncp, d), const(ncp, d), const(nbp, ncp)]
                 + [page(s, j) for s in range(ns) for j in range(n_pages)],
        out_specs=per_seq(t, NSA_HEADS * d))
    return pl.pallas_call(
        functools.partial(_nsa_sample_kernel, past=past, n_cmp=n_cmp, n_blk=n_blk),
        grid_spec=grid_spec,
        out_shape=jax.ShapeDtypeStruct((b, t, NSA_HEADS * d), jnp.float32),
        compiler_params=pltpu.CompilerParams(dimension_semantics=("parallel",), vmem_limit_bytes=VMEM_LIMIT),
        name="nsa_sample",
    )(page_table, q, gates, nsa_new, win_cache, win_new, pe, w1.reshape(2, CMP_LEN, d, CMP_HID).astype(bf),
      w2.astype(bf), cos, sin, _share_matrix_t(n_cmp, n_blk, ncp, nbp), *([pool] * (ns * n_pages)))


DSA_TQ = 128
DSA_TK = 256
INT_MIN = -2 ** 31


def _code_to_float(code):
    return pltpu.bitcast(jnp.where(code >= 0, code, code ^ 0x7FFFFFFF), jnp.float32)


def _dsa_prompt_kernel(q_ref, qi_ref, wt_ref, k_ref, v_ref, ki_ref, tri_ref, o_ref,
                       key_ref, sel_ref, m_ref, l_ref, acc_ref, *, n_keys, topk):
    nh, tq, d, tk = DSA_HEADS, DSA_TQ, HEAD_DIM, DSA_TK
    start = pl.program_id(1) * tq
    qpos_r = start + lax.broadcasted_iota(jnp.int32, (1, tq), 1)
    n_kt = jnp.minimum((start + tq - 1) // tk + 1, n_keys // tk)

    wt = wt_ref[0] * (IDX_HEADS ** -0.5 * IDX_DIM ** -0.5)

    def score_body(kt, carry):
        off = pl.multiple_of(kt * tk, tk)
        ki = ki_ref[0, pl.ds(off, tk), :]
        acc = jnp.zeros((tk, tq), jnp.float32)
        for h in range(0, IDX_HEADS, 2):
            pair = qi_ref[0, h:h + 2].reshape(2 * tq, IDX_DIM)
            dots = lax.dot_general(ki, pair, _NT, preferred_element_type=jnp.float32)
            acc = (acc + jnp.maximum(dots[:, :tq], 0.0) * wt[h:h + 1, :]
                   + jnp.maximum(dots[:, tq:], 0.0) * wt[h + 1:h + 2, :])
        kpos_c = off + lax.broadcasted_iota(jnp.int32, (tk, 1), 0)
        key_ref[kt] = jnp.where(kpos_c <= qpos_r, acc, NEG)
        return carry

    lax.fori_loop(0, n_kt, score_body, 0)

    def count(pred):
        def body(kt, part):
            hit = pred(key_ref[kt]).astype(jnp.int32)
            return part + jnp.sum(hit.reshape(tk // 8, 8, tq), axis=0)
        part = lax.fori_loop(0, n_kt, body, jnp.zeros((8, tq), jnp.int32))
        return jnp.sum(part, axis=0, keepdims=True)

    code = jnp.where(count(lambda sc: sc >= 0.0) >= topk, 0, INT_MIN).astype(jnp.int32)

    def bit_body(i, c):
        cand = c | jnp.left_shift(jnp.int32(1), 30 - i)
        cand_f = _code_to_float(cand)
        return jnp.where(count(lambda sc: sc >= cand_f) >= topk, cand, c)

    code = lax.fori_loop(0, 31, bit_body, code)
    thr = _code_to_float(code)
    take_all = code == INT_MIN
    room = (topk - count(lambda sc: sc > thr)).astype(jnp.float32)

    def sel_body(kt, seen):
        kk = key_ref[kt]
        eq = (kk == thr).astype(jnp.float32)
        prefix = jnp.dot(tri_ref[...], eq.astype(jnp.bfloat16), preferred_element_type=jnp.float32)
        keep = (kk > thr) | ((eq > 0.0) & (prefix + seen <= room)) | take_all
        kpos_c = kt * tk + lax.broadcasted_iota(jnp.int32, (tk, 1), 0)
        sel_ref[kt] = (keep & (kpos_c <= qpos_r)).astype(jnp.bfloat16)
        return seen + jnp.sum(eq, axis=0, keepdims=True)

    lax.fori_loop(0, n_kt, sel_body, jnp.zeros((1, tq), jnp.float32))

    eye = (lax.broadcasted_iota(jnp.int32, (tq, tq), 0)
           == lax.broadcasted_iota(jnp.int32, (tq, tq), 1)).astype(jnp.bfloat16)

    def sel_mask(kt):
        return lax.dot_general(eye, sel_ref[kt], _NT, preferred_element_type=jnp.float32)

    qh = jnp.concatenate([q_ref[0, :, h * d:(h + 1) * d] for h in range(nh)], axis=0)
    o = _softmax_tiles(qh, k_ref, v_ref, 0, n_kt, tk, sel_mask, m_ref, l_ref, acc_ref, nh, tq)
    for h in range(nh):
        o_ref[0, :, h * d:(h + 1) * d] = o[h]


def dsa_prompt_attention(q, q_idx, w_idx, k, v, k_idx):
    b, t, _ = q.shape
    d = HEAD_DIM
    topk = min(DSA_TOPK, t // 4)
    assert t % DSA_TK == 0 and t % DSA_TQ == 0
    bf = jnp.bfloat16
    tri = jnp.asarray(np.tril(np.ones((DSA_TK, DSA_TK), np.float32)), bf)
    whole = lambda n, w: pl.BlockSpec((1, n, w), lambda bi, qi: (bi, 0, 0))
    return pl.pallas_call(
        functools.partial(_dsa_prompt_kernel, n_keys=t, topk=topk),
        grid=(b, t // DSA_TQ),
        in_specs=[pl.BlockSpec((1, DSA_TQ, DSA_HEADS * d), lambda bi, qi: (bi, qi, 0)),
                  pl.BlockSpec((1, IDX_HEADS, DSA_TQ, IDX_DIM), lambda bi, qi: (bi, 0, qi, 0)),
                  pl.BlockSpec((1, IDX_HEADS, DSA_TQ), lambda bi, qi: (bi, 0, qi)),
                  whole(t, d), whole(t, d), whole(t, IDX_DIM),
                  pl.BlockSpec((DSA_TK, DSA_TK), lambda bi, qi: (0, 0))],
        out_specs=pl.BlockSpec((1, DSA_TQ, DSA_HEADS * d), lambda bi, qi: (bi, qi, 0)),
        out_shape=jax.ShapeDtypeStruct((b, t, DSA_HEADS * d), jnp.float32),
        scratch_shapes=[pltpu.VMEM((t // DSA_TK, DSA_TK, DSA_TQ), jnp.float32),
                        pltpu.VMEM((t // DSA_TK, DSA_TK, DSA_TQ), bf),
                        pltpu.VMEM((DSA_HEADS, DSA_TQ, LANE), jnp.float32),
                        pltpu.VMEM((DSA_HEADS, DSA_TQ, LANE), jnp.float32),
                        pltpu.VMEM((DSA_HEADS, DSA_TQ, d), jnp.float32)],
        compiler_params=pltpu.CompilerParams(
            dimension_semantics=("parallel", "arbitrary"), vmem_limit_bytes=VMEM_LIMIT),
        name="dsa_prompt",
    )(q, q_idx.astype(bf), jnp.swapaxes(w_idx, 1, 2),
      k.astype(bf), v.astype(bf), k_idx.astype(bf), tri)


SEARCH_BITS = 3


def _dsa_sample_kernel(pt_ref, q_ref, qi_ref, w_ref, new_ref, inew_ref, tri_ref, *rest, past, topk):
    del pt_ref
    n_pages = (len(rest) - 1) // 2
    kv_pages, idx_pages, o_ref = rest[:n_pages], rest[n_pages:2 * n_pages], rest[-1]
    nh, d, pg = DSA_HEADS, HEAD_DIM, PAGE_SIZE
    tq = q_ref.shape[0]
    bf = jnp.bfloat16
    qpos_c = past + lax.broadcasted_iota(jnp.int32, (tq, 1), 0)
    lane = lax.broadcasted_iota(jnp.int32, (1, LANE), 1)

    qi = qi_ref[...]
    wb = jnp.broadcast_to(w_ref[...] * (IDX_HEADS ** -0.5 * IDX_DIM ** -0.5), (IDX_HEADS * tq, LANE))

    def key_tile(ki_t, key0):
        dots = jnp.dot(qi, ki_t, preferred_element_type=jnp.float32)
        score = jnp.sum((jnp.maximum(dots, 0.0) * wb).reshape(IDX_HEADS, tq, LANE), axis=0)
        return jnp.where(key0 + lane <= qpos_c, score, NEG)

    keys = [key_tile(p[...].astype(bf), j * pg) for j, p in enumerate(idx_pages)]
    keys.append(key_tile(inew_ref[...].astype(bf), past))

    def count(pred):
        hits = pred(keys[0]).astype(jnp.int32)
        for kk in keys[1:]:
            hits = hits + pred(kk).astype(jnp.int32)
        return jnp.sum(hits, axis=-1, keepdims=True)

    def at_least(code):
        cand = _code_to_float(code)
        return (count(lambda sc: sc >= cand) >= topk).astype(jnp.int32)

    code = jnp.where(count(lambda sc: sc >= 0.0) >= topk, 0, INT_MIN).astype(jnp.int32)
    n_steps, last_bits = divmod(31, SEARCH_BITS)

    def radix_body(i, c):
        shift = 31 - SEARCH_BITS * (i + 1)
        digit = sum(at_least(c | jnp.left_shift(jnp.int32(j), shift)) for j in range(1, 2 ** SEARCH_BITS))
        return c | jnp.left_shift(digit, shift)

    code = lax.fori_loop(0, n_steps, radix_body, code)
    if last_bits:
        code = code | sum(at_least(code | j) for j in range(1, 2 ** last_bits))
    thr = _code_to_float(code)
    take_all = code == INT_MIN
    room = (topk - count(lambda sc: sc > thr)).astype(jnp.float32)
    eqs = [(kk == thr).astype(jnp.float32) for kk in keys]
    totals = [jnp.sum(e, axis=-1, keepdims=True) for e in eqs]
    stacked = jnp.concatenate(eqs + [jnp.zeros((-len(eqs) * tq % MIN_MXU_ROWS, LANE), jnp.float32)] * (
        1 if len(eqs) * tq % MIN_MXU_ROWS else 0), axis=0).astype(bf)
    prefixes = jnp.dot(stacked, tri_ref[...], preferred_element_type=jnp.float32)
    seen = jnp.zeros((tq, 1), jnp.float32)
    masks = []
    for j, (kk, e) in enumerate(zip(keys, eqs)):
        prefix = prefixes[j * tq:(j + 1) * tq]
        keep = (kk > thr) | ((e > 0.0) & (prefix + seen <= room)) | take_all
        key0 = j * pg if j < n_pages else past
        masks.append((keep & (key0 + lane <= qpos_c)).astype(jnp.float32))
        seen = seen + totals[j]

    def component(ref, comp, rows):
        return ref[pl.ds(comp, rows, stride=2), :]

    tiles = [(component(p, 0, pg).astype(bf), component(p, 1, pg).astype(bf), masks[j]) for j, p in enumerate(kv_pages)]
    tiles.append((_pad_rows(component(new_ref, 0, tq), LANE).astype(bf),
                  _pad_rows(component(new_ref, 1, tq), LANE).astype(bf), masks[n_pages]))
    q = q_ref[...].astype(jnp.float32)
    qh = jnp.concatenate([q[:, h * d:(h + 1) * d] for h in range(nh)], axis=0).astype(bf)
    o = _softmax_flat(qh, tiles, nh, tq)
    for h in range(nh):
        o_ref[:, h * d:(h + 1) * d] = o[h]


def dsa_sample_attention(layer, q, q_idx, w_idx, dsa_new, idx_new, pool, idx_pool, page_table):
    b, t, _ = q.shape
    d, n_pages = HEAD_DIM, page_table.shape[1]
    past = n_pages * PAGE_SIZE
    topk = min(DSA_TOPK, (past + t) // 4)
    pool = pool.reshape(pool.shape[0], pool.shape[1], PAGE_SIZE * 2, d)
    idx_pool_t = jnp.swapaxes(idx_pool, 2, 3)
    idx_new_t = jnp.pad(jnp.swapaxes(idx_new, 1, 2), ((0, 0), (0, 0), (0, LANE - t)))
    tri = jnp.asarray(np.triu(np.ones((LANE, LANE), np.float32)), jnp.bfloat16)
    per_seq = lambda *tail: pl.BlockSpec((None,) + tail, lambda i, pt: (i,) + (0,) * len(tail))
    kv_page = lambda j: pl.BlockSpec((None, None, PAGE_SIZE * 2, d), lambda i, pt: (layer, pt[i, j], 0, 0))
    idx_page = lambda j: pl.BlockSpec((None, None, IDX_DIM, PAGE_SIZE), lambda i, pt: (layer, pt[i, j], 0, 0))
    grid_spec = pltpu.PrefetchScalarGridSpec(
        num_scalar_prefetch=1, grid=(b,),
        in_specs=[per_seq(t, DSA_HEADS * d), per_seq(IDX_HEADS * t, IDX_DIM), per_seq(IDX_HEADS * t, 1),
                  per_seq(t * 2, d), per_seq(IDX_DIM, LANE), pl.BlockSpec((LANE, LANE), lambda i, pt: (0, 0))]
                 + [kv_page(j) for j in range(n_pages)] + [idx_page(j) for j in range(n_pages)],
        out_specs=per_seq(t, DSA_HEADS * d))
    return pl.pallas_call(
        functools.partial(_dsa_sample_kernel, past=past, topk=topk),
        grid_spec=grid_spec,
        out_shape=jax.ShapeDtypeStruct((b, t, DSA_HEADS * d), jnp.float32),
        compiler_params=pltpu.CompilerParams(dimension_semantics=("parallel",), vmem_limit_bytes=VMEM_LIMIT),
        name="dsa_sample",
    )(page_table, q, q_idx.reshape(b, IDX_HEADS * t, IDX_DIM).astype(jnp.bfloat16),
      jnp.swapaxes(w_idx, 1, 2).reshape(b, IDX_HEADS * t, 1), dsa_new.reshape(b, t * 2, d), idx_new_t, tri,
      *([pool] * n_pages), *([idx_pool_t] * n_pages))


GDN_GROUP = 256
GDN_SUB = 16
GDN_HB = 4
CONV_PAD = 8


def _hp_dot(a, b):
    bf = jnp.bfloat16
    ah, bh = a.astype(bf), b.astype(bf)
    al, bl = (a - ah.astype(jnp.float32)).astype(bf), (b - bh.astype(jnp.float32)).astype(bf)
    dot = functools.partial(jnp.dot, preferred_element_type=jnp.float32)
    return dot(ah, bh) + dot(ah, bl) + dot(al, bh)


def _unit_lower_inverse(a, row, col):
    assert GDN_SUB == 16 and GDN_CHUNK == 64
    n = range(len(a))
    eye = (row == col).astype(jnp.float32)
    sub = jnp.right_shift(row, 4) == jnp.right_shift(col, 4)
    a16 = [jnp.where(sub, a[i], 0.0) for i in n]
    t16 = [eye - a16[i] for i in n]
    power = a16
    for _ in range(3):
        power = [_hp_dot(power[i], power[i]) for i in n]
        t16 = [t16[i] + _hp_dot(t16[i], power[i]) for i in n]
    b = [_hp_dot(t16[i], a[i] - a16[i]) for i in n]
    b2 = [_hp_dot(b[i], b[i]) for i in n]
    imb = [eye - b[i] for i in n]
    left = [imb[i] + _hp_dot(imb[i], b2[i]) for i in n]
    return [_hp_dot(left[i], t16[i]) for i in n]


def _gdn_prompt_kernel(xq_ref, xk_ref, xv_ref, pq_ref, pk_ref, pv_ref, a_ref, b_ref, z_ref, cwq_ref, cwk_ref, cwv_ref,
                       alog_ref, dtb_ref, ng_ref, s0_ref, o_ref, s_ref, hist_ref):
    f32, bf = jnp.float32, jnp.bfloat16
    d, g, c = HEAD_DIM, GDN_GROUP, GDN_CHUNK
    heads = range(GDN_HB)
    lanes = [slice(h * d, (h + 1) * d) for h in heads]
    row = lax.broadcasted_iota(jnp.int32, (g, g), 0)
    col = lax.broadcasted_iota(jnp.int32, (g, g), 1)
    same = jnp.right_shift(row, 6) == jnp.right_shift(col, 6)
    incl = same & (row >= col)
    strict = same & (row > col)
    eye = (row == col).astype(f32)
    tril_b, same_b = incl.astype(bf), same.astype(bf)

    @pl.when(pl.program_id(2) == 0)
    def _():
        s_ref[...] = s0_ref[...]
        for part, p_ref in enumerate((pq_ref, pk_ref, pv_ref)):
            hist_ref[part] = p_ref[...]

    pre = a_ref[...] + dtb_ref[...]
    gate = -jnp.exp(alog_ref[...]) * (jnp.maximum(pre, 0.0) + jnp.log(1.0 + jnp.exp(-jnp.abs(pre))))
    parts = _split3_bf16(gate)
    gcum_all = sum(jnp.dot(tril_b, p, preferred_element_type=f32) for p in parts)
    gtot_all = sum(jnp.dot(same_b, p, preferred_element_type=f32) for p in parts)
    beta_all = 1.0 / (1.0 + jnp.exp(-b_ref[...]))

    def conv(part, x_ref, w_ref, h):
        win = jnp.concatenate([hist_ref[part, :, lanes[h]], x_ref[:, lanes[h]]], axis=0)
        y = sum(pltpu.roll(win, g + CONV_PAD - (CONV_PAD - GDN_CONV + 1 + i), 0)[:g] * w_ref[i:i + 1, lanes[h]]
                for i in range(GDN_CONV))
        return y / (1.0 + jnp.exp(-y))

    q = [conv(0, xq_ref, cwq_ref, h) for h in heads]
    k = [conv(1, xk_ref, cwk_ref, h) for h in heads]
    v = [conv(2, xv_ref, cwv_ref, h) for h in heads]
    for part, x_ref in enumerate((xq_ref, xk_ref, xv_ref)):
        hist_ref[part] = x_ref[g - CONV_PAD:g, :]
    q = [q[h] * lax.rsqrt(jnp.sum(q[h] * q[h], axis=-1, keepdims=True) + EPS) * (d ** -0.5) for h in heads]
    k = [k[h] * lax.rsqrt(jnp.sum(k[h] * k[h], axis=-1, keepdims=True) + EPS) for h in heads]
    gc = [gcum_all[:, h:h + 1] for h in heads]
    gl = [gtot_all[:, h:h + 1] for h in heads]
    beta = [beta_all[:, h:h + 1] for h in heads]
    g_i = [jnp.broadcast_to(gc[h], (g, g)) for h in heads]
    g_j = [jnp.sum(g_i[h] * eye, axis=0, keepdims=True) for h in heads]
    decay = [jnp.where(incl, jnp.exp(jnp.where(incl, g_i[h] - g_j[h], 0.0)), 0.0) for h in heads]
    kb = [k[h] * beta[h] for h in heads]
    k16 = [k[h].astype(bf) for h in heads]
    eg = [jnp.exp(gc[h]) for h in heads]
    a_mat = [jnp.where(strict, lax.dot_general(kb[h].astype(bf), k16[h], _NT, preferred_element_type=f32) * decay[h], 0.0)
             for h in heads]
    qk = [(lax.dot_general(q[h].astype(bf), k16[h], _NT, preferred_element_type=f32) * decay[h]).astype(bf)
          for h in heads]
    rhs = [jnp.concatenate([v[h] * beta[h], kb[h] * eg[h]], axis=1) for h in heads]
    inv = _unit_lower_inverse(a_mat, row, col)
    sol = [_hp_dot(inv[h], rhs[h]) for h in heads]
    u = [sol[h][:, :d] for h in heads]
    w16 = [sol[h][:, d:].astype(bf) for h in heads]
    q_dec = [(q[h] * eg[h]).astype(bf) for h in heads]
    k_dec = [(k[h] * jnp.exp(gl[h] - gc[h])).astype(bf) for h in heads]
    chunk_decay = [jnp.exp(gl[h]) for h in heads]
    for ci in range(g // c):
        rows = slice(ci * c, (ci + 1) * c)
        state = [s_ref[h] for h in heads]
        s16 = [state[h].astype(bf) for h in heads]
        nv16 = [(u[h][rows] - jnp.dot(w16[h][rows], s16[h], preferred_element_type=f32)).astype(bf) for h in heads]
        pad = lambda n: [jnp.zeros((n, d), bf)] if n else []
        placed = [jnp.concatenate(pad(ci * c) + [nv16[h]] + pad(g - (ci + 1) * c), axis=0) for h in heads]
        out = [jnp.dot(q_dec[h][rows], s16[h], preferred_element_type=f32)
               + jnp.dot(qk[h][rows], placed[h], preferred_element_type=f32) for h in heads]
        for h in heads:
            s_ref[h] = state[h] * chunk_decay[h][ci * c:ci * c + 1, :] + lax.dot_general(
                k_dec[h][rows], nv16[h], (((0,), (0,)), ((), ())), preferred_element_type=f32)
        for h in heads:
            zz = z_ref[rows, lanes[h]]
            normed = out[h] * lax.rsqrt(jnp.mean(out[h] * out[h], axis=-1, keepdims=True) + EPS) * ng_ref[...]
            o_ref[rows, lanes[h]] = normed * (zz / (1.0 + jnp.exp(-zz)))


def gdn_prompt(src, qkv_off, z_off, a, beta_logit, conv_prev, s0, conv_w, a_log, dt_bias, norm_g):
    b, t, _ = src.shape
    d, hb = HEAD_DIM, GDN_HB
    ng = GDN_HEADS // hb
    assert t % GDN_GROUP == 0 and GDN_HEADS % hb == 0 and qkv_off % (hb * d) == 0 and z_off % (hb * d) == 0
    qb, zb = qkv_off // (hb * d), z_off // (hb * d)
    prev = jnp.pad(conv_prev.astype(src.dtype), ((0, 0), (CONV_PAD - GDN_CONV + 1, 0), (0, 0)))
    by_group = lambda x: jnp.swapaxes(x.reshape(b, t, ng, hb), 1, 2)
    g = GDN_GROUP
    cols = lambda part: pl.BlockSpec((None, g, hb * d), lambda bi, gi, ti: (bi, ti, qb + part * ng + gi))
    hist = lambda part: pl.BlockSpec((None, CONV_PAD, hb * d), lambda bi, gi, ti: (bi, 0, part * ng + gi))
    cw = lambda part: pl.BlockSpec((GDN_CONV, hb * d), lambda bi, gi, ti: (0, part * ng + gi))
    tok = pl.BlockSpec((None, None, g, hb), lambda bi, gi, ti: (bi, gi, ti, 0))
    head_const = pl.BlockSpec((None, 1, hb), lambda bi, gi, ti: (gi, 0, 0))
    state = pl.BlockSpec((None, hb, d, d), lambda bi, gi, ti: (bi, gi, 0, 0))
    return pl.pallas_call(
        _gdn_prompt_kernel,
        grid=(b, ng, t // g),
        in_specs=[cols(0), cols(1), cols(2), hist(0), hist(1), hist(2), tok, tok,
                  pl.BlockSpec((None, g, hb * d), lambda bi, gi, ti: (bi, ti, zb + gi)),
                  cw(0), cw(1), cw(2), head_const, head_const,
                  pl.BlockSpec((1, d), lambda bi, gi, ti: (0, 0)), state],
        out_specs=[pl.BlockSpec((None, g, hb * d), lambda bi, gi, ti: (bi, ti, gi)), state],
        out_shape=[jax.ShapeDtypeStruct((b, t, W_C), jnp.float32),
                   jax.ShapeDtypeStruct((b, GDN_HEADS, d, d), jnp.float32)],
        scratch_shapes=[pltpu.VMEM((3, CONV_PAD, hb * d), jnp.float32)],
        compiler_params=pltpu.CompilerParams(
            dimension_semantics=("parallel", "parallel", "arbitrary"), vmem_limit_bytes=VMEM_LIMIT),
        name="gdn_prompt",
    )(src, src, src, prev, prev, prev, by_group(a), by_group(beta_logit), src, conv_w, conv_w, conv_w,
      a_log.reshape(ng, 1, hb).astype(jnp.float32), dt_bias.reshape(ng, 1, hb).astype(jnp.float32),
      norm_g.reshape(1, d).astype(jnp.float32), s0)


def _gdn_decode_kernel(x_ref, prev_ref, cw_ref, a_ref, b_ref, z_ref, alog_ref, dtb_ref, ng_ref, s0_ref, o_ref, s_ref):
    f32, bf = jnp.float32, jnp.bfloat16
    d, t = HEAD_DIM, x_ref.shape[0]
    heads = range(GDN_HEADS)
    rows = max(t, MIN_MXU_ROWS)
    mx = lambda x: _pad_rows(x, rows).astype(bf)
    win = jnp.concatenate([prev_ref[...], x_ref[...]], axis=0)
    n = win.shape[0]
    y = sum(pltpu.roll(win, n - (CONV_PAD - GDN_CONV + 1 + i), 0)[:t] * cw_ref[i:i + 1, :] for i in range(GDN_CONV))
    y = y / (1.0 + jnp.exp(-y))
    pre = a_ref[...] + dtb_ref[...]
    gate = -jnp.exp(alog_ref[...]) * (jnp.maximum(pre, 0.0) + jnp.log(1.0 + jnp.exp(-jnp.abs(pre))))
    beta_all = 1.0 / (1.0 + jnp.exp(-b_ref[...]))
    run, gcum_rows = None, []
    for ti in range(t):
        run = gate[ti:ti + 1, :] if run is None else run + gate[ti:ti + 1, :]
        gcum_rows.append(run)
    gcum_all = jnp.concatenate(gcum_rows, axis=0)
    ri = lax.broadcasted_iota(jnp.int32, (t, t), 0)
    ci = lax.broadcasted_iota(jnp.int32, (t, t), 1)
    eye = (ri == ci).astype(f32)

    q = [y[:, h * d:(h + 1) * d] for h in heads]
    k = [y[:, W_C + h * d:W_C + (h + 1) * d] for h in heads]
    v = [y[:, 2 * W_C + h * d:2 * W_C + (h + 1) * d] for h in heads]
    q = [q[h] * lax.rsqrt(jnp.sum(q[h] * q[h], axis=-1, keepdims=True) + EPS) * (d ** -0.5) for h in heads]
    k = [k[h] * lax.rsqrt(jnp.sum(k[h] * k[h], axis=-1, keepdims=True) + EPS) for h in heads]
    gc = [gcum_all[:, h:h + 1] for h in heads]
    gr = [jnp.sum(jnp.broadcast_to(gc[h], (t, t)) * eye, axis=0, keepdims=True) for h in heads]
    beta = [beta_all[:, h:h + 1] for h in heads]
    kb = [k[h] * beta[h] for h in heads]
    k16 = [mx(k[h]) for h in heads]
    dec = [jnp.where(ri >= ci, jnp.exp(jnp.where(ri >= ci, gc[h] - gr[h], 0.0)), 0.0) for h in heads]
    dec_t = [jnp.where(ci >= ri, jnp.exp(jnp.where(ci >= ri, gr[h] - gc[h], 0.0)), 0.0) for h in heads]
    a_t = [jnp.where(ci > ri, lax.dot_general(k16[h], mx(kb[h]), _NT, preferred_element_type=f32)[:t, :t] * dec_t[h], 0.0)
           for h in heads]
    qk = [lax.dot_general(mx(q[h]), k16[h], _NT, preferred_element_type=f32)[:t, :t] * dec[h] for h in heads]
    eg = [jnp.exp(gc[h]) for h in heads]
    sol = [jnp.concatenate([v[h] * beta[h], kb[h] * eg[h]], axis=1) for h in heads]
    row = lax.broadcasted_iota(jnp.int32, (t, 1), 0)
    for i in range(1, t):
        upd = [sol[h][i:i + 1, :] - jnp.sum(a_t[h][:, i:i + 1] * sol[h], axis=0, keepdims=True) for h in heads]
        sol = [jnp.where(row == i, upd[h], sol[h]) for h in heads]
    u = [sol[h][:, :d] for h in heads]
    glast = [gc[h][t - 1:t, :] for h in heads]
    state = [s0_ref[h] for h in heads]
    s16 = [state[h].astype(bf) for h in heads]
    both = [jnp.dot(jnp.concatenate([mx(sol[h][:, d:]), mx(q[h] * eg[h])], axis=0), s16[h], preferred_element_type=f32)
            for h in heads]
    nv16 = [mx(u[h] - both[h][:t]) for h in heads]
    qk16 = [_pad_rows(jnp.concatenate([qk[h], jnp.zeros((t, rows - t), f32)], axis=1), rows).astype(bf) for h in heads]
    out = [both[h][rows:rows + t] + jnp.dot(qk16[h], nv16[h], preferred_element_type=f32)[:t] for h in heads]
    k_dec = [mx(k[h] * jnp.exp(glast[h] - gc[h])) for h in heads]
    for h in heads:
        s_ref[h] = state[h] * jnp.exp(glast[h]) + lax.dot_general(k_dec[h], nv16[h], (((0,), (0,)), ((), ())),
                                                                  preferred_element_type=f32)
        zz = z_ref[:, h * d:(h + 1) * d]
        normed = out[h] * lax.rsqrt(jnp.mean(out[h] * out[h], axis=-1, keepdims=True) + EPS) * ng_ref[...]
        o_ref[:, h * d:(h + 1) * d] = normed * (zz / (1.0 + jnp.exp(-zz)))


def gdn_decode(layer, src, qkv_off, z_off, a, beta_logit, conv_prev, s0, conv_w, a_log, dt_bias, norm_g):
    b, t, _ = src.shape
    d, h = HEAD_DIM, GDN_HEADS
    assert qkv_off % (3 * W_C) == 0 and z_off % W_C == 0
    prev = jnp.pad(conv_prev.astype(src.dtype), ((0, 0), (CONV_PAD - GDN_CONV + 1, 0), (0, 0)))
    per_seq = lambda rows, width, blk: pl.BlockSpec((None, rows, width), lambda i: (i, 0, blk))
    const = lambda *shape: pl.BlockSpec(shape, lambda i: (0,) * len(shape))
    return pl.pallas_call(
        _gdn_decode_kernel,
        grid=(b,),
        in_specs=[per_seq(t, 3 * W_C, qkv_off // (3 * W_C)), per_seq(CONV_PAD, 3 * W_C, 0), const(GDN_CONV, 3 * W_C),
                  per_seq(t, h, 0), per_seq(t, h, 0), per_seq(t, W_C, z_off // W_C),
                  const(1, h), const(1, h), const(1, d),
                  pl.BlockSpec((None, None, h, d, d), lambda i: (layer, i, 0, 0, 0))],
        out_specs=[per_seq(t, W_C, 0), pl.BlockSpec((None, h, d, d), lambda i: (i, 0, 0, 0))],
        out_shape=[jax.ShapeDtypeStruct((b, t, W_C), jnp.float32), jax.ShapeDtypeStruct((b, h, d, d), jnp.float32)],
        compiler_params=pltpu.CompilerParams(dimension_semantics=("parallel",), vmem_limit_bytes=VMEM_LIMIT),
        name="gdn_decode",
    )(src, prev, conv_w, a, beta_logit, src, a_log.reshape(1, h).astype(jnp.float32),
      dt_bias.reshape(1, h).astype(jnp.float32), norm_g.reshape(1, d).astype(jnp.float32), s0)


MEM_TQ = 256
MIN_MXU_ROWS = 16


def _mem_attn_kernel(q_ref, kv_ref, o_ref):
    nh, d, bf = MEM_HEADS, HEAD_DIM, jnp.bfloat16
    tq = q_ref.shape[0]
    m = kv_ref.shape[0] // (2 * nh)
    rows = max(tq, MIN_MXU_ROWS)
    for h in range(nh):
        k = kv_ref[pl.ds(h, m, stride=2 * nh), :].astype(bf)
        v = kv_ref[pl.ds(nh + h, m, stride=2 * nh), :].astype(bf)
        q = _pad_rows(q_ref[:, h * d:(h + 1) * d], rows).astype(bf)
        s = lax.dot_general(q, k, _NT, preferred_element_type=jnp.float32) * (d ** -0.5)
        e = jnp.exp(s - jnp.max(s, axis=-1, keepdims=True))
        p = e / jnp.sum(e, axis=-1, keepdims=True)
        o_ref[:, h * d:(h + 1) * d] = jnp.dot(p.astype(bf), v, preferred_element_type=jnp.float32)[:tq]


def mem_attention_core(q, mem_kv, layer):
    b, t, w = q.shape
    m = mem_kv.shape[2]
    tq = min(MEM_TQ, t)
    assert t % tq == 0
    kv = mem_kv.reshape(mem_kv.shape[0], b, m * 2 * MEM_HEADS, HEAD_DIM)
    return pl.pallas_call(
        _mem_attn_kernel,
        grid=(b, t // tq),
        in_specs=[pl.BlockSpec((None, tq, w), lambda bi, qi: (bi, qi, 0)),
                  pl.BlockSpec((None, None, m * 2 * MEM_HEADS, HEAD_DIM), lambda bi, qi: (layer, bi, 0, 0))],
        out_specs=pl.BlockSpec((None, tq, w), lambda bi, qi: (bi, qi, 0)),
        out_shape=jax.ShapeDtypeStruct((b, t, w), jnp.float32),
        compiler_params=pltpu.CompilerParams(
            dimension_semantics=("parallel", "arbitrary"), vmem_limit_bytes=VMEM_LIMIT),
        name="mem_attn",
    )(q, kv)


def rms_norm(x, g):
    xf = x.astype(jnp.float32)
    y = xf * lax.rsqrt(jnp.mean(xf * xf, axis=-1, keepdims=True) + EPS)
    return (y * g.astype(jnp.float32)).astype(x.dtype)


def l2norm(x):
    xf = x.astype(jnp.float32)
    return xf * lax.rsqrt(jnp.sum(xf * xf, axis=-1, keepdims=True) + EPS)


def rope(x, pos):
    half = x.shape[-1] // 2
    inv = ROPE_THETA ** (-jnp.arange(half, dtype=jnp.float32) / half)
    ang = pos.astype(jnp.float32)[:, None] * inv
    cos, sin = jnp.cos(ang)[:, None, :], jnp.sin(ang)[:, None, :]
    xf = x.astype(jnp.float32)
    x1, x2 = xf[..., :half], xf[..., half:]
    return jnp.concatenate([x1 * cos - x2 * sin, x2 * cos + x1 * sin], axis=-1).astype(x.dtype)


def masked_softmax(s, mask):
    s = jnp.where(mask, s.astype(jnp.float32), NEG)
    e = jnp.where(mask, jnp.exp(s - jnp.max(s, axis=-1, keepdims=True)), 0.0)
    return e / jnp.maximum(jnp.sum(e, axis=-1, keepdims=True), 1e-30)


def split_cols(h):
    return jnp.split(h[..., :sum(IN_SPLITS)], [int(o) for o in np.cumsum(IN_SPLITS)[:-1]], axis=-1)


def gather_pages(pool, page_table):
    rows = pool[page_table]
    return rows.reshape(rows.shape[0], -1, *rows.shape[3:])


def over_query_blocks(fn, *qs):
    b, t = qs[0].shape[:2]
    if t <= Q_BLOCK:
        return fn(0, *qs)
    nb = t // Q_BLOCK
    blocks = tuple(q.reshape(b, nb, Q_BLOCK, *q.shape[2:]).swapaxes(0, 1) for q in qs)
    out = lax.map(lambda a: fn(a[0] * Q_BLOCK, *a[1]), (jnp.arange(nb), blocks))
    out = out.swapaxes(0, 1)
    return out.reshape(b, t, *out.shape[3:])


def nsa_compress(rows, pe, w1, w2):
    b, n_keys, d = rows.shape
    r = CMP_LEN // CMP_STRIDE
    n_cmp = (n_keys - CMP_LEN) // CMP_STRIDE + 1
    seg = rows[:, :(n_cmp + r - 1) * CMP_STRIDE].reshape(b, n_cmp + r - 1, CMP_STRIDE * d)
    w1r = w1.reshape(r, CMP_STRIDE * d, CMP_HID)
    h = pe.reshape(-1) @ w1 + sum(seg[:, i:i + n_cmp] @ w1r[i] for i in range(r))
    return jax.nn.gelu(h) @ w2


def nsa_attention(q, gates, k_cmp, v_cmp, k_slc, v_slc, k_win, v_win, past, win_pos0, banded):
    b, n_keys, d = k_slc.shape
    scale = d ** -0.5
    n_cmp = k_cmp.shape[1]
    cmp_end = jnp.arange(n_cmp) * CMP_STRIDE + CMP_LEN - 1
    n_blk = -(-n_keys // SEL_BLOCK)
    n_sel = min(SEL_COUNT, n_blk)
    pad = ((0, 0), (0, n_blk * SEL_BLOCK - n_keys), (0, 0))
    k_blk = jnp.pad(k_slc, pad).reshape(b, n_blk, SEL_BLOCK, d)
    v_blk = jnp.pad(v_slc, pad).reshape(b, n_blk, SEL_BLOCK, d)
    c0 = jnp.arange(n_cmp)[:, None] * CMP_STRIDE
    j0 = jnp.arange(n_blk)[None, :] * SEL_BLOCK
    share = jnp.clip(jnp.minimum(c0 + CMP_LEN, j0 + SEL_BLOCK) - jnp.maximum(c0, j0), 0, None).astype(jnp.float32) / CMP_LEN
    blk_id = jnp.arange(n_blk)
    if banded:
        wpad = ((0, 0), (WINDOW, 0), (0, 0))
        k_win, v_win = jnp.pad(k_win, wpad), jnp.pad(v_win, wpad)

    def block(start, qb, gb):
        nq = qb.shape[1]
        qp = past + start + jnp.arange(nq)
        s = jnp.einsum('bqhd,bcd->bqhc', qb, k_cmp) * scale
        p_cmp = masked_softmax(s, (cmp_end[None, :] <= qp[:, None])[None, :, None, :])
        o_cmp = jnp.einsum('bqhc,bcd->bqhd', p_cmp.astype(v_cmp.dtype), v_cmp)
        imp = jnp.einsum('bqhc,cj->bqj', p_cmp, share)
        forced = (blk_id[None, :] == qp[:, None] // SEL_BLOCK) | (blk_id[None, :] == 0)
        future = blk_id[None, :] * SEL_BLOCK > qp[:, None]
        imp = jnp.where(forced[None], FORCED, jnp.where(future[None], -1.0, imp))
        _, sel = lax.top_k(imp, n_sel)
        ks = jax.vmap(lambda kb, i: kb[i])(k_blk, sel).reshape(b, nq, n_sel * SEL_BLOCK, d)
        vs = jax.vmap(lambda vb, i: vb[i])(v_blk, sel).reshape(b, nq, n_sel * SEL_BLOCK, d)
        kpos = (sel[..., None] * SEL_BLOCK + jnp.arange(SEL_BLOCK)).reshape(b, nq, n_sel * SEL_BLOCK)
        s = jnp.einsum('bqhd,bqkd->bqhk', qb, ks) * scale
        p = masked_softmax(s, (kpos <= qp[None, :, None])[:, :, None, :])
        o_slc = jnp.einsum('bqhk,bqkd->bqhd', p.astype(vs.dtype), vs)
        if banded:
            n_w = WINDOW + nq
            kw = lax.dynamic_slice_in_dim(k_win, start, n_w, axis=1)
            vw = lax.dynamic_slice_in_dim(v_win, start, n_w, axis=1)
            kp = past + start - WINDOW + jnp.arange(n_w)
        else:
            kw, vw = k_win, v_win
            kp = win_pos0 + jnp.arange(k_win.shape[1])
        dlt = qp[:, None] - kp[None, :]
        s = jnp.einsum('bqhd,bkd->bqhk', qb, kw) * scale
        p = masked_softmax(s, ((dlt >= 0) & (dlt < WINDOW) & (kp[None, :] >= 0))[None, :, None, :])
        o_win = jnp.einsum('bqhk,bkd->bqhd', p.astype(vw.dtype), vw)
        g = jax.nn.sigmoid(gb.astype(jnp.float32))
        o = g[..., 0:1] * o_cmp + g[..., 1:2] * o_slc + g[..., 2:3] * o_win
        return o.astype(qb.dtype)

    return over_query_blocks(block, q, gates)


def dsa_attention(q, q_idx, w_idx, k, v, k_idx, past):
    b, n_keys, d = k.shape
    topk = min(DSA_TOPK, n_keys // 4)
    kpos = jnp.arange(n_keys)

    def block(start, qb, qib, wb):
        qp = past + start + jnp.arange(qb.shape[1])
        causal = kpos[None, :] <= qp[:, None]
        dots = jnp.einsum('bqhd,bsd->bqhs', qib, k_idx).astype(jnp.float32) * IDX_DIM ** -0.5
        score = jnp.einsum('bqh,bqhs->bqs', wb.astype(jnp.float32) * IDX_HEADS ** -0.5, jax.nn.relu(dots))
        score = jnp.where(causal[None], score, NEG)
        _, sel = lax.top_k(score, topk)
        ks = jax.vmap(lambda kk, i: kk[i])(k, sel)
        vs = jax.vmap(lambda vv, i: vv[i])(v, sel)
        s = jnp.einsum('bqhd,bqkd->bqhk', qb, ks) * d ** -0.5
        p = masked_softmax(s, (sel <= qp[None, :, None])[:, :, None, :])
        return jnp.einsum('bqhk,bqkd->bqhd', p.astype(vs.dtype), vs)

    return over_query_blocks(block, q, q_idx, w_idx)


def gated_delta_chunked(q, k, v, g, beta, s0):
    f32 = jnp.float32
    b, t, h, dk = k.shape
    dv = v.shape[-1]
    c = min(GDN_CHUNK, t)
    n = -(-t // c)
    pad = n * c - t

    def to_chunks(a):
        a = jnp.pad(a.astype(f32), [(0, 0), (0, pad)] + [(0, 0)] * (a.ndim - 2))
        a = a.reshape(b, n, c, *a.shape[2:])
        return jnp.moveaxis(a, (1, 3), (0, 2))

    qc, kc, vc, gc, bc = (to_chunks(a) for a in (q, k, v, g, beta))
    gcum = jnp.cumsum(gc, axis=-1)
    pos = jnp.arange(c)
    strict = pos[:, None] > pos[None, :]
    incl = pos[:, None] >= pos[None, :]
    diff = gcum[..., :, None] - gcum[..., None, :]
    decay = jnp.where(incl, jnp.exp(jnp.where(incl, diff, 0.0)), 0.0)
    k_beta = kc * bc[..., None]
    a_mat = jnp.where(strict, jnp.einsum('nbhid,nbhjd->nbhij', k_beta, kc) * decay, 0.0)
    rhs = jnp.concatenate([vc * bc[..., None], k_beta * jnp.exp(gcum)[..., None]], axis=-1)
    sol = lax.linalg.triangular_solve(a_mat + jnp.eye(c, dtype=f32), rhs,
                                      left_side=True, lower=True, unit_diagonal=True)
    u, w = sol[..., :dv], sol[..., dv:]
    qk = jnp.einsum('nbhid,nbhjd->nbhij', qc, kc) * decay
    q_dec = qc * jnp.exp(gcum)[..., None]
    k_dec = kc * jnp.exp(gcum[..., -1:] - gcum)[..., None]
    chunk_decay = jnp.exp(gcum[..., -1])

    def step(state, xs):
        u_i, w_i, qd_i, qk_i, kd_i, cd_i = xs
        new_v = u_i - jnp.einsum('bhck,bhkv->bhcv', w_i, state)
        o_i = jnp.einsum('bhck,bhkv->bhcv', qd_i, state) + jnp.einsum('bhij,bhjv->bhiv', qk_i, new_v)
        state = state * cd_i[..., None, None] + jnp.einsum('bhck,bhcv->bhkv', kd_i, new_v)
        return state, o_i

    s_final, o = lax.scan(step, s0.astype(f32), (u, w, q_dec, qk, k_dec, chunk_decay))
    o = jnp.moveaxis(o, (0, 2), (1, 3)).reshape(b, n * c, h, dv)[:, :t]
    return o, s_final


def gdn_mixer(qkv, a, beta_logit, z, conv_prev, s0, conv_w, a_log, dt_bias, norm_g):
    b, t, _ = qkv.shape
    xp = jnp.concatenate([conv_prev.astype(qkv.dtype), qkv], axis=1)
    conv = jax.nn.silu(sum(xp[:, i:i + t] * conv_w[i] for i in range(GDN_CONV)))
    q, k, v = (cc.reshape(b, t, GDN_HEADS, HEAD_DIM) for cc in jnp.split(conv, 3, axis=-1))
    q = l2norm(q) * HEAD_DIM ** -0.5
    k = l2norm(k)
    g = -jnp.exp(a_log.astype(jnp.float32)) * jax.nn.softplus(a.astype(jnp.float32) + dt_bias.astype(jnp.float32))
    beta = jax.nn.sigmoid(beta_logit.astype(jnp.float32))
    o, s_new = gated_delta_chunked(q, k, v, g, beta, s0)
    o = rms_norm(o, norm_g) * jax.nn.silu(z.reshape(b, t, GDN_HEADS, HEAD_DIM).astype(jnp.float32))
    return o.reshape(b, t, W_C).astype(qkv.dtype), s_new, xp[:, t:]


def mem_attention(x, gain, mem_kv, layer, w_q, w_o):
    o = mem_attention_core(dense(x, w_q, gain=gain), mem_kv, layer)
    return dense(o, w_o, resid=x)


def run_group(x, mem, cache, p):
    prompt = cache is None
    b, t, _ = x.shape
    past = 0 if prompt else cache['page_table'].shape[1] * cache['nsa_kv'].shape[2]
    names = ('nsa_kv', 'dsa_kv', 'idx_k', 'win_kv', 'gdn', 'conv') + (('mem_kv',) if prompt else ())
    out = {nm: [] for nm in names}
    bg, tg = (b, t) if prompt else (1, b * t)
    row_pos = past + jnp.arange(tg) % t
    for l in range(DEPTH):
        hproj = dense(x, p['w_in'][l], gain=p['norm_mix_g'][l])
        (q_a, q_b, q_i, nsa_new, win_new, dsa_new, idx_new,
         k_slc, v_slc, k_win, v_win, k_dsa, v_dsa, k_idx) = rope_split(hproj.reshape(bg, tg, PROJ_WIDTH), row_pos)
        nsa_new = nsa_new.reshape(b, t, NSA_ROWS, HEAD_DIM)
        win_new = win_new.reshape(b, t, 2, HEAD_DIM)
        dsa_new = dsa_new.reshape(b, t, 2, HEAD_DIM)
        idx_new = idx_new.reshape(b, t, IDX_DIM)
        g_a, w_i, a_c, b_c = (small_col(hproj, nm) for nm in ('g_a', 'w_i', 'a_c', 'b_c'))
        qkv_off, z_off = PROJ_OFF['qkv_c'], PROJ_OFF['z_c']
        qkv_c = hproj[..., qkv_off:qkv_off + 3 * W_C]
        if prompt:
            conv_prev = jnp.zeros((b, GDN_CONV - 1, 3 * W_C), x.dtype)
            s0 = jnp.zeros((b, GDN_HEADS, HEAD_DIM, HEAD_DIM), jnp.float32)
            mem_kv = dense(mem, p['w_mem_kv'][l], gain=p['mem_norm_g'][l]).reshape(b, -1, 2, MEM_HEADS, HEAD_DIM)
            k_cmp, v_cmp = nsa_compress_prompt(nsa_new, p['nsa_cmp_pe'][l], p['nsa_cmp_w1'][l], p['nsa_cmp_w2'][l])
            o_a = nsa_prompt_attention(q_a, g_a, k_cmp, v_cmp, (t - CMP_LEN) // CMP_STRIDE + 1,
                                       k_slc, v_slc, k_win, v_win)
            o_b = dsa_prompt_attention(q_b, q_i, w_i, k_dsa, v_dsa, k_idx)
            win_out = win_new[:, -min(WINDOW, t):]
            o_c, s_new = gdn_prompt(hproj, qkv_off, z_off, a_c, b_c, conv_prev, s0, p['gdn_conv_w'][l],
                                    p['gdn_a_log'][l], p['gdn_dt_bias'][l], p['gdn_norm_g'][l])
            conv_new = jnp.concatenate([conv_prev, qkv_c[:, -(GDN_CONV - 1):]], axis=1)[:, -(GDN_CONV - 1):]
        else:
            pt = cache['page_table']
            conv_prev, s0 = cache['conv'][l], cache['gdn'][l]
            mem_kv = None
            q_i = jnp.swapaxes(q_i.reshape(IDX_HEADS, b, t, IDX_DIM), 0, 1)
            o_a = nsa_sample_attention(l, q_a.reshape(b, t, W_A), g_a, nsa_new, win_new, cache['nsa_kv'], cache['win_kv'],
                                       pt, p['nsa_cmp_pe'][l], p['nsa_cmp_w1'][l], p['nsa_cmp_w2'][l])
            o_b = dsa_sample_attention(l, q_b.reshape(b, t, W_B), q_i, w_i, dsa_new, idx_new, cache['dsa_kv'],
                                       cache['idx_k'], pt)
            win_out = win_new
            o_c, s_new = gdn_decode(l, hproj, qkv_off, z_off, a_c, b_c, conv_prev, cache['gdn'], p['gdn_conv_w'][l],
                                    p['gdn_a_log'][l], p['gdn_dt_bias'][l], p['gdn_norm_g'][l])
            conv_new = jnp.concatenate([conv_prev, qkv_c], axis=1)[:, -(GDN_CONV - 1):]
        mix = branch_mix(o_a.reshape(b * t, W_A), o_b.reshape(b * t, W_B), o_c.reshape(b * t, W_C),
                         hproj.reshape(b * t, PROJ_WIDTH), PROJ_OFF['merge'],
                         p['w_branch_a'][l], p['w_branch_b'][l], p['w_branch_c'][l])
        x = dense(mix, p['w_mix_out'][l], resid=x).reshape(b, t, D_MODEL)
        mem_src, mem_layer = (mem_kv[None], 0) if prompt else (cache['mem_kv'], l)
        x = mem_attention(x, p['norm_mem_g'][l], mem_src, mem_layer, p['w_mem_q'][l], p['w_mem_o'][l])
        x = dense(ffn_act(x, p['norm_ffn_g'][l], p['w_ffn_gate'][l], p['w_ffn_up'][l]), p['w_ffn_down'][l], resid=x)
        out['nsa_kv'].append(nsa_new)
        out['dsa_kv'].append(dsa_new)
        out['idx_k'].append(idx_new)
        out['win_kv'].append(win_out)
        out['gdn'].append(s_new)
        out['conv'].append(conv_new)
        if prompt:
            out['mem_kv'].append(mem_kv)
    y = rms_norm(x, p['norm_final_g'])
    out = {nm: jnp.stack(v) for nm, v in out.items()}
    if not prompt:
        wlen = cache['win_kv'].shape[2]
        out['win_kv'] = jnp.concatenate([cache['win_kv'], out['win_kv']], axis=2)[:, :, -wlen:]
    return y, out


def kernel(x_prompt, x_sample, mem_prompt, cache_nsa_kv, cache_dsa_kv, cache_dsa_idx_k, cache_win_kv, cache_mem_kv, state_gdn, state_conv, page_table, norm_mix_g, w_in, nsa_cmp_pe, nsa_cmp_w1, nsa_cmp_w2, gdn_conv_w, gdn_a_log, gdn_dt_bias, gdn_norm_g, w_branch_a, w_branch_b, w_branch_c, w_mix_out, norm_mem_g, mem_norm_g, w_mem_q, w_mem_kv, w_mem_o, norm_ffn_g, w_ffn_gate, w_ffn_up, w_ffn_down, norm_final_g):
    p = dict(norm_mix_g=norm_mix_g, w_in=w_in, nsa_cmp_pe=nsa_cmp_pe, nsa_cmp_w1=nsa_cmp_w1,
             nsa_cmp_w2=nsa_cmp_w2, gdn_conv_w=gdn_conv_w, gdn_a_log=gdn_a_log, gdn_dt_bias=gdn_dt_bias,
             gdn_norm_g=gdn_norm_g, w_branch_a=w_branch_a, w_branch_b=w_branch_b, w_branch_c=w_branch_c,
             w_mix_out=w_mix_out, norm_mem_g=norm_mem_g, mem_norm_g=mem_norm_g, w_mem_q=w_mem_q,
             w_mem_kv=w_mem_kv, w_mem_o=w_mem_o, norm_ffn_g=norm_ffn_g, w_ffn_gate=w_ffn_gate,
             w_ffn_up=w_ffn_up, w_ffn_down=w_ffn_down, norm_final_g=norm_final_g)
    for nm in ('w_branch_a', 'w_branch_b', 'w_branch_c', 'w_mix_out', 'w_mem_q', 'w_mem_kv', 'w_mem_o',
               'w_ffn_gate', 'w_ffn_up', 'w_ffn_down'):
        p[nm] = p[nm].astype(jnp.bfloat16)
    p['w_in'] = permute_w_in(w_in.astype(jnp.bfloat16))
    cache = dict(nsa_kv=cache_nsa_kv, dsa_kv=cache_dsa_kv, idx_k=cache_dsa_idx_k, win_kv=cache_win_kv,
                 mem_kv=cache_mem_kv, gdn=state_gdn, conv=state_conv, page_table=page_table)
    y_prompt, sp = run_group(x_prompt, mem_prompt, None, p)
    y_sample, ss = run_group(x_sample, None, cache, p)
    return (y_prompt, y_sample,
            sp['nsa_kv'], sp['dsa_kv'], sp['idx_k'], sp['win_kv'], sp['gdn'], sp['conv'], sp['mem_kv'],
            ss['nsa_kv'], ss['dsa_kv'], ss['idx_k'], ss['win_kv'], ss['gdn'], ss['conv'])
```

```python
import functools
import math

import jax
import jax.numpy as jnp
import numpy as np
from jax import lax
from jax.experimental import pallas as pl
from jax.experimental.pallas import tpu as pltpu

D_MODEL = 2048
DEPTH = 2
PAGE_SIZE = 128
HEAD_DIM = 128
NSA_HEADS = D_MODEL // (4 * HEAD_DIM)
DSA_HEADS = D_MODEL // (4 * HEAD_DIM)
GDN_HEADS = D_MODEL // (2 * HEAD_DIM)
W_A = NSA_HEADS * HEAD_DIM
W_B = DSA_HEADS * HEAD_DIM
W_C = GDN_HEADS * HEAD_DIM
CMP_LEN = 32
CMP_STRIDE = 16
CMP_HID = 2 * HEAD_DIM
SEL_BLOCK = 64
SEL_COUNT = 16
WINDOW = 512
IDX_HEADS = 16
IDX_DIM = 64
DSA_TOPK = 256
GDN_CONV = 4
GDN_CHUNK = 64
MEM_HEADS = 4
ROPE_THETA = 10000.0
Q_BLOCK = 128
EPS = 1e-6
NEG = -1e30
FORCED = 1e9
IN_SPLITS = (W_A, 6 * HEAD_DIM, 3 * NSA_HEADS,
             W_B, 2 * HEAD_DIM, IDX_HEADS * IDX_DIM, IDX_HEADS, IDX_DIM,
             3 * W_C, GDN_HEADS, GDN_HEADS, W_C,
             3 * D_MODEL)

LANE = 128
VMEM_LIMIT = 48 * 1024 * 1024


DENSE_VMEM_BUDGET = 40 * 1024 * 1024
NORM_ROWS = 128


def _divisor_tiles(n, cap):
    return [t for t in range(min(cap, n), 0, -LANE) if t % LANE == 0 and n % t == 0]


def _dense_tiles(m, n, per_row_bytes, per_col_bytes, per_out_bytes):
    for tm in (1024, 512, 256, 128):
        if m % tm:
            continue
        for tn in _divisor_tiles(n, 1024):
            if tm * per_row_bytes + tn * per_col_bytes + tm * tn * per_out_bytes <= DENSE_VMEM_BUDGET:
                return tm, tn
    raise ValueError("no dense tile fits VMEM")


def _stage_rows(x_ref, g_ref, xn_ref):
    def body(r, carry):
        rows = pl.ds(pl.multiple_of(r * NORM_ROWS, NORM_ROWS), NORM_ROWS)
        xf = x_ref[rows, :].astype(jnp.float32)
        if g_ref is not None:
            xf = xf * lax.rsqrt(jnp.mean(xf * xf, axis=-1, keepdims=True) + EPS) * g_ref[...]
        xn_ref[rows, :] = xf.astype(jnp.bfloat16)
        return carry
    lax.fori_loop(0, x_ref.shape[0] // NORM_ROWS, body, 0)


def _dense_kernel(*refs, has_gain, has_resid, staged):
    it = iter(refs)
    x_ref = next(it)
    g_ref = next(it) if has_gain else None
    w_ref = next(it)
    r_ref = next(it) if has_resid else None
    o_ref = next(it)
    xn_ref = next(it) if staged else x_ref

    if staged:
        @pl.when(pl.program_id(1) == 0)
        def _():
            _stage_rows(x_ref, g_ref, xn_ref)

    acc = jnp.dot(xn_ref[...], w_ref[...], preferred_element_type=jnp.float32)
    if has_resid:
        acc = acc + r_ref[...]
    o_ref[...] = acc.astype(o_ref.dtype)


def _weight_spec(w, layer, rows, tn):
    assert w.shape[1] == rows
    return pl.BlockSpec((None, rows, tn), lambda i, j: (layer, 0, j))


def dense(x, w, layer, gain=None, resid=None, out_dtype=jnp.float32):
    lead, kdim, n = x.shape[:-1], x.shape[-1], w.shape[-1]
    x2 = x.reshape(-1, kdim)
    m = x2.shape[0]
    ob = jnp.dtype(out_dtype).itemsize
    staged = gain is not None or x2.dtype != jnp.bfloat16
    tm, tn = _dense_tiles(m, n, per_row_bytes=kdim * (2 * x2.dtype.itemsize + (2 if staged else 0)),
                          per_col_bytes=kdim * 2 * 2, per_out_bytes=2 * ob + (8 if resid is not None else 0))
    args, specs = [x2], [pl.BlockSpec((tm, kdim), lambda i, j: (i, 0))]
    if gain is not None:
        args.append(gain.reshape(1, kdim).astype(jnp.float32))
        specs.append(pl.BlockSpec((1, kdim), lambda i, j: (0, 0)))
    args.append(w)
    specs.append(_weight_spec(w, layer, kdim, tn))
    if resid is not None:
        args.append(resid.reshape(m, n))
        specs.append(pl.BlockSpec((tm, tn), lambda i, j: (i, j)))
    out = pl.pallas_call(
        functools.partial(_dense_kernel, has_gain=gain is not None, has_resid=resid is not None, staged=staged),
        grid=(m // tm, n // tn),
        in_specs=specs,
        out_specs=pl.BlockSpec((tm, tn), lambda i, j: (i, j)),
        out_shape=jax.ShapeDtypeStruct((m, n), out_dtype),
        scratch_shapes=[pltpu.VMEM((tm, kdim), jnp.bfloat16)] if staged else [],
        compiler_params=pltpu.CompilerParams(
            dimension_semantics=("parallel", "arbitrary"), vmem_limit_bytes=VMEM_LIMIT),
        name="dense",
    )(*args)
    return out.reshape(*lead, n)


def _ffn_act_kernel(x_ref, g_ref, wg_ref, wu_ref, o_ref, xn_ref):
    @pl.when(pl.program_id(1) == 0)
    def _():
        _stage_rows(x_ref, g_ref, xn_ref)

    xn = xn_ref[...]
    a = jnp.dot(xn, wg_ref[...], preferred_element_type=jnp.float32)
    u = jnp.dot(xn, wu_ref[...], preferred_element_type=jnp.float32)
    o_ref[...] = (a / (1.0 + jnp.exp(-a)) * u).astype(o_ref.dtype)


def ffn_act(x, gain, w_gate, w_up, layer):
    lead, kdim, n = x.shape[:-1], x.shape[-1], w_gate.shape[-1]
    x2 = x.reshape(-1, kdim)
    m = x2.shape[0]
    tm, tn = _dense_tiles(m, n, per_row_bytes=kdim * (2 * 4 + 2), per_col_bytes=2 * kdim * 2 * 2, per_out_bytes=2 * 2)
    out = pl.pallas_call(
        _ffn_act_kernel,
        grid=(m // tm, n // tn),
        in_specs=[pl.BlockSpec((tm, kdim), lambda i, j: (i, 0)),
                  pl.BlockSpec((1, kdim), lambda i, j: (0, 0)),
                  _weight_spec(w_gate, layer, kdim, tn), _weight_spec(w_up, layer, kdim, tn)],
        out_specs=pl.BlockSpec((tm, tn), lambda i, j: (i, j)),
        out_shape=jax.ShapeDtypeStruct((m, n), jnp.bfloat16),
        scratch_shapes=[pltpu.VMEM((tm, kdim), jnp.bfloat16)],
        compiler_params=pltpu.CompilerParams(
            dimension_semantics=("parallel", "arbitrary"), vmem_limit_bytes=VMEM_LIMIT),
        name="ffn_act",
    )(x2, gain.reshape(1, kdim).astype(jnp.float32), w_gate, w_up)
    return out.reshape(*lead, n)


def _branch_mix_kernel(oa_ref, ob_ref, oc_ref, ma_ref, mb_ref, mc_ref, wa_ref, wb_ref, wc_ref, o_ref, xn_ref):
    @pl.when(pl.program_id(1) == 0)
    def _():
        off = 0
        for src in (oa_ref, ob_ref, oc_ref):
            xn_ref[:, off:off + src.shape[1]] = src[...].astype(jnp.bfloat16)
            off += src.shape[1]

    def gated(m_ref, w_ref, lo, hi):
        y = jnp.dot(xn_ref[:, lo:hi], w_ref[...], preferred_element_type=jnp.float32)
        return y / (1.0 + jnp.exp(-m_ref[...]))

    o_ref[...] = (gated(ma_ref, wa_ref, 0, W_A) + gated(mb_ref, wb_ref, W_A, W_A + W_B)
                  + gated(mc_ref, wc_ref, W_A + W_B, W_A + W_B + W_C)).astype(o_ref.dtype)


def branch_mix(o_a, o_b, o_c, merge, merge_off, w_a, w_b, w_c, layer):
    m, n = merge.shape[0], w_a.shape[-1]
    ktot = W_A + W_B + W_C
    tm, tn = _dense_tiles(m, n, per_row_bytes=ktot * (2 * 4 + 2), per_col_bytes=ktot * 2 * 2, per_out_bytes=2 * 2 + 3 * 8)
    nb = n // tn
    assert merge_off % tn == 0
    mb = merge_off // tn
    row = lambda width: pl.BlockSpec((tm, width), lambda i, j: (i, 0))
    return pl.pallas_call(
        _branch_mix_kernel,
        grid=(m // tm, nb),
        in_specs=[row(W_A), row(W_B), row(W_C),
                  pl.BlockSpec((tm, tn), lambda i, j: (i, mb + j)),
                  pl.BlockSpec((tm, tn), lambda i, j: (i, mb + j + nb)),
                  pl.BlockSpec((tm, tn), lambda i, j: (i, mb + j + 2 * nb)),
                  _weight_spec(w_a, layer, W_A, tn), _weight_spec(w_b, layer, W_B, tn),
                  _weight_spec(w_c, layer, W_C, tn)],
        out_specs=pl.BlockSpec((tm, tn), lambda i, j: (i, j)),
        out_shape=jax.ShapeDtypeStruct((m, n), jnp.bfloat16),
        scratch_shapes=[pltpu.VMEM((tm, ktot), jnp.bfloat16)],
        compiler_params=pltpu.CompilerParams(
            dimension_semantics=("parallel", "arbitrary"), vmem_limit_bytes=VMEM_LIMIT),
        name="branch_mix",
    )(o_a, o_b, o_c, merge, merge, merge, w_a, w_b, w_c)


REF_COLS = tuple(zip(('q_a', 'kv_a', 'g_a', 'q_b', 'kv_b', 'q_i', 'w_i', 'k_i', 'qkv_c', 'a_c', 'b_c', 'z_c', 'merge'),
                     IN_SPLITS))
SMALL_COLS = ('k_i', 'g_a', 'w_i', 'a_c', 'b_c')
WIDE_COLS = ('q_a', 'kv_a', 'q_b', 'kv_b', 'q_i', 'qkv_c', 'z_c', 'merge')


def _layout():
    width = dict(REF_COLS)
    assert all(width[n] % LANE == 0 for n in WIDE_COLS) and sum(width[n] for n in SMALL_COLS) <= LANE
    off, pos = {}, 0
    for n in WIDE_COLS:
        off[n] = pos
        pos += width[n]
    off['small'] = pos
    small, spos = {}, 0
    for n in SMALL_COLS:
        small[n] = (spos, width[n])
        spos += width[n]
    return off, small, pos + LANE


PROJ_OFF, SMALL_OFF, PROJ_WIDTH = _layout()
ROPE_COLS = PROJ_OFF['qkv_c']
ROPE_TM = 256


def _column_moves():
    ref_off, pos = {}, 0
    for n, wd in REF_COLS:
        ref_off[n] = pos
        pos += wd
    width = dict(REF_COLS)
    moves = [(ref_off[n], PROJ_OFF[n], width[n]) for n in WIDE_COLS]
    moves += [(ref_off[n], PROJ_OFF['small'] + SMALL_OFF[n][0], width[n]) for n in SMALL_COLS]
    return moves, PROJ_OFF['small'] + sum(width[n] for n in SMALL_COLS)


PERMUTE_ROWS = 128


def _permute_kernel(w_ref, o_ref):
    moves, used = _column_moves()
    for src, dst, wd in moves:
        o_ref[:, dst:dst + wd] = w_ref[:, src:src + wd].astype(o_ref.dtype)
    o_ref[:, used:] = jnp.zeros((o_ref.shape[0], o_ref.shape[1] - used), o_ref.dtype)


def permute_w_in(w):
    nl, kdim, n_in = w.shape
    return pl.pallas_call(
        _permute_kernel,
        grid=(nl, kdim // PERMUTE_ROWS),
        in_specs=[pl.BlockSpec((None, PERMUTE_ROWS, n_in), lambda l, i: (l, i, 0))],
        out_specs=pl.BlockSpec((None, PERMUTE_ROWS, PROJ_WIDTH), lambda l, i: (l, i, 0)),
        out_shape=jax.ShapeDtypeStruct((nl, kdim, PROJ_WIDTH), jnp.bfloat16),
        compiler_params=pltpu.CompilerParams(
            dimension_semantics=("parallel", "parallel"), vmem_limit_bytes=VMEM_LIMIT),
        name="permute_w_in",
    )(w)


def small_col(hproj, name):
    lo, wd = SMALL_OFF[name]
    return hproj[..., PROJ_OFF['small'] + lo:PROJ_OFF['small'] + lo + wd]


def _rope_tables(pos, rows, head_dim=HEAD_DIM):
    half = head_dim // 2
    inv = ROPE_THETA ** (-jnp.arange(half, dtype=jnp.float32) / half)
    ang = pos.astype(jnp.float32)[:, None] * inv
    cos, sin = jnp.cos(ang), jnp.sin(ang)
    reps = LANE // head_dim
    pad = ((0, rows - pos.shape[0]), (0, 0))
    return (jnp.pad(jnp.tile(jnp.concatenate([cos, cos], axis=-1), (1, reps)), pad),
            jnp.pad(jnp.tile(jnp.concatenate([-sin, sin], axis=-1), (1, reps)), pad))


def _rope_split_kernel(h_ref, small_ref, c128_ref, s128_ref, c64_ref, s64_ref,
                       qa_ref, qb_ref, qi_ref, nsa_ref, win_ref, dsa_ref, idx_ref,
                       kslc_ref, vslc_ref, kwin_ref, vwin_ref, kdsa_ref, vdsa_ref, kidx_ref):
    d, bf = HEAD_DIM, jnp.bfloat16
    c128, s128, c64, s64 = c128_ref[...], s128_ref[...], c64_ref[...], s64_ref[...]
    first_half = (lax.broadcasted_iota(jnp.int32, (1, LANE), 1) & (IDX_DIM - 1)) < IDX_DIM // 2

    def rope128(x):
        return x * c128 + pltpu.roll(x, d // 2, 1) * s128

    def rope64(x):
        rot = jnp.where(first_half, pltpu.roll(x, LANE - IDX_DIM // 2, 1), pltpu.roll(x, IDX_DIM // 2, 1))
        return x * c64 + rot * s64

    col = lambda name, i: h_ref[:, PROJ_OFF[name] + i * d:PROJ_OFF[name] + (i + 1) * d]
    for i in range(NSA_HEADS):
        qa_ref[:, i * d:(i + 1) * d] = (rope128(col('q_a', i)) * (d ** -0.5)).astype(bf)
    for i in range(DSA_HEADS):
        qb_ref[:, i * d:(i + 1) * d] = (rope128(col('q_b', i)) * (d ** -0.5)).astype(bf)
    k_slc, v_slc, k_win, v_win = rope128(col('kv_a', 2)), col('kv_a', 3), rope128(col('kv_a', 4)), col('kv_a', 5)
    for i, part in enumerate((col('kv_a', 0), col('kv_a', 1), k_slc, v_slc)):
        nsa_ref[:, i * d:(i + 1) * d] = part
    win_ref[:, 0:d], win_ref[:, d:2 * d] = k_win, v_win
    kslc_ref[...], vslc_ref[...], kwin_ref[...], vwin_ref[...] = (k_slc.astype(bf), v_slc.astype(bf),
                                                                   k_win.astype(bf), v_win.astype(bf))
    k_dsa, v_dsa = rope128(col('kv_b', 0)), col('kv_b', 1)
    dsa_ref[:, 0:d], dsa_ref[:, d:2 * d] = k_dsa, v_dsa
    kdsa_ref[...], vdsa_ref[...] = k_dsa.astype(bf), v_dsa.astype(bf)
    for i in range(IDX_HEADS * IDX_DIM // LANE):
        pair = rope64(col('q_i', i)).astype(bf)
        qi_ref[2 * i] = pair[:, :IDX_DIM]
        qi_ref[2 * i + 1] = pair[:, IDX_DIM:]
    assert SMALL_OFF['k_i'][0] == 0
    k_idx = rope64(small_ref[...])[:, :IDX_DIM]
    idx_ref[...] = k_idx
    kidx_ref[...] = k_idx.astype(bf)


def rope_split(hproj, pos):
    bg, tg, _ = hproj.shape
    d, bf, f32 = HEAD_DIM, jnp.bfloat16, jnp.float32
    tm = min(ROPE_TM, tg)
    assert tg % tm == 0
    c128, s128 = _rope_tables(pos, tg, d)
    c64, s64 = _rope_tables(pos, tg, IDX_DIM)
    rows = lambda w: pl.BlockSpec((None, tm, w), lambda b, i: (b, i, 0))
    table = pl.BlockSpec((tm, LANE), lambda b, i: (i, 0))
    shape = lambda w, dt: jax.ShapeDtypeStruct((bg, tg, w), dt)
    outs = [(W_A, bf), (W_B, bf), None, (NSA_ROWS * d, f32), (2 * d, f32), (2 * d, f32), (IDX_DIM, f32),
            (d, bf), (d, bf), (d, bf), (d, bf), (d, bf), (d, bf), (IDX_DIM, bf)]
    out_specs = [pl.BlockSpec((None, IDX_HEADS, tm, IDX_DIM), lambda b, i: (b, 0, i, 0)) if o is None else rows(o[0])
                 for o in outs]
    out_shape = [jax.ShapeDtypeStruct((bg, IDX_HEADS, tg, IDX_DIM), bf) if o is None else shape(*o) for o in outs]
    return pl.pallas_call(
        _rope_split_kernel,
        grid=(bg, tg // tm),
        in_specs=[pl.BlockSpec((None, tm, ROPE_COLS), lambda b, i: (b, i, 0)),
                  pl.BlockSpec((None, tm, LANE), lambda b, i: (b, i, PROJ_OFF['small'] // LANE)),
                  table, table, table, table],
        out_specs=out_specs, out_shape=out_shape,
        compiler_params=pltpu.CompilerParams(
            dimension_semantics=("parallel", "parallel"), vmem_limit_bytes=VMEM_LIMIT),
        name="rope_split",
    )(hproj, hproj, c128, s128, c64, s64)


_NT = (((1,), (1,)), ((), ()))


def _softmax_tiles(qh, k_ref, v_ref, lo, hi, tk, mask_fn, m_ref, l_ref, acc_ref, nh, tq):
    assert qh.shape[-1] == LANE and tk % LANE == 0
    m_ref[...] = jnp.full(m_ref.shape, NEG, jnp.float32)
    l_ref[...] = jnp.zeros(l_ref.shape, jnp.float32)
    acc_ref[...] = jnp.zeros(acc_ref.shape, jnp.float32)

    def body(kt, carry):
        off = pl.multiple_of(kt * tk, tk)
        k = k_ref[0, pl.ds(off, tk), :]
        v = v_ref[0, pl.ds(off, tk), :]
        bias = (mask_fn(kt) - 1.0) * (-NEG)
        heads = range(nh)
        chunks = []
        for h in heads:
            s = lax.dot_general(qh[h * tq:(h + 1) * tq], k, _NT, preferred_element_type=jnp.float32) + bias
            chunks.append([s[:, c * LANE:(c + 1) * LANE] for c in range(tk // LANE)])
        m_old = [m_ref[h] for h in heads]
        m_new = [jnp.maximum(m_old[h], jnp.max(functools.reduce(jnp.maximum, chunks[h]), axis=-1, keepdims=True))
                 for h in heads]
        ps = [[jnp.exp(ch - m_new[h]) for ch in chunks[h]] for h in heads]
        alpha = [jnp.exp(m_old[h] - m_new[h]) for h in heads]
        pv = [jnp.dot(jnp.concatenate(ps[h], axis=-1).astype(jnp.bfloat16), v, preferred_element_type=jnp.float32)
              for h in heads]
        for h in heads:
            l_ref[h] = alpha[h] * l_ref[h] + functools.reduce(jnp.add, ps[h])
            acc_ref[h] = alpha[h] * acc_ref[h] + pv[h]
            m_ref[h] = m_new[h]
        return carry

    lax.fori_loop(lo, hi, body, 0)
    return acc_ref[...] / jnp.maximum(jnp.sum(l_ref[...], axis=-1, keepdims=True), 1e-30)


def _split3_bf16(x):
    hi = x.astype(jnp.bfloat16)
    r1 = x - hi.astype(jnp.float32)
    mid = r1.astype(jnp.bfloat16)
    lo = (r1 - mid.astype(jnp.float32)).astype(jnp.bfloat16)
    return hi, mid, lo


def _select_blocks(psum, share, qpos_r, n_blk, tq):
    nbp, tqp = share.shape[0], psum.shape[0]
    imp_t = sum(lax.dot_general(share, part, _NT, preferred_element_type=jnp.float32)
                for part in _split3_bf16(psum))
    blk = lax.broadcasted_iota(jnp.int32, (nbp, tqp), 0)
    forced = (blk == jnp.right_shift(qpos_r, 6)) | (blk == 0)
    future = blk * SEL_BLOCK > qpos_r
    imp_t = jnp.where(forced, FORCED, jnp.where(future, -1.0, imp_t))
    imp_t = jnp.where(blk < n_blk, imp_t, -2.0)
    rank = jnp.zeros((nbp, tqp), jnp.float32)
    for i in range(n_blk):
        row = imp_t[i:i + 1, :]
        beats = (row > imp_t) | ((row == imp_t) & (blk > i))
        rank = rank + beats.astype(jnp.float32)
    sel_t = (rank < float(min(SEL_COUNT, n_blk))).astype(jnp.bfloat16)
    eye = (lax.broadcasted_iota(jnp.int32, (tq, tqp), 0) == lax.broadcasted_iota(jnp.int32, (tq, tqp), 1))
    return lax.dot_general(eye.astype(jnp.bfloat16), sel_t, _NT,
                           preferred_element_type=jnp.float32).astype(jnp.bfloat16)


def _softmax_flat(qh, tiles, nh, tq):
    scores = []
    for k, _, maskf in tiles:
        s = lax.dot_general(qh, k, _NT, preferred_element_type=jnp.float32).reshape(nh, tq, k.shape[0])
        scores.append(s + ((maskf - 1.0) * (-NEG))[None])
    m = scores[0]
    for s in scores[1:]:
        m = jnp.maximum(m, s)
    m = jnp.max(m, axis=-1, keepdims=True)
    lsum, acc = None, None
    for s, (_, v, maskf) in zip(scores, tiles):
        p = jnp.exp(s - m) * maskf[None]
        pv = jnp.dot(p.reshape(nh * tq, p.shape[-1]).astype(jnp.bfloat16), v, preferred_element_type=jnp.float32)
        lsum = p if lsum is None else lsum + p
        acc = pv if acc is None else acc + pv
    l = jnp.sum(lsum, axis=-1, keepdims=True)
    return acc.reshape(nh, tq, acc.shape[-1]) / jnp.maximum(l, 1e-30)


NSA_TQ = 128
NSA_TK_SLC = 512
NSA_TK_WIN = 128


def _nsa_prompt_kernel(q_ref, g_ref, kc_ref, vc_ref, ks_ref, vs_ref, kw_ref, vw_ref, share_ref, o_ref,
                       m_ref, l_ref, acc_ref, *, n_cmp, n_blk, n_keys):
    nh, tq, d = NSA_HEADS, NSA_TQ, HEAD_DIM
    nbp = share_ref.shape[0]
    ncp = share_ref.shape[1]
    start = pl.program_id(1) * tq
    qpos_c = start + lax.broadcasted_iota(jnp.int32, (tq, 1), 0)
    qpos_r = start + lax.broadcasted_iota(jnp.int32, (1, tq), 1)

    qh = jnp.concatenate([q_ref[0, :, h * d:(h + 1) * d] for h in range(nh)], axis=0)

    c_r = lax.broadcasted_iota(jnp.int32, (1, ncp), 1)
    mask_c = ((c_r * CMP_STRIDE + (CMP_LEN - 1) <= qpos_c) & (c_r < n_cmp)).astype(jnp.float32)
    s = lax.dot_general(qh, kc_ref[0], _NT, preferred_element_type=jnp.float32).reshape(nh, tq, ncp)
    sm = s + ((mask_c - 1.0) * (-NEG))[None]
    e = jnp.exp(sm - jnp.max(sm, axis=-1, keepdims=True)) * mask_c[None]
    p_cmp = e / jnp.maximum(jnp.sum(e, axis=-1, keepdims=True), 1e-30)
    o_cmp = jnp.dot(p_cmp.reshape(nh * tq, ncp).astype(jnp.bfloat16), vc_ref[0],
                    preferred_element_type=jnp.float32)

    sel = _select_blocks(jnp.sum(p_cmp, axis=0), share_ref[...], qpos_r, n_blk, tq)

    def slc_mask(kt):
        kpos = kt * NSA_TK_SLC + lax.broadcasted_iota(jnp.int32, (1, NSA_TK_SLC), 1)
        kblk = kt * (NSA_TK_SLC // SEL_BLOCK) + jnp.right_shift(
            lax.broadcasted_iota(jnp.int32, (nbp, NSA_TK_SLC), 1), 6)
        expand = (kblk == lax.broadcasted_iota(jnp.int32, (nbp, NSA_TK_SLC), 0)).astype(jnp.bfloat16)
        chosen = jnp.dot(sel, expand, preferred_element_type=jnp.float32)
        return chosen * (kpos <= qpos_c).astype(jnp.float32)

    hi_slc = jnp.minimum((start + tq - 1) // NSA_TK_SLC + 1, n_keys // NSA_TK_SLC)
    o_slc = _softmax_tiles(qh, ks_ref, vs_ref, 0, hi_slc, NSA_TK_SLC, slc_mask, m_ref, l_ref, acc_ref, nh, tq)

    tiles = []
    for i in range(WINDOW // NSA_TK_WIN + 1):
        kt = pl.program_id(1) * (tq // NSA_TK_WIN) - WINDOW // NSA_TK_WIN + i
        off = pl.multiple_of(jnp.maximum(kt, 0) * NSA_TK_WIN, NSA_TK_WIN)
        kpos = kt * NSA_TK_WIN + lax.broadcasted_iota(jnp.int32, (1, NSA_TK_WIN), 1)
        dlt = qpos_c - kpos
        maskf = ((dlt >= 0) & (dlt < WINDOW) & (kpos >= 0)).astype(jnp.float32)
        tiles.append((kw_ref[0, pl.ds(off, NSA_TK_WIN), :], vw_ref[0, pl.ds(off, NSA_TK_WIN), :], maskf))
    o_win = _softmax_flat(qh, tiles, nh, tq)

    gate = 1.0 / (1.0 + jnp.exp(-g_ref[0]))
    o_cmp = o_cmp.reshape(nh, tq, d)
    for h in range(nh):
        o_ref[0, :, h * d:(h + 1) * d] = (gate[:, 3 * h:3 * h + 1] * o_cmp[h]
                                          + gate[:, 3 * h + 1:3 * h + 2] * o_slc[h]
                                          + gate[:, 3 * h + 2:3 * h + 3] * o_win[h])


def _share_matrix_t(n_cmp, n_blk, ncp, nbp):
    c0 = np.arange(n_cmp)[None, :] * CMP_STRIDE
    j0 = np.arange(n_blk)[:, None] * SEL_BLOCK
    share = np.clip(np.minimum(c0 + CMP_LEN, j0 + SEL_BLOCK) - np.maximum(c0, j0), 0, None) / CMP_LEN
    out = np.zeros((nbp, ncp), np.float32)
    out[:n_blk, :n_cmp] = share
    return jnp.asarray(out, jnp.bfloat16)


def nsa_prompt_attention(q, gates, k_cmp, v_cmp, n_cmp, k_slc, v_slc, k_win, v_win):
    b, t, _ = q.shape
    d = HEAD_DIM
    n_blk = -(-t // SEL_BLOCK)
    ncp = -(-n_cmp // LANE) * LANE
    nbp = -(-n_blk // 16) * 16
    assert t % NSA_TK_SLC == 0 and t % NSA_TQ == 0
    bf = jnp.bfloat16
    cpad = ((0, 0), (0, ncp - k_cmp.shape[1]), (0, 0))
    kc, vc = jnp.pad(k_cmp, cpad).astype(bf), jnp.pad(v_cmp, cpad).astype(bf)
    whole = lambda n: pl.BlockSpec((1, n, d), lambda bi, qi: (bi, 0, 0))
    return pl.pallas_call(
        functools.partial(_nsa_prompt_kernel, n_cmp=n_cmp, n_blk=n_blk, n_keys=t),
        grid=(b, t // NSA_TQ),
        in_specs=[pl.BlockSpec((1, NSA_TQ, NSA_HEADS * d), lambda bi, qi: (bi, qi, 0)),
                  pl.BlockSpec((1, NSA_TQ, 3 * NSA_HEADS), lambda bi, qi: (bi, qi, 0)),
                  whole(ncp), whole(ncp), whole(t), whole(t), whole(t), whole(t),
                  pl.BlockSpec((nbp, ncp), lambda bi, qi: (0, 0))],
        out_specs=pl.BlockSpec((1, NSA_TQ, NSA_HEADS * d), lambda bi, qi: (bi, qi, 0)),
        out_shape=jax.ShapeDtypeStruct((b, t, NSA_HEADS * d), jnp.float32),
        scratch_shapes=[pltpu.VMEM((NSA_HEADS, NSA_TQ, LANE), jnp.float32),
                        pltpu.VMEM((NSA_HEADS, NSA_TQ, LANE), jnp.float32),
                        pltpu.VMEM((NSA_HEADS, NSA_TQ, d), jnp.float32)],
        compiler_params=pltpu.CompilerParams(
            dimension_semantics=("parallel", "arbitrary"), vmem_limit_bytes=VMEM_LIMIT),
        name="nsa_prompt",
    )(q, gates, kc, vc, k_slc.astype(bf), v_slc.astype(bf), k_win.astype(bf), v_win.astype(bf),
      _share_matrix_t(n_cmp, n_blk, ncp, nbp))


NSA_ROWS = 4
NSA_SEQ = 2


def _gelu_tanh(x):
    return 0.5 * x * (1.0 + jnp.tanh(math.sqrt(2.0 / math.pi) * (x + 0.044715 * (x * x * x))))


def _pad_rows(x, rows):
    if rows == x.shape[0]:
        return x
    return jnp.concatenate([x, jnp.zeros((rows - x.shape[0], x.shape[1]), x.dtype)], axis=0)


def _compress_tokens(seg_rows, pe_ref, w1_ref, w2_ref, kv, ncp):
    d, bf = HEAD_DIM, jnp.bfloat16
    first = jnp.zeros((ncp, CMP_HID), jnp.float32)
    second = jnp.zeros((ncp, CMP_HID), jnp.float32)
    for r in range(0, CMP_STRIDE, 2):
        ya, yb = seg_rows(r), seg_rows(r + 1)
        for acc_off in (0, CMP_STRIDE):
            lhs = jnp.concatenate([ya + pe_ref[kv, acc_off + r:acc_off + r + 1, :],
                                   yb + pe_ref[kv, acc_off + r + 1:acc_off + r + 2, :]], axis=1).astype(bf)
            w = w1_ref[kv, acc_off + r:acc_off + r + 2].reshape(2 * d, CMP_HID)
            y = jnp.dot(lhs, w, preferred_element_type=jnp.float32)
            if acc_off == 0:
                first = first + y
            else:
                second = second + y
    hid = first + pltpu.roll(second, ncp - 1, 0)
    return jnp.dot(_gelu_tanh(hid).astype(bf), w2_ref[kv], preferred_element_type=jnp.float32)


def _compress_prompt_kernel(rows_ref, pe_ref, w1_ref, w2_ref, cos_ref, sin_ref, kc_ref, vc_ref):
    ncp, d = kc_ref.shape[0], HEAD_DIM

    def seg_rows(kv, r):
        return rows_ref[pl.ds(NSA_ROWS * r + kv, ncp, stride=NSA_ROWS * CMP_STRIDE), :]

    kc = _compress_tokens(functools.partial(seg_rows, 0), pe_ref, w1_ref, w2_ref, 0, ncp)
    kc_ref[...] = (kc * cos_ref[...] + pltpu.roll(kc, d // 2, 1) * sin_ref[...]).astype(kc_ref.dtype)
    vc_ref[...] = _compress_tokens(functools.partial(seg_rows, 1), pe_ref, w1_ref, w2_ref, 1, ncp).astype(vc_ref.dtype)


def nsa_compress_prompt(nsa_new, pe, w1, w2):
    b, t = nsa_new.shape[:2]
    d, bf = HEAD_DIM, jnp.bfloat16
    ncp = t // CMP_STRIDE
    n_cmp = (t - CMP_LEN) // CMP_STRIDE + 1
    assert ncp % 8 == 0 and n_cmp <= ncp
    cos, sin = _rope_tables(jnp.arange(n_cmp) * CMP_STRIDE + CMP_LEN - 1, ncp)
    const = lambda *shape: pl.BlockSpec(shape, lambda i: (0,) * len(shape))
    out = pl.BlockSpec((None, ncp, d), lambda i: (i, 0, 0))
    return pl.pallas_call(
        _compress_prompt_kernel,
        grid=(b,),
        in_specs=[pl.BlockSpec((None, t * NSA_ROWS, d), lambda i: (i, 0, 0)),
                  const(2, CMP_LEN, d), const(2, CMP_LEN, d, CMP_HID), const(2, CMP_HID, d), const(ncp, d), const(ncp, d)],
        out_specs=[out, out],
        out_shape=[jax.ShapeDtypeStruct((b, ncp, d), bf)] * 2,
        compiler_params=pltpu.CompilerParams(dimension_semantics=("parallel",), vmem_limit_bytes=VMEM_LIMIT),
        name="nsa_compress",
    )(nsa_new.reshape(b, t * NSA_ROWS, d), pe, w1.reshape(2, CMP_LEN, d, CMP_HID).astype(bf), w2.astype(bf), cos, sin)


def _gated_store(o_ref, g_ref, branches, nh, d):
    gate = 1.0 / (1.0 + jnp.exp(-g_ref[...]))
    for h in range(nh):
        o_ref[:, h * d:(h + 1) * d] = sum(gate[:, 3 * h + i:3 * h + i + 1] * br[h] for i, br in enumerate(branches))


def _nsa_sample_kernel(pt_ref, q_ref, g_ref, new_ref, winc_ref, winn_ref, pe_ref, w1_ref, w2_ref, cos_ref, sin_ref,
                       share_ref, *rest, past, n_cmp, n_blk):
    del pt_ref
    all_pages, o_ref = rest[:-1], rest[-1]
    n_seq = q_ref.shape[0]
    n_pages = len(all_pages) // n_seq
    ncp = share_ref.shape[1]
    seg_per_page = PAGE_SIZE // CMP_STRIDE
    assert n_pages * seg_per_page == ncp == LANE and CMP_LEN == 2 * CMP_STRIDE

    def seg_rows(kv, r):
        return jnp.concatenate([p[pl.ds(NSA_ROWS * r + kv, seg_per_page, stride=NSA_ROWS * CMP_STRIDE), :]
                                for p in all_pages], axis=0)

    kc_all = _compress_tokens(functools.partial(seg_rows, 0), pe_ref, w1_ref, w2_ref, 0, n_seq * ncp)
    vc_all = _compress_tokens(functools.partial(seg_rows, 1), pe_ref, w1_ref, w2_ref, 1, n_seq * ncp)
    for s in range(n_seq):
        _nsa_sample_one(q_ref.at[s], g_ref.at[s], new_ref.at[s], winc_ref.at[s], winn_ref.at[s], cos_ref, sin_ref,
                        share_ref, all_pages[s * n_pages:(s + 1) * n_pages], o_ref.at[s],
                        kc_all[s * ncp:(s + 1) * ncp], vc_all[s * ncp:(s + 1) * ncp], past, n_cmp, n_blk)


def _nsa_sample_one(q_ref, g_ref, new_ref, winc_ref, winn_ref, cos_ref, sin_ref, share_ref, pages, o_ref,
                    kc, vc, past, n_cmp, n_blk):
    nh, d, pg = NSA_HEADS, HEAD_DIM, PAGE_SIZE
    tq = q_ref.shape[0]
    nbp, ncp = share_ref.shape
    bf = jnp.bfloat16
    qpos_c = past + lax.broadcasted_iota(jnp.int32, (tq, 1), 0)
    qpos_r = past + lax.broadcasted_iota(jnp.int32, (1, LANE), 1)
    lane = lax.broadcasted_iota(jnp.int32, (1, LANE), 1)
    kc = (kc * cos_ref[...] + pltpu.roll(kc, d // 2, 1) * sin_ref[...]).astype(bf)
    vc = vc.astype(bf)

    q = q_ref[...].astype(jnp.float32)
    qh = jnp.concatenate([q[:, h * d:(h + 1) * d] for h in range(nh)], axis=0).astype(bf)

    mask_c = ((lane * CMP_STRIDE + (CMP_LEN - 1) <= qpos_c) & (lane < n_cmp)).astype(jnp.float32)
    s = lax.dot_general(qh, kc, _NT, preferred_element_type=jnp.float32).reshape(nh, tq, ncp)
    sm = s + ((mask_c - 1.0) * (-NEG))[None]
    e = jnp.exp(sm - jnp.max(sm, axis=-1, keepdims=True)) * mask_c[None]
    p_cmp = e / jnp.maximum(jnp.sum(e, axis=-1, keepdims=True), 1e-30)
    o_cmp = jnp.dot(p_cmp.reshape(nh * tq, ncp).astype(bf), vc, preferred_element_type=jnp.float32).reshape(nh, tq, d)

    sel = _select_blocks(_pad_rows(jnp.sum(p_cmp, axis=0), LANE), share_ref[...], qpos_r, n_blk, LANE)

    def slc_mask(key0):
        kblk = key0 // SEL_BLOCK + jnp.right_shift(lax.broadcasted_iota(jnp.int32, (nbp, LANE), 1), 6)
        expand = (kblk == lax.broadcasted_iota(jnp.int32, (nbp, LANE), 0)).astype(bf)
        chosen = jnp.dot(sel, expand, preferred_element_type=jnp.float32)[:tq]
        return chosen * (key0 + lane <= qpos_c).astype(jnp.float32)

    def component(ref, comp, n_comp, row0, rows):
        return ref[pl.ds(n_comp * row0 + comp, rows, stride=n_comp), :]

    tiles = [(component(p, 2, NSA_ROWS, 0, pg).astype(bf), component(p, 3, NSA_ROWS, 0, pg).astype(bf), slc_mask(j * pg))
             for j, p in enumerate(pages)]
    tiles.append((_pad_rows(component(new_ref, 2, NSA_ROWS, 0, tq), LANE).astype(bf),
                  _pad_rows(component(new_ref, 3, NSA_ROWS, 0, tq), LANE).astype(bf), slc_mask(past)))
    o_slc = _softmax_flat(qh, tiles, nh, tq)

    def win_mask(key0):
        dlt = qpos_c - (key0 + lane)
        return ((dlt >= 0) & (dlt < WINDOW)).astype(jnp.float32)

    n_wc = winc_ref.shape[0] // 2
    tiles = [(component(winc_ref, 0, 2, j * LANE, LANE).astype(bf), component(winc_ref, 1, 2, j * LANE, LANE).astype(bf),
              win_mask(past - n_wc + j * LANE)) for j in range(n_wc // LANE)]
    tiles.append((_pad_rows(component(winn_ref, 0, 2, 0, tq), LANE).astype(bf),
                  _pad_rows(component(winn_ref, 1, 2, 0, tq), LANE).astype(bf), win_mask(past)))
    o_win = _softmax_flat(qh, tiles, nh, tq)

    _gated_store(o_ref, g_ref, (o_cmp, o_slc, o_win), nh, d)


def nsa_sample_attention(layer, q, gates, nsa_new, win_new, pool, win_cache, page_table, pe, w1, w2):
    b, t, _ = q.shape
    wlen = win_cache.shape[2]
    nsa_new = nsa_new.reshape(b, t * NSA_ROWS, HEAD_DIM)
    win_new = win_new.reshape(b, t * 2, HEAD_DIM)
    pool = pool.reshape(pool.shape[0], pool.shape[1], PAGE_SIZE * NSA_ROWS, HEAD_DIM)
    win_cache = win_cache.reshape(win_cache.shape[0], b, wlen * 2, HEAD_DIM)
    d, n_pages = HEAD_DIM, page_table.shape[1]
    past = n_pages * PAGE_SIZE
    n_keys = past + t
    n_cmp = (n_keys - CMP_LEN) // CMP_STRIDE + 1
    n_blk = -(-n_keys // SEL_BLOCK)
    ncp = -(-n_cmp // LANE) * LANE
    nbp = -(-n_blk // 16) * 16
    assert (n_cmp + 1) * CMP_STRIDE <= past, "compressed tokens must come from cached rows only"
    cos, sin = _rope_tables(jnp.arange(n_cmp) * CMP_STRIDE + CMP_LEN - 1, ncp)
    bf = jnp.bfloat16
    ns = NSA_SEQ
    assert b % ns == 0
    per_seq = lambda *tail: pl.BlockSpec((ns,) + tail, lambda i, pt: (i,) + (0,) * len(tail))
    const = lambda *shape: pl.BlockSpec(shape, lambda i, pt: (0,) * len(shape))
    page = lambda s, j: pl.BlockSpec((None, None, PAGE_SIZE * NSA_ROWS, d),
                                     lambda i, pt: (layer, pt[i * ns + s, j], 0, 0))
    grid_spec = pltpu.PrefetchScalarGridSpec(
        num_scalar_prefetch=1, grid=(b // ns,),
        in_specs=[per_seq(t, NSA_HEADS * d), per_seq(t, 3 * NSA_HEADS), per_seq(t * NSA_ROWS, d),
                  pl.BlockSpec((None, ns, wlen * 2, d), lambda i, pt: (layer, i, 0, 0)),
                  per_seq(t * 2, d),
                  const(2, CMP_LEN, d), const(2, CMP_LEN, d, CMP_HID), const(2, CMP_HID, d),
                  const(ncp, d), const(ncp, d), const(nbp, ncp)]
                 + [page(s, j) for s in range(ns) for j in range(n_pages)],
        out_specs=per_seq(t, NSA_HEADS * d))
    return pl.pallas_call(
        functools.partial(_nsa_sample_kernel, past=past, n_cmp=n_cmp, n_blk=n_blk),
        grid_spec=grid_spec,
        out_shape=jax.ShapeDtypeStruct((b, t, NSA_HEADS * d), jnp.float32),
        compiler_params=pltpu.CompilerParams(dimension_semantics=("parallel",), vmem_limit_bytes=VMEM_LIMIT),
        name="nsa_sample",
    )(page_table, q, gates, nsa_new, win_cache, win_new, pe, w1.reshape(2, CMP_LEN, d, CMP_HID).astype(bf),
      w2.astype(bf), cos, sin, _share_matrix_t(n_cmp, n_blk, ncp, nbp), *([pool] * (ns * n_pages)))


DSA_TQ = 128
DSA_TK = 256
INT_MIN = -2 ** 31


def _code_to_float(code):
    return pltpu.bitcast(jnp.where(code >= 0, code, code ^ 0x7FFFFFFF), jnp.float32)


def _dsa_prompt_kernel(q_ref, qi_ref, wt_ref, k_ref, v_ref, ki_ref, tri_ref, o_ref,
                       key_ref, sel_ref, m_ref, l_ref, acc_ref, *, n_keys, topk):
    nh, tq, d, tk = DSA_HEADS, DSA_TQ, HEAD_DIM, DSA_TK
    start = pl.program_id(1) * tq
    qpos_r = start + lax.broadcasted_iota(jnp.int32, (1, tq), 1)
    n_kt = jnp.minimum((start + tq - 1) // tk + 1, n_keys // tk)

    wt = wt_ref[0] * (IDX_HEADS ** -0.5 * IDX_DIM ** -0.5)

    def score_body(kt, carry):
        off = pl.multiple_of(kt * tk, tk)
        ki = ki_ref[0, pl.ds(off, tk), :]
        acc = jnp.zeros((tk, tq), jnp.float32)
        for h in range(0, IDX_HEADS, 2):
            pair = qi_ref[0, h:h + 2].reshape(2 * tq, IDX_DIM)
            dots = lax.dot_general(ki, pair, _NT, preferred_element_type=jnp.float32)
            acc = (acc + jnp.maximum(dots[:, :tq], 0.0) * wt[h:h + 1, :]
                   + jnp.maximum(dots[:, tq:], 0.0) * wt[h + 1:h + 2, :])
        kpos_c = off + lax.broadcasted_iota(jnp.int32, (tk, 1), 0)
        key_ref[kt] = jnp.where(kpos_c <= qpos_r, acc, NEG)
        return carry

    lax.fori_loop(0, n_kt, score_body, 0)

    def count(pred):
        def body(kt, part):
            hit = pred(key_ref[kt]).astype(jnp.int32)
            return part + jnp.sum(hit.reshape(tk // 8, 8, tq), axis=0)
        part = lax.fori_loop(0, n_kt, body, jnp.zeros((8, tq), jnp.int32))
        return jnp.sum(part, axis=0, keepdims=True)

    code = jnp.where(count(lambda sc: sc >= 0.0) >= topk, 0, INT_MIN).astype(jnp.int32)

    def bit_body(i, c):
        cand = c | jnp.left_shift(jnp.int32(1), 30 - i)
        cand_f = _code_to_float(cand)
        return jnp.where(count(lambda sc: sc >= cand_f) >= topk, cand, c)

    code = lax.fori_loop(0, 31, bit_body, code)
    thr = _code_to_float(code)
    take_all = code == INT_MIN
    room = (topk - count(lambda sc: sc > thr)).astype(jnp.float32)

    def sel_body(kt, seen):
        kk = key_ref[kt]
        eq = (kk == thr).astype(jnp.float32)
        prefix = jnp.dot(tri_ref[...], eq.astype(jnp.bfloat16), preferred_element_type=jnp.float32)
        keep = (kk > thr) | ((eq > 0.0) & (prefix + seen <= room)) | take_all
        kpos_c = kt * tk + lax.broadcasted_iota(jnp.int32, (tk, 1), 0)
        sel_ref[kt] = (keep & (kpos_c <= qpos_r)).astype(jnp.bfloat16)
        return seen + jnp.sum(eq, axis=0, keepdims=True)

    lax.fori_loop(0, n_kt, sel_body, jnp.zeros((1, tq), jnp.float32))

    eye = (lax.broadcasted_iota(jnp.int32, (tq, tq), 0)
           == lax.broadcasted_iota(jnp.int32, (tq, tq), 1)).astype(jnp.bfloat16)

    def sel_mask(kt):
        return lax.dot_general(eye, sel_ref[kt], _NT, preferred_element_type=jnp.float32)

    qh = jnp.concatenate([q_ref[0, :, h * d:(h + 1) * d] for h in range(nh)], axis=0)
    o = _softmax_tiles(qh, k_ref, v_ref, 0, n_kt, tk, sel_mask, m_ref, l_ref, acc_ref, nh, tq)
    for h in range(nh):
        o_ref[0, :, h * d:(h + 1) * d] = o[h]


def dsa_prompt_attention(q, q_idx, w_idx, k, v, k_idx):
    b, t, _ = q.shape
    d = HEAD_DIM
    topk = min(DSA_TOPK, t // 4)
    assert t % DSA_TK == 0 and t % DSA_TQ == 0
    bf = jnp.bfloat16
    tri = jnp.asarray(np.tril(np.ones((DSA_TK, DSA_TK), np.float32)), bf)
    whole = lambda n, w: pl.BlockSpec((1, n, w), lambda bi, qi: (bi, 0, 0))
    return pl.pallas_call(
        functools.partial(_dsa_prompt_kernel, n_keys=t, topk=topk),
        grid=(b, t // DSA_TQ),
        in_specs=[pl.BlockSpec((1, DSA_TQ, DSA_HEADS * d), lambda bi, qi: (bi, qi, 0)),
                  pl.BlockSpec((1, IDX_HEADS, DSA_TQ, IDX_DIM), lambda bi, qi: (bi, 0, qi, 0)),
                  pl.BlockSpec((1, IDX_HEADS, DSA_TQ), lambda bi, qi: (bi, 0, qi)),
                  whole(t, d), whole(t, d), whole(t, IDX_DIM),
                  pl.BlockSpec((DSA_TK, DSA_TK), lambda bi, qi: (0, 0))],
        out_specs=pl.BlockSpec((1, DSA_TQ, DSA_HEADS * d), lambda bi, qi: (bi, qi, 0)),
        out_shape=jax.ShapeDtypeStruct((b, t, DSA_HEADS * d), jnp.float32),
        scratch_shapes=[pltpu.VMEM((t // DSA_TK, DSA_TK, DSA_TQ), jnp.float32),
                        pltpu.VMEM((t // DSA_TK, DSA_TK, DSA_TQ), bf),
                        pltpu.VMEM((DSA_HEADS, DSA_TQ, LANE), jnp.float32),
                        pltpu.VMEM((DSA_HEADS, DSA_TQ, LANE), jnp.float32),
                        pltpu.VMEM((DSA_HEADS, DSA_TQ, d), jnp.float32)],
        compiler_params=pltpu.CompilerParams(
            dimension_semantics=("parallel", "arbitrary"), vmem_limit_bytes=VMEM_LIMIT),
        name="dsa_prompt",
    )(q, q_idx.astype(bf), jnp.swapaxes(w_idx, 1, 2),
      k.astype(bf), v.astype(bf), k_idx.astype(bf), tri)


SEARCH_BITS = 3


def _dsa_sample_kernel(pt_ref, q_ref, qi_ref, w_ref, new_ref, inew_ref, tri_ref, *rest, past, topk):
    del pt_ref
    n_pages = (len(rest) - 1) // 2
    kv_pages, idx_pages, o_ref = rest[:n_pages], rest[n_pages:2 * n_pages], rest[-1]
    nh, d, pg = DSA_HEADS, HEAD_DIM, PAGE_SIZE
    tq = q_ref.shape[0]
    bf = jnp.bfloat16
    qpos_c = past + lax.broadcasted_iota(jnp.int32, (tq, 1), 0)
    lane = lax.broadcasted_iota(jnp.int32, (1, LANE), 1)

    qi = qi_ref[...]
    wb = jnp.broadcast_to(w_ref[...] * (IDX_HEADS ** -0.5 * IDX_DIM ** -0.5), (IDX_HEADS * tq, LANE))

    def key_tile(ki_t, key0):
        dots = jnp.dot(qi, ki_t, preferred_element_type=jnp.float32)
        score = jnp.sum((jnp.maximum(dots, 0.0) * wb).reshape(IDX_HEADS, tq, LANE), axis=0)
        return jnp.where(key0 + lane <= qpos_c, score, NEG)

    keys = [key_tile(p[...].astype(bf), j * pg) for j, p in enumerate(idx_pages)]
    keys.append(key_tile(inew_ref[...].astype(bf), past))

    def count(pred):
        hits = pred(keys[0]).astype(jnp.int32)
        for kk in keys[1:]:
            hits = hits + pred(kk).astype(jnp.int32)
        return jnp.sum(hits, axis=-1, keepdims=True)

    def at_least(code):
        cand = _code_to_float(code)
        return (count(lambda sc: sc >= cand) >= topk).astype(jnp.int32)

    code = jnp.where(count(lambda sc: sc >= 0.0) >= topk, 0, INT_MIN).astype(jnp.int32)
    n_steps, last_bits = divmod(31, SEARCH_BITS)

    def radix_body(i, c):
        shift = 31 - SEARCH_BITS * (i + 1)
        digit = sum(at_least(c | jnp.left_shift(jnp.int32(j), shift)) for j in range(1, 2 ** SEARCH_BITS))
        return c | jnp.left_shift(digit, shift)

    code = lax.fori_loop(0, n_steps, radix_body, code)
    if last_bits:
        code = code | sum(at_least(code | j) for j in range(1, 2 ** last_bits))
    thr = _code_to_float(code)
    take_all = code == INT_MIN
    room = (topk - count(lambda sc: sc > thr)).astype(jnp.float32)
    eqs = [(kk == thr).astype(jnp.float32) for kk in keys]
    totals = [jnp.sum(e, axis=-1, keepdims=True) for e in eqs]
    stacked = jnp.concatenate(eqs + [jnp.zeros((-len(eqs) * tq % MIN_MXU_ROWS, LANE), jnp.float32)] * (
        1 if len(eqs) * tq % MIN_MXU_ROWS else 0), axis=0).astype(bf)
    prefixes = jnp.dot(stacked, tri_ref[...], preferred_element_type=jnp.float32)
    seen = jnp.zeros((tq, 1), jnp.float32)
    masks = []
    for j, (kk, e) in enumerate(zip(keys, eqs)):
        prefix = prefixes[j * tq:(j + 1) * tq]
        keep = (kk > thr) | ((e > 0.0) & (prefix + seen <= room)) | take_all
        key0 = j * pg if j < n_pages else past
        masks.append((keep & (key0 + lane <= qpos_c)).astype(jnp.float32))
        seen = seen + totals[j]

    def component(ref, comp, rows):
        return ref[pl.ds(comp, rows, stride=2), :]

    tiles = [(component(p, 0, pg).astype(bf), component(p, 1, pg).astype(bf), masks[j]) for j, p in enumerate(kv_pages)]
    tiles.append((_pad_rows(component(new_ref, 0, tq), LANE).astype(bf),
                  _pad_rows(component(new_ref, 1, tq), LANE).astype(bf), masks[n_pages]))
    q = q_ref[...].astype(jnp.float32)
    qh = jnp.concatenate([q[:, h * d:(h + 1) * d] for h in range(nh)], axis=0).astype(bf)
    o = _softmax_flat(qh, tiles, nh, tq)
    for h in range(nh):
        o_ref[:, h * d:(h + 1) * d] = o[h]


def dsa_sample_attention(layer, q, q_idx, w_idx, dsa_new, idx_new, pool, idx_pool, page_table):
    b, t, _ = q.shape
    d, n_pages = HEAD_DIM, page_table.shape[1]
    past = n_pages * PAGE_SIZE
    topk = min(DSA_TOPK, (past + t) // 4)
    pool = pool.reshape(pool.shape[0], pool.shape[1], PAGE_SIZE * 2, d)
    idx_pool_t = jnp.swapaxes(idx_pool, 2, 3)
    idx_new_t = jnp.pad(jnp.swapaxes(idx_new, 1, 2), ((0, 0), (0, 0), (0, LANE - t)))
    tri = jnp.asarray(np.triu(np.ones((LANE, LANE), np.float32)), jnp.bfloat16)
    per_seq = lambda *tail: pl.BlockSpec((None,) + tail, lambda i, pt: (i,) + (0,) * len(tail))
    kv_page = lambda j: pl.BlockSpec((None, None, PAGE_SIZE * 2, d), lambda i, pt: (layer, pt[i, j], 0, 0))
    idx_page = lambda j: pl.BlockSpec((None, None, IDX_DIM, PAGE_SIZE), lambda i, pt: (layer, pt[i, j], 0, 0))
    grid_spec = pltpu.PrefetchScalarGridSpec(
        num_scalar_prefetch=1, grid=(b,),
        in_specs=[per_seq(t, DSA_HEADS * d), per_seq(IDX_HEADS * t, IDX_DIM), per_seq(IDX_HEADS * t, 1),
                  per_seq(t * 2, d), per_seq(IDX_DIM, LANE), pl.BlockSpec((LANE, LANE), lambda i, pt: (0, 0))]
                 + [kv_page(j) for j in range(n_pages)] + [idx_page(j) for j in range(n_pages)],
        out_specs=per_seq(t, DSA_HEADS * d))
    return pl.pallas_call(
        functools.partial(_dsa_sample_kernel, past=past, topk=topk),
        grid_spec=grid_spec,
        out_shape=jax.ShapeDtypeStruct((b, t, DSA_HEADS * d), jnp.float32),
        compiler_params=pltpu.CompilerParams(dimension_semantics=("parallel",), vmem_limit_bytes=VMEM_LIMIT),
        name="dsa_sample",
    )(page_table, q, q_idx.reshape(b, IDX_HEADS * t, IDX_DIM).astype(jnp.bfloat16),
      jnp.swapaxes(w_idx, 1, 2).reshape(b, IDX_HEADS * t, 1), dsa_new.reshape(b, t * 2, d), idx_new_t, tri,
      *([pool] * n_pages), *([idx_pool_t] * n_pages))


GDN_GROUP = 256
GDN_SUB = 16
GDN_HB = 8
CONV_PAD = 8


def _hp_dot(a, b):
    bf = jnp.bfloat16
    ah, bh = a.astype(bf), b.astype(bf)
    al, bl = (a - ah.astype(jnp.float32)).astype(bf), (b - bh.astype(jnp.float32)).astype(bf)
    dot = functools.partial(jnp.dot, preferred_element_type=jnp.float32)
    return dot(ah, bh) + dot(ah, bl) + dot(al, bh)


def _unit_lower_inverse(a, row, col):
    assert GDN_SUB == 16 and GDN_CHUNK == 64
    n = range(len(a))
    eye = (row == col).astype(jnp.float32)
    sub = jnp.right_shift(row, 4) == jnp.right_shift(col, 4)
    a16 = [jnp.where(sub, a[i], 0.0) for i in n]
    t16 = [eye - a16[i] for i in n]
    power = a16
    for _ in range(3):
        power = [_hp_dot(power[i], power[i]) for i in n]
        t16 = [t16[i] + _hp_dot(t16[i], power[i]) for i in n]
    b = [_hp_dot(t16[i], a[i] - a16[i]) for i in n]
    b2 = [_hp_dot(b[i], b[i]) for i in n]
    imb = [eye - b[i] for i in n]
    left = [imb[i] + _hp_dot(imb[i], b2[i]) for i in n]
    return [_hp_dot(left[i], t16[i]) for i in n]


def _gdn_prompt_kernel(xq_ref, xk_ref, xv_ref, pq_ref, pk_ref, pv_ref, a_ref, b_ref, z_ref, cwq_ref, cwk_ref, cwv_ref,
                       alog_ref, dtb_ref, ng_ref, s0_ref, o_ref, s_ref, hist_ref):
    f32, bf = jnp.float32, jnp.bfloat16
    d, g, c = HEAD_DIM, GDN_GROUP, GDN_CHUNK
    heads = range(GDN_HB)
    lanes = [slice(h * d, (h + 1) * d) for h in heads]
    row = lax.broadcasted_iota(jnp.int32, (g, g), 0)
    col = lax.broadcasted_iota(jnp.int32, (g, g), 1)
    same = jnp.right_shift(row, 6) == jnp.right_shift(col, 6)
    incl = same & (row >= col)
    strict = same & (row > col)
    eye = (row == col).astype(f32)
    tril_b, same_b = incl.astype(bf), same.astype(bf)

    @pl.when(pl.program_id(2) == 0)
    def _():
        s_ref[...] = s0_ref[...]
        for part, p_ref in enumerate((pq_ref, pk_ref, pv_ref)):
            hist_ref[part] = p_ref[...]

    pre = a_ref[...] + dtb_ref[...]
    gate = -jnp.exp(alog_ref[...]) * (jnp.maximum(pre, 0.0) + jnp.log(1.0 + jnp.exp(-jnp.abs(pre))))
    parts = _split3_bf16(gate)
    gcum_all = sum(jnp.dot(tril_b, p, preferred_element_type=f32) for p in parts)
    gtot_all = sum(jnp.dot(same_b, p, preferred_element_type=f32) for p in parts)
    beta_all = 1.0 / (1.0 + jnp.exp(-b_ref[...]))

    def conv(part, x_ref, w_ref, h):
        win = jnp.concatenate([hist_ref[part, :, lanes[h]], x_ref[:, lanes[h]]], axis=0)
        y = sum(pltpu.roll(win, g + CONV_PAD - (CONV_PAD - GDN_CONV + 1 + i), 0)[:g] * w_ref[i:i + 1, lanes[h]]
                for i in range(GDN_CONV))
        return y / (1.0 + jnp.exp(-y))

    q = [conv(0, xq_ref, cwq_ref, h) for h in heads]
    k = [conv(1, xk_ref, cwk_ref, h) for h in heads]
    v = [conv(2, xv_ref, cwv_ref, h) for h in heads]
    for part, x_ref in enumerate((xq_ref, xk_ref, xv_ref)):
        hist_ref[part] = x_ref[g - CONV_PAD:g, :]
    q = [q[h] * lax.rsqrt(jnp.sum(q[h] * q[h], axis=-1, keepdims=True) + EPS) * (d ** -0.5) for h in heads]
    k = [k[h] * lax.rsqrt(jnp.sum(k[h] * k[h], axis=-1, keepdims=True) + EPS) for h in heads]
    gc = [gcum_all[:, h:h + 1] for h in heads]
    gl = [gtot_all[:, h:h + 1] for h in heads]
    beta = [beta_all[:, h:h + 1] for h in heads]
    g_i = [jnp.broadcast_to(gc[h], (g, g)) for h in heads]
    g_j = [jnp.sum(g_i[h] * eye, axis=0, keepdims=True) for h in heads]
    decay = [jnp.where(incl, jnp.exp(jnp.where(incl, g_i[h] - g_j[h], 0.0)), 0.0) for h in heads]
    kb = [k[h] * beta[h] for h in heads]
    k16 = [k[h].astype(bf) for h in heads]
    eg = [jnp.exp(gc[h]) for h in heads]
    a_mat = [jnp.where(strict, lax.dot_general(kb[h].astype(bf), k16[h], _NT, preferred_element_type=f32) * decay[h], 0.0)
             for h in heads]
    qk = [(lax.dot_general(q[h].astype(bf), k16[h], _NT, preferred_element_type=f32) * decay[h]).astype(bf)
          for h in heads]
    rhs = [jnp.concatenate([v[h] * beta[h], kb[h] * eg[h]], axis=1) for h in heads]
    inv = _unit_lower_inverse(a_mat, row, col)
    sol = [_hp_dot(inv[h], rhs[h]) for h in heads]
    u = [sol[h][:, :d] for h in heads]
    w16 = [sol[h][:, d:].astype(bf) for h in heads]
    q_dec = [(q[h] * eg[h]).astype(bf) for h in heads]
    k_dec = [(k[h] * jnp.exp(gl[h] - gc[h])).astype(bf) for h in heads]
    chunk_decay = [jnp.exp(gl[h]) for h in heads]
    for ci in range(g // c):
        rows = slice(ci * c, (ci + 1) * c)
        state = [s_ref[h] for h in heads]
        s16 = [state[h].astype(bf) for h in heads]
        nv16 = [(u[h][rows] - jnp.dot(w16[h][rows], s16[h], preferred_element_type=f32)).astype(bf) for h in heads]
        pad = lambda n: [jnp.zeros((n, d), bf)] if n else []
        placed = [jnp.concatenate(pad(ci * c) + [nv16[h]] + pad(g - (ci + 1) * c), axis=0) for h in heads]
        out = [jnp.dot(q_dec[h][rows], s16[h], preferred_element_type=f32)
               + jnp.dot(qk[h][rows], placed[h], preferred_element_type=f32) for h in heads]
        for h in heads:
            s_ref[h] = state[h] * chunk_decay[h][ci * c:ci * c + 1, :] + lax.dot_general(
                k_dec[h][rows], nv16[h], (((0,), (0,)), ((), ())), preferred_element_type=f32)
        for h in heads:
            zz = z_ref[rows, lanes[h]]
            normed = out[h] * lax.rsqrt(jnp.mean(out[h] * out[h], axis=-1, keepdims=True) + EPS) * ng_ref[...]
            o_ref[rows, lanes[h]] = normed * (zz / (1.0 + jnp.exp(-zz)))


def gdn_prompt(src, qkv_off, z_off, a, beta_logit, conv_prev, s0, conv_w, a_log, dt_bias, norm_g):
    b, t, _ = src.shape
    d, hb = HEAD_DIM, GDN_HB
    ng = GDN_HEADS // hb
    assert t % GDN_GROUP == 0 and GDN_HEADS % hb == 0 and qkv_off % (hb * d) == 0 and z_off % (hb * d) == 0
    qb, zb = qkv_off // (hb * d), z_off // (hb * d)
    prev = jnp.pad(conv_prev.astype(src.dtype), ((0, 0), (CONV_PAD - GDN_CONV + 1, 0), (0, 0)))
    by_group = lambda x: jnp.swapaxes(x.reshape(b, t, ng, hb), 1, 2)
    g = GDN_GROUP
    cols = lambda part: pl.BlockSpec((None, g, hb * d), lambda bi, gi, ti: (bi, ti, qb + part * ng + gi))
    hist = lambda part: pl.BlockSpec((None, CONV_PAD, hb * d), lambda bi, gi, ti: (bi, 0, part * ng + gi))
    cw = lambda part: pl.BlockSpec((GDN_CONV, hb * d), lambda bi, gi, ti: (0, part * ng + gi))
    tok = pl.BlockSpec((None, None, g, hb), lambda bi, gi, ti: (bi, gi, ti, 0))
    head_const = pl.BlockSpec((None, 1, hb), lambda bi, gi, ti: (gi, 0, 0))
    state = pl.BlockSpec((None, hb, d, d), lambda bi, gi, ti: (bi, gi, 0, 0))
    return pl.pallas_call(
        _gdn_prompt_kernel,
        grid=(b, ng, t // g),
        in_specs=[cols(0), cols(1), cols(2), hist(0), hist(1), hist(2), tok, tok,
                  pl.BlockSpec((None, g, hb * d), lambda bi, gi, ti: (bi, ti, zb + gi)),
                  cw(0), cw(1), cw(2), head_const, head_const,
                  pl.BlockSpec((1, d), lambda bi, gi, ti: (0, 0)), state],
        out_specs=[pl.BlockSpec((None, g, hb * d), lambda bi, gi, ti: (bi, ti, gi)), state],
        out_shape=[jax.ShapeDtypeStruct((b, t, W_C), jnp.float32),
                   jax.ShapeDtypeStruct((b, GDN_HEADS, d, d), jnp.float32)],
        scratch_shapes=[pltpu.VMEM((3, CONV_PAD, hb * d), jnp.float32)],
        compiler_params=pltpu.CompilerParams(
            dimension_semantics=("parallel", "parallel", "arbitrary"), vmem_limit_bytes=VMEM_LIMIT),
        name="gdn_prompt",
    )(src, src, src, prev, prev, prev, by_group(a), by_group(beta_logit), src, conv_w, conv_w, conv_w,
      a_log.reshape(ng, 1, hb).astype(jnp.float32), dt_bias.reshape(ng, 1, hb).astype(jnp.float32),
      norm_g.reshape(1, d).astype(jnp.float32), s0)


def _gdn_decode_kernel(x_ref, prev_ref, cw_ref, a_ref, b_ref, z_ref, alog_ref, dtb_ref, ng_ref, s0_ref, o_ref, s_ref):
    f32, bf = jnp.float32, jnp.bfloat16
    d, t = HEAD_DIM, x_ref.shape[0]
    heads = range(GDN_HEADS)
    rows = max(t, MIN_MXU_ROWS)
    mx = lambda x: _pad_rows(x, rows).astype(bf)
    win = jnp.concatenate([prev_ref[...], x_ref[...]], axis=0)
    n = win.shape[0]
    y = sum(pltpu.roll(win, n - (CONV_PAD - GDN_CONV + 1 + i), 0)[:t] * cw_ref[i:i + 1, :] for i in range(GDN_CONV))
    y = y / (1.0 + jnp.exp(-y))
    pre = a_ref[...] + dtb_ref[...]
    gate = -jnp.exp(alog_ref[...]) * (jnp.maximum(pre, 0.0) + jnp.log(1.0 + jnp.exp(-jnp.abs(pre))))
    beta_all = 1.0 / (1.0 + jnp.exp(-b_ref[...]))
    run, gcum_rows = None, []
    for ti in range(t):
        run = gate[ti:ti + 1, :] if run is None else run + gate[ti:ti + 1, :]
        gcum_rows.append(run)
    gcum_all = jnp.concatenate(gcum_rows, axis=0)
    ri = lax.broadcasted_iota(jnp.int32, (t, t), 0)
    ci = lax.broadcasted_iota(jnp.int32, (t, t), 1)
    eye = (ri == ci).astype(f32)

    q = [y[:, h * d:(h + 1) * d] for h in heads]
    k = [y[:, W_C + h * d:W_C + (h + 1) * d] for h in heads]
    v = [y[:, 2 * W_C + h * d:2 * W_C + (h + 1) * d] for h in heads]
    q = [q[h] * lax.rsqrt(jnp.sum(q[h] * q[h], axis=-1, keepdims=True) + EPS) * (d ** -0.5) for h in heads]
    k = [k[h] * lax.rsqrt(jnp.sum(k[h] * k[h], axis=-1, keepdims=True) + EPS) for h in heads]
    gc = [gcum_all[:, h:h + 1] for h in heads]
    gr = [jnp.sum(jnp.broadcast_to(gc[h], (t, t)) * eye, axis=0, keepdims=True) for h in heads]
    beta = [beta_all[:, h:h + 1] for h in heads]
    kb = [k[h] * beta[h] for h in heads]
    k16 = [mx(k[h]) for h in heads]
    dec = [jnp.where(ri >= ci, jnp.exp(jnp.where(ri >= ci, gc[h] - gr[h], 0.0)), 0.0) for h in heads]
    dec_t = [jnp.where(ci >= ri, jnp.exp(jnp.where(ci >= ri, gr[h] - gc[h], 0.0)), 0.0) for h in heads]
    a_t = [jnp.where(ci > ri, lax.dot_general(k16[h], mx(kb[h]), _NT, preferred_element_type=f32)[:t, :t] * dec_t[h], 0.0)
           for h in heads]
    qk = [lax.dot_general(mx(q[h]), k16[h], _NT, preferred_element_type=f32)[:t, :t] * dec[h] for h in heads]
    eg = [jnp.exp(gc[h]) for h in heads]
    sol = [jnp.concatenate([v[h] * beta[h], kb[h] * eg[h]], axis=1) for h in heads]
    row = lax.broadcasted_iota(jnp.int32, (t, 1), 0)
    for i in range(1, t):
        upd = [sol[h][i:i + 1, :] - jnp.sum(a_t[h][:, i:i + 1] * sol[h], axis=0, keepdims=True) for h in heads]
        sol = [jnp.where(row == i, upd[h], sol[h]) for h in heads]
    u = [sol[h][:, :d] for h in heads]
    glast = [gc[h][t - 1:t, :] for h in heads]
    state = [s0_ref[h] for h in heads]
    s16 = [state[h].astype(bf) for h in heads]
    both = [jnp.dot(jnp.concatenate([mx(sol[h][:, d:]), mx(q[h] * eg[h])], axis=0), s16[h], preferred_element_type=f32)
            for h in heads]
    nv16 = [mx(u[h] - both[h][:t]) for h in heads]
    qk16 = [_pad_rows(jnp.concatenate([qk[h], jnp.zeros((t, rows - t), f32)], axis=1), rows).astype(bf) for h in heads]
    out = [both[h][rows:rows + t] + jnp.dot(qk16[h], nv16[h], preferred_element_type=f32)[:t] for h in heads]
    k_dec = [mx(k[h] * jnp.exp(glast[h] - gc[h])) for h in heads]
    for h in heads:
        s_ref[h] = state[h] * jnp.exp(glast[h]) + lax.dot_general(k_dec[h], nv16[h], (((0,), (0,)), ((), ())),
                                                                  preferred_element_type=f32)
        zz = z_ref[:, h * d:(h + 1) * d]
        normed = out[h] * lax.rsqrt(jnp.mean(out[h] * out[h], axis=-1, keepdims=True) + EPS) * ng_ref[...]
        o_ref[:, h * d:(h + 1) * d] = normed * (zz / (1.0 + jnp.exp(-zz)))


def gdn_decode(layer, src, qkv_off, z_off, a, beta_logit, conv_prev, s0, conv_w, a_log, dt_bias, norm_g):
    b, t, _ = src.shape
    d, h = HEAD_DIM, GDN_HEADS
    assert qkv_off % (3 * W_C) == 0 and z_off % W_C == 0
    prev = jnp.pad(conv_prev.astype(src.dtype), ((0, 0), (CONV_PAD - GDN_CONV + 1, 0), (0, 0)))
    per_seq = lambda rows, width, blk: pl.BlockSpec((None, rows, width), lambda i: (i, 0, blk))
    const = lambda *shape: pl.BlockSpec(shape, lambda i: (0,) * len(shape))
    return pl.pallas_call(
        _gdn_decode_kernel,
        grid=(b,),
        in_specs=[per_seq(t, 3 * W_C, qkv_off // (3 * W_C)), per_seq(CONV_PAD, 3 * W_C, 0), const(GDN_CONV, 3 * W_C),
                  per_seq(t, h, 0), per_seq(t, h, 0), per_seq(t, W_C, z_off // W_C),
                  const(1, h), const(1, h), const(1, d),
                  pl.BlockSpec((None, None, h, d, d), lambda i: (layer, i, 0, 0, 0))],
        out_specs=[per_seq(t, W_C, 0), pl.BlockSpec((None, h, d, d), lambda i: (i, 0, 0, 0))],
        out_shape=[jax.ShapeDtypeStruct((b, t, W_C), jnp.float32), jax.ShapeDtypeStruct((b, h, d, d), jnp.float32)],
        compiler_params=pltpu.CompilerParams(dimension_semantics=("parallel",), vmem_limit_bytes=VMEM_LIMIT),
        name="gdn_decode",
    )(src, prev, conv_w, a, beta_logit, src, a_log.reshape(1, h).astype(jnp.float32),
      dt_bias.reshape(1, h).astype(jnp.float32), norm_g.reshape(1, d).astype(jnp.float32), s0)


MEM_TQ = 256
MIN_MXU_ROWS = 16


def _mem_attn_kernel(q_ref, kv_ref, o_ref):
    nh, d, bf = MEM_HEADS, HEAD_DIM, jnp.bfloat16
    tq = q_ref.shape[0]
    m = kv_ref.shape[0] // (2 * nh)
    rows = max(tq, MIN_MXU_ROWS)
    for h in range(nh):
        k = kv_ref[pl.ds(h, m, stride=2 * nh), :].astype(bf)
        v = kv_ref[pl.ds(nh + h, m, stride=2 * nh), :].astype(bf)
        q = _pad_rows(q_ref[:, h * d:(h + 1) * d], rows).astype(bf)
        s = lax.dot_general(q, k, _NT, preferred_element_type=jnp.float32) * (d ** -0.5)
        e = jnp.exp(s - jnp.max(s, axis=-1, keepdims=True))
        p = e / jnp.sum(e, axis=-1, keepdims=True)
        o_ref[:, h * d:(h + 1) * d] = jnp.dot(p.astype(bf), v, preferred_element_type=jnp.float32)[:tq]


def mem_attention_core(q, mem_kv, layer):
    b, t, w = q.shape
    m = mem_kv.shape[2]
    tq = min(MEM_TQ, t)
    assert t % tq == 0
    kv = mem_kv.reshape(mem_kv.shape[0], b, m * 2 * MEM_HEADS, HEAD_DIM)
    return pl.pallas_call(
        _mem_attn_kernel,
        grid=(b, t // tq),
        in_specs=[pl.BlockSpec((None, tq, w), lambda bi, qi: (bi, qi, 0)),
                  pl.BlockSpec((None, None, m * 2 * MEM_HEADS, HEAD_DIM), lambda bi, qi: (layer, bi, 0, 0))],
        out_specs=pl.BlockSpec((None, tq, w), lambda bi, qi: (bi, qi, 0)),
        out_shape=jax.ShapeDtypeStruct((b, t, w), jnp.float32),
        compiler_params=pltpu.CompilerParams(
            dimension_semantics=("parallel", "arbitrary"), vmem_limit_bytes=VMEM_LIMIT),
        name="mem_attn",
    )(q, kv)


def rms_norm(x, g):
    xf = x.astype(jnp.float32)
    y = xf * lax.rsqrt(jnp.mean(xf * xf, axis=-1, keepdims=True) + EPS)
    return (y * g.astype(jnp.float32)).astype(x.dtype)


def l2norm(x):
    xf = x.astype(jnp.float32)
    return xf * lax.rsqrt(jnp.sum(xf * xf, axis=-1, keepdims=True) + EPS)


def rope(x, pos):
    half = x.shape[-1] // 2
    inv = ROPE_THETA ** (-jnp.arange(half, dtype=jnp.float32) / half)
    ang = pos.astype(jnp.float32)[:, None] * inv
    cos, sin = jnp.cos(ang)[:, None, :], jnp.sin(ang)[:, None, :]
    xf = x.astype(jnp.float32)
    x1, x2 = xf[..., :half], xf[..., half:]
    return jnp.concatenate([x1 * cos - x2 * sin, x2 * cos + x1 * sin], axis=-1).astype(x.dtype)


def masked_softmax(s, mask):
    s = jnp.where(mask, s.astype(jnp.float32), NEG)
    e = jnp.where(mask, jnp.exp(s - jnp.max(s, axis=-1, keepdims=True)), 0.0)
    return e / jnp.maximum(jnp.sum(e, axis=-1, keepdims=True), 1e-30)


def split_cols(h):
    return jnp.split(h[..., :sum(IN_SPLITS)], [int(o) for o in np.cumsum(IN_SPLITS)[:-1]], axis=-1)


def gather_pages(pool, page_table):
    rows = pool[page_table]
    return rows.reshape(rows.shape[0], -1, *rows.shape[3:])


def over_query_blocks(fn, *qs):
    b, t = qs[0].shape[:2]
    if t <= Q_BLOCK:
        return fn(0, *qs)
    nb = t // Q_BLOCK
    blocks = tuple(q.reshape(b, nb, Q_BLOCK, *q.shape[2:]).swapaxes(0, 1) for q in qs)
    out = lax.map(lambda a: fn(a[0] * Q_BLOCK, *a[1]), (jnp.arange(nb), blocks))
    out = out.swapaxes(0, 1)
    return out.reshape(b, t, *out.shape[3:])


def nsa_compress(rows, pe, w1, w2):
    b, n_keys, d = rows.shape
    r = CMP_LEN // CMP_STRIDE
    n_cmp = (n_keys - CMP_LEN) // CMP_STRIDE + 1
    seg = rows[:, :(n_cmp + r - 1) * CMP_STRIDE].reshape(b, n_cmp + r - 1, CMP_STRIDE * d)
    w1r = w1.reshape(r, CMP_STRIDE * d, CMP_HID)
    h = pe.reshape(-1) @ w1 + sum(seg[:, i:i + n_cmp] @ w1r[i] for i in range(r))
    return jax.nn.gelu(h) @ w2


def nsa_attention(q, gates, k_cmp, v_cmp, k_slc, v_slc, k_win, v_win, past, win_pos0, banded):
    b, n_keys, d = k_slc.shape
    scale = d ** -0.5
    n_cmp = k_cmp.shape[1]
    cmp_end = jnp.arange(n_cmp) * CMP_STRIDE + CMP_LEN - 1
    n_blk = -(-n_keys // SEL_BLOCK)
    n_sel = min(SEL_COUNT, n_blk)
    pad = ((0, 0), (0, n_blk * SEL_BLOCK - n_keys), (0, 0))
    k_blk = jnp.pad(k_slc, pad).reshape(b, n_blk, SEL_BLOCK, d)
    v_blk = jnp.pad(v_slc, pad).reshape(b, n_blk, SEL_BLOCK, d)
    c0 = jnp.arange(n_cmp)[:, None] * CMP_STRIDE
    j0 = jnp.arange(n_blk)[None, :] * SEL_BLOCK
    share = jnp.clip(jnp.minimum(c0 + CMP_LEN, j0 + SEL_BLOCK) - jnp.maximum(c0, j0), 0, None).astype(jnp.float32) / CMP_LEN
    blk_id = jnp.arange(n_blk)
    if banded:
        wpad = ((0, 0), (WINDOW, 0), (0, 0))
        k_win, v_win = jnp.pad(k_win, wpad), jnp.pad(v_win, wpad)

    def block(start, qb, gb):
        nq = qb.shape[1]
        qp = past + start + jnp.arange(nq)
        s = jnp.einsum('bqhd,bcd->bqhc', qb, k_cmp) * scale
        p_cmp = masked_softmax(s, (cmp_end[None, :] <= qp[:, None])[None, :, None, :])
        o_cmp = jnp.einsum('bqhc,bcd->bqhd', p_cmp.astype(v_cmp.dtype), v_cmp)
        imp = jnp.einsum('bqhc,cj->bqj', p_cmp, share)
        forced = (blk_id[None, :] == qp[:, None] // SEL_BLOCK) | (blk_id[None, :] == 0)
        future = blk_id[None, :] * SEL_BLOCK > qp[:, None]
        imp = jnp.where(forced[None], FORCED, jnp.where(future[None], -1.0, imp))
        _, sel = lax.top_k(imp, n_sel)
        ks = jax.vmap(lambda kb, i: kb[i])(k_blk, sel).reshape(b, nq, n_sel * SEL_BLOCK, d)
        vs = jax.vmap(lambda vb, i: vb[i])(v_blk, sel).reshape(b, nq, n_sel * SEL_BLOCK, d)
        kpos = (sel[..., None] * SEL_BLOCK + jnp.arange(SEL_BLOCK)).reshape(b, nq, n_sel * SEL_BLOCK)
        s = jnp.einsum('bqhd,bqkd->bqhk', qb, ks) * scale
        p = masked_softmax(s, (kpos <= qp[None, :, None])[:, :, None, :])
        o_slc = jnp.einsum('bqhk,bqkd->bqhd', p.astype(vs.dtype), vs)
        if banded:
            n_w = WINDOW + nq
            kw = lax.dynamic_slice_in_dim(k_win, start, n_w, axis=1)
            vw = lax.dynamic_slice_in_dim(v_win, start, n_w, axis=1)
            kp = past + start - WINDOW + jnp.arange(n_w)
        else:
            kw, vw = k_win, v_win
            kp = win_pos0 + jnp.arange(k_win.shape[1])
        dlt = qp[:, None] - kp[None, :]
        s = jnp.einsum('bqhd,bkd->bqhk', qb, kw) * scale
        p = masked_softmax(s, ((dlt >= 0) & (dlt < WINDOW) & (kp[None, :] >= 0))[None, :, None, :])
        o_win = jnp.einsum('bqhk,bkd->bqhd', p.astype(vw.dtype), vw)
        g = jax.nn.sigmoid(gb.astype(jnp.float32))
        o = g[..., 0:1] * o_cmp + g[..., 1:2] * o_slc + g[..., 2:3] * o_win
        return o.astype(qb.dtype)

    return over_query_blocks(block, q, gates)


def dsa_attention(q, q_idx, w_idx, k, v, k_idx, past):
    b, n_keys, d = k.shape
    topk = min(DSA_TOPK, n_keys // 4)
    kpos = jnp.arange(n_keys)

    def block(start, qb, qib, wb):
        qp = past + start + jnp.arange(qb.shape[1])
        causal = kpos[None, :] <= qp[:, None]
        dots = jnp.einsum('bqhd,bsd->bqhs', qib, k_idx).astype(jnp.float32) * IDX_DIM ** -0.5
        score = jnp.einsum('bqh,bqhs->bqs', wb.astype(jnp.float32) * IDX_HEADS ** -0.5, jax.nn.relu(dots))
        score = jnp.where(causal[None], score, NEG)
        _, sel = lax.top_k(score, topk)
        ks = jax.vmap(lambda kk, i: kk[i])(k, sel)
        vs = jax.vmap(lambda vv, i: vv[i])(v, sel)
        s = jnp.einsum('bqhd,bqkd->bqhk', qb, ks) * d ** -0.5
        p = masked_softmax(s, (sel <= qp[None, :, None])[:, :, None, :])
        return jnp.einsum('bqhk,bqkd->bqhd', p.astype(vs.dtype), vs)

    return over_query_blocks(block, q, q_idx, w_idx)


def gated_delta_chunked(q, k, v, g, beta, s0):
    f32 = jnp.float32
    b, t, h, dk = k.shape
    dv = v.shape[-1]
    c = min(GDN_CHUNK, t)
    n = -(-t // c)
    pad = n * c - t

    def to_chunks(a):
        a = jnp.pad(a.astype(f32), [(0, 0), (0, pad)] + [(0, 0)] * (a.ndim - 2))
        a = a.reshape(b, n, c, *a.shape[2:])
        return jnp.moveaxis(a, (1, 3), (0, 2))

    qc, kc, vc, gc, bc = (to_chunks(a) for a in (q, k, v, g, beta))
    gcum = jnp.cumsum(gc, axis=-1)
    pos = jnp.arange(c)
    strict = pos[:, None] > pos[None, :]
    incl = pos[:, None] >= pos[None, :]
    diff = gcum[..., :, None] - gcum[..., None, :]
    decay = jnp.where(incl, jnp.exp(jnp.where(incl, diff, 0.0)), 0.0)
    k_beta = kc * bc[..., None]
    a_mat = jnp.where(strict, jnp.einsum('nbhid,nbhjd->nbhij', k_beta, kc) * decay, 0.0)
    rhs = jnp.concatenate([vc * bc[..., None], k_beta * jnp.exp(gcum)[..., None]], axis=-1)
    sol = lax.linalg.triangular_solve(a_mat + jnp.eye(c, dtype=f32), rhs,
                                      left_side=True, lower=True, unit_diagonal=True)
    u, w = sol[..., :dv], sol[..., dv:]
    qk = jnp.einsum('nbhid,nbhjd->nbhij', qc, kc) * decay
    q_dec = qc * jnp.exp(gcum)[..., None]
    k_dec = kc * jnp.exp(gcum[..., -1:] - gcum)[..., None]
    chunk_decay = jnp.exp(gcum[..., -1])

    def step(state, xs):
        u_i, w_i, qd_i, qk_i, kd_i, cd_i = xs
        new_v = u_i - jnp.einsum('bhck,bhkv->bhcv', w_i, state)
        o_i = jnp.einsum('bhck,bhkv->bhcv', qd_i, state) + jnp.einsum('bhij,bhjv->bhiv', qk_i, new_v)
        state = state * cd_i[..., None, None] + jnp.einsum('bhck,bhcv->bhkv', kd_i, new_v)
        return state, o_i

    s_final, o = lax.scan(step, s0.astype(f32), (u, w, q_dec, qk, k_dec, chunk_decay))
    o = jnp.moveaxis(o, (0, 2), (1, 3)).reshape(b, n * c, h, dv)[:, :t]
    return o, s_final


def gdn_mixer(qkv, a, beta_logit, z, conv_prev, s0, conv_w, a_log, dt_bias, norm_g):
    b, t, _ = qkv.shape
    xp = jnp.concatenate([conv_prev.astype(qkv.dtype), qkv], axis=1)
    conv = jax.nn.silu(sum(xp[:, i:i + t] * conv_w[i] for i in range(GDN_CONV)))
    q, k, v = (cc.reshape(b, t, GDN_HEADS, HEAD_DIM) for cc in jnp.split(conv, 3, axis=-1))
    q = l2norm(q) * HEAD_DIM ** -0.5
    k = l2norm(k)
    g = -jnp.exp(a_log.astype(jnp.float32)) * jax.nn.softplus(a.astype(jnp.float32) + dt_bias.astype(jnp.float32))
    beta = jax.nn.sigmoid(beta_logit.astype(jnp.float32))
    o, s_new = gated_delta_chunked(q, k, v, g, beta, s0)
    o = rms_norm(o, norm_g) * jax.nn.silu(z.reshape(b, t, GDN_HEADS, HEAD_DIM).astype(jnp.float32))
    return o.reshape(b, t, W_C).astype(qkv.dtype), s_new, xp[:, t:]


def mem_attention(x, gain, mem_kv, mem_layer, w_q, w_o, layer):
    o = mem_attention_core(dense(x, w_q, layer, gain=gain), mem_kv, mem_layer)
    return dense(o, w_o, layer, resid=x)


def run_group(x, mem, cache, p):
    prompt = cache is None
    b, t, _ = x.shape
    past = 0 if prompt else cache['page_table'].shape[1] * cache['nsa_kv'].shape[2]
    names = ('nsa_kv', 'dsa_kv', 'idx_k', 'win_kv', 'gdn', 'conv') + (('mem_kv',) if prompt else ())
    out = {nm: [] for nm in names}
    bg, tg = (b, t) if prompt else (1, b * t)
    row_pos = past + jnp.arange(tg) % t
    for l in range(DEPTH):
        hproj = dense(x, p['w_in'], l, gain=p['norm_mix_g'][l])
        (q_a, q_b, q_i, nsa_new, win_new, dsa_new, idx_new,
         k_slc, v_slc, k_win, v_win, k_dsa, v_dsa, k_idx) = rope_split(hproj.reshape(bg, tg, PROJ_WIDTH), row_pos)
        nsa_new = nsa_new.reshape(b, t, NSA_ROWS, HEAD_DIM)
        win_new = win_new.reshape(b, t, 2, HEAD_DIM)
        dsa_new = dsa_new.reshape(b, t, 2, HEAD_DIM)
        idx_new = idx_new.reshape(b, t, IDX_DIM)
        g_a, w_i, a_c, b_c = (small_col(hproj, nm) for nm in ('g_a', 'w_i', 'a_c', 'b_c'))
        qkv_off, z_off = PROJ_OFF['qkv_c'], PROJ_OFF['z_c']
        qkv_c = hproj[..., qkv_off:qkv_off + 3 * W_C]
        if prompt:
            conv_prev = jnp.zeros((b, GDN_CONV - 1, 3 * W_C), x.dtype)
            s0 = jnp.zeros((b, GDN_HEADS, HEAD_DIM, HEAD_DIM), jnp.float32)
            mem_kv = dense(mem, p['w_mem_kv'], l, gain=p['mem_norm_g'][l]).reshape(b, -1, 2, MEM_HEADS, HEAD_DIM)
            k_cmp, v_cmp = nsa_compress_prompt(nsa_new, p['nsa_cmp_pe'][l], p['nsa_cmp_w1'][l], p['nsa_cmp_w2'][l])
            o_a = nsa_prompt_attention(q_a, g_a, k_cmp, v_cmp, (t - CMP_LEN) // CMP_STRIDE + 1,
                                       k_slc, v_slc, k_win, v_win)
            o_b = dsa_prompt_attention(q_b, q_i, w_i, k_dsa, v_dsa, k_idx)
            win_out = win_new[:, -min(WINDOW, t):]
            o_c, s_new = gdn_prompt(hproj, qkv_off, z_off, a_c, b_c, conv_prev, s0, p['gdn_conv_w'][l],
                                    p['gdn_a_log'][l], p['gdn_dt_bias'][l], p['gdn_norm_g'][l])
            conv_new = jnp.concatenate([conv_prev, qkv_c[:, -(GDN_CONV - 1):]], axis=1)[:, -(GDN_CONV - 1):]
        else:
            pt = cache['page_table']
            conv_prev, s0 = cache['conv'][l], cache['gdn'][l]
            mem_kv = None
            q_i = jnp.swapaxes(q_i.reshape(IDX_HEADS, b, t, IDX_DIM), 0, 1)
            o_a = nsa_sample_attention(l, q_a.reshape(b, t, W_A), g_a, nsa_new, win_new, cache['nsa_kv'], cache['win_kv'],
                                       pt, p['nsa_cmp_pe'][l], p['nsa_cmp_w1'][l], p['nsa_cmp_w2'][l])
            o_b = dsa_sample_attention(l, q_b.reshape(b, t, W_B), q_i, w_i, dsa_new, idx_new, cache['dsa_kv'],
                                       cache['idx_k'], pt)
            win_out = win_new
            o_c, s_new = gdn_decode(l, hproj, qkv_off, z_off, a_c, b_c, conv_prev, cache['gdn'], p['gdn_conv_w'][l],
                                    p['gdn_a_log'][l], p['gdn_dt_bias'][l], p['gdn_norm_g'][l])
            conv_new = jnp.concatenate([conv_prev, qkv_c], axis=1)[:, -(GDN_CONV - 1):]
        mix = branch_mix(o_a.reshape(b * t, W_A), o_b.reshape(b * t, W_B), o_c.reshape(b * t, W_C),
                         hproj.reshape(b * t, PROJ_WIDTH), PROJ_OFF['merge'],
                         p['w_branch_a'], p['w_branch_b'], p['w_branch_c'], l)
        x = dense(mix, p['w_mix_out'], l, resid=x).reshape(b, t, D_MODEL)
        mem_src, mem_layer = (mem_kv[None], 0) if prompt else (cache['mem_kv'], l)
        x = mem_attention(x, p['norm_mem_g'][l], mem_src, mem_layer, p['w_mem_q'], p['w_mem_o'], l)
        x = dense(ffn_act(x, p['norm_ffn_g'][l], p['w_ffn_gate'], p['w_ffn_up'], l), p['w_ffn_down'], l, resid=x)
        out['nsa_kv'].append(nsa_new)
        out['dsa_kv'].append(dsa_new)
        out['idx_k'].append(idx_new)
        out['win_kv'].append(win_out)
        out['gdn'].append(s_new)
        out['conv'].append(conv_new)
        if prompt:
            out['mem_kv'].append(mem_kv)
    y = rms_norm(x, p['norm_final_g'])
    out = {nm: jnp.stack(v) for nm, v in out.items()}
    if not prompt:
        wlen = cache['win_kv'].shape[2]
        out['win_kv'] = jnp.concatenate([cache['win_kv'], out['win_kv']], axis=2)[:, :, -wlen:]
    return y, out


def kernel(x_prompt, x_sample, mem_prompt, cache_nsa_kv, cache_dsa_kv, cache_dsa_idx_k, cache_win_kv, cache_mem_kv, state_gdn, state_conv, page_table, norm_mix_g, w_in, nsa_cmp_pe, nsa_cmp_w1, nsa_cmp_w2, gdn_conv_w, gdn_a_log, gdn_dt_bias, gdn_norm_g, w_branch_a, w_branch_b, w_branch_c, w_mix_out, norm_mem_g, mem_norm_g, w_mem_q, w_mem_kv, w_mem_o, norm_ffn_g, w_ffn_gate, w_ffn_up, w_ffn_down, norm_final_g):
    p = dict(norm_mix_g=norm_mix_g, w_in=w_in, nsa_cmp_pe=nsa_cmp_pe, nsa_cmp_w1=nsa_cmp_w1,
             nsa_cmp_w2=nsa_cmp_w2, gdn_conv_w=gdn_conv_w, gdn_a_log=gdn_a_log, gdn_dt_bias=gdn_dt_bias,
             gdn_norm_g=gdn_norm_g, w_branch_a=w_branch_a, w_branch_b=w_branch_b, w_branch_c=w_branch_c,
             w_mix_out=w_mix_out, norm_mem_g=norm_mem_g, mem_norm_g=mem_norm_g, w_mem_q=w_mem_q,
             w_mem_kv=w_mem_kv, w_mem_o=w_mem_o, norm_ffn_g=norm_ffn_g, w_ffn_gate=w_ffn_gate,
             w_ffn_up=w_ffn_up, w_ffn_down=w_ffn_down, norm_final_g=norm_final_g)
    for nm in ('w_branch_a', 'w_branch_b', 'w_branch_c', 'w_mix_out', 'w_mem_q', 'w_mem_kv', 'w_mem_o',
               'w_ffn_gate', 'w_ffn_up', 'w_ffn_down'):
        p[nm] = p[nm].astype(jnp.bfloat16)
    p['w_in'] = permute_w_in(w_in.astype(jnp.bfloat16))
    cache = dict(nsa_kv=cache_nsa_kv, dsa_kv=cache_dsa_kv, idx_k=cache_dsa_idx_k, win_kv=cache_win_kv,
                 mem_kv=cache_mem_kv, gdn=state_gdn, conv=state_conv, page_table=page_table)
    y_prompt, sp = run_group(x_prompt, mem_prompt, None, p)
    y_sample, ss = run_group(x_sample, None, cache, p)
    return (y_prompt, y_sample,
            sp['nsa_kv'], sp['dsa_kv'], sp['idx_k'], sp['win_kv'], sp['gdn'], sp['conv'], sp['mem_kv'],
            ss['nsa_kv'], ss['dsa_kv'], ss['idx_k'], ss['win_kv'], ss['gdn'], ss['conv'])
```

```python
import functools
import math

import jax
import jax.numpy as jnp
import numpy as np
from jax import lax
from jax.experimental import pallas as pl
from jax.experimental.pallas import tpu as pltpu

D_MODEL = 2048
DEPTH = 2
PAGE_SIZE = 128
HEAD_DIM = 128
NSA_HEADS = D_MODEL // (4 * HEAD_DIM)
DSA_HEADS = D_MODEL // (4 * HEAD_DIM)
GDN_HEADS = D_MODEL // (2 * HEAD_DIM)
W_A = NSA_HEADS * HEAD_DIM
W_B = DSA_HEADS * HEAD_DIM
W_C = GDN_HEADS * HEAD_DIM
CMP_LEN = 32
CMP_STRIDE = 16
CMP_HID = 2 * HEAD_DIM
SEL_BLOCK = 64
SEL_COUNT = 16
WINDOW = 512
IDX_HEADS = 16
IDX_DIM = 64
DSA_TOPK = 256
GDN_CONV = 4
GDN_CHUNK = 64
MEM_HEADS = 4
ROPE_THETA = 10000.0
Q_BLOCK = 128
EPS = 1e-6
NEG = -1e30
FORCED = 1e9
IN_SPLITS = (W_A, 6 * HEAD_DIM, 3 * NSA_HEADS,
             W_B, 2 * HEAD_DIM, IDX_HEADS * IDX_DIM, IDX_HEADS, IDX_DIM,
             3 * W_C, GDN_HEADS, GDN_HEADS, W_C,
             3 * D_MODEL)

LANE = 128
VMEM_LIMIT = 48 * 1024 * 1024


DENSE_VMEM_BUDGET = 40 * 1024 * 1024
NORM_ROWS = 128


def _divisor_tiles(n, cap):
    return [t for t in range(min(cap, n), 0, -LANE) if t % LANE == 0 and n % t == 0]


def _dense_tiles(m, n, per_row_bytes, per_col_bytes, per_out_bytes):
    for tm in (1024, 512, 256, 128):
        if m % tm:
            continue
        for tn in _divisor_tiles(n, 1024):
            if tm * per_row_bytes + tn * per_col_bytes + tm * tn * per_out_bytes <= DENSE_VMEM_BUDGET:
                return tm, tn
    raise ValueError("no dense tile fits VMEM")


def _stage_rows(x_ref, g_ref, xn_ref):
    def body(r, carry):
        rows = pl.ds(pl.multiple_of(r * NORM_ROWS, NORM_ROWS), NORM_ROWS)
        xf = x_ref[rows, :].astype(jnp.float32)
        if g_ref is not None:
            xf = xf * lax.rsqrt(jnp.mean(xf * xf, axis=-1, keepdims=True) + EPS) * g_ref[...]
        xn_ref[rows, :] = xf.astype(jnp.bfloat16)
        return carry
    lax.fori_loop(0, x_ref.shape[0] // NORM_ROWS, body, 0)


def _dense_kernel(*refs, has_gain, has_resid, staged):
    it = iter(refs)
    x_ref = next(it)
    g_ref = next(it) if has_gain else None
    w_ref = next(it)
    r_ref = next(it) if has_resid else None
    o_ref = next(it)
    xn_ref = next(it) if staged else x_ref

    if staged:
        @pl.when(pl.program_id(1) == 0)
        def _():
            _stage_rows(x_ref, g_ref, xn_ref)

    acc = jnp.dot(xn_ref[...], w_ref[...], preferred_element_type=jnp.float32)
    if has_resid:
        acc = acc + r_ref[...]
    o_ref[...] = acc.astype(o_ref.dtype)


def _weight_spec(w, layer, rows, tn):
    assert w.shape[1] == rows
    return pl.BlockSpec((None, rows, tn), lambda i, j: (layer, 0, j))


def dense(x, w, layer, gain=None, resid=None, out_dtype=jnp.float32):
    lead, kdim, n = x.shape[:-1], x.shape[-1], w.shape[-1]
    x2 = x.reshape(-1, kdim)
    m = x2.shape[0]
    ob = jnp.dtype(out_dtype).itemsize
    staged = gain is not None or x2.dtype != jnp.bfloat16
    tm, tn = _dense_tiles(m, n, per_row_bytes=kdim * (2 * x2.dtype.itemsize + (2 if staged else 0)),
                          per_col_bytes=kdim * 2 * 2, per_out_bytes=2 * ob + (8 if resid is not None else 0))
    args, specs = [x2], [pl.BlockSpec((tm, kdim), lambda i, j: (i, 0))]
    if gain is not None:
        args.append(gain.reshape(1, kdim).astype(jnp.float32))
        specs.append(pl.BlockSpec((1, kdim), lambda i, j: (0, 0)))
    args.append(w)
    specs.append(_weight_spec(w, layer, kdim, tn))
    if resid is not None:
        args.append(resid.reshape(m, n))
        specs.append(pl.BlockSpec((tm, tn), lambda i, j: (i, j)))
    out = pl.pallas_call(
        functools.partial(_dense_kernel, has_gain=gain is not None, has_resid=resid is not None, staged=staged),
        grid=(m // tm, n // tn),
        in_specs=specs,
        out_specs=pl.BlockSpec((tm, tn), lambda i, j: (i, j)),
        out_shape=jax.ShapeDtypeStruct((m, n), out_dtype),
        scratch_shapes=[pltpu.VMEM((tm, kdim), jnp.bfloat16)] if staged else [],
        compiler_params=pltpu.CompilerParams(
            dimension_semantics=("parallel", "arbitrary"), vmem_limit_bytes=VMEM_LIMIT),
        name="dense",
    )(*args)
    return out.reshape(*lead, n)


def _ffn_act_kernel(x_ref, g_ref, wg_ref, wu_ref, o_ref, xn_ref):
    @pl.when(pl.program_id(1) == 0)
    def _():
        _stage_rows(x_ref, g_ref, xn_ref)

    xn = xn_ref[...]
    a = jnp.dot(xn, wg_ref[...], preferred_element_type=jnp.float32)
    u = jnp.dot(xn, wu_ref[...], preferred_element_type=jnp.float32)
    o_ref[...] = (a / (1.0 + jnp.exp(-a)) * u).astype(o_ref.dtype)


def ffn_act(x, gain, w_gate, w_up, layer):
    lead, kdim, n = x.shape[:-1], x.shape[-1], w_gate.shape[-1]
    x2 = x.reshape(-1, kdim)
    m = x2.shape[0]
    tm, tn = _dense_tiles(m, n, per_row_bytes=kdim * (2 * 4 + 2), per_col_bytes=2 * kdim * 2 * 2, per_out_bytes=2 * 2)
    out = pl.pallas_call(
        _ffn_act_kernel,
        grid=(m // tm, n // tn),
        in_specs=[pl.BlockSpec((tm, kdim), lambda i, j: (i, 0)),
                  pl.BlockSpec((1, kdim), lambda i, j: (0, 0)),
                  _weight_spec(w_gate, layer, kdim, tn), _weight_spec(w_up, layer, kdim, tn)],
        out_specs=pl.BlockSpec((tm, tn), lambda i, j: (i, j)),
        out_shape=jax.ShapeDtypeStruct((m, n), jnp.bfloat16),
        scratch_shapes=[pltpu.VMEM((tm, kdim), jnp.bfloat16)],
        compiler_params=pltpu.CompilerParams(
            dimension_semantics=("parallel", "arbitrary"), vmem_limit_bytes=VMEM_LIMIT),
        name="ffn_act",
    )(x2, gain.reshape(1, kdim).astype(jnp.float32), w_gate, w_up)
    return out.reshape(*lead, n)


def _branch_mix_kernel(oa_ref, ob_ref, oc_ref, ma_ref, mb_ref, mc_ref, wa_ref, wb_ref, wc_ref, o_ref, xn_ref):
    @pl.when(pl.program_id(1) == 0)
    def _():
        off = 0
        for src in (oa_ref, ob_ref, oc_ref):
            xn_ref[:, off:off + src.shape[1]] = src[...].astype(jnp.bfloat16)
            off += src.shape[1]

    def gated(m_ref, w_ref, lo, hi):
        y = jnp.dot(xn_ref[:, lo:hi], w_ref[...], preferred_element_type=jnp.float32)
        return y / (1.0 + jnp.exp(-m_ref[...]))

    o_ref[...] = (gated(ma_ref, wa_ref, 0, W_A) + gated(mb_ref, wb_ref, W_A, W_A + W_B)
                  + gated(mc_ref, wc_ref, W_A + W_B, W_A + W_B + W_C)).astype(o_ref.dtype)


def branch_mix(o_a, o_b, o_c, merge, merge_off, w_a, w_b, w_c, layer):
    m, n = merge.shape[0], w_a.shape[-1]
    ktot = W_A + W_B + W_C
    tm, tn = _dense_tiles(m, n, per_row_bytes=ktot * (2 * 4 + 2), per_col_bytes=ktot * 2 * 2, per_out_bytes=2 * 2 + 3 * 8)
    nb = n // tn
    assert merge_off % tn == 0
    mb = merge_off // tn
    row = lambda width: pl.BlockSpec((tm, width), lambda i, j: (i, 0))
    return pl.pallas_call(
        _branch_mix_kernel,
        grid=(m // tm, nb),
        in_specs=[row(W_A), row(W_B), row(W_C),
                  pl.BlockSpec((tm, tn), lambda i, j: (i, mb + j)),
                  pl.BlockSpec((tm, tn), lambda i, j: (i, mb + j + nb)),
                  pl.BlockSpec((tm, tn), lambda i, j: (i, mb + j + 2 * nb)),
                  _weight_spec(w_a, layer, W_A, tn), _weight_spec(w_b, layer, W_B, tn),
                  _weight_spec(w_c, layer, W_C, tn)],
        out_specs=pl.BlockSpec((tm, tn), lambda i, j: (i, j)),
        out_shape=jax.ShapeDtypeStruct((m, n), jnp.bfloat16),
        scratch_shapes=[pltpu.VMEM((tm, ktot), jnp.bfloat16)],
        compiler_params=pltpu.CompilerParams(
            dimension_semantics=("parallel", "arbitrary"), vmem_limit_bytes=VMEM_LIMIT),
        name="branch_mix",
    )(o_a, o_b, o_c, merge, merge, merge, w_a, w_b, w_c)


REF_COLS = tuple(zip(('q_a', 'kv_a', 'g_a', 'q_b', 'kv_b', 'q_i', 'w_i', 'k_i', 'qkv_c', 'a_c', 'b_c', 'z_c', 'merge'),
                     IN_SPLITS))
SMALL_COLS = ('k_i', 'g_a', 'w_i', 'a_c', 'b_c')
WIDE_COLS = ('q_a', 'kv_a', 'q_b', 'kv_b', 'q_i', 'qkv_c', 'z_c', 'merge')


def _layout():
    width = dict(REF_COLS)
    assert all(width[n] % LANE == 0 for n in WIDE_COLS) and sum(width[n] for n in SMALL_COLS) <= LANE
    off, pos = {}, 0
    for n in WIDE_COLS:
        off[n] = pos
        pos += width[n]
    off['small'] = pos
    small, spos = {}, 0
    for n in SMALL_COLS:
        small[n] = (spos, width[n])
        spos += width[n]
    return off, small, pos + LANE


PROJ_OFF, SMALL_OFF, PROJ_WIDTH = _layout()
ROPE_COLS = PROJ_OFF['qkv_c']
ROPE_TM = 256


def _column_moves():
    ref_off, pos = {}, 0
    for n, wd in REF_COLS:
        ref_off[n] = pos
        pos += wd
    width = dict(REF_COLS)
    moves = [(ref_off[n], PROJ_OFF[n], width[n]) for n in WIDE_COLS]
    moves += [(ref_off[n], PROJ_OFF['small'] + SMALL_OFF[n][0], width[n]) for n in SMALL_COLS]
    return moves, PROJ_OFF['small'] + sum(width[n] for n in SMALL_COLS)


PERMUTE_ROWS = 128


def _permute_kernel(w_ref, o_ref):
    moves, used = _column_moves()
    for src, dst, wd in moves:
        o_ref[:, dst:dst + wd] = w_ref[:, src:src + wd].astype(o_ref.dtype)
    o_ref[:, used:] = jnp.zeros((o_ref.shape[0], o_ref.shape[1] - used), o_ref.dtype)


def permute_w_in(w):
    nl, kdim, n_in = w.shape
    return pl.pallas_call(
        _permute_kernel,
        grid=(nl, kdim // PERMUTE_ROWS),
        in_specs=[pl.BlockSpec((None, PERMUTE_ROWS, n_in), lambda l, i: (l, i, 0))],
        out_specs=pl.BlockSpec((None, PERMUTE_ROWS, PROJ_WIDTH), lambda l, i: (l, i, 0)),
        out_shape=jax.ShapeDtypeStruct((nl, kdim, PROJ_WIDTH), jnp.bfloat16),
        compiler_params=pltpu.CompilerParams(
            dimension_semantics=("parallel", "parallel"), vmem_limit_bytes=VMEM_LIMIT),
        name="permute_w_in",
    )(w)


def small_col(hproj, name):
    lo, wd = SMALL_OFF[name]
    return hproj[..., PROJ_OFF['small'] + lo:PROJ_OFF['small'] + lo + wd]


def _rope_tables(pos, rows, head_dim=HEAD_DIM):
    half = head_dim // 2
    inv = ROPE_THETA ** (-jnp.arange(half, dtype=jnp.float32) / half)
    ang = pos.astype(jnp.float32)[:, None] * inv
    cos, sin = jnp.cos(ang), jnp.sin(ang)
    reps = LANE // head_dim
    pad = ((0, rows - pos.shape[0]), (0, 0))
    return (jnp.pad(jnp.tile(jnp.concatenate([cos, cos], axis=-1), (1, reps)), pad),
            jnp.pad(jnp.tile(jnp.concatenate([-sin, sin], axis=-1), (1, reps)), pad))


def _rope_split_kernel(h_ref, small_ref, c128_ref, s128_ref, c64_ref, s64_ref,
                       qa_ref, qb_ref, qi_ref, nsa_ref, win_ref, dsa_ref, idx_ref,
                       kslc_ref, vslc_ref, kwin_ref, vwin_ref, kdsa_ref, vdsa_ref, kidx_ref):
    d, bf = HEAD_DIM, jnp.bfloat16
    c128, s128, c64, s64 = c128_ref[...], s128_ref[...], c64_ref[...], s64_ref[...]
    first_half = (lax.broadcasted_iota(jnp.int32, (1, LANE), 1) & (IDX_DIM - 1)) < IDX_DIM // 2

    def rope128(x):
        return x * c128 + pltpu.roll(x, d // 2, 1) * s128

    def rope64(x):
        rot = jnp.where(first_half, pltpu.roll(x, LANE - IDX_DIM // 2, 1), pltpu.roll(x, IDX_DIM // 2, 1))
        return x * c64 + rot * s64

    col = lambda name, i: h_ref[:, PROJ_OFF[name] + i * d:PROJ_OFF[name] + (i + 1) * d]
    for i in range(NSA_HEADS):
        qa_ref[:, i * d:(i + 1) * d] = (rope128(col('q_a', i)) * (d ** -0.5)).astype(bf)
    for i in range(DSA_HEADS):
        qb_ref[:, i * d:(i + 1) * d] = (rope128(col('q_b', i)) * (d ** -0.5)).astype(bf)
    k_slc, v_slc, k_win, v_win = rope128(col('kv_a', 2)), col('kv_a', 3), rope128(col('kv_a', 4)), col('kv_a', 5)
    for i, part in enumerate((col('kv_a', 0), col('kv_a', 1), k_slc, v_slc)):
        nsa_ref[:, i * d:(i + 1) * d] = part
    win_ref[:, 0:d], win_ref[:, d:2 * d] = k_win, v_win
    kslc_ref[...], vslc_ref[...], kwin_ref[...], vwin_ref[...] = (k_slc.astype(bf), v_slc.astype(bf),
                                                                   k_win.astype(bf), v_win.astype(bf))
    k_dsa, v_dsa = rope128(col('kv_b', 0)), col('kv_b', 1)
    dsa_ref[:, 0:d], dsa_ref[:, d:2 * d] = k_dsa, v_dsa
    kdsa_ref[...], vdsa_ref[...] = k_dsa.astype(bf), v_dsa.astype(bf)
    for i in range(IDX_HEADS * IDX_DIM // LANE):
        pair = rope64(col('q_i', i)).astype(bf)
        qi_ref[2 * i] = pair[:, :IDX_DIM]
        qi_ref[2 * i + 1] = pair[:, IDX_DIM:]
    assert SMALL_OFF['k_i'][0] == 0
    k_idx = rope64(small_ref[...])[:, :IDX_DIM]
    idx_ref[...] = k_idx
    kidx_ref[...] = k_idx.astype(bf)


def rope_split(hproj, pos):
    bg, tg, _ = hproj.shape
    d, bf, f32 = HEAD_DIM, jnp.bfloat16, jnp.float32
    tm = min(ROPE_TM, tg)
    assert tg % tm == 0
    c128, s128 = _rope_tables(pos, tg, d)
    c64, s64 = _rope_tables(pos, tg, IDX_DIM)
    rows = lambda w: pl.BlockSpec((None, tm, w), lambda b, i: (b, i, 0))
    table = pl.BlockSpec((tm, LANE), lambda b, i: (i, 0))
    shape = lambda w, dt: jax.ShapeDtypeStruct((bg, tg, w), dt)
    outs = [(W_A, bf), (W_B, bf), None, (NSA_ROWS * d, f32), (2 * d, f32), (2 * d, f32), (IDX_DIM, f32),
            (d, bf), (d, bf), (d, bf), (d, bf), (d, bf), (d, bf), (IDX_DIM, bf)]
    out_specs = [pl.BlockSpec((None, IDX_HEADS, tm, IDX_DIM), lambda b, i: (b, 0, i, 0)) if o is None else rows(o[0])
                 for o in outs]
    out_shape = [jax.ShapeDtypeStruct((bg, IDX_HEADS, tg, IDX_DIM), bf) if o is None else shape(*o) for o in outs]
    return pl.pallas_call(
        _rope_split_kernel,
        grid=(bg, tg // tm),
        in_specs=[pl.BlockSpec((None, tm, ROPE_COLS), lambda b, i: (b, i, 0)),
                  pl.BlockSpec((None, tm, LANE), lambda b, i: (b, i, PROJ_OFF['small'] // LANE)),
                  table, table, table, table],
        out_specs=out_specs, out_shape=out_shape,
        compiler_params=pltpu.CompilerParams(
            dimension_semantics=("parallel", "parallel"), vmem_limit_bytes=VMEM_LIMIT),
        name="rope_split",
    )(hproj, hproj, c128, s128, c64, s64)


_NT = (((1,), (1,)), ((), ()))


def _softmax_tiles(qh, k_ref, v_ref, lo, hi, tk, mask_fn, m_ref, l_ref, acc_ref, nh, tq):
    assert qh.shape[-1] == LANE and tk % LANE == 0
    m_ref[...] = jnp.full(m_ref.shape, NEG, jnp.float32)
    l_ref[...] = jnp.zeros(l_ref.shape, jnp.float32)
    acc_ref[...] = jnp.zeros(acc_ref.shape, jnp.float32)

    def body(kt, carry):
        off = pl.multiple_of(kt * tk, tk)
        k = k_ref[0, pl.ds(off, tk), :]
        v = v_ref[0, pl.ds(off, tk), :]
        bias = (mask_fn(kt) - 1.0) * (-NEG)
        heads = range(nh)
        chunks = []
        for h in heads:
            s = lax.dot_general(qh[h * tq:(h + 1) * tq], k, _NT, preferred_element_type=jnp.float32) + bias
            chunks.append([s[:, c * LANE:(c + 1) * LANE] for c in range(tk // LANE)])
        m_old = [m_ref[h] for h in heads]
        m_new = [jnp.maximum(m_old[h], jnp.max(functools.reduce(jnp.maximum, chunks[h]), axis=-1, keepdims=True))
                 for h in heads]
        ps = [[jnp.exp(ch - m_new[h]) for ch in chunks[h]] for h in heads]
        alpha = [jnp.exp(m_old[h] - m_new[h]) for h in heads]
        pv = [jnp.dot(jnp.concatenate(ps[h], axis=-1).astype(jnp.bfloat16), v, preferred_element_type=jnp.float32)
              for h in heads]
        for h in heads:
            l_ref[h] = alpha[h] * l_ref[h] + functools.reduce(jnp.add, ps[h])
            acc_ref[h] = alpha[h] * acc_ref[h] + pv[h]
            m_ref[h] = m_new[h]
        return carry

    lax.fori_loop(lo, hi, body, 0)
    return acc_ref[...] / jnp.maximum(jnp.sum(l_ref[...], axis=-1, keepdims=True), 1e-30)


def _split3_bf16(x):
    hi = x.astype(jnp.bfloat16)
    r1 = x - hi.astype(jnp.float32)
    mid = r1.astype(jnp.bfloat16)
    lo = (r1 - mid.astype(jnp.float32)).astype(jnp.bfloat16)
    return hi, mid, lo


def _select_blocks(psum, share, qpos_r, n_blk, tq):
    nbp, tqp = share.shape[0], psum.shape[0]
    imp_t = sum(lax.dot_general(share, part, _NT, preferred_element_type=jnp.float32)
                for part in _split3_bf16(psum))
    blk = lax.broadcasted_iota(jnp.int32, (nbp, tqp), 0)
    forced = (blk == jnp.right_shift(qpos_r, 6)) | (blk == 0)
    future = blk * SEL_BLOCK > qpos_r
    imp_t = jnp.where(forced, FORCED, jnp.where(future, -1.0, imp_t))
    imp_t = jnp.where(blk < n_blk, imp_t, -2.0)
    rank = jnp.zeros((nbp, tqp), jnp.float32)
    for i in range(n_blk):
        row = imp_t[i:i + 1, :]
        beats = (row > imp_t) | ((row == imp_t) & (blk > i))
        rank = rank + beats.astype(jnp.float32)
    sel_t = (rank < float(min(SEL_COUNT, n_blk))).astype(jnp.bfloat16)
    eye = (lax.broadcasted_iota(jnp.int32, (tq, tqp), 0) == lax.broadcasted_iota(jnp.int32, (tq, tqp), 1))
    return lax.dot_general(eye.astype(jnp.bfloat16), sel_t, _NT,
                           preferred_element_type=jnp.float32).astype(jnp.bfloat16)


def _softmax_flat(qh, tiles, nh, tq):
    scores = []
    for k, _, maskf in tiles:
        s = lax.dot_general(qh, k, _NT, preferred_element_type=jnp.float32).reshape(nh, tq, k.shape[0])
        scores.append(s + ((maskf - 1.0) * (-NEG))[None])
    m = scores[0]
    for s in scores[1:]:
        m = jnp.maximum(m, s)
    m = jnp.max(m, axis=-1, keepdims=True)
    lsum, acc = None, None
    for s, (_, v, maskf) in zip(scores, tiles):
        p = jnp.exp(s - m) * maskf[None]
        pv = jnp.dot(p.reshape(nh * tq, p.shape[-1]).astype(jnp.bfloat16), v, preferred_element_type=jnp.float32)
        lsum = p if lsum is None else lsum + p
        acc = pv if acc is None else acc + pv
    l = jnp.sum(lsum, axis=-1, keepdims=True)
    return acc.reshape(nh, tq, acc.shape[-1]) / jnp.maximum(l, 1e-30)


NSA_TQ = 128
NSA_TK_SLC = 512
NSA_TK_WIN = 128


def _nsa_prompt_kernel(q_ref, g_ref, kc_ref, vc_ref, ks_ref, vs_ref, kw_ref, vw_ref, share_ref, o_ref,
                       m_ref, l_ref, acc_ref, *, n_cmp, n_blk, n_keys):
    nh, tq, d = NSA_HEADS, NSA_TQ, HEAD_DIM
    nbp = share_ref.shape[0]
    ncp = share_ref.shape[1]
    start = pl.program_id(1) * tq
    qpos_c = start + lax.broadcasted_iota(jnp.int32, (tq, 1), 0)
    qpos_r = start + lax.broadcasted_iota(jnp.int32, (1, tq), 1)

    qh = jnp.concatenate([q_ref[0, :, h * d:(h + 1) * d] for h in range(nh)], axis=0)

    c_r = lax.broadcasted_iota(jnp.int32, (1, ncp), 1)
    mask_c = ((c_r * CMP_STRIDE + (CMP_LEN - 1) <= qpos_c) & (c_r < n_cmp)).astype(jnp.float32)
    s = lax.dot_general(qh, kc_ref[0], _NT, preferred_element_type=jnp.float32).reshape(nh, tq, ncp)
    sm = s + ((mask_c - 1.0) * (-NEG))[None]
    e = jnp.exp(sm - jnp.max(sm, axis=-1, keepdims=True)) * mask_c[None]
    p_cmp = e / jnp.maximum(jnp.sum(e, axis=-1, keepdims=True), 1e-30)
    o_cmp = jnp.dot(p_cmp.reshape(nh * tq, ncp).astype(jnp.bfloat16), vc_ref[0],
                    preferred_element_type=jnp.float32)

    sel = _select_blocks(jnp.sum(p_cmp, axis=0), share_ref[...], qpos_r, n_blk, tq)

    def slc_mask(kt):
        kpos = kt * NSA_TK_SLC + lax.broadcasted_iota(jnp.int32, (1, NSA_TK_SLC), 1)
        kblk = kt * (NSA_TK_SLC // SEL_BLOCK) + jnp.right_shift(
            lax.broadcasted_iota(jnp.int32, (nbp, NSA_TK_SLC), 1), 6)
        expand = (kblk == lax.broadcasted_iota(jnp.int32, (nbp, NSA_TK_SLC), 0)).astype(jnp.bfloat16)
        chosen = jnp.dot(sel, expand, preferred_element_type=jnp.float32)
        return chosen * (kpos <= qpos_c).astype(jnp.float32)

    hi_slc = jnp.minimum((start + tq - 1) // NSA_TK_SLC + 1, n_keys // NSA_TK_SLC)
    o_slc = _softmax_tiles(qh, ks_ref, vs_ref, 0, hi_slc, NSA_TK_SLC, slc_mask, m_ref, l_ref, acc_ref, nh, tq)

    tiles = []
    for i in range(WINDOW // NSA_TK_WIN + 1):
        kt = pl.program_id(1) * (tq // NSA_TK_WIN) - WINDOW // NSA_TK_WIN + i
        off = pl.multiple_of(jnp.maximum(kt, 0) * NSA_TK_WIN, NSA_TK_WIN)
        kpos = kt * NSA_TK_WIN + lax.broadcasted_iota(jnp.int32, (1, NSA_TK_WIN), 1)
        dlt = qpos_c - kpos
        maskf = ((dlt >= 0) & (dlt < WINDOW) & (kpos >= 0)).astype(jnp.float32)
        tiles.append((kw_ref[0, pl.ds(off, NSA_TK_WIN), :], vw_ref[0, pl.ds(off, NSA_TK_WIN), :], maskf))
    o_win = _softmax_flat(qh, tiles, nh, tq)

    gate = 1.0 / (1.0 + jnp.exp(-g_ref[0]))
    o_cmp = o_cmp.reshape(nh, tq, d)
    for h in range(nh):
        o_ref[0, :, h * d:(h + 1) * d] = (gate[:, 3 * h:3 * h + 1] * o_cmp[h]
                                          + gate[:, 3 * h + 1:3 * h + 2] * o_slc[h]
                                          + gate[:, 3 * h + 2:3 * h + 3] * o_win[h])


def _share_matrix_t(n_cmp, n_blk, ncp, nbp):
    c0 = np.arange(n_cmp)[None, :] * CMP_STRIDE
    j0 = np.arange(n_blk)[:, None] * SEL_BLOCK
    share = np.clip(np.minimum(c0 + CMP_LEN, j0 + SEL_BLOCK) - np.maximum(c0, j0), 0, None) / CMP_LEN
    out = np.zeros((nbp, ncp), np.float32)
    out[:n_blk, :n_cmp] = share
    return jnp.asarray(out, jnp.bfloat16)


def nsa_prompt_attention(q, gates, k_cmp, v_cmp, n_cmp, k_slc, v_slc, k_win, v_win):
    b, t, _ = q.shape
    d = HEAD_DIM
    n_blk = -(-t // SEL_BLOCK)
    ncp = -(-n_cmp // LANE) * LANE
    nbp = -(-n_blk // 16) * 16
    assert t % NSA_TK_SLC == 0 and t % NSA_TQ == 0
    bf = jnp.bfloat16
    cpad = ((0, 0), (0, ncp - k_cmp.shape[1]), (0, 0))
    kc, vc = jnp.pad(k_cmp, cpad).astype(bf), jnp.pad(v_cmp, cpad).astype(bf)
    whole = lambda n: pl.BlockSpec((1, n, d), lambda bi, qi: (bi, 0, 0))
    return pl.pallas_call(
        functools.partial(_nsa_prompt_kernel, n_cmp=n_cmp, n_blk=n_blk, n_keys=t),
        grid=(b, t // NSA_TQ),
        in_specs=[pl.BlockSpec((1, NSA_TQ, NSA_HEADS * d), lambda bi, qi: (bi, qi, 0)),
                  pl.BlockSpec((1, NSA_TQ, 3 * NSA_HEADS), lambda bi, qi: (bi, qi, 0)),
                  whole(ncp), whole(ncp), whole(t), whole(t), whole(t), whole(t),
                  pl.BlockSpec((nbp, ncp), lambda bi, qi: (0, 0))],
        out_specs=pl.BlockSpec((1, NSA_TQ, NSA_HEADS * d), lambda bi, qi: (bi, qi, 0)),
        out_shape=jax.ShapeDtypeStruct((b, t, NSA_HEADS * d), jnp.float32),
        scratch_shapes=[pltpu.VMEM((NSA_HEADS, NSA_TQ, LANE), jnp.float32),
                        pltpu.VMEM((NSA_HEADS, NSA_TQ, LANE), jnp.float32),
                        pltpu.VMEM((NSA_HEADS, NSA_TQ, d), jnp.float32)],
        compiler_params=pltpu.CompilerParams(
            dimension_semantics=("parallel", "arbitrary"), vmem_limit_bytes=VMEM_LIMIT),
        name="nsa_prompt",
    )(q, gates, kc, vc, k_slc.astype(bf), v_slc.astype(bf), k_win.astype(bf), v_win.astype(bf),
      _share_matrix_t(n_cmp, n_blk, ncp, nbp))


NSA_ROWS = 4
NSA_SEQ = 2


def _gelu_tanh(x):
    return 0.5 * x * (1.0 + jnp.tanh(math.sqrt(2.0 / math.pi) * (x + 0.044715 * (x * x * x))))


def _pad_rows(x, rows):
    if rows == x.shape[0]:
        return x
    return jnp.concatenate([x, jnp.zeros((rows - x.shape[0], x.shape[1]), x.dtype)], axis=0)


def _compress_tokens(seg_rows, pe_ref, w1_ref, w2_ref, kv, ncp):
    d, bf = HEAD_DIM, jnp.bfloat16
    first = jnp.zeros((ncp, CMP_HID), jnp.float32)
    second = jnp.zeros((ncp, CMP_HID), jnp.float32)
    for r in range(0, CMP_STRIDE, 2):
        ya, yb = seg_rows(r), seg_rows(r + 1)
        for acc_off in (0, CMP_STRIDE):
            lhs = jnp.concatenate([ya + pe_ref[kv, acc_off + r:acc_off + r + 1, :],
                                   yb + pe_ref[kv, acc_off + r + 1:acc_off + r + 2, :]], axis=1).astype(bf)
            w = w1_ref[kv, acc_off + r:acc_off + r + 2].reshape(2 * d, CMP_HID)
            y = jnp.dot(lhs, w, preferred_element_type=jnp.float32)
            if acc_off == 0:
                first = first + y
            else:
                second = second + y
    hid = first + pltpu.roll(second, ncp - 1, 0)
    return jnp.dot(_gelu_tanh(hid).astype(bf), w2_ref[kv], preferred_element_type=jnp.float32)


def _compress_prompt_kernel(rows_ref, pe_ref, w1_ref, w2_ref, cos_ref, sin_ref, kc_ref, vc_ref):
    ncp, d = kc_ref.shape[0], HEAD_DIM

    def seg_rows(kv, r):
        return rows_ref[pl.ds(NSA_ROWS * r + kv, ncp, stride=NSA_ROWS * CMP_STRIDE), :]

    kc = _compress_tokens(functools.partial(seg_rows, 0), pe_ref, w1_ref, w2_ref, 0, ncp)
    kc_ref[...] = (kc * cos_ref[...] + pltpu.roll(kc, d // 2, 1) * sin_ref[...]).astype(kc_ref.dtype)
    vc_ref[...] = _compress_tokens(functools.partial(seg_rows, 1), pe_ref, w1_ref, w2_ref, 1, ncp).astype(vc_ref.dtype)


def nsa_compress_prompt(nsa_new, pe, w1, w2):
    b, t = nsa_new.shape[:2]
    d, bf = HEAD_DIM, jnp.bfloat16
    ncp = t // CMP_STRIDE
    n_cmp = (t - CMP_LEN) // CMP_STRIDE + 1
    assert ncp % 8 == 0 and n_cmp <= ncp
    cos, sin = _rope_tables(jnp.arange(n_cmp) * CMP_STRIDE + CMP_LEN - 1, ncp)
    const = lambda *shape: pl.BlockSpec(shape, lambda i: (0,) * len(shape))
    out = pl.BlockSpec((None, ncp, d), lambda i: (i, 0, 0))
    return pl.pallas_call(
        _compress_prompt_kernel,
        grid=(b,),
        in_specs=[pl.BlockSpec((None, t * NSA_ROWS, d), lambda i: (i, 0, 0)),
                  const(2, CMP_LEN, d), const(2, CMP_LEN, d, CMP_HID), const(2, CMP_HID, d), const(ncp, d), const(ncp, d)],
        out_specs=[out, out],
        out_shape=[jax.ShapeDtypeStruct((b, ncp, d), bf)] * 2,
        compiler_params=pltpu.CompilerParams(dimension_semantics=("parallel",), vmem_limit_bytes=VMEM_LIMIT),
        name="nsa_compress",
    )(nsa_new.reshape(b, t * NSA_ROWS, d), pe, w1.reshape(2, CMP_LEN, d, CMP_HID).astype(bf), w2.astype(bf), cos, sin)


def _gated_store(o_ref, g_ref, branches, nh, d):
    gate = 1.0 / (1.0 + jnp.exp(-g_ref[...]))
    for h in range(nh):
        o_ref[:, h * d:(h + 1) * d] = sum(gate[:, 3 * h + i:3 * h + i + 1] * br[h] for i, br in enumerate(branches))


def _nsa_sample_kernel(pt_ref, q_ref, g_ref, new_ref, winc_ref, winn_ref, pe_ref, w1_ref, w2_ref, cos_ref, sin_ref,
                       share_ref, *rest, past, n_cmp, n_blk):
    del pt_ref
    all_pages, o_ref = rest[:-1], rest[-1]
    n_seq = q_ref.shape[0]
    n_pages = len(all_pages) // n_seq
    ncp = share_ref.shape[1]
    seg_per_page = PAGE_SIZE // CMP_STRIDE
    assert n_pages * seg_per_page == ncp == LANE and CMP_LEN == 2 * CMP_STRIDE

    def seg_rows(kv, r):
        return jnp.concatenate([p[pl.ds(NSA_ROWS * r + kv, seg_per_page, stride=NSA_ROWS * CMP_STRIDE), :]
                                for p in all_pages], axis=0)

    kc_all = _compress_tokens(functools.partial(seg_rows, 0), pe_ref, w1_ref, w2_ref, 0, n_seq * ncp)
    vc_all = _compress_tokens(functools.partial(seg_rows, 1), pe_ref, w1_ref, w2_ref, 1, n_seq * ncp)
    for s in range(n_seq):
        _nsa_sample_one(q_ref.at[s], g_ref.at[s], new_ref.at[s], winc_ref.at[s], winn_ref.at[s], cos_ref, sin_ref,
                        share_ref, all_pages[s * n_pages:(s + 1) * n_pages], o_ref.at[s],
                        kc_all[s * ncp:(s + 1) * ncp], vc_all[s * ncp:(s + 1) * ncp], past, n_cmp, n_blk)


def _nsa_sample_one(q_ref, g_ref, new_ref, winc_ref, winn_ref, cos_ref, sin_ref, share_ref, pages, o_ref,
                    kc, vc, past, n_cmp, n_blk):
    nh, d, pg = NSA_HEADS, HEAD_DIM, PAGE_SIZE
    tq = q_ref.shape[0]
    nbp, ncp = share_ref.shape
    bf = jnp.bfloat16
    qpos_c = past + lax.broadcasted_iota(jnp.int32, (tq, 1), 0)
    qpos_r = past + lax.broadcasted_iota(jnp.int32, (1, LANE), 1)
    lane = lax.broadcasted_iota(jnp.int32, (1, LANE), 1)
    kc = (kc * cos_ref[...] + pltpu.roll(kc, d // 2, 1) * sin_ref[...]).astype(bf)
    vc = vc.astype(bf)

    q = q_ref[...].astype(jnp.float32)
    qh = jnp.concatenate([q[:, h * d:(h + 1) * d] for h in range(nh)], axis=0).astype(bf)

    mask_c = ((lane * CMP_STRIDE + (CMP_LEN - 1) <= qpos_c) & (lane < n_cmp)).astype(jnp.float32)
    s = lax.dot_general(qh, kc, _NT, preferred_element_type=jnp.float32).reshape(nh, tq, ncp)
    sm = s + ((mask_c - 1.0) * (-NEG))[None]
    e = jnp.exp(sm - jnp.max(sm, axis=-1, keepdims=True)) * mask_c[None]
    p_cmp = e / jnp.maximum(jnp.sum(e, axis=-1, keepdims=True), 1e-30)
    o_cmp = jnp.dot(p_cmp.reshape(nh * tq, ncp).astype(bf), vc, preferred_element_type=jnp.float32).reshape(nh, tq, d)

    sel = _select_blocks(_pad_rows(jnp.sum(p_cmp, axis=0), LANE), share_ref[...], qpos_r, n_blk, LANE)

    def slc_mask(key0):
        kblk = key0 // SEL_BLOCK + jnp.right_shift(lax.broadcasted_iota(jnp.int32, (nbp, LANE), 1), 6)
        expand = (kblk == lax.broadcasted_iota(jnp.int32, (nbp, LANE), 0)).astype(bf)
        chosen = jnp.dot(sel, expand, preferred_element_type=jnp.float32)[:tq]
        return chosen * (key0 + lane <= qpos_c).astype(jnp.float32)

    def component(ref, comp, n_comp, row0, rows):
        return ref[pl.ds(n_comp * row0 + comp, rows, stride=n_comp), :]

    tiles = [(component(p, 2, NSA_ROWS, 0, pg).astype(bf), component(p, 3, NSA_ROWS, 0, pg).astype(bf), slc_mask(j * pg))
             for j, p in enumerate(pages)]
    tiles.append((_pad_rows(component(new_ref, 2, NSA_ROWS, 0, tq), LANE).astype(bf),
                  _pad_rows(component(new_ref, 3, NSA_ROWS, 0, tq), LANE).astype(bf), slc_mask(past)))
    o_slc = _softmax_flat(qh, tiles, nh, tq)

    def win_mask(key0):
        dlt = qpos_c - (key0 + lane)
        return ((dlt >= 0) & (dlt < WINDOW)).astype(jnp.float32)

    n_wc = winc_ref.shape[0] // 2
    tiles = [(component(winc_ref, 0, 2, j * LANE, LANE).astype(bf), component(winc_ref, 1, 2, j * LANE, LANE).astype(bf),
              win_mask(past - n_wc + j * LANE)) for j in range(n_wc // LANE)]
    tiles.append((_pad_rows(component(winn_ref, 0, 2, 0, tq), LANE).astype(bf),
                  _pad_rows(component(winn_ref, 1, 2, 0, tq), LANE).astype(bf), win_mask(past)))
    o_win = _softmax_flat(qh, tiles, nh, tq)

    _gated_store(o_ref, g_ref, (o_cmp, o_slc, o_win), nh, d)


def nsa_sample_attention(layer, q, gates, nsa_new, win_new, pool, win_cache, page_table, pe, w1, w2):
    b, t, _ = q.shape
    wlen = win_cache.shape[2]
    nsa_new = nsa_new.reshape(b, t * NSA_ROWS, HEAD_DIM)
    win_new = win_new.reshape(b, t * 2, HEAD_DIM)
    pool = pool.reshape(pool.shape[0], pool.shape[1], PAGE_SIZE * NSA_ROWS, HEAD_DIM)
    win_cache = win_cache.reshape(win_cache.shape[0], b, wlen * 2, HEAD_DIM)
    d, n_pages = HEAD_DIM, page_table.shape[1]
    past = n_pages * PAGE_SIZE
    n_keys = past + t
    n_cmp = (n_keys - CMP_LEN) // CMP_STRIDE + 1
    n_blk = -(-n_keys // SEL_BLOCK)
    ncp = -(-n_cmp // LANE) * LANE
    nbp = -(-n_blk // 16) * 16
    assert (n_cmp + 1) * CMP_STRIDE <= past, "compressed tokens must come from cached rows only"
    cos, sin = _rope_tables(jnp.arange(n_cmp) * CMP_STRIDE + CMP_LEN - 1, ncp)
    bf = jnp.bfloat16
    ns = NSA_SEQ
    assert b % ns == 0
    per_seq = lambda *tail: pl.BlockSpec((ns,) + tail, lambda i, pt: (i,) + (0,) * len(tail))
    const = lambda *shape: pl.BlockSpec(shape, lambda i, pt: (0,) * len(shape))
    page = lambda s, j: pl.BlockSpec((None, None, PAGE_SIZE * NSA_ROWS, d),
                                     lambda i, pt: (layer, pt[i * ns + s, j], 0, 0))
    grid_spec = pltpu.PrefetchScalarGridSpec(
        num_scalar_prefetch=1, grid=(b // ns,),
        in_specs=[per_seq(t, NSA_HEADS * d), per_seq(t, 3 * NSA_HEADS), per_seq(t * NSA_ROWS, d),
                  pl.BlockSpec((None, ns, wlen * 2, d), lambda i, pt: (layer, i, 0, 0)),
                  per_seq(t * 2, d),
                  const(2, CMP_LEN, d), const(2, CMP_LEN, d, CMP_HID), const(2, CMP_HID, d),
                  const(ncp, d), const(ncp, d), const(nbp, ncp)]
                 + [page(s, j) for s in range(ns) for j in range(n_pages)],
        out_specs=per_seq(t, NSA_HEADS * d))
    return pl.pallas_call(
        functools.partial(_nsa_sample_kernel, past=past, n_cmp=n_cmp, n_blk=n_blk),
        grid_spec=grid_spec,
        out_shape=jax.ShapeDtypeStruct((b, t, NSA_HEADS * d), jnp.float32),
        compiler_params=pltpu.CompilerParams(dimension_semantics=("parallel",), vmem_limit_bytes=VMEM_LIMIT),
        name="nsa_sample",
    )(page_table, q, gates, nsa_new, win_cache, win_new, pe, w1.reshape(2, CMP_LEN, d, CMP_HID).astype(bf),
      w2.astype(bf), cos, sin, _share_matrix_t(n_cmp, n_blk, ncp, nbp), *([pool] * (ns * n_pages)))


DSA_TQ = 128
DSA_TK = 256
INT_MIN = -2 ** 31


def _code_to_float(code):
    return pltpu.bitcast(jnp.where(code >= 0, code, code ^ 0x7FFFFFFF), jnp.float32)


def _dsa_prompt_kernel(q_ref, qi_ref, wt_ref, k_ref, v_ref, ki_ref, tri_ref, o_ref,
                       key_ref, sel_ref, m_ref, l_ref, acc_ref, *, n_keys, topk):
    nh, tq, d, tk = DSA_HEADS, DSA_TQ, HEAD_DIM, DSA_TK
    start = pl.program_id(1) * tq
    qpos_r = start + lax.broadcasted_iota(jnp.int32, (1, tq), 1)
    n_kt = jnp.minimum((start + tq - 1) // tk + 1, n_keys // tk)

    wt = wt_ref[0] * (IDX_HEADS ** -0.5 * IDX_DIM ** -0.5)

    def score_body(kt, carry):
        off = pl.multiple_of(kt * tk, tk)
        ki = ki_ref[0, pl.ds(off, tk), :]
        acc = jnp.zeros((tk, tq), jnp.float32)
        for h in range(0, IDX_HEADS, 2):
            pair = qi_ref[0, h:h + 2].reshape(2 * tq, IDX_DIM)
            dots = lax.dot_general(ki, pair, _NT, preferred_element_type=jnp.float32)
            acc = (acc + jnp.maximum(dots[:, :tq], 0.0) * wt[h:h + 1, :]
                   + jnp.maximum(dots[:, tq:], 0.0) * wt[h + 1:h + 2, :])
        kpos_c = off + lax.broadcasted_iota(jnp.int32, (tk, 1), 0)
        key_ref[kt] = jnp.where(kpos_c <= qpos_r, acc, NEG)
        return carry

    lax.fori_loop(0, n_kt, score_body, 0)

    def count(pred):
        def body(kt, part):
            hit = pred(key_ref[kt]).astype(jnp.int32)
            return part + jnp.sum(hit.reshape(tk // 8, 8, tq), axis=0)
        part = lax.fori_loop(0, n_kt, body, jnp.zeros((8, tq), jnp.int32))
        return jnp.sum(part, axis=0, keepdims=True)

    code = jnp.where(count(lambda sc: sc >= 0.0) >= topk, 0, INT_MIN).astype(jnp.int32)

    def bit_body(i, c):
        cand = c | jnp.left_shift(jnp.int32(1), 30 - i)
        cand_f = _code_to_float(cand)
        return jnp.where(count(lambda sc: sc >= cand_f) >= topk, cand, c)

    code = lax.fori_loop(0, 31, bit_body, code)
    thr = _code_to_float(code)
    take_all = code == INT_MIN
    room = (topk - count(lambda sc: sc > thr)).astype(jnp.float32)

    def sel_body(kt, seen):
        kk = key_ref[kt]
        eq = (kk == thr).astype(jnp.float32)
        prefix = jnp.dot(tri_ref[...], eq.astype(jnp.bfloat16), preferred_element_type=jnp.float32)
        keep = (kk > thr) | ((eq > 0.0) & (prefix + seen <= room)) | take_all
        kpos_c = kt * tk + lax.broadcasted_iota(jnp.int32, (tk, 1), 0)
        sel_ref[kt] = (keep & (kpos_c <= qpos_r)).astype(jnp.bfloat16)
        return seen + jnp.sum(eq, axis=0, keepdims=True)

    lax.fori_loop(0, n_kt, sel_body, jnp.zeros((1, tq), jnp.float32))

    eye = (lax.broadcasted_iota(jnp.int32, (tq, tq), 0)
           == lax.broadcasted_iota(jnp.int32, (tq, tq), 1)).astype(jnp.bfloat16)

    def sel_mask(kt):
        return lax.dot_general(eye, sel_ref[kt], _NT, preferred_element_type=jnp.float32)

    qh = jnp.concatenate([q_ref[0, :, h * d:(h + 1) * d] for h in range(nh)], axis=0)
    o = _softmax_tiles(qh, k_ref, v_ref, 0, n_kt, tk, sel_mask, m_ref, l_ref, acc_ref, nh, tq)
    for h in range(nh):
        o_ref[0, :, h * d:(h + 1) * d] = o[h]


def dsa_prompt_attention(q, q_idx, w_idx, k, v, k_idx):
    b, t, _ = q.shape
    d = HEAD_DIM
    topk = min(DSA_TOPK, t // 4)
    assert t % DSA_TK == 0 and t % DSA_TQ == 0
    bf = jnp.bfloat16
    tri = jnp.asarray(np.tril(np.ones((DSA_TK, DSA_TK), np.float32)), bf)
    whole = lambda n, w: pl.BlockSpec((1, n, w), lambda bi, qi: (bi, 0, 0))
    return pl.pallas_call(
        functools.partial(_dsa_prompt_kernel, n_keys=t, topk=topk),
        grid=(b, t // DSA_TQ),
        in_specs=[pl.BlockSpec((1, DSA_TQ, DSA_HEADS * d), lambda bi, qi: (bi, qi, 0)),
                  pl.BlockSpec((1, IDX_HEADS, DSA_TQ, IDX_DIM), lambda bi, qi: (bi, 0, qi, 0)),
                  pl.BlockSpec((1, IDX_HEADS, DSA_TQ), lambda bi, qi: (bi, 0, qi)),
                  whole(t, d), whole(t, d), whole(t, IDX_DIM),
                  pl.BlockSpec((DSA_TK, DSA_TK), lambda bi, qi: (0, 0))],
        out_specs=pl.BlockSpec((1, DSA_TQ, DSA_HEADS * d), lambda bi, qi: (bi, qi, 0)),
        out_shape=jax.ShapeDtypeStruct((b, t, DSA_HEADS * d), jnp.float32),
        scratch_shapes=[pltpu.VMEM((t // DSA_TK, DSA_TK, DSA_TQ), jnp.float32),
                        pltpu.VMEM((t // DSA_TK, DSA_TK, DSA_TQ), bf),
                        pltpu.VMEM((DSA_HEADS, DSA_TQ, LANE), jnp.float32),
                        pltpu.VMEM((DSA_HEADS, DSA_TQ, LANE), jnp.float32),
                        pltpu.VMEM((DSA_HEADS, DSA_TQ, d), jnp.float32)],
        compiler_params=pltpu.CompilerParams(
            dimension_semantics=("parallel", "arbitrary"), vmem_limit_bytes=VMEM_LIMIT),
        name="dsa_prompt",
    )(q, q_idx.astype(bf), jnp.swapaxes(w_idx, 1, 2),
      k.astype(bf), v.astype(bf), k_idx.astype(bf), tri)


SEARCH_BITS = 3


def _dsa_sample_kernel(pt_ref, q_ref, qi_ref, w_ref, new_ref, inew_ref, tri_ref, *rest, past, topk):
    del pt_ref
    n_pages = (len(rest) - 1) // 2
    kv_pages, idx_pages, o_ref = rest[:n_pages], rest[n_pages:2 * n_pages], rest[-1]
    nh, d, pg = DSA_HEADS, HEAD_DIM, PAGE_SIZE
    tq = q_ref.shape[0]
    bf = jnp.bfloat16
    qpos_c = past + lax.broadcasted_iota(jnp.int32, (tq, 1), 0)
    lane = lax.broadcasted_iota(jnp.int32, (1, LANE), 1)

    qi = qi_ref[...]
    wb = jnp.broadcast_to(w_ref[...] * (IDX_HEADS ** -0.5 * IDX_DIM ** -0.5), (IDX_HEADS * tq, LANE))

    def key_tile(ki_t, key0):
        dots = jnp.dot(qi, ki_t, preferred_element_type=jnp.float32)
        score = jnp.sum((jnp.maximum(dots, 0.0) * wb).reshape(IDX_HEADS, tq, LANE), axis=0)
        return jnp.where(key0 + lane <= qpos_c, score, NEG)

    keys = [key_tile(p[...].astype(bf), j * pg) for j, p in enumerate(idx_pages)]
    keys.append(key_tile(inew_ref[...].astype(bf), past))

    def count(pred):
        hits = pred(keys[0]).astype(jnp.int32)
        for kk in keys[1:]:
            hits = hits + pred(kk).astype(jnp.int32)
        return jnp.sum(hits, axis=-1, keepdims=True)

    def at_least(code):
        cand = _code_to_float(code)
        return (count(lambda sc: sc >= cand) >= topk).astype(jnp.int32)

    code = jnp.where(count(lambda sc: sc >= 0.0) >= topk, 0, INT_MIN).astype(jnp.int32)
    n_steps, last_bits = divmod(31, SEARCH_BITS)

    def radix_body(i, c):
        shift = 31 - SEARCH_BITS * (i + 1)
        digit = sum(at_least(c | jnp.left_shift(jnp.int32(j), shift)) for j in range(1, 2 ** SEARCH_BITS))
        return c | jnp.left_shift(digit, shift)

    code = lax.fori_loop(0, n_steps, radix_body, code)
    if last_bits:
        code = code | sum(at_least(code | j) for j in range(1, 2 ** last_bits))
    thr = _code_to_float(code)
    take_all = code == INT_MIN
    room = (topk - count(lambda sc: sc > thr)).astype(jnp.float32)
    eqs = [(kk == thr).astype(jnp.float32) for kk in keys]
    totals = [jnp.sum(e, axis=-1, keepdims=True) for e in eqs]
    stacked = jnp.concatenate(eqs + [jnp.zeros((-len(eqs) * tq % MIN_MXU_ROWS, LANE), jnp.float32)] * (
        1 if len(eqs) * tq % MIN_MXU_ROWS else 0), axis=0).astype(bf)
    prefixes = jnp.dot(stacked, tri_ref[...], preferred_element_type=jnp.float32)
    seen = jnp.zeros((tq, 1), jnp.float32)
    masks = []
    for j, (kk, e) in enumerate(zip(keys, eqs)):
        prefix = prefixes[j * tq:(j + 1) * tq]
        keep = (kk > thr) | ((e > 0.0) & (prefix + seen <= room)) | take_all
        key0 = j * pg if j < n_pages else past
        masks.append((keep & (key0 + lane <= qpos_c)).astype(jnp.float32))
        seen = seen + totals[j]

    def component(ref, comp, rows):
        return ref[pl.ds(comp, rows, stride=2), :]

    tiles = [(component(p, 0, pg).astype(bf), component(p, 1, pg).astype(bf), masks[j]) for j, p in enumerate(kv_pages)]
    tiles.append((_pad_rows(component(new_ref, 0, tq), LANE).astype(bf),
                  _pad_rows(component(new_ref, 1, tq), LANE).astype(bf), masks[n_pages]))
    q = q_ref[...].astype(jnp.float32)
    qh = jnp.concatenate([q[:, h * d:(h + 1) * d] for h in range(nh)], axis=0).astype(bf)
    o = _softmax_flat(qh, tiles, nh, tq)
    for h in range(nh):
        o_ref[:, h * d:(h + 1) * d] = o[h]


def dsa_sample_attention(layer, q, q_idx, w_idx, dsa_new, idx_new, pool, idx_pool, page_table):
    b, t, _ = q.shape
    d, n_pages = HEAD_DIM, page_table.shape[1]
    past = n_pages * PAGE_SIZE
    topk = min(DSA_TOPK, (past + t) // 4)
    pool = pool.reshape(pool.shape[0], pool.shape[1], PAGE_SIZE * 2, d)
    idx_pool_t = jnp.swapaxes(idx_pool, 2, 3)
    idx_new_t = jnp.pad(jnp.swapaxes(idx_new, 1, 2), ((0, 0), (0, 0), (0, LANE - t)))
    tri = jnp.asarray(np.triu(np.ones((LANE, LANE), np.float32)), jnp.bfloat16)
    per_seq = lambda *tail: pl.BlockSpec((None,) + tail, lambda i, pt: (i,) + (0,) * len(tail))
    kv_page = lambda j: pl.BlockSpec((None, None, PAGE_SIZE * 2, d), lambda i, pt: (layer, pt[i, j], 0, 0))
    idx_page = lambda j: pl.BlockSpec((None, None, IDX_DIM, PAGE_SIZE), lambda i, pt: (layer, pt[i, j], 0, 0))
    grid_spec = pltpu.PrefetchScalarGridSpec(
        num_scalar_prefetch=1, grid=(b,),
        in_specs=[per_seq(t, DSA_HEADS * d), per_seq(IDX_HEADS * t, IDX_DIM), per_seq(IDX_HEADS * t, 1),
                  per_seq(t * 2, d), per_seq(IDX_DIM, LANE), pl.BlockSpec((LANE, LANE), lambda i, pt: (0, 0))]
                 + [kv_page(j) for j in range(n_pages)] + [idx_page(j) for j in range(n_pages)],
        out_specs=per_seq(t, DSA_HEADS * d))
    return pl.pallas_call(
        functools.partial(_dsa_sample_kernel, past=past, topk=topk),
        grid_spec=grid_spec,
        out_shape=jax.ShapeDtypeStruct((b, t, DSA_HEADS * d), jnp.float32),
        compiler_params=pltpu.CompilerParams(dimension_semantics=("parallel",), vmem_limit_bytes=VMEM_LIMIT),
        name="dsa_sample",
    )(page_table, q, q_idx.reshape(b, IDX_HEADS * t, IDX_DIM).astype(jnp.bfloat16),
      jnp.swapaxes(w_idx, 1, 2).reshape(b, IDX_HEADS * t, 1), dsa_new.reshape(b, t * 2, d), idx_new_t, tri,
      *([pool] * n_pages), *([idx_pool_t] * n_pages))


GDN_GROUP = 256
GDN_SUB = 16
GDN_HB = 8
CONV_PAD = 8


def _hp_dot(a, b):
    bf = jnp.bfloat16
    ah, bh = a.astype(bf), b.astype(bf)
    al, bl = (a - ah.astype(jnp.float32)).astype(bf), (b - bh.astype(jnp.float32)).astype(bf)
    dot = functools.partial(jnp.dot, preferred_element_type=jnp.float32)
    return dot(ah, bh) + dot(ah, bl) + dot(al, bh)


def _unit_lower_inverse(a, row, col):
    assert GDN_SUB == 16 and GDN_CHUNK == 64
    n = range(len(a))
    eye = (row == col).astype(jnp.float32)
    sub = jnp.right_shift(row, 4) == jnp.right_shift(col, 4)
    a16 = [jnp.where(sub, a[i], 0.0) for i in n]
    t16 = [eye - a16[i] for i in n]
    power = a16
    for _ in range(3):
        power = [_hp_dot(power[i], power[i]) for i in n]
        t16 = [t16[i] + _hp_dot(t16[i], power[i]) for i in n]
    b = [_hp_dot(t16[i], a[i] - a16[i]) for i in n]
    b2 = [_hp_dot(b[i], b[i]) for i in n]
    imb = [eye - b[i] for i in n]
    left = [imb[i] + _hp_dot(imb[i], b2[i]) for i in n]
    return [_hp_dot(left[i], t16[i]) for i in n]


def _gdn_prompt_kernel(xq_ref, xk_ref, xv_ref, pq_ref, pk_ref, pv_ref, a_ref, b_ref, z_ref, cwq_ref, cwk_ref, cwv_ref,
                       alog_ref, dtb_ref, ng_ref, s0_ref, o_ref, s_ref, hist_ref):
    f32, bf = jnp.float32, jnp.bfloat16
    d, g, c = HEAD_DIM, GDN_GROUP, GDN_CHUNK
    heads = range(GDN_HB)
    lanes = [slice(h * d, (h + 1) * d) for h in heads]
    row = lax.broadcasted_iota(jnp.int32, (g, g), 0)
    col = lax.broadcasted_iota(jnp.int32, (g, g), 1)
    same = jnp.right_shift(row, 6) == jnp.right_shift(col, 6)
    incl = same & (row >= col)
    strict = same & (row > col)
    eye = (row == col).astype(f32)
    tril_b, same_b = incl.astype(bf), same.astype(bf)

    @pl.when(pl.program_id(2) == 0)
    def _():
        s_ref[...] = s0_ref[...]
        for part, p_ref in enumerate((pq_ref, pk_ref, pv_ref)):
            hist_ref[part] = p_ref[...]

    pre = a_ref[...] + dtb_ref[...]
    gate = -jnp.exp(alog_ref[...]) * (jnp.maximum(pre, 0.0) + jnp.log(1.0 + jnp.exp(-jnp.abs(pre))))
    parts = _split3_bf16(gate)
    gcum_all = sum(jnp.dot(tril_b, p, preferred_element_type=f32) for p in parts)
    gtot_all = sum(jnp.dot(same_b, p, preferred_element_type=f32) for p in parts)
    beta_all = 1.0 / (1.0 + jnp.exp(-b_ref[...]))

    def conv(part, x_ref, w_ref, h):
        win = jnp.concatenate([hist_ref[part, :, lanes[h]], x_ref[:, lanes[h]]], axis=0)
        y = sum(pltpu.roll(win, g + CONV_PAD - (CONV_PAD - GDN_CONV + 1 + i), 0)[:g] * w_ref[i:i + 1, lanes[h]]
                for i in range(GDN_CONV))
        return y / (1.0 + jnp.exp(-y))

    q = [conv(0, xq_ref, cwq_ref, h) for h in heads]
    k = [conv(1, xk_ref, cwk_ref, h) for h in heads]
    v = [conv(2, xv_ref, cwv_ref, h) for h in heads]
    for part, x_ref in enumerate((xq_ref, xk_ref, xv_ref)):
        hist_ref[part] = x_ref[g - CONV_PAD:g, :]
    q = [q[h] * lax.rsqrt(jnp.sum(q[h] * q[h], axis=-1, keepdims=True) + EPS) * (d ** -0.5) for h in heads]
    k = [k[h] * lax.rsqrt(jnp.sum(k[h] * k[h], axis=-1, keepdims=True) + EPS) for h in heads]
    gc = [gcum_all[:, h:h + 1] for h in heads]
    gl = [gtot_all[:, h:h + 1] for h in heads]
    beta = [beta_all[:, h:h + 1] for h in heads]
    g_i = [jnp.broadcast_to(gc[h], (g, g)) for h in heads]
    g_j = [jnp.sum(g_i[h] * eye, axis=0, keepdims=True) for h in heads]
    decay = [jnp.where(incl, jnp.exp(jnp.where(incl, g_i[h] - g_j[h], 0.0)), 0.0) for h in heads]
    kb = [k[h] * beta[h] for h in heads]
    k16 = [k[h].astype(bf) for h in heads]
    eg = [jnp.exp(gc[h]) for h in heads]
    a_mat = [jnp.where(strict, lax.dot_general(kb[h].astype(bf), k16[h], _NT, preferred_element_type=f32) * decay[h], 0.0)
             for h in heads]
    qk = [(lax.dot_general(q[h].astype(bf), k16[h], _NT, preferred_element_type=f32) * decay[h]).astype(bf)
          for h in heads]
    rhs = [jnp.concatenate([v[h] * beta[h], kb[h] * eg[h]], axis=1) for h in heads]
    inv = _unit_lower_inverse(a_mat, row, col)
    sol = [_hp_dot(inv[h], rhs[h]) for h in heads]
    u = [sol[h][:, :d] for h in heads]
    w16 = [sol[h][:, d:].astype(bf) for h in heads]
    q_dec = [(q[h] * eg[h]).astype(bf) for h in heads]
    k_dec = [(k[h] * jnp.exp(gl[h] - gc[h])).astype(bf) for h in heads]
    chunk_decay = [jnp.exp(gl[h]) for h in heads]
    for ci in range(g // c):
        rows = slice(ci * c, (ci + 1) * c)
        state = [s_ref[h] for h in heads]
        s16 = [state[h].astype(bf) for h in heads]
        nv16 = [(u[h][rows] - jnp.dot(w16[h][rows], s16[h], preferred_element_type=f32)).astype(bf) for h in heads]
        pad = lambda n: [jnp.zeros((n, d), bf)] if n else []
        placed = [jnp.concatenate(pad(ci * c) + [nv16[h]] + pad(g - (ci + 1) * c), axis=0) for h in heads]
        out = [jnp.dot(q_dec[h][rows], s16[h], preferred_element_type=f32)
               + jnp.dot(qk[h][rows], placed[h], preferred_element_type=f32) for h in heads]
        for h in heads:
            s_ref[h] = state[h] * chunk_decay[h][ci * c:ci * c + 1, :] + lax.dot_general(
                k_dec[h][rows], nv16[h], (((0,), (0,)), ((), ())), preferred_element_type=f32)
        for h in heads:
            zz = z_ref[rows, lanes[h]]
            normed = out[h] * lax.rsqrt(jnp.mean(out[h] * out[h], axis=-1, keepdims=True) + EPS) * ng_ref[...]
            o_ref[rows, lanes[h]] = normed * (zz / (1.0 + jnp.exp(-zz)))


def gdn_prompt(src, qkv_off, z_off, a, beta_logit, conv_prev, s0, conv_w, a_log, dt_bias, norm_g):
    b, t, _ = src.shape
    d, hb = HEAD_DIM, GDN_HB
    ng = GDN_HEADS // hb
    assert t % GDN_GROUP == 0 and GDN_HEADS % hb == 0 and qkv_off % (hb * d) == 0 and z_off % (hb * d) == 0
    qb, zb = qkv_off // (hb * d), z_off // (hb * d)
    prev = jnp.pad(conv_prev.astype(src.dtype), ((0, 0), (CONV_PAD - GDN_CONV + 1, 0), (0, 0)))
    by_group = lambda x: jnp.swapaxes(x.reshape(b, t, ng, hb), 1, 2)
    g = GDN_GROUP
    cols = lambda part: pl.BlockSpec((None, g, hb * d), lambda bi, gi, ti: (bi, ti, qb + part * ng + gi))
    hist = lambda part: pl.BlockSpec((None, CONV_PAD, hb * d), lambda bi, gi, ti: (bi, 0, part * ng + gi))
    cw = lambda part: pl.BlockSpec((GDN_CONV, hb * d), lambda bi, gi, ti: (0, part * ng + gi))
    tok = pl.BlockSpec((None, None, g, hb), lambda bi, gi, ti: (bi, gi, ti, 0))
    head_const = pl.BlockSpec((None, 1, hb), lambda bi, gi, ti: (gi, 0, 0))
    state = pl.BlockSpec((None, hb, d, d), lambda bi, gi, ti: (bi, gi, 0, 0))
    return pl.pallas_call(
        _gdn_prompt_kernel,
        grid=(b, ng, t // g),
        in_specs=[cols(0), cols(1), cols(2), hist(0), hist(1), hist(2), tok, tok,
                  pl.BlockSpec((None, g, hb * d), lambda bi, gi, ti: (bi, ti, zb + gi)),
                  cw(0), cw(1), cw(2), head_const, head_const,
                  pl.BlockSpec((1, d), lambda bi, gi, ti: (0, 0)), state],
        out_specs=[pl.BlockSpec((None, g, hb * d), lambda bi, gi, ti: (bi, ti, gi)), state],
        out_shape=[jax.ShapeDtypeStruct((b, t, W_C), jnp.float32),
                   jax.ShapeDtypeStruct((b, GDN_HEADS, d, d), jnp.float32)],
        scratch_shapes=[pltpu.VMEM((3, CONV_PAD, hb * d), jnp.float32)],
        compiler_params=pltpu.CompilerParams(
            dimension_semantics=("parallel", "parallel", "arbitrary"), vmem_limit_bytes=VMEM_LIMIT),
        name="gdn_prompt",
    )(src, src, src, prev, prev, prev, by_group(a), by_group(beta_logit), src, conv_w, conv_w, conv_w,
      a_log.reshape(ng, 1, hb).astype(jnp.float32), dt_bias.reshape(ng, 1, hb).astype(jnp.float32),
      norm_g.reshape(1, d).astype(jnp.float32), s0)


def _gdn_decode_kernel(x_ref, prev_ref, cw_ref, a_ref, b_ref, z_ref, alog_ref, dtb_ref, ng_ref, s0_ref, o_ref, s_ref):
    f32, bf = jnp.float32, jnp.bfloat16
    d, t = HEAD_DIM, x_ref.shape[0]
    heads = range(GDN_HEADS)
    rows = max(t, MIN_MXU_ROWS)
    mx = lambda x: _pad_rows(x, rows).astype(bf)
    win = jnp.concatenate([prev_ref[...], x_ref[...]], axis=0)
    n = win.shape[0]
    y = sum(pltpu.roll(win, n - (CONV_PAD - GDN_CONV + 1 + i), 0)[:t] * cw_ref[i:i + 1, :] for i in range(GDN_CONV))
    y = y / (1.0 + jnp.exp(-y))
    pre = a_ref[...] + dtb_ref[...]
    gate = -jnp.exp(alog_ref[...]) * (jnp.maximum(pre, 0.0) + jnp.log(1.0 + jnp.exp(-jnp.abs(pre))))
    beta_all = 1.0 / (1.0 + jnp.exp(-b_ref[...]))
    run, gcum_rows = None, []
    for ti in range(t):
        run = gate[ti:ti + 1, :] if run is None else run + gate[ti:ti + 1, :]
        gcum_rows.append(run)
    gcum_all = jnp.concatenate(gcum_rows, axis=0)
    ri = lax.broadcasted_iota(jnp.int32, (t, t), 0)
    ci = lax.broadcasted_iota(jnp.int32, (t, t), 1)
    eye = (ri == ci).astype(f32)

    q = [y[:, h * d:(h + 1) * d] for h in heads]
    k = [y[:, W_C + h * d:W_C + (h + 1) * d] for h in heads]
    v = [y[:, 2 * W_C + h * d:2 * W_C + (h + 1) * d] for h in heads]
    q = [q[h] * lax.rsqrt(jnp.sum(q[h] * q[h], axis=-1, keepdims=True) + EPS) * (d ** -0.5) for h in heads]
    k = [k[h] * lax.rsqrt(jnp.sum(k[h] * k[h], axis=-1, keepdims=True) + EPS) for h in heads]
    gc = [gcum_all[:, h:h + 1] for h in heads]
    gr = [jnp.sum(jnp.broadcast_to(gc[h], (t, t)) * eye, axis=0, keepdims=True) for h in heads]
    beta = [beta_all[:, h:h + 1] for h in heads]
    kb = [k[h] * beta[h] for h in heads]
    k16 = [mx(k[h]) for h in heads]
    dec = [jnp.where(ri >= ci, jnp.exp(jnp.where(ri >= ci, gc[h] - gr[h], 0.0)), 0.0) for h in heads]
    dec_t = [jnp.where(ci >= ri, jnp.exp(jnp.where(ci >= ri, gr[h] - gc[h], 0.0)), 0.0) for h in heads]
    a_t = [jnp.where(ci > ri, lax.dot_general(k16[h], mx(kb[h]), _NT, preferred_element_type=f32)[:t, :t] * dec_t[h], 0.0)
           for h in heads]
    qk = [lax.dot_general(mx(q[h]), k16[h], _NT, preferred_element_type=f32)[:t, :t] * dec[h] for h in heads]
    eg = [jnp.exp(gc[h]) for h in heads]
    sol = [jnp.concatenate([v[h] * beta[h], kb[h] * eg[h]], axis=1) for h in heads]
    row = lax.broadcasted_iota(jnp.int32, (t, 1), 0)
    for i in range(1, t):
        upd = [sol[h][i:i + 1, :] - jnp.sum(a_t[h][:, i:i + 1] * sol[h], axis=0, keepdims=True) for h in heads]
        sol = [jnp.where(row == i, upd[h], sol[h]) for h in heads]
    u = [sol[h][:, :d] for h in heads]
    glast = [gc[h][t - 1:t, :] for h in heads]
    state = [s0_ref[h] for h in heads]
    s16 = [state[h].astype(bf) for h in heads]
    both = [jnp.dot(jnp.concatenate([mx(sol[h][:, d:]), mx(q[h] * eg[h])], axis=0), s16[h], preferred_element_type=f32)
            for h in heads]
    nv16 = [mx(u[h] - both[h][:t]) for h in heads]
    qk16 = [_pad_rows(jnp.concatenate([qk[h], jnp.zeros((t, rows - t), f32)], axis=1), rows).astype(bf) for h in heads]
    out = [both[h][rows:rows + t] + jnp.dot(qk16[h], nv16[h], preferred_element_type=f32)[:t] for h in heads]
    k_dec = [mx(k[h] * jnp.exp(glast[h] - gc[h])) for h in heads]
    for h in heads:
        s_ref[h] = state[h] * jnp.exp(glast[h]) + lax.dot_general(k_dec[h], nv16[h], (((0,), (0,)), ((), ())),
                                                                  preferred_element_type=f32)
        zz = z_ref[:, h * d:(h + 1) * d]
        normed = out[h] * lax.rsqrt(jnp.mean(out[h] * out[h], axis=-1, keepdims=True) + EPS) * ng_ref[...]
        o_ref[:, h * d:(h + 1) * d] = normed * (zz / (1.0 + jnp.exp(-zz)))


def gdn_decode(layer, src, qkv_off, z_off, a, beta_logit, conv_prev, s0, conv_w, a_log, dt_bias, norm_g):
    b, t, _ = src.shape
    d, h = HEAD_DIM, GDN_HEADS
    assert qkv_off % (3 * W_C) == 0 and z_off % W_C == 0
    prev = jnp.pad(conv_prev.astype(src.dtype), ((0, 0), (CONV_PAD - GDN_CONV + 1, 0), (0, 0)))
    per_seq = lambda rows, width, blk: pl.BlockSpec((None, rows, width), lambda i: (i, 0, blk))
    const = lambda *shape: pl.BlockSpec(shape, lambda i: (0,) * len(shape))
    return pl.pallas_call(
        _gdn_decode_kernel,
        grid=(b,),
        in_specs=[per_seq(t, 3 * W_C, qkv_off // (3 * W_C)), per_seq(CONV_PAD, 3 * W_C, 0), const(GDN_CONV, 3 * W_C),
                  per_seq(t, h, 0), per_seq(t, h, 0), per_seq(t, W_C, z_off // W_C),
                  const(1, h), const(1, h), const(1, d),
                  pl.BlockSpec((None, None, h, d, d), lambda i: (layer, i, 0, 0, 0))],
        out_specs=[per_seq(t, W_C, 0), pl.BlockSpec((None, h, d, d), lambda i: (i, 0, 0, 0))],
        out_shape=[jax.ShapeDtypeStruct((b, t, W_C), jnp.float32), jax.ShapeDtypeStruct((b, h, d, d), jnp.float32)],
        compiler_params=pltpu.CompilerParams(dimension_semantics=("parallel",), vmem_limit_bytes=VMEM_LIMIT),
        name="gdn_decode",
    )(src, prev, conv_w, a, beta_logit, src, a_log.reshape(1, h).astype(jnp.float32),
      dt_bias.reshape(1, h).astype(jnp.float32), norm_g.reshape(1, d).astype(jnp.float32), s0)


MEM_TQ = 256
MIN_MXU_ROWS = 16


def _mem_attn_kernel(q_ref, kv_ref, o_ref):
    nh, d, bf = MEM_HEADS, HEAD_DIM, jnp.bfloat16
    tq = q_ref.shape[0]
    m = kv_ref.shape[0] // (2 * nh)
    rows = max(tq, MIN_MXU_ROWS)
    for h in range(nh):
        k = kv_ref[pl.ds(h, m, stride=2 * nh), :].astype(bf)
        v = kv_ref[pl.ds(nh + h, m, stride=2 * nh), :].astype(bf)
        q = _pad_rows(q_ref[:, h * d:(h + 1) * d], rows).astype(bf)
        s = lax.dot_general(q, k, _NT, preferred_element_type=jnp.float32) * (d ** -0.5)
        e = jnp.exp(s - jnp.max(s, axis=-1, keepdims=True))
        p = e / jnp.sum(e, axis=-1, keepdims=True)
        o_ref[:, h * d:(h + 1) * d] = jnp.dot(p.astype(bf), v, preferred_element_type=jnp.float32)[:tq]


def mem_attention_core(q, mem_kv, layer):
    b, t, w = q.shape
    m = mem_kv.shape[2]
    tq = min(MEM_TQ, t)
    assert t % tq == 0
    kv = mem_kv.reshape(mem_kv.shape[0], b, m * 2 * MEM_HEADS, HEAD_DIM)
    return pl.pallas_call(
        _mem_attn_kernel,
        grid=(b, t // tq),
        in_specs=[pl.BlockSpec((None, tq, w), lambda bi, qi: (bi, qi, 0)),
                  pl.BlockSpec((None, None, m * 2 * MEM_HEADS, HEAD_DIM), lambda bi, qi: (layer, bi, 0, 0))],
        out_specs=pl.BlockSpec((None, tq, w), lambda bi, qi: (bi, qi, 0)),
        out_shape=jax.ShapeDtypeStruct((b, t, w), jnp.float32),
        compiler_params=pltpu.CompilerParams(
            dimension_semantics=("parallel", "arbitrary"), vmem_limit_bytes=VMEM_LIMIT),
        name="mem_attn",
    )(q, kv)


def rms_norm(x, g):
    xf = x.astype(jnp.float32)
    y = xf * lax.rsqrt(jnp.mean(xf * xf, axis=-1, keepdims=True) + EPS)
    return (y * g.astype(jnp.float32)).astype(x.dtype)


def mem_attention(x, gain, mem_kv, mem_layer, w_q, w_o, layer):
    o = mem_attention_core(dense(x, w_q, layer, gain=gain), mem_kv, mem_layer)
    return dense(o, w_o, layer, resid=x)


def run_group(x, mem, cache, p):
    prompt = cache is None
    b, t, _ = x.shape
    past = 0 if prompt else cache['page_table'].shape[1] * cache['nsa_kv'].shape[2]
    names = ('nsa_kv', 'dsa_kv', 'idx_k', 'win_kv', 'gdn', 'conv') + (('mem_kv',) if prompt else ())
    out = {nm: [] for nm in names}
    bg, tg = (b, t) if prompt else (1, b * t)
    row_pos = past + jnp.arange(tg) % t
    for l in range(DEPTH):
        hproj = dense(x, p['w_in'], l, gain=p['norm_mix_g'][l])
        (q_a, q_b, q_i, nsa_new, win_new, dsa_new, idx_new,
         k_slc, v_slc, k_win, v_win, k_dsa, v_dsa, k_idx) = rope_split(hproj.reshape(bg, tg, PROJ_WIDTH), row_pos)
        nsa_new = nsa_new.reshape(b, t, NSA_ROWS, HEAD_DIM)
        win_new = win_new.reshape(b, t, 2, HEAD_DIM)
        dsa_new = dsa_new.reshape(b, t, 2, HEAD_DIM)
        idx_new = idx_new.reshape(b, t, IDX_DIM)
        g_a, w_i, a_c, b_c = (small_col(hproj, nm) for nm in ('g_a', 'w_i', 'a_c', 'b_c'))
        qkv_off, z_off = PROJ_OFF['qkv_c'], PROJ_OFF['z_c']
        qkv_c = hproj[..., qkv_off:qkv_off + 3 * W_C]
        if prompt:
            conv_prev = jnp.zeros((b, GDN_CONV - 1, 3 * W_C), x.dtype)
            s0 = jnp.zeros((b, GDN_HEADS, HEAD_DIM, HEAD_DIM), jnp.float32)
            mem_kv = dense(mem, p['w_mem_kv'], l, gain=p['mem_norm_g'][l]).reshape(b, -1, 2, MEM_HEADS, HEAD_DIM)
            k_cmp, v_cmp = nsa_compress_prompt(nsa_new, p['nsa_cmp_pe'][l], p['nsa_cmp_w1'][l], p['nsa_cmp_w2'][l])
            o_a = nsa_prompt_attention(q_a, g_a, k_cmp, v_cmp, (t - CMP_LEN) // CMP_STRIDE + 1,
                                       k_slc, v_slc, k_win, v_win)
            o_b = dsa_prompt_attention(q_b, q_i, w_i, k_dsa, v_dsa, k_idx)
            win_out = win_new[:, -min(WINDOW, t):]
            o_c, s_new = gdn_prompt(hproj, qkv_off, z_off, a_c, b_c, conv_prev, s0, p['gdn_conv_w'][l],
                                    p['gdn_a_log'][l], p['gdn_dt_bias'][l], p['gdn_norm_g'][l])
            conv_new = jnp.concatenate([conv_prev, qkv_c[:, -(GDN_CONV - 1):]], axis=1)[:, -(GDN_CONV - 1):]
        else:
            pt = cache['page_table']
            conv_prev, s0 = cache['conv'][l], cache['gdn'][l]
            mem_kv = None
            q_i = jnp.swapaxes(q_i.reshape(IDX_HEADS, b, t, IDX_DIM), 0, 1)
            o_a = nsa_sample_attention(l, q_a.reshape(b, t, W_A), g_a, nsa_new, win_new, cache['nsa_kv'], cache['win_kv'],
                                       pt, p['nsa_cmp_pe'][l], p['nsa_cmp_w1'][l], p['nsa_cmp_w2'][l])
            o_b = dsa_sample_attention(l, q_b.reshape(b, t, W_B), q_i, w_i, dsa_new, idx_new, cache['dsa_kv'],
                                       cache['idx_k'], pt)
            win_out = win_new
            o_c, s_new = gdn_decode(l, hproj, qkv_off, z_off, a_c, b_c, conv_prev, cache['gdn'], p['gdn_conv_w'][l],
                                    p['gdn_a_log'][l], p['gdn_dt_bias'][l], p['gdn_norm_g'][l])
            conv_new = jnp.concatenate([conv_prev, qkv_c], axis=1)[:, -(GDN_CONV - 1):]
        mix = branch_mix(o_a.reshape(b * t, W_A), o_b.reshape(b * t, W_B), o_c.reshape(b * t, W_C),
                         hproj.reshape(b * t, PROJ_WIDTH), PROJ_OFF['merge'],
                         p['w_branch_a'], p['w_branch_b'], p['w_branch_c'], l)
        x = dense(mix, p['w_mix_out'], l, resid=x).reshape(b, t, D_MODEL)
        mem_src, mem_layer = (mem_kv[None], 0) if prompt else (cache['mem_kv'], l)
        x = mem_attention(x, p['norm_mem_g'][l], mem_src, mem_layer, p['w_mem_q'], p['w_mem_o'], l)
        x = dense(ffn_act(x, p['norm_ffn_g'][l], p['w_ffn_gate'], p['w_ffn_up'], l), p['w_ffn_down'], l, resid=x)
        out['nsa_kv'].append(nsa_new)
        out['dsa_kv'].append(dsa_new)
        out['idx_k'].append(idx_new)
        out['win_kv'].append(win_out)
        out['gdn'].append(s_new)
        out['conv'].append(conv_new)
        if prompt:
            out['mem_kv'].append(mem_kv)
    y = rms_norm(x, p['norm_final_g'])
    out = {nm: jnp.stack(v) for nm, v in out.items()}
    if not prompt:
        wlen = cache['win_kv'].shape[2]
        out['win_kv'] = jnp.concatenate([cache['win_kv'], out['win_kv']], axis=2)[:, :, -wlen:]
    return y, out


def kernel(x_prompt, x_sample, mem_prompt, cache_nsa_kv, cache_dsa_kv, cache_dsa_idx_k, cache_win_kv, cache_mem_kv, state_gdn, state_conv, page_table, norm_mix_g, w_in, nsa_cmp_pe, nsa_cmp_w1, nsa_cmp_w2, gdn_conv_w, gdn_a_log, gdn_dt_bias, gdn_norm_g, w_branch_a, w_branch_b, w_branch_c, w_mix_out, norm_mem_g, mem_norm_g, w_mem_q, w_mem_kv, w_mem_o, norm_ffn_g, w_ffn_gate, w_ffn_up, w_ffn_down, norm_final_g):
    p = dict(norm_mix_g=norm_mix_g, w_in=w_in, nsa_cmp_pe=nsa_cmp_pe, nsa_cmp_w1=nsa_cmp_w1,
             nsa_cmp_w2=nsa_cmp_w2, gdn_conv_w=gdn_conv_w, gdn_a_log=gdn_a_log, gdn_dt_bias=gdn_dt_bias,
             gdn_norm_g=gdn_norm_g, w_branch_a=w_branch_a, w_branch_b=w_branch_b, w_branch_c=w_branch_c,
             w_mix_out=w_mix_out, norm_mem_g=norm_mem_g, mem_norm_g=mem_norm_g, w_mem_q=w_mem_q,
             w_mem_kv=w_mem_kv, w_mem_o=w_mem_o, norm_ffn_g=norm_ffn_g, w_ffn_gate=w_ffn_gate,
             w_ffn_up=w_ffn_up, w_ffn_down=w_ffn_down, norm_final_g=norm_final_g)
    for nm in ('w_branch_a', 'w_branch_b', 'w_branch_c', 'w_mix_out', 'w_mem_q', 'w_mem_kv', 'w_mem_o',
               'w_ffn_gate', 'w_ffn_up', 'w_ffn_down'):
        p[nm] = p[nm].astype(jnp.bfloat16)
    p['w_in'] = permute_w_in(w_in.astype(jnp.bfloat16))
    cache = dict(nsa_kv=cache_nsa_kv, dsa_kv=cache_dsa_kv, idx_k=cache_dsa_idx_k, win_kv=cache_win_kv,
                 mem_kv=cache_mem_kv, gdn=state_gdn, conv=state_conv, page_table=page_table)
    y_prompt, sp = run_group(x_prompt, mem_prompt, None, p)
    y_sample, ss = run_group(x_sample, None, cache, p)
    return (y_prompt, y_sample,
            sp['nsa_kv'], sp['dsa_kv'], sp['idx_k'], sp['win_kv'], sp['gdn'], sp['conv'], sp['mem_kv'],
            ss['nsa_kv'], ss['dsa_kv'], ss['idx_k'], ss['win_kv'], ss['gdn'], ss['conv'])
```

```python
import functools
import math

import jax
import jax.numpy as jnp
import numpy as np
from jax import lax
from jax.experimental import pallas as pl
from jax.experimental.pallas import tpu as pltpu

D_MODEL = 2048
DEPTH = 2
PAGE_SIZE = 128
HEAD_DIM = 128
NSA_HEADS = D_MODEL // (4 * HEAD_DIM)
DSA_HEADS = D_MODEL // (4 * HEAD_DIM)
GDN_HEADS = D_MODEL // (2 * HEAD_DIM)
W_A = NSA_HEADS * HEAD_DIM
W_B = DSA_HEADS * HEAD_DIM
W_C = GDN_HEADS * HEAD_DIM
CMP_LEN = 32
CMP_STRIDE = 16
CMP_HID = 2 * HEAD_DIM
SEL_BLOCK = 64
SEL_COUNT = 16
WINDOW = 512
IDX_HEADS = 16
IDX_DIM = 64
DSA_TOPK = 256
GDN_CONV = 4
GDN_CHUNK = 64
MEM_HEADS = 4
ROPE_THETA = 10000.0
Q_BLOCK = 128
EPS = 1e-6
NEG = -1e30
FORCED = 1e9
IN_SPLITS = (W_A, 6 * HEAD_DIM, 3 * NSA_HEADS,
             W_B, 2 * HEAD_DIM, IDX_HEADS * IDX_DIM, IDX_HEADS, IDX_DIM,
             3 * W_C, GDN_HEADS, GDN_HEADS, W_C,
             3 * D_MODEL)

LANE = 128
VMEM_LIMIT = 48 * 1024 * 1024


DENSE_VMEM_BUDGET = 40 * 1024 * 1024
NORM_ROWS = 128


def _divisor_tiles(n, cap):
    return [t for t in range(min(cap, n), 0, -LANE) if t % LANE == 0 and n % t == 0]


def _dense_tiles(m, n, per_row_bytes, per_col_bytes, per_out_bytes):
    for tm in (1024, 512, 256, 128):
        if m % tm:
            continue
        for tn in _divisor_tiles(n, 1024):
            if tm * per_row_bytes + tn * per_col_bytes + tm * tn * per_out_bytes <= DENSE_VMEM_BUDGET:
                return tm, tn
    raise ValueError("no dense tile fits VMEM")


def _stage_rows(x_ref, g_ref, xn_ref):
    def body(r, carry):
        rows = pl.ds(pl.multiple_of(r * NORM_ROWS, NORM_ROWS), NORM_ROWS)
        xf = x_ref[rows, :].astype(jnp.float32)
        if g_ref is not None:
            xf = xf * lax.rsqrt(jnp.mean(xf * xf, axis=-1, keepdims=True) + EPS) * g_ref[...]
        xn_ref[rows, :] = xf.astype(jnp.bfloat16)
        return carry
    lax.fori_loop(0, x_ref.shape[0] // NORM_ROWS, body, 0)


def _dense_kernel(*refs, has_gain, has_resid, staged):
    it = iter(refs)
    x_ref = next(it)
    g_ref = next(it) if has_gain else None
    w_ref = next(it)
    r_ref = next(it) if has_resid else None
    o_ref = next(it)
    xn_ref = next(it) if staged else x_ref

    if staged:
        @pl.when(pl.program_id(1) == 0)
        def _():
            _stage_rows(x_ref, g_ref, xn_ref)

    acc = jnp.dot(xn_ref[...], w_ref[...], preferred_element_type=jnp.float32)
    if has_resid:
        acc = acc + r_ref[...]
    o_ref[...] = acc.astype(o_ref.dtype)


def _weight_spec(w, layer, rows, tn):
    assert w.shape[1] == rows
    return pl.BlockSpec((None, rows, tn), lambda i, j: (layer, 0, j))


def dense(x, w, layer, gain=None, resid=None, out_dtype=jnp.float32):
    lead, kdim, n = x.shape[:-1], x.shape[-1], w.shape[-1]
    x2 = x.reshape(-1, kdim)
    m = x2.shape[0]
    ob = jnp.dtype(out_dtype).itemsize
    staged = gain is not None or x2.dtype != jnp.bfloat16
    tm, tn = _dense_tiles(m, n, per_row_bytes=kdim * (2 * x2.dtype.itemsize + (2 if staged else 0)),
                          per_col_bytes=kdim * 2 * 2, per_out_bytes=2 * ob + (8 if resid is not None else 0))
    args, specs = [x2], [pl.BlockSpec((tm, kdim), lambda i, j: (i, 0))]
    if gain is not None:
        args.append(gain.reshape(1, kdim).astype(jnp.float32))
        specs.append(pl.BlockSpec((1, kdim), lambda i, j: (0, 0)))
    args.append(w)
    specs.append(_weight_spec(w, layer, kdim, tn))
    if resid is not None:
        args.append(resid.reshape(m, n))
        specs.append(pl.BlockSpec((tm, tn), lambda i, j: (i, j)))
    out = pl.pallas_call(
        functools.partial(_dense_kernel, has_gain=gain is not None, has_resid=resid is not None, staged=staged),
        grid=(m // tm, n // tn),
        in_specs=specs,
        out_specs=pl.BlockSpec((tm, tn), lambda i, j: (i, j)),
        out_shape=jax.ShapeDtypeStruct((m, n), out_dtype),
        scratch_shapes=[pltpu.VMEM((tm, kdim), jnp.bfloat16)] if staged else [],
        compiler_params=pltpu.CompilerParams(
            dimension_semantics=("parallel", "arbitrary"), vmem_limit_bytes=VMEM_LIMIT),
        name="dense",
    )(*args)
    return out.reshape(*lead, n)


def _ffn_act_kernel(x_ref, g_ref, wg_ref, wu_ref, o_ref, xn_ref):
    @pl.when(pl.program_id(1) == 0)
    def _():
        _stage_rows(x_ref, g_ref, xn_ref)

    xn = xn_ref[...]
    a = jnp.dot(xn, wg_ref[...], preferred_element_type=jnp.float32)
    u = jnp.dot(xn, wu_ref[...], preferred_element_type=jnp.float32)
    o_ref[...] = (a / (1.0 + jnp.exp(-a)) * u).astype(o_ref.dtype)


def ffn_act(x, gain, w_gate, w_up, layer):
    lead, kdim, n = x.shape[:-1], x.shape[-1], w_gate.shape[-1]
    x2 = x.reshape(-1, kdim)
    m = x2.shape[0]
    tm, tn = _dense_tiles(m, n, per_row_bytes=kdim * (2 * 4 + 2), per_col_bytes=2 * kdim * 2 * 2, per_out_bytes=2 * 2)
    out = pl.pallas_call(
        _ffn_act_kernel,
        grid=(m // tm, n // tn),
        in_specs=[pl.BlockSpec((tm, kdim), lambda i, j: (i, 0)),
                  pl.BlockSpec((1, kdim), lambda i, j: (0, 0)),
                  _weight_spec(w_gate, layer, kdim, tn), _weight_spec(w_up, layer, kdim, tn)],
        out_specs=pl.BlockSpec((tm, tn), lambda i, j: (i, j)),
        out_shape=jax.ShapeDtypeStruct((m, n), jnp.bfloat16),
        scratch_shapes=[pltpu.VMEM((tm, kdim), jnp.bfloat16)],
        compiler_params=pltpu.CompilerParams(
            dimension_semantics=("parallel", "arbitrary"), vmem_limit_bytes=VMEM_LIMIT),
        name="ffn_act",
    )(x2, gain.reshape(1, kdim).astype(jnp.float32), w_gate, w_up)
    return out.reshape(*lead, n)


def _branch_mix_kernel(oa_ref, ob_ref, oc_ref, ma_ref, mb_ref, mc_ref, wa_ref, wb_ref, wc_ref, o_ref, xn_ref):
    @pl.when(pl.program_id(1) == 0)
    def _():
        off = 0
        for src in (oa_ref, ob_ref, oc_ref):
            xn_ref[:, off:off + src.shape[1]] = src[...].astype(jnp.bfloat16)
            off += src.shape[1]

    def gated(m_ref, w_ref, lo, hi):
        y = jnp.dot(xn_ref[:, lo:hi], w_ref[...], preferred_element_type=jnp.float32)
        return y / (1.0 + jnp.exp(-m_ref[...]))

    o_ref[...] = (gated(ma_ref, wa_ref, 0, W_A) + gated(mb_ref, wb_ref, W_A, W_A + W_B)
                  + gated(mc_ref, wc_ref, W_A + W_B, W_A + W_B + W_C)).astype(o_ref.dtype)


def branch_mix(o_a, o_b, o_c, merge, merge_off, w_a, w_b, w_c, layer):
    m, n = merge.shape[0], w_a.shape[-1]
    ktot = W_A + W_B + W_C
    tm, tn = _dense_tiles(m, n, per_row_bytes=ktot * (2 * 4 + 2), per_col_bytes=ktot * 2 * 2, per_out_bytes=2 * 2 + 3 * 8)
    nb = n // tn
    assert merge_off % tn == 0
    mb = merge_off // tn
    row = lambda width: pl.BlockSpec((tm, width), lambda i, j: (i, 0))
    return pl.pallas_call(
        _branch_mix_kernel,
        grid=(m // tm, nb),
        in_specs=[row(W_A), row(W_B), row(W_C),
                  pl.BlockSpec((tm, tn), lambda i, j: (i, mb + j)),
                  pl.BlockSpec((tm, tn), lambda i, j: (i, mb + j + nb)),
                  pl.BlockSpec((tm, tn), lambda i, j: (i, mb + j + 2 * nb)),
                  _weight_spec(w_a, layer, W_A, tn), _weight_spec(w_b, layer, W_B, tn),
                  _weight_spec(w_c, layer, W_C, tn)],
        out_specs=pl.BlockSpec((tm, tn), lambda i, j: (i, j)),
        out_shape=jax.ShapeDtypeStruct((m, n), jnp.bfloat16),
        scratch_shapes=[pltpu.VMEM((tm, ktot), jnp.bfloat16)],
        compiler_params=pltpu.CompilerParams(
            dimension_semantics=("parallel", "arbitrary"), vmem_limit_bytes=VMEM_LIMIT),
        name="branch_mix",
    )(o_a, o_b, o_c, merge, merge, merge, w_a, w_b, w_c)


REF_COLS = tuple(zip(('q_a', 'kv_a', 'g_a', 'q_b', 'kv_b', 'q_i', 'w_i', 'k_i', 'qkv_c', 'a_c', 'b_c', 'z_c', 'merge'),
                     IN_SPLITS))
SMALL_COLS = ('k_i', 'g_a', 'w_i', 'a_c', 'b_c')
WIDE_COLS = ('q_a', 'kv_a', 'q_b', 'kv_b', 'q_i', 'qkv_c', 'z_c', 'merge')


def _layout():
    width = dict(REF_COLS)
    assert all(width[n] % LANE == 0 for n in WIDE_COLS) and sum(width[n] for n in SMALL_COLS) <= LANE
    off, pos = {}, 0
    for n in WIDE_COLS:
        off[n] = pos
        pos += width[n]
    off['small'] = pos
    small, spos = {}, 0
    for n in SMALL_COLS:
        small[n] = (spos, width[n])
        spos += width[n]
    return off, small, pos + LANE


PROJ_OFF, SMALL_OFF, PROJ_WIDTH = _layout()
ROPE_COLS = PROJ_OFF['qkv_c']
ROPE_TM = 512


def _column_moves():
    ref_off, pos = {}, 0
    for n, wd in REF_COLS:
        ref_off[n] = pos
        pos += wd
    width = dict(REF_COLS)
    moves = [(ref_off[n], PROJ_OFF[n], width[n]) for n in WIDE_COLS]
    moves += [(ref_off[n], PROJ_OFF['small'] + SMALL_OFF[n][0], width[n]) for n in SMALL_COLS]
    return moves, PROJ_OFF['small'] + sum(width[n] for n in SMALL_COLS)


PERMUTE_ROWS = 128


def _permute_kernel(w_ref, o_ref):
    moves, used = _column_moves()
    for src, dst, wd in moves:
        o_ref[:, dst:dst + wd] = w_ref[:, src:src + wd].astype(o_ref.dtype)
    o_ref[:, used:] = jnp.zeros((o_ref.shape[0], o_ref.shape[1] - used), o_ref.dtype)


def permute_w_in(w):
    nl, kdim, n_in = w.shape
    return pl.pallas_call(
        _permute_kernel,
        grid=(nl, kdim // PERMUTE_ROWS),
        in_specs=[pl.BlockSpec((None, PERMUTE_ROWS, n_in), lambda l, i: (l, i, 0))],
        out_specs=pl.BlockSpec((None, PERMUTE_ROWS, PROJ_WIDTH), lambda l, i: (l, i, 0)),
        out_shape=jax.ShapeDtypeStruct((nl, kdim, PROJ_WIDTH), jnp.bfloat16),
        compiler_params=pltpu.CompilerParams(
            dimension_semantics=("parallel", "parallel"), vmem_limit_bytes=VMEM_LIMIT),
        name="permute_w_in",
    )(w)


def small_col(hproj, name):
    lo, wd = SMALL_OFF[name]
    return hproj[..., PROJ_OFF['small'] + lo:PROJ_OFF['small'] + lo + wd]


def _rope_tables(pos, rows, head_dim=HEAD_DIM):
    half = head_dim // 2
    inv = ROPE_THETA ** (-jnp.arange(half, dtype=jnp.float32) / half)
    ang = pos.astype(jnp.float32)[:, None] * inv
    cos, sin = jnp.cos(ang), jnp.sin(ang)
    reps = LANE // head_dim
    pad = ((0, rows - pos.shape[0]), (0, 0))
    return (jnp.pad(jnp.tile(jnp.concatenate([cos, cos], axis=-1), (1, reps)), pad),
            jnp.pad(jnp.tile(jnp.concatenate([-sin, sin], axis=-1), (1, reps)), pad))


def _rope_split_kernel(h_ref, small_ref, c128_ref, s128_ref, c64_ref, s64_ref,
                       qa_ref, qb_ref, qi_ref, nsa_ref, win_ref, dsa_ref, idx_ref,
                       kslc_ref, vslc_ref, kwin_ref, vwin_ref, kdsa_ref, vdsa_ref, kidx_ref):
    d, bf = HEAD_DIM, jnp.bfloat16
    c128, s128, c64, s64 = c128_ref[...], s128_ref[...], c64_ref[...], s64_ref[...]
    first_half = (lax.broadcasted_iota(jnp.int32, (1, LANE), 1) & (IDX_DIM - 1)) < IDX_DIM // 2

    def rope128(x):
        return x * c128 + pltpu.roll(x, d // 2, 1) * s128

    def rope64(x):
        rot = jnp.where(first_half, pltpu.roll(x, LANE - IDX_DIM // 2, 1), pltpu.roll(x, IDX_DIM // 2, 1))
        return x * c64 + rot * s64

    col = lambda name, i: h_ref[:, PROJ_OFF[name] + i * d:PROJ_OFF[name] + (i + 1) * d]
    for i in range(NSA_HEADS):
        qa_ref[:, i * d:(i + 1) * d] = (rope128(col('q_a', i)) * (d ** -0.5)).astype(bf)
    for i in range(DSA_HEADS):
        qb_ref[:, i * d:(i + 1) * d] = (rope128(col('q_b', i)) * (d ** -0.5)).astype(bf)
    k_slc, v_slc, k_win, v_win = rope128(col('kv_a', 2)), col('kv_a', 3), rope128(col('kv_a', 4)), col('kv_a', 5)
    for i, part in enumerate((col('kv_a', 0), col('kv_a', 1), k_slc, v_slc)):
        nsa_ref[:, i * d:(i + 1) * d] = part
    win_ref[:, 0:d], win_ref[:, d:2 * d] = k_win, v_win
    kslc_ref[...], vslc_ref[...], kwin_ref[...], vwin_ref[...] = (k_slc.astype(bf), v_slc.astype(bf),
                                                                   k_win.astype(bf), v_win.astype(bf))
    k_dsa, v_dsa = rope128(col('kv_b', 0)), col('kv_b', 1)
    dsa_ref[:, 0:d], dsa_ref[:, d:2 * d] = k_dsa, v_dsa
    kdsa_ref[...], vdsa_ref[...] = k_dsa.astype(bf), v_dsa.astype(bf)
    for i in range(IDX_HEADS * IDX_DIM // LANE):
        pair = rope64(col('q_i', i)).astype(bf)
        qi_ref[2 * i] = pair[:, :IDX_DIM]
        qi_ref[2 * i + 1] = pair[:, IDX_DIM:]
    assert SMALL_OFF['k_i'][0] == 0
    k_idx = rope64(small_ref[...])[:, :IDX_DIM]
    idx_ref[...] = k_idx
    kidx_ref[...] = k_idx.astype(bf)


def rope_split(hproj, pos):
    bg, tg, _ = hproj.shape
    d, bf, f32 = HEAD_DIM, jnp.bfloat16, jnp.float32
    tm = min(ROPE_TM, tg)
    assert tg % tm == 0
    c128, s128 = _rope_tables(pos, tg, d)
    c64, s64 = _rope_tables(pos, tg, IDX_DIM)
    rows = lambda w: pl.BlockSpec((None, tm, w), lambda b, i: (b, i, 0))
    table = pl.BlockSpec((tm, LANE), lambda b, i: (i, 0))
    shape = lambda w, dt: jax.ShapeDtypeStruct((bg, tg, w), dt)
    outs = [(W_A, bf), (W_B, bf), None, (NSA_ROWS * d, f32), (2 * d, f32), (2 * d, f32), (IDX_DIM, f32),
            (d, bf), (d, bf), (d, bf), (d, bf), (d, bf), (d, bf), (IDX_DIM, bf)]
    out_specs = [pl.BlockSpec((None, IDX_HEADS, tm, IDX_DIM), lambda b, i: (b, 0, i, 0)) if o is None else rows(o[0])
                 for o in outs]
    out_shape = [jax.ShapeDtypeStruct((bg, IDX_HEADS, tg, IDX_DIM), bf) if o is None else shape(*o) for o in outs]
    return pl.pallas_call(
        _rope_split_kernel,
        grid=(bg, tg // tm),
        in_specs=[pl.BlockSpec((None, tm, ROPE_COLS), lambda b, i: (b, i, 0)),
                  pl.BlockSpec((None, tm, LANE), lambda b, i: (b, i, PROJ_OFF['small'] // LANE)),
                  table, table, table, table],
        out_specs=out_specs, out_shape=out_shape,
        compiler_params=pltpu.CompilerParams(
            dimension_semantics=("parallel", "parallel"), vmem_limit_bytes=VMEM_LIMIT),
        name="rope_split",
    )(hproj, hproj, c128, s128, c64, s64)


_NT = (((1,), (1,)), ((), ()))


def _softmax_tiles(qh, k_ref, v_ref, lo, hi, tk, mask_fn, m_ref, l_ref, acc_ref, nh, tq):
    assert qh.shape[-1] == LANE and tk % LANE == 0
    m_ref[...] = jnp.full(m_ref.shape, NEG, jnp.float32)
    l_ref[...] = jnp.zeros(l_ref.shape, jnp.float32)
    acc_ref[...] = jnp.zeros(acc_ref.shape, jnp.float32)

    def body(kt, carry):
        off = pl.multiple_of(kt * tk, tk)
        k = k_ref[0, pl.ds(off, tk), :]
        v = v_ref[0, pl.ds(off, tk), :]
        bias = (mask_fn(kt) - 1.0) * (-NEG)
        heads = range(nh)
        chunks = []
        for h in heads:
            s = lax.dot_general(qh[h * tq:(h + 1) * tq], k, _NT, preferred_element_type=jnp.float32) + bias
            chunks.append([s[:, c * LANE:(c + 1) * LANE] for c in range(tk // LANE)])
        m_old = [m_ref[h] for h in heads]
        m_new = [jnp.maximum(m_old[h], jnp.max(functools.reduce(jnp.maximum, chunks[h]), axis=-1, keepdims=True))
                 for h in heads]
        ps = [[jnp.exp(ch - m_new[h]) for ch in chunks[h]] for h in heads]
        alpha = [jnp.exp(m_old[h] - m_new[h]) for h in heads]
        pv = [jnp.dot(jnp.concatenate(ps[h], axis=-1).astype(jnp.bfloat16), v, preferred_element_type=jnp.float32)
              for h in heads]
        for h in heads:
            l_ref[h] = alpha[h] * l_ref[h] + functools.reduce(jnp.add, ps[h])
            acc_ref[h] = alpha[h] * acc_ref[h] + pv[h]
            m_ref[h] = m_new[h]
        return carry

    lax.fori_loop(lo, hi, body, 0)
    return acc_ref[...] / jnp.maximum(jnp.sum(l_ref[...], axis=-1, keepdims=True), 1e-30)


def _split3_bf16(x):
    hi = x.astype(jnp.bfloat16)
    r1 = x - hi.astype(jnp.float32)
    mid = r1.astype(jnp.bfloat16)
    lo = (r1 - mid.astype(jnp.float32)).astype(jnp.bfloat16)
    return hi, mid, lo


def _select_blocks(psum, share, qpos_r, n_blk, tq):
    nbp, tqp = share.shape[0], psum.shape[0]
    imp_t = sum(lax.dot_general(share, part, _NT, preferred_element_type=jnp.float32)
                for part in _split3_bf16(psum))
    blk = lax.broadcasted_iota(jnp.int32, (nbp, tqp), 0)
    forced = (blk == jnp.right_shift(qpos_r, 6)) | (blk == 0)
    future = blk * SEL_BLOCK > qpos_r
    imp_t = jnp.where(forced, FORCED, jnp.where(future, -1.0, imp_t))
    imp_t = jnp.where(blk < n_blk, imp_t, -2.0)
    rank = jnp.zeros((nbp, tqp), jnp.float32)
    for i in range(n_blk):
        row = imp_t[i:i + 1, :]
        beats = (row > imp_t) | ((row == imp_t) & (blk > i))
        rank = rank + beats.astype(jnp.float32)
    sel_t = (rank < float(min(SEL_COUNT, n_blk))).astype(jnp.bfloat16)
    eye = (lax.broadcasted_iota(jnp.int32, (tq, tqp), 0) == lax.broadcasted_iota(jnp.int32, (tq, tqp), 1))
    return lax.dot_general(eye.astype(jnp.bfloat16), sel_t, _NT,
                           preferred_element_type=jnp.float32).astype(jnp.bfloat16)


def _softmax_flat(qh, tiles, nh, tq):
    scores = []
    for k, _, maskf in tiles:
        s = lax.dot_general(qh, k, _NT, preferred_element_type=jnp.float32).reshape(nh, tq, k.shape[0])
        scores.append(s + ((maskf - 1.0) * (-NEG))[None])
    m = scores[0]
    for s in scores[1:]:
        m = jnp.maximum(m, s)
    m = jnp.max(m, axis=-1, keepdims=True)
    lsum, acc = None, None
    for s, (_, v, maskf) in zip(scores, tiles):
        p = jnp.exp(s - m) * maskf[None]
        pv = jnp.dot(p.reshape(nh * tq, p.shape[-1]).astype(jnp.bfloat16), v, preferred_element_type=jnp.float32)
        lsum = p if lsum is None else lsum + p
        acc = pv if acc is None else acc + pv
    l = jnp.sum(lsum, axis=-1, keepdims=True)
    return acc.reshape(nh, tq, acc.shape[-1]) / jnp.maximum(l, 1e-30)


NSA_TQ = 128
NSA_TK_SLC = 512
NSA_TK_WIN = 128


def _nsa_prompt_kernel(q_ref, g_ref, kc_ref, vc_ref, ks_ref, vs_ref, kw_ref, vw_ref, share_ref, o_ref,
                       m_ref, l_ref, acc_ref, *, n_cmp, n_blk, n_keys):
    nh, tq, d = NSA_HEADS, NSA_TQ, HEAD_DIM
    nbp = share_ref.shape[0]
    ncp = share_ref.shape[1]
    start = pl.program_id(1) * tq
    qpos_c = start + lax.broadcasted_iota(jnp.int32, (tq, 1), 0)
    qpos_r = start + lax.broadcasted_iota(jnp.int32, (1, tq), 1)

    qh = jnp.concatenate([q_ref[0, :, h * d:(h + 1) * d] for h in range(nh)], axis=0)

    c_r = lax.broadcasted_iota(jnp.int32, (1, ncp), 1)
    mask_c = ((c_r * CMP_STRIDE + (CMP_LEN - 1) <= qpos_c) & (c_r < n_cmp)).astype(jnp.float32)
    s = lax.dot_general(qh, kc_ref[0], _NT, preferred_element_type=jnp.float32).reshape(nh, tq, ncp)
    sm = s + ((mask_c - 1.0) * (-NEG))[None]
    e = jnp.exp(sm - jnp.max(sm, axis=-1, keepdims=True)) * mask_c[None]
    p_cmp = e / jnp.maximum(jnp.sum(e, axis=-1, keepdims=True), 1e-30)
    o_cmp = jnp.dot(p_cmp.reshape(nh * tq, ncp).astype(jnp.bfloat16), vc_ref[0],
                    preferred_element_type=jnp.float32)

    sel = _select_blocks(jnp.sum(p_cmp, axis=0), share_ref[...], qpos_r, n_blk, tq)

    def slc_mask(kt):
        kpos = kt * NSA_TK_SLC + lax.broadcasted_iota(jnp.int32, (1, NSA_TK_SLC), 1)
        kblk = kt * (NSA_TK_SLC // SEL_BLOCK) + jnp.right_shift(
            lax.broadcasted_iota(jnp.int32, (nbp, NSA_TK_SLC), 1), 6)
        expand = (kblk == lax.broadcasted_iota(jnp.int32, (nbp, NSA_TK_SLC), 0)).astype(jnp.bfloat16)
        chosen = jnp.dot(sel, expand, preferred_element_type=jnp.float32)
        return chosen * (kpos <= qpos_c).astype(jnp.float32)

    hi_slc = jnp.minimum((start + tq - 1) // NSA_TK_SLC + 1, n_keys // NSA_TK_SLC)
    o_slc = _softmax_tiles(qh, ks_ref, vs_ref, 0, hi_slc, NSA_TK_SLC, slc_mask, m_ref, l_ref, acc_ref, nh, tq)

    tiles = []
    for i in range(WINDOW // NSA_TK_WIN + 1):
        kt = pl.program_id(1) * (tq // NSA_TK_WIN) - WINDOW // NSA_TK_WIN + i
        off = pl.multiple_of(jnp.maximum(kt, 0) * NSA_TK_WIN, NSA_TK_WIN)
        kpos = kt * NSA_TK_WIN + lax.broadcasted_iota(jnp.int32, (1, NSA_TK_WIN), 1)
        dlt = qpos_c - kpos
        maskf = ((dlt >= 0) & (dlt < WINDOW) & (kpos >= 0)).astype(jnp.float32)
        tiles.append((kw_ref[0, pl.ds(off, NSA_TK_WIN), :], vw_ref[0, pl.ds(off, NSA_TK_WIN), :], maskf))
    o_win = _softmax_flat(qh, tiles, nh, tq)

    gate = 1.0 / (1.0 + jnp.exp(-g_ref[0]))
    o_cmp = o_cmp.reshape(nh, tq, d)
    for h in range(nh):
        o_ref[0, :, h * d:(h + 1) * d] = (gate[:, 3 * h:3 * h + 1] * o_cmp[h]
                                          + gate[:, 3 * h + 1:3 * h + 2] * o_slc[h]
                                          + gate[:, 3 * h + 2:3 * h + 3] * o_win[h])


def _share_matrix_t(n_cmp, n_blk, ncp, nbp):
    c0 = np.arange(n_cmp)[None, :] * CMP_STRIDE
    j0 = np.arange(n_blk)[:, None] * SEL_BLOCK
    share = np.clip(np.minimum(c0 + CMP_LEN, j0 + SEL_BLOCK) - np.maximum(c0, j0), 0, None) / CMP_LEN
    out = np.zeros((nbp, ncp), np.float32)
    out[:n_blk, :n_cmp] = share
    return jnp.asarray(out, jnp.bfloat16)


def nsa_prompt_attention(q, gates, k_cmp, v_cmp, n_cmp, k_slc, v_slc, k_win, v_win):
    b, t, _ = q.shape
    d = HEAD_DIM
    n_blk = -(-t // SEL_BLOCK)
    ncp = -(-n_cmp // LANE) * LANE
    nbp = -(-n_blk // 16) * 16
    assert t % NSA_TK_SLC == 0 and t % NSA_TQ == 0
    bf = jnp.bfloat16
    cpad = ((0, 0), (0, ncp - k_cmp.shape[1]), (0, 0))
    kc, vc = jnp.pad(k_cmp, cpad).astype(bf), jnp.pad(v_cmp, cpad).astype(bf)
    whole = lambda n: pl.BlockSpec((1, n, d), lambda bi, qi: (bi, 0, 0))
    return pl.pallas_call(
        functools.partial(_nsa_prompt_kernel, n_cmp=n_cmp, n_blk=n_blk, n_keys=t),
        grid=(b, t // NSA_TQ),
        in_specs=[pl.BlockSpec((1, NSA_TQ, NSA_HEADS * d), lambda bi, qi: (bi, qi, 0)),
                  pl.BlockSpec((1, NSA_TQ, 3 * NSA_HEADS), lambda bi, qi: (bi, qi, 0)),
                  whole(ncp), whole(ncp), whole(t), whole(t), whole(t), whole(t),
                  pl.BlockSpec((nbp, ncp), lambda bi, qi: (0, 0))],
        out_specs=pl.BlockSpec((1, NSA_TQ, NSA_HEADS * d), lambda bi, qi: (bi, qi, 0)),
        out_shape=jax.ShapeDtypeStruct((b, t, NSA_HEADS * d), jnp.float32),
        scratch_shapes=[pltpu.VMEM((NSA_HEADS, NSA_TQ, LANE), jnp.float32),
                        pltpu.VMEM((NSA_HEADS, NSA_TQ, LANE), jnp.float32),
                        pltpu.VMEM((NSA_HEADS, NSA_TQ, d), jnp.float32)],
        compiler_params=pltpu.CompilerParams(
            dimension_semantics=("parallel", "arbitrary"), vmem_limit_bytes=VMEM_LIMIT),
        name="nsa_prompt",
    )(q, gates, kc, vc, k_slc.astype(bf), v_slc.astype(bf), k_win.astype(bf), v_win.astype(bf),
      _share_matrix_t(n_cmp, n_blk, ncp, nbp))


NSA_ROWS = 4
NSA_SEQ = 2


def _gelu_tanh(x):
    return 0.5 * x * (1.0 + jnp.tanh(math.sqrt(2.0 / math.pi) * (x + 0.044715 * (x * x * x))))


def _pad_rows(x, rows):
    if rows == x.shape[0]:
        return x
    return jnp.concatenate([x, jnp.zeros((rows - x.shape[0], x.shape[1]), x.dtype)], axis=0)


def _compress_tokens(seg_rows, pe_ref, w1_ref, w2_ref, kv, ncp):
    d, bf = HEAD_DIM, jnp.bfloat16
    first = jnp.zeros((ncp, CMP_HID), jnp.float32)
    second = jnp.zeros((ncp, CMP_HID), jnp.float32)
    for r in range(0, CMP_STRIDE, 2):
        ya, yb = seg_rows(r), seg_rows(r + 1)
        for acc_off in (0, CMP_STRIDE):
            lhs = jnp.concatenate([ya + pe_ref[kv, acc_off + r:acc_off + r + 1, :],
                                   yb + pe_ref[kv, acc_off + r + 1:acc_off + r + 2, :]], axis=1).astype(bf)
            w = w1_ref[kv, acc_off + r:acc_off + r + 2].reshape(2 * d, CMP_HID)
            y = jnp.dot(lhs, w, preferred_element_type=jnp.float32)
            if acc_off == 0:
                first = first + y
            else:
                second = second + y
    hid = first + pltpu.roll(second, ncp - 1, 0)
    return jnp.dot(_gelu_tanh(hid).astype(bf), w2_ref[kv], preferred_element_type=jnp.float32)


def _compress_prompt_kernel(rows_ref, pe_ref, w1_ref, w2_ref, cos_ref, sin_ref, kc_ref, vc_ref):
    ncp, d = kc_ref.shape[0], HEAD_DIM

    def seg_rows(kv, r):
        return rows_ref[pl.ds(NSA_ROWS * r + kv, ncp, stride=NSA_ROWS * CMP_STRIDE), :]

    kc = _compress_tokens(functools.partial(seg_rows, 0), pe_ref, w1_ref, w2_ref, 0, ncp)
    kc_ref[...] = (kc * cos_ref[...] + pltpu.roll(kc, d // 2, 1) * sin_ref[...]).astype(kc_ref.dtype)
    vc_ref[...] = _compress_tokens(functools.partial(seg_rows, 1), pe_ref, w1_ref, w2_ref, 1, ncp).astype(vc_ref.dtype)


def nsa_compress_prompt(nsa_new, pe, w1, w2):
    b, t = nsa_new.shape[:2]
    d, bf = HEAD_DIM, jnp.bfloat16
    ncp = t // CMP_STRIDE
    n_cmp = (t - CMP_LEN) // CMP_STRIDE + 1
    assert ncp % 8 == 0 and n_cmp <= ncp
    cos, sin = _rope_tables(jnp.arange(n_cmp) * CMP_STRIDE + CMP_LEN - 1, ncp)
    const = lambda *shape: pl.BlockSpec(shape, lambda i: (0,) * len(shape))
    out = pl.BlockSpec((None, ncp, d), lambda i: (i, 0, 0))
    return pl.pallas_call(
        _compress_prompt_kernel,
        grid=(b,),
        in_specs=[pl.BlockSpec((None, t * NSA_ROWS, d), lambda i: (i, 0, 0)),
                  const(2, CMP_LEN, d), const(2, CMP_LEN, d, CMP_HID), const(2, CMP_HID, d), const(ncp, d), const(ncp, d)],
        out_specs=[out, out],
        out_shape=[jax.ShapeDtypeStruct((b, ncp, d), bf)] * 2,
        compiler_params=pltpu.CompilerParams(dimension_semantics=("parallel",), vmem_limit_bytes=VMEM_LIMIT),
        name="nsa_compress",
    )(nsa_new.reshape(b, t * NSA_ROWS, d), pe, w1.reshape(2, CMP_LEN, d, CMP_HID).astype(bf), w2.astype(bf), cos, sin)


def _gated_store(o_ref, g_ref, branches, nh, d):
    gate = 1.0 / (1.0 + jnp.exp(-g_ref[...]))
    for h in range(nh):
        o_ref[:, h * d:(h + 1) * d] = sum(gate[:, 3 * h + i:3 * h + i + 1] * br[h] for i, br in enumerate(branches))


def _nsa_sample_kernel(pt_ref, q_ref, g_ref, new_ref, winc_ref, winn_ref, pe_ref, w1_ref, w2_ref, cos_ref, sin_ref,
                       share_ref, *rest, past, n_cmp, n_blk):
    del pt_ref
    all_pages, o_ref = rest[:-1], rest[-1]
    n_seq = q_ref.shape[0]
    n_pages = len(all_pages) // n_seq
    ncp = share_ref.shape[1]
    seg_per_page = PAGE_SIZE // CMP_STRIDE
    assert n_pages * seg_per_page == ncp == LANE and CMP_LEN == 2 * CMP_STRIDE

    def seg_rows(kv, r):
        return jnp.concatenate([p[pl.ds(NSA_ROWS * r + kv, seg_per_page, stride=NSA_ROWS * CMP_STRIDE), :]
                                for p in all_pages], axis=0)

    kc_all = _compress_tokens(functools.partial(seg_rows, 0), pe_ref, w1_ref, w2_ref, 0, n_seq * ncp)
    vc_all = _compress_tokens(functools.partial(seg_rows, 1), pe_ref, w1_ref, w2_ref, 1, n_seq * ncp)
    for s in range(n_seq):
        _nsa_sample_one(q_ref.at[s], g_ref.at[s], new_ref.at[s], winc_ref.at[s], winn_ref.at[s], cos_ref, sin_ref,
                        share_ref, all_pages[s * n_pages:(s + 1) * n_pages], o_ref.at[s],
                        kc_all[s * ncp:(s + 1) * ncp], vc_all[s * ncp:(s + 1) * ncp], past, n_cmp, n_blk)


def _nsa_sample_one(q_ref, g_ref, new_ref, winc_ref, winn_ref, cos_ref, sin_ref, share_ref, pages, o_ref,
                    kc, vc, past, n_cmp, n_blk):
    nh, d, pg = NSA_HEADS, HEAD_DIM, PAGE_SIZE
    tq = q_ref.shape[0]
    nbp, ncp = share_ref.shape
    bf = jnp.bfloat16
    qpos_c = past + lax.broadcasted_iota(jnp.int32, (tq, 1), 0)
    qpos_r = past + lax.broadcasted_iota(jnp.int32, (1, LANE), 1)
    lane = lax.broadcasted_iota(jnp.int32, (1, LANE), 1)
    kc = (kc * cos_ref[...] + pltpu.roll(kc, d // 2, 1) * sin_ref[...]).astype(bf)
    vc = vc.astype(bf)

    q = q_ref[...].astype(jnp.float32)
    qh = jnp.concatenate([q[:, h * d:(h + 1) * d] for h in range(nh)], axis=0).astype(bf)

    mask_c = ((lane * CMP_STRIDE + (CMP_LEN - 1) <= qpos_c) & (lane < n_cmp)).astype(jnp.float32)
    s = lax.dot_general(qh, kc, _NT, preferred_element_type=jnp.float32).reshape(nh, tq, ncp)
    sm = s + ((mask_c - 1.0) * (-NEG))[None]
    e = jnp.exp(sm - jnp.max(sm, axis=-1, keepdims=True)) * mask_c[None]
    p_cmp = e / jnp.maximum(jnp.sum(e, axis=-1, keepdims=True), 1e-30)
    o_cmp = jnp.dot(p_cmp.reshape(nh * tq, ncp).astype(bf), vc, preferred_element_type=jnp.float32).reshape(nh, tq, d)

    sel = _select_blocks(_pad_rows(jnp.sum(p_cmp, axis=0), LANE), share_ref[...], qpos_r, n_blk, LANE)

    def slc_mask(key0):
        kblk = key0 // SEL_BLOCK + jnp.right_shift(lax.broadcasted_iota(jnp.int32, (nbp, LANE), 1), 6)
        expand = (kblk == lax.broadcasted_iota(jnp.int32, (nbp, LANE), 0)).astype(bf)
        chosen = jnp.dot(sel, expand, preferred_element_type=jnp.float32)[:tq]
        return chosen * (key0 + lane <= qpos_c).astype(jnp.float32)

    def component(ref, comp, n_comp, row0, rows):
        return ref[pl.ds(n_comp * row0 + comp, rows, stride=n_comp), :]

    tiles = [(component(p, 2, NSA_ROWS, 0, pg).astype(bf), component(p, 3, NSA_ROWS, 0, pg).astype(bf), slc_mask(j * pg))
             for j, p in enumerate(pages)]
    tiles.append((_pad_rows(component(new_ref, 2, NSA_ROWS, 0, tq), LANE).astype(bf),
                  _pad_rows(component(new_ref, 3, NSA_ROWS, 0, tq), LANE).astype(bf), slc_mask(past)))
    o_slc = _softmax_flat(qh, tiles, nh, tq)

    def win_mask(key0):
        dlt = qpos_c - (key0 + lane)
        return ((dlt >= 0) & (dlt < WINDOW)).astype(jnp.float32)

    n_wc = winc_ref.shape[0] // 2
    tiles = [(component(winc_ref, 0, 2, j * LANE, LANE).astype(bf), component(winc_ref, 1, 2, j * LANE, LANE).astype(bf),
              win_mask(past - n_wc + j * LANE)) for j in range(n_wc // LANE)]
    tiles.append((_pad_rows(component(winn_ref, 0, 2, 0, tq), LANE).astype(bf),
                  _pad_rows(component(winn_ref, 1, 2, 0, tq), LANE).astype(bf), win_mask(past)))
    o_win = _softmax_flat(qh, tiles, nh, tq)

    _gated_store(o_ref, g_ref, (o_cmp, o_slc, o_win), nh, d)


def nsa_sample_attention(layer, q, gates, nsa_new, win_new, pool, win_cache, page_table, pe, w1, w2):
    b, t, _ = q.shape
    wlen = win_cache.shape[2]
    nsa_new = nsa_new.reshape(b, t * NSA_ROWS, HEAD_DIM)
    win_new = win_new.reshape(b, t * 2, HEAD_DIM)
    pool = pool.reshape(pool.shape[0], pool.shape[1], PAGE_SIZE * NSA_ROWS, HEAD_DIM)
    win_cache = win_cache.reshape(win_cache.shape[0], b, wlen * 2, HEAD_DIM)
    d, n_pages = HEAD_DIM, page_table.shape[1]
    past = n_pages * PAGE_SIZE
    n_keys = past + t
    n_cmp = (n_keys - CMP_LEN) // CMP_STRIDE + 1
    n_blk = -(-n_keys // SEL_BLOCK)
    ncp = -(-n_cmp // LANE) * LANE
    nbp = -(-n_blk // 16) * 16
    assert (n_cmp + 1) * CMP_STRIDE <= past, "compressed tokens must come from cached rows only"
    cos, sin = _rope_tables(jnp.arange(n_cmp) * CMP_STRIDE + CMP_LEN - 1, ncp)
    bf = jnp.bfloat16
    ns = NSA_SEQ
    assert b % ns == 0
    per_seq = lambda *tail: pl.BlockSpec((ns,) + tail, lambda i, pt: (i,) + (0,) * len(tail))
    const = lambda *shape: pl.BlockSpec(shape, lambda i, pt: (0,) * len(shape))
    page = lambda s, j: pl.BlockSpec((None, None, PAGE_SIZE * NSA_ROWS, d),
                                     lambda i, pt: (layer, pt[i * ns + s, j], 0, 0))
    grid_spec = pltpu.PrefetchScalarGridSpec(
        num_scalar_prefetch=1, grid=(b // ns,),
        in_specs=[per_seq(t, NSA_HEADS * d), per_seq(t, 3 * NSA_HEADS), per_seq(t * NSA_ROWS, d),
                  pl.BlockSpec((None, ns, wlen * 2, d), lambda i, pt: (layer, i, 0, 0)),
                  per_seq(t * 2, d),
                  const(2, CMP_LEN, d), const(2, CMP_LEN, d, CMP_HID), const(2, CMP_HID, d),
                  const(ncp, d), const(ncp, d), const(nbp, ncp)]
                 + [page(s, j) for s in range(ns) for j in range(n_pages)],
        out_specs=per_seq(t, NSA_HEADS * d))
    return pl.pallas_call(
        functools.partial(_nsa_sample_kernel, past=past, n_cmp=n_cmp, n_blk=n_blk),
        grid_spec=grid_spec,
        out_shape=jax.ShapeDtypeStruct((b, t, NSA_HEADS * d), jnp.float32),
        compiler_params=pltpu.CompilerParams(dimension_semantics=("parallel",), vmem_limit_bytes=VMEM_LIMIT),
        name="nsa_sample",
    )(page_table, q, gates, nsa_new, win_cache, win_new, pe, w1.reshape(2, CMP_LEN, d, CMP_HID).astype(bf),
      w2.astype(bf), cos, sin, _share_matrix_t(n_cmp, n_blk, ncp, nbp), *([pool] * (ns * n_pages)))


DSA_TQ = 128
DSA_TK = 256
INT_MIN = -2 ** 31


def _code_to_float(code):
    return pltpu.bitcast(jnp.where(code >= 0, code, code ^ 0x7FFFFFFF), jnp.float32)


def _dsa_prompt_kernel(q_ref, qi_ref, wt_ref, k_ref, v_ref, ki_ref, tri_ref, o_ref,
                       key_ref, sel_ref, m_ref, l_ref, acc_ref, *, n_keys, topk):
    nh, tq, d, tk = DSA_HEADS, DSA_TQ, HEAD_DIM, DSA_TK
    start = pl.program_id(1) * tq
    qpos_r = start + lax.broadcasted_iota(jnp.int32, (1, tq), 1)
    n_kt = jnp.minimum((start + tq - 1) // tk + 1, n_keys // tk)

    wt = wt_ref[0] * (IDX_HEADS ** -0.5 * IDX_DIM ** -0.5)

    def score_body(kt, carry):
        off = pl.multiple_of(kt * tk, tk)
        ki = ki_ref[0, pl.ds(off, tk), :]
        acc = jnp.zeros((tk, tq), jnp.float32)
        for h in range(0, IDX_HEADS, 2):
            pair = qi_ref[0, h:h + 2].reshape(2 * tq, IDX_DIM)
            dots = lax.dot_general(ki, pair, _NT, preferred_element_type=jnp.float32)
            acc = (acc + jnp.maximum(dots[:, :tq], 0.0) * wt[h:h + 1, :]
                   + jnp.maximum(dots[:, tq:], 0.0) * wt[h + 1:h + 2, :])
        kpos_c = off + lax.broadcasted_iota(jnp.int32, (tk, 1), 0)
        key_ref[kt] = jnp.where(kpos_c <= qpos_r, acc, NEG)
        return carry

    lax.fori_loop(0, n_kt, score_body, 0)

    def count(pred):
        def body(kt, part):
            hit = pred(key_ref[kt]).astype(jnp.int32)
            return part + jnp.sum(hit.reshape(tk // 8, 8, tq), axis=0)
        part = lax.fori_loop(0, n_kt, body, jnp.zeros((8, tq), jnp.int32))
        return jnp.sum(part, axis=0, keepdims=True)

    code = jnp.where(count(lambda sc: sc >= 0.0) >= topk, 0, INT_MIN).astype(jnp.int32)

    def bit_body(i, c):
        cand = c | jnp.left_shift(jnp.int32(1), 30 - i)
        cand_f = _code_to_float(cand)
        return jnp.where(count(lambda sc: sc >= cand_f) >= topk, cand, c)

    code = lax.fori_loop(0, 31, bit_body, code)
    thr = _code_to_float(code)
    take_all = code == INT_MIN
    room = (topk - count(lambda sc: sc > thr)).astype(jnp.float32)

    def sel_body(kt, seen):
        kk = key_ref[kt]
        eq = (kk == thr).astype(jnp.float32)
        prefix = jnp.dot(tri_ref[...], eq.astype(jnp.bfloat16), preferred_element_type=jnp.float32)
        keep = (kk > thr) | ((eq > 0.0) & (prefix + seen <= room)) | take_all
        kpos_c = kt * tk + lax.broadcasted_iota(jnp.int32, (tk, 1), 0)
        sel_ref[kt] = (keep & (kpos_c <= qpos_r)).astype(jnp.bfloat16)
        return seen + jnp.sum(eq, axis=0, keepdims=True)

    lax.fori_loop(0, n_kt, sel_body, jnp.zeros((1, tq), jnp.float32))

    eye = (lax.broadcasted_iota(jnp.int32, (tq, tq), 0)
           == lax.broadcasted_iota(jnp.int32, (tq, tq), 1)).astype(jnp.bfloat16)

    def sel_mask(kt):
        return lax.dot_general(eye, sel_ref[kt], _NT, preferred_element_type=jnp.float32)

    qh = jnp.concatenate([q_ref[0, :, h * d:(h + 1) * d] for h in range(nh)], axis=0)
    o = _softmax_tiles(qh, k_ref, v_ref, 0, n_kt, tk, sel_mask, m_ref, l_ref, acc_ref, nh, tq)
    for h in range(nh):
        o_ref[0, :, h * d:(h + 1) * d] = o[h]


def dsa_prompt_attention(q, q_idx, w_idx, k, v, k_idx):
    b, t, _ = q.shape
    d = HEAD_DIM
    topk = min(DSA_TOPK, t // 4)
    assert t % DSA_TK == 0 and t % DSA_TQ == 0
    bf = jnp.bfloat16
    tri = jnp.asarray(np.tril(np.ones((DSA_TK, DSA_TK), np.float32)), bf)
    whole = lambda n, w: pl.BlockSpec((1, n, w), lambda bi, qi: (bi, 0, 0))
    return pl.pallas_call(
        functools.partial(_dsa_prompt_kernel, n_keys=t, topk=topk),
        grid=(b, t // DSA_TQ),
        in_specs=[pl.BlockSpec((1, DSA_TQ, DSA_HEADS * d), lambda bi, qi: (bi, qi, 0)),
                  pl.BlockSpec((1, IDX_HEADS, DSA_TQ, IDX_DIM), lambda bi, qi: (bi, 0, qi, 0)),
                  pl.BlockSpec((1, IDX_HEADS, DSA_TQ), lambda bi, qi: (bi, 0, qi)),
                  whole(t, d), whole(t, d), whole(t, IDX_DIM),
                  pl.BlockSpec((DSA_TK, DSA_TK), lambda bi, qi: (0, 0))],
        out_specs=pl.BlockSpec((1, DSA_TQ, DSA_HEADS * d), lambda bi, qi: (bi, qi, 0)),
        out_shape=jax.ShapeDtypeStruct((b, t, DSA_HEADS * d), jnp.float32),
        scratch_shapes=[pltpu.VMEM((t // DSA_TK, DSA_TK, DSA_TQ), jnp.float32),
                        pltpu.VMEM((t // DSA_TK, DSA_TK, DSA_TQ), bf),
                        pltpu.VMEM((DSA_HEADS, DSA_TQ, LANE), jnp.float32),
                        pltpu.VMEM((DSA_HEADS, DSA_TQ, LANE), jnp.float32),
                        pltpu.VMEM((DSA_HEADS, DSA_TQ, d), jnp.float32)],
        compiler_params=pltpu.CompilerParams(
            dimension_semantics=("parallel", "arbitrary"), vmem_limit_bytes=VMEM_LIMIT),
        name="dsa_prompt",
    )(q, q_idx.astype(bf), jnp.swapaxes(w_idx, 1, 2),
      k.astype(bf), v.astype(bf), k_idx.astype(bf), tri)


SEARCH_BITS = 3


def _dsa_sample_kernel(pt_ref, q_ref, qi_ref, w_ref, new_ref, inew_ref, tri_ref, *rest, past, topk):
    del pt_ref
    n_pages = (len(rest) - 1) // 2
    kv_pages, idx_pages, o_ref = rest[:n_pages], rest[n_pages:2 * n_pages], rest[-1]
    nh, d, pg = DSA_HEADS, HEAD_DIM, PAGE_SIZE
    tq = q_ref.shape[0]
    bf = jnp.bfloat16
    qpos_c = past + lax.broadcasted_iota(jnp.int32, (tq, 1), 0)
    lane = lax.broadcasted_iota(jnp.int32, (1, LANE), 1)

    qi = qi_ref[...]
    wb = jnp.broadcast_to(w_ref[...] * (IDX_HEADS ** -0.5 * IDX_DIM ** -0.5), (IDX_HEADS * tq, LANE))

    def key_tile(ki_t, key0):
        dots = jnp.dot(qi, ki_t, preferred_element_type=jnp.float32)
        score = jnp.sum((jnp.maximum(dots, 0.0) * wb).reshape(IDX_HEADS, tq, LANE), axis=0)
        return jnp.where(key0 + lane <= qpos_c, score, NEG)

    keys = [key_tile(p[...].astype(bf), j * pg) for j, p in enumerate(idx_pages)]
    keys.append(key_tile(inew_ref[...].astype(bf), past))

    def count(pred):
        hits = pred(keys[0]).astype(jnp.int32)
        for kk in keys[1:]:
            hits = hits + pred(kk).astype(jnp.int32)
        return jnp.sum(hits, axis=-1, keepdims=True)

    def at_least(code):
        cand = _code_to_float(code)
        return (count(lambda sc: sc >= cand) >= topk).astype(jnp.int32)

    code = jnp.where(count(lambda sc: sc >= 0.0) >= topk, 0, INT_MIN).astype(jnp.int32)
    n_steps, last_bits = divmod(31, SEARCH_BITS)

    def radix_body(i, c):
        shift = 31 - SEARCH_BITS * (i + 1)
        digit = sum(at_least(c | jnp.left_shift(jnp.int32(j), shift)) for j in range(1, 2 ** SEARCH_BITS))
        return c | jnp.left_shift(digit, shift)

    code = lax.fori_loop(0, n_steps, radix_body, code)
    if last_bits:
        code = code | sum(at_least(code | j) for j in range(1, 2 ** last_bits))
    thr = _code_to_float(code)
    take_all = code == INT_MIN
    room = (topk - count(lambda sc: sc > thr)).astype(jnp.float32)
    eqs = [(kk == thr).astype(jnp.float32) for kk in keys]
    totals = [jnp.sum(e, axis=-1, keepdims=True) for e in eqs]
    stacked = jnp.concatenate(eqs + [jnp.zeros((-len(eqs) * tq % MIN_MXU_ROWS, LANE), jnp.float32)] * (
        1 if len(eqs) * tq % MIN_MXU_ROWS else 0), axis=0).astype(bf)
    prefixes = jnp.dot(stacked, tri_ref[...], preferred_element_type=jnp.float32)
    seen = jnp.zeros((tq, 1), jnp.float32)
    masks = []
    for j, (kk, e) in enumerate(zip(keys, eqs)):
        prefix = prefixes[j * tq:(j + 1) * tq]
        keep = (kk > thr) | ((e > 0.0) & (prefix + seen <= room)) | take_all
        key0 = j * pg if j < n_pages else past
        masks.append((keep & (key0 + lane <= qpos_c)).astype(jnp.float32))
        seen = seen + totals[j]

    def component(ref, comp, rows):
        return ref[pl.ds(comp, rows, stride=2), :]

    tiles = [(component(p, 0, pg).astype(bf), component(p, 1, pg).astype(bf), masks[j]) for j, p in enumerate(kv_pages)]
    tiles.append((_pad_rows(component(new_ref, 0, tq), LANE).astype(bf),
                  _pad_rows(component(new_ref, 1, tq), LANE).astype(bf), masks[n_pages]))
    q = q_ref[...].astype(jnp.float32)
    qh = jnp.concatenate([q[:, h * d:(h + 1) * d] for h in range(nh)], axis=0).astype(bf)
    o = _softmax_flat(qh, tiles, nh, tq)
    for h in range(nh):
        o_ref[:, h * d:(h + 1) * d] = o[h]


def dsa_sample_attention(layer, q, q_idx, w_idx, dsa_new, idx_new, pool, idx_pool, page_table):
    b, t, _ = q.shape
    d, n_pages = HEAD_DIM, page_table.shape[1]
    past = n_pages * PAGE_SIZE
    topk = min(DSA_TOPK, (past + t) // 4)
    pool = pool.reshape(pool.shape[0], pool.shape[1], PAGE_SIZE * 2, d)
    idx_pool_t = jnp.swapaxes(idx_pool, 2, 3)
    idx_new_t = jnp.pad(jnp.swapaxes(idx_new, 1, 2), ((0, 0), (0, 0), (0, LANE - t)))
    tri = jnp.asarray(np.triu(np.ones((LANE, LANE), np.float32)), jnp.bfloat16)
    per_seq = lambda *tail: pl.BlockSpec((None,) + tail, lambda i, pt: (i,) + (0,) * len(tail))
    kv_page = lambda j: pl.BlockSpec((None, None, PAGE_SIZE * 2, d), lambda i, pt: (layer, pt[i, j], 0, 0))
    idx_page = lambda j: pl.BlockSpec((None, None, IDX_DIM, PAGE_SIZE), lambda i, pt: (layer, pt[i, j], 0, 0))
    grid_spec = pltpu.PrefetchScalarGridSpec(
        num_scalar_prefetch=1, grid=(b,),
        in_specs=[per_seq(t, DSA_HEADS * d), per_seq(IDX_HEADS * t, IDX_DIM), per_seq(IDX_HEADS * t, 1),
                  per_seq(t * 2, d), per_seq(IDX_DIM, LANE), pl.BlockSpec((LANE, LANE), lambda i, pt: (0, 0))]
                 + [kv_page(j) for j in range(n_pages)] + [idx_page(j) for j in range(n_pages)],
        out_specs=per_seq(t, DSA_HEADS * d))
    return pl.pallas_call(
        functools.partial(_dsa_sample_kernel, past=past, topk=topk),
        grid_spec=grid_spec,
        out_shape=jax.ShapeDtypeStruct((b, t, DSA_HEADS * d), jnp.float32),
        compiler_params=pltpu.CompilerParams(dimension_semantics=("parallel",), vmem_limit_bytes=VMEM_LIMIT),
        name="dsa_sample",
    )(page_table, q, q_idx.reshape(b, IDX_HEADS * t, IDX_DIM).astype(jnp.bfloat16),
      jnp.swapaxes(w_idx, 1, 2).reshape(b, IDX_HEADS * t, 1), dsa_new.reshape(b, t * 2, d), idx_new_t, tri,
      *([pool] * n_pages), *([idx_pool_t] * n_pages))


GDN_GROUP = 256
GDN_SUB = 16
GDN_HB = 8
CONV_PAD = 8


def _hp_dot(a, b):
    bf = jnp.bfloat16
    ah, bh = a.astype(bf), b.astype(bf)
    al, bl = (a - ah.astype(jnp.float32)).astype(bf), (b - bh.astype(jnp.float32)).astype(bf)
    dot = functools.partial(jnp.dot, preferred_element_type=jnp.float32)
    return dot(ah, bh) + dot(ah, bl) + dot(al, bh)


def _unit_lower_inverse(a, row, col):
    assert GDN_SUB == 16 and GDN_CHUNK == 64
    n = range(len(a))
    eye = (row == col).astype(jnp.float32)
    sub = jnp.right_shift(row, 4) == jnp.right_shift(col, 4)
    a16 = [jnp.where(sub, a[i], 0.0) for i in n]
    t16 = [eye - a16[i] for i in n]
    power = a16
    for _ in range(3):
        power = [_hp_dot(power[i], power[i]) for i in n]
        t16 = [t16[i] + _hp_dot(t16[i], power[i]) for i in n]
    b = [_hp_dot(t16[i], a[i] - a16[i]) for i in n]
    b2 = [_hp_dot(b[i], b[i]) for i in n]
    imb = [eye - b[i] for i in n]
    left = [imb[i] + _hp_dot(imb[i], b2[i]) for i in n]
    return [_hp_dot(left[i], t16[i]) for i in n]


def _gdn_prompt_kernel(xq_ref, xk_ref, xv_ref, pq_ref, pk_ref, pv_ref, a_ref, b_ref, z_ref, cwq_ref, cwk_ref, cwv_ref,
                       alog_ref, dtb_ref, ng_ref, s0_ref, o_ref, s_ref, hist_ref):
    f32, bf = jnp.float32, jnp.bfloat16
    d, g, c = HEAD_DIM, GDN_GROUP, GDN_CHUNK
    heads = range(GDN_HB)
    lanes = [slice(h * d, (h + 1) * d) for h in heads]
    row = lax.broadcasted_iota(jnp.int32, (g, g), 0)
    col = lax.broadcasted_iota(jnp.int32, (g, g), 1)
    same = jnp.right_shift(row, 6) == jnp.right_shift(col, 6)
    incl = same & (row >= col)
    strict = same & (row > col)
    eye = (row == col).astype(f32)
    tril_b, same_b = incl.astype(bf), same.astype(bf)

    @pl.when(pl.program_id(2) == 0)
    def _():
        s_ref[...] = s0_ref[...]
        for part, p_ref in enumerate((pq_ref, pk_ref, pv_ref)):
            hist_ref[part] = p_ref[...]

    pre = a_ref[...] + dtb_ref[...]
    gate = -jnp.exp(alog_ref[...]) * (jnp.maximum(pre, 0.0) + jnp.log(1.0 + jnp.exp(-jnp.abs(pre))))
    parts = _split3_bf16(gate)
    gcum_all = sum(jnp.dot(tril_b, p, preferred_element_type=f32) for p in parts)
    gtot_all = sum(jnp.dot(same_b, p, preferred_element_type=f32) for p in parts)
    beta_all = 1.0 / (1.0 + jnp.exp(-b_ref[...]))

    def conv(part, x_ref, w_ref, h):
        win = jnp.concatenate([hist_ref[part, :, lanes[h]], x_ref[:, lanes[h]]], axis=0)
        y = sum(pltpu.roll(win, g + CONV_PAD - (CONV_PAD - GDN_CONV + 1 + i), 0)[:g] * w_ref[i:i + 1, lanes[h]]
                for i in range(GDN_CONV))
        return y / (1.0 + jnp.exp(-y))

    q = [conv(0, xq_ref, cwq_ref, h) for h in heads]
    k = [conv(1, xk_ref, cwk_ref, h) for h in heads]
    v = [conv(2, xv_ref, cwv_ref, h) for h in heads]
    for part, x_ref in enumerate((xq_ref, xk_ref, xv_ref)):
        hist_ref[part] = x_ref[g - CONV_PAD:g, :]
    q = [q[h] * lax.rsqrt(jnp.sum(q[h] * q[h], axis=-1, keepdims=True) + EPS) * (d ** -0.5) for h in heads]
    k = [k[h] * lax.rsqrt(jnp.sum(k[h] * k[h], axis=-1, keepdims=True) + EPS) for h in heads]
    gc = [gcum_all[:, h:h + 1] for h in heads]
    gl = [gtot_all[:, h:h + 1] for h in heads]
    beta = [beta_all[:, h:h + 1] for h in heads]
    g_i = [jnp.broadcast_to(gc[h], (g, g)) for h in heads]
    g_j = [jnp.sum(g_i[h] * eye, axis=0, keepdims=True) for h in heads]
    decay = [jnp.where(incl, jnp.exp(jnp.where(incl, g_i[h] - g_j[h], 0.0)), 0.0) for h in heads]
    kb = [k[h] * beta[h] for h in heads]
    k16 = [k[h].astype(bf) for h in heads]
    eg = [jnp.exp(gc[h]) for h in heads]
    a_mat = [jnp.where(strict, lax.dot_general(kb[h].astype(bf), k16[h], _NT, preferred_element_type=f32) * decay[h], 0.0)
             for h in heads]
    qk = [(lax.dot_general(q[h].astype(bf), k16[h], _NT, preferred_element_type=f32) * decay[h]).astype(bf)
          for h in heads]
    rhs = [jnp.concatenate([v[h] * beta[h], kb[h] * eg[h]], axis=1) for h in heads]
    inv = _unit_lower_inverse(a_mat, row, col)
    sol = [_hp_dot(inv[h], rhs[h]) for h in heads]
    u = [sol[h][:, :d] for h in heads]
    w16 = [sol[h][:, d:].astype(bf) for h in heads]
    q_dec = [(q[h] * eg[h]).astype(bf) for h in heads]
    k_dec = [(k[h] * jnp.exp(gl[h] - gc[h])).astype(bf) for h in heads]
    chunk_decay = [jnp.exp(gl[h]) for h in heads]
    for ci in range(g // c):
        rows = slice(ci * c, (ci + 1) * c)
        state = [s_ref[h] for h in heads]
        s16 = [state[h].astype(bf) for h in heads]
        nv16 = [(u[h][rows] - jnp.dot(w16[h][rows], s16[h], preferred_element_type=f32)).astype(bf) for h in heads]
        pad = lambda n: [jnp.zeros((n, d), bf)] if n else []
        placed = [jnp.concatenate(pad(ci * c) + [nv16[h]] + pad(g - (ci + 1) * c), axis=0) for h in heads]
        out = [jnp.dot(q_dec[h][rows], s16[h], preferred_element_type=f32)
               + jnp.dot(qk[h][rows], placed[h], preferred_element_type=f32) for h in heads]
        for h in heads:
            s_ref[h] = state[h] * chunk_decay[h][ci * c:ci * c + 1, :] + lax.dot_general(
                k_dec[h][rows], nv16[h], (((0,), (0,)), ((), ())), preferred_element_type=f32)
        for h in heads:
            zz = z_ref[rows, lanes[h]]
            normed = out[h] * lax.rsqrt(jnp.mean(out[h] * out[h], axis=-1, keepdims=True) + EPS) * ng_ref[...]
            o_ref[rows, lanes[h]] = normed * (zz / (1.0 + jnp.exp(-zz)))


def gdn_prompt(src, qkv_off, z_off, a, beta_logit, conv_prev, s0, conv_w, a_log, dt_bias, norm_g):
    b, t, _ = src.shape
    d, hb = HEAD_DIM, GDN_HB
    ng = GDN_HEADS // hb
    assert t % GDN_GROUP == 0 and GDN_HEADS % hb == 0 and qkv_off % (hb * d) == 0 and z_off % (hb * d) == 0
    qb, zb = qkv_off // (hb * d), z_off // (hb * d)
    prev = jnp.pad(conv_prev.astype(src.dtype), ((0, 0), (CONV_PAD - GDN_CONV + 1, 0), (0, 0)))
    by_group = lambda x: jnp.swapaxes(x.reshape(b, t, ng, hb), 1, 2)
    g = GDN_GROUP
    cols = lambda part: pl.BlockSpec((None, g, hb * d), lambda bi, gi, ti: (bi, ti, qb + part * ng + gi))
    hist = lambda part: pl.BlockSpec((None, CONV_PAD, hb * d), lambda bi, gi, ti: (bi, 0, part * ng + gi))
    cw = lambda part: pl.BlockSpec((GDN_CONV, hb * d), lambda bi, gi, ti: (0, part * ng + gi))
    tok = pl.BlockSpec((None, None, g, hb), lambda bi, gi, ti: (bi, gi, ti, 0))
    head_const = pl.BlockSpec((None, 1, hb), lambda bi, gi, ti: (gi, 0, 0))
    state = pl.BlockSpec((None, hb, d, d), lambda bi, gi, ti: (bi, gi, 0, 0))
    return pl.pallas_call(
        _gdn_prompt_kernel,
        grid=(b, ng, t // g),
        in_specs=[cols(0), cols(1), cols(2), hist(0), hist(1), hist(2), tok, tok,
                  pl.BlockSpec((None, g, hb * d), lambda bi, gi, ti: (bi, ti, zb + gi)),
                  cw(0), cw(1), cw(2), head_const, head_const,
                  pl.BlockSpec((1, d), lambda bi, gi, ti: (0, 0)), state],
        out_specs=[pl.BlockSpec((None, g, hb * d), lambda bi, gi, ti: (bi, ti, gi)), state],
        out_shape=[jax.ShapeDtypeStruct((b, t, W_C), jnp.float32),
                   jax.ShapeDtypeStruct((b, GDN_HEADS, d, d), jnp.float32)],
        scratch_shapes=[pltpu.VMEM((3, CONV_PAD, hb * d), jnp.float32)],
        compiler_params=pltpu.CompilerParams(
            dimension_semantics=("parallel", "parallel", "arbitrary"), vmem_limit_bytes=VMEM_LIMIT),
        name="gdn_prompt",
    )(src, src, src, prev, prev, prev, by_group(a), by_group(beta_logit), src, conv_w, conv_w, conv_w,
      a_log.reshape(ng, 1, hb).astype(jnp.float32), dt_bias.reshape(ng, 1, hb).astype(jnp.float32),
      norm_g.reshape(1, d).astype(jnp.float32), s0)


def _gdn_decode_kernel(x_ref, prev_ref, cw_ref, a_ref, b_ref, z_ref, alog_ref, dtb_ref, ng_ref, s0_ref, o_ref, s_ref):
    f32, bf = jnp.float32, jnp.bfloat16
    d, t = HEAD_DIM, x_ref.shape[0]
    heads = range(GDN_HEADS)
    rows = max(t, MIN_MXU_ROWS)
    mx = lambda x: _pad_rows(x, rows).astype(bf)
    win = jnp.concatenate([prev_ref[...], x_ref[...]], axis=0)
    n = win.shape[0]
    y = sum(pltpu.roll(win, n - (CONV_PAD - GDN_CONV + 1 + i), 0)[:t] * cw_ref[i:i + 1, :] for i in range(GDN_CONV))
    y = y / (1.0 + jnp.exp(-y))
    pre = a_ref[...] + dtb_ref[...]
    gate = -jnp.exp(alog_ref[...]) * (jnp.maximum(pre, 0.0) + jnp.log(1.0 + jnp.exp(-jnp.abs(pre))))
    beta_all = 1.0 / (1.0 + jnp.exp(-b_ref[...]))
    run, gcum_rows = None, []
    for ti in range(t):
        run = gate[ti:ti + 1, :] if run is None else run + gate[ti:ti + 1, :]
        gcum_rows.append(run)
    gcum_all = jnp.concatenate(gcum_rows, axis=0)
    ri = lax.broadcasted_iota(jnp.int32, (t, t), 0)
    ci = lax.broadcasted_iota(jnp.int32, (t, t), 1)
    eye = (ri == ci).astype(f32)

    q = [y[:, h * d:(h + 1) * d] for h in heads]
    k = [y[:, W_C + h * d:W_C + (h + 1) * d] for h in heads]
    v = [y[:, 2 * W_C + h * d:2 * W_C + (h + 1) * d] for h in heads]
    q = [q[h] * lax.rsqrt(jnp.sum(q[h] * q[h], axis=-1, keepdims=True) + EPS) * (d ** -0.5) for h in heads]
    k = [k[h] * lax.rsqrt(jnp.sum(k[h] * k[h], axis=-1, keepdims=True) + EPS) for h in heads]
    gc = [gcum_all[:, h:h + 1] for h in heads]
    gr = [jnp.sum(jnp.broadcast_to(gc[h], (t, t)) * eye, axis=0, keepdims=True) for h in heads]
    beta = [beta_all[:, h:h + 1] for h in heads]
    kb = [k[h] * beta[h] for h in heads]
    k16 = [mx(k[h]) for h in heads]
    dec = [jnp.where(ri >= ci, jnp.exp(jnp.where(ri >= ci, gc[h] - gr[h], 0.0)), 0.0) for h in heads]
    dec_t = [jnp.where(ci >= ri, jnp.exp(jnp.where(ci >= ri, gr[h] - gc[h], 0.0)), 0.0) for h in heads]
    a_t = [jnp.where(ci > ri, lax.dot_general(k16[h], mx(kb[h]), _NT, preferred_element_type=f32)[:t, :t] * dec_t[h], 0.0)
           for h in heads]
    qk = [lax.dot_general(mx(q[h]), k16[h], _NT, preferred_element_type=f32)[:t, :t] * dec[h] for h in heads]
    eg = [jnp.exp(gc[h]) for h in heads]
    sol = [jnp.concatenate([v[h] * beta[h], kb[h] * eg[h]], axis=1) for h in heads]
    row = lax.broadcasted_iota(jnp.int32, (t, 1), 0)
    for i in range(1, t):
        upd = [sol[h][i:i + 1, :] - jnp.sum(a_t[h][:, i:i + 1] * sol[h], axis=0, keepdims=True) for h in heads]
        sol = [jnp.where(row == i, upd[h], sol[h]) for h in heads]
    u = [sol[h][:, :d] for h in heads]
    glast = [gc[h][t - 1:t, :] for h in heads]
    state = [s0_ref[h] for h in heads]
    s16 = [state[h].astype(bf) for h in heads]
    both = [jnp.dot(jnp.concatenate([mx(sol[h][:, d:]), mx(q[h] * eg[h])], axis=0), s16[h], preferred_element_type=f32)
            for h in heads]
    nv16 = [mx(u[h] - both[h][:t]) for h in heads]
    qk16 = [_pad_rows(jnp.concatenate([qk[h], jnp.zeros((t, rows - t), f32)], axis=1), rows).astype(bf) for h in heads]
    out = [both[h][rows:rows + t] + jnp.dot(qk16[h], nv16[h], preferred_element_type=f32)[:t] for h in heads]
    k_dec = [mx(k[h] * jnp.exp(glast[h] - gc[h])) for h in heads]
    for h in heads:
        s_ref[h] = state[h] * jnp.exp(glast[h]) + lax.dot_general(k_dec[h], nv16[h], (((0,), (0,)), ((), ())),
                                                                  preferred_element_type=f32)
        zz = z_ref[:, h * d:(h + 1) * d]
        normed = out[h] * lax.rsqrt(jnp.mean(out[h] * out[h], axis=-1, keepdims=True) + EPS) * ng_ref[...]
        o_ref[:, h * d:(h + 1) * d] = normed * (zz / (1.0 + jnp.exp(-zz)))


def gdn_decode(layer, src, qkv_off, z_off, a, beta_logit, conv_prev, s0, conv_w, a_log, dt_bias, norm_g):
    b, t, _ = src.shape
    d, h = HEAD_DIM, GDN_HEADS
    assert qkv_off % (3 * W_C) == 0 and z_off % W_C == 0
    prev = jnp.pad(conv_prev.astype(src.dtype), ((0, 0), (CONV_PAD - GDN_CONV + 1, 0), (0, 0)))
    per_seq = lambda rows, width, blk: pl.BlockSpec((None, rows, width), lambda i: (i, 0, blk))
    const = lambda *shape: pl.BlockSpec(shape, lambda i: (0,) * len(shape))
    return pl.pallas_call(
        _gdn_decode_kernel,
        grid=(b,),
        in_specs=[per_seq(t, 3 * W_C, qkv_off // (3 * W_C)), per_seq(CONV_PAD, 3 * W_C, 0), const(GDN_CONV, 3 * W_C),
                  per_seq(t, h, 0), per_seq(t, h, 0), per_seq(t, W_C, z_off // W_C),
                  const(1, h), const(1, h), const(1, d),
                  pl.BlockSpec((None, None, h, d, d), lambda i: (layer, i, 0, 0, 0))],
        out_specs=[per_seq(t, W_C, 0), pl.BlockSpec((None, h, d, d), lambda i: (i, 0, 0, 0))],
        out_shape=[jax.ShapeDtypeStruct((b, t, W_C), jnp.float32), jax.ShapeDtypeStruct((b, h, d, d), jnp.float32)],
        compiler_params=pltpu.CompilerParams(dimension_semantics=("parallel",), vmem_limit_bytes=VMEM_LIMIT),
        name="gdn_decode",
    )(src, prev, conv_w, a, beta_logit, src, a_log.reshape(1, h).astype(jnp.float32),
      dt_bias.reshape(1, h).astype(jnp.float32), norm_g.reshape(1, d).astype(jnp.float32), s0)


MEM_TQ = 512
MIN_MXU_ROWS = 16


def _mem_attn_kernel(q_ref, kv_ref, o_ref):
    nh, d, bf = MEM_HEADS, HEAD_DIM, jnp.bfloat16
    tq = q_ref.shape[0]
    m = kv_ref.shape[0] // (2 * nh)
    rows = max(tq, MIN_MXU_ROWS)
    for h in range(nh):
        k = kv_ref[pl.ds(h, m, stride=2 * nh), :].astype(bf)
        v = kv_ref[pl.ds(nh + h, m, stride=2 * nh), :].astype(bf)
        q = _pad_rows(q_ref[:, h * d:(h + 1) * d], rows).astype(bf)
        s = lax.dot_general(q, k, _NT, preferred_element_type=jnp.float32) * (d ** -0.5)
        e = jnp.exp(s - jnp.max(s, axis=-1, keepdims=True))
        p = e / jnp.sum(e, axis=-1, keepdims=True)
        o_ref[:, h * d:(h + 1) * d] = jnp.dot(p.astype(bf), v, preferred_element_type=jnp.float32)[:tq]


def mem_attention_core(q, mem_kv, layer):
    b, t, w = q.shape
    m = mem_kv.shape[2]
    tq = min(MEM_TQ, t)
    assert t % tq == 0
    kv = mem_kv.reshape(mem_kv.shape[0], b, m * 2 * MEM_HEADS, HEAD_DIM)
    return pl.pallas_call(
        _mem_attn_kernel,
        grid=(b, t // tq),
        in_specs=[pl.BlockSpec((None, tq, w), lambda bi, qi: (bi, qi, 0)),
                  pl.BlockSpec((None, None, m * 2 * MEM_HEADS, HEAD_DIM), lambda bi, qi: (layer, bi, 0, 0))],
        out_specs=pl.BlockSpec((None, tq, w), lambda bi, qi: (bi, qi, 0)),
        out_shape=jax.ShapeDtypeStruct((b, t, w), jnp.float32),
        compiler_params=pltpu.CompilerParams(
            dimension_semantics=("parallel", "arbitrary"), vmem_limit_bytes=VMEM_LIMIT),
        name="mem_attn",
    )(q, kv)


def rms_norm(x, g):
    xf = x.astype(jnp.float32)
    y = xf * lax.rsqrt(jnp.mean(xf * xf, axis=-1, keepdims=True) + EPS)
    return (y * g.astype(jnp.float32)).astype(x.dtype)


def mem_attention(x, gain, mem_kv, mem_layer, w_q, w_o, layer):
    o = mem_attention_core(dense(x, w_q, layer, gain=gain), mem_kv, mem_layer)
    return dense(o, w_o, layer, resid=x)


def run_group(x, mem, cache, p):
    prompt = cache is None
    b, t, _ = x.shape
    past = 0 if prompt else cache['page_table'].shape[1] * cache['nsa_kv'].shape[2]
    names = ('nsa_kv', 'dsa_kv', 'idx_k', 'win_kv', 'gdn', 'conv') + (('mem_kv',) if prompt else ())
    out = {nm: [] for nm in names}
    bg, tg = (b, t) if prompt else (1, b * t)
    row_pos = past + jnp.arange(tg) % t
    for l in range(DEPTH):
        hproj = dense(x, p['w_in'], l, gain=p['norm_mix_g'][l])
        (q_a, q_b, q_i, nsa_new, win_new, dsa_new, idx_new,
         k_slc, v_slc, k_win, v_win, k_dsa, v_dsa, k_idx) = rope_split(hproj.reshape(bg, tg, PROJ_WIDTH), row_pos)
        nsa_new = nsa_new.reshape(b, t, NSA_ROWS, HEAD_DIM)
        win_new = win_new.reshape(b, t, 2, HEAD_DIM)
        dsa_new = dsa_new.reshape(b, t, 2, HEAD_DIM)
        idx_new = idx_new.reshape(b, t, IDX_DIM)
        g_a, w_i, a_c, b_c = (small_col(hproj, nm) for nm in ('g_a', 'w_i', 'a_c', 'b_c'))
        qkv_off, z_off = PROJ_OFF['qkv_c'], PROJ_OFF['z_c']
        qkv_c = hproj[..., qkv_off:qkv_off + 3 * W_C]
        if prompt:
            conv_prev = jnp.zeros((b, GDN_CONV - 1, 3 * W_C), x.dtype)
            s0 = jnp.zeros((b, GDN_HEADS, HEAD_DIM, HEAD_DIM), jnp.float32)
            mem_kv = dense(mem, p['w_mem_kv'], l, gain=p['mem_norm_g'][l]).reshape(b, -1, 2, MEM_HEADS, HEAD_DIM)
            k_cmp, v_cmp = nsa_compress_prompt(nsa_new, p['nsa_cmp_pe'][l], p['nsa_cmp_w1'][l], p['nsa_cmp_w2'][l])
            o_a = nsa_prompt_attention(q_a, g_a, k_cmp, v_cmp, (t - CMP_LEN) // CMP_STRIDE + 1,
                                       k_slc, v_slc, k_win, v_win)
            o_b = dsa_prompt_attention(q_b, q_i, w_i, k_dsa, v_dsa, k_idx)
            win_out = win_new[:, -min(WINDOW, t):]
            o_c, s_new = gdn_prompt(hproj, qkv_off, z_off, a_c, b_c, conv_prev, s0, p['gdn_conv_w'][l],
                                    p['gdn_a_log'][l], p['gdn_dt_bias'][l], p['gdn_norm_g'][l])
            conv_new = jnp.concatenate([conv_prev, qkv_c[:, -(GDN_CONV - 1):]], axis=1)[:, -(GDN_CONV - 1):]
        else:
            pt = cache['page_table']
            conv_prev, s0 = cache['conv'][l], cache['gdn'][l]
            mem_kv = None
            q_i = jnp.swapaxes(q_i.reshape(IDX_HEADS, b, t, IDX_DIM), 0, 1)
            o_a = nsa_sample_attention(l, q_a.reshape(b, t, W_A), g_a, nsa_new, win_new, cache['nsa_kv'], cache['win_kv'],
                                       pt, p['nsa_cmp_pe'][l], p['nsa_cmp_w1'][l], p['nsa_cmp_w2'][l])
            o_b = dsa_sample_attention(l, q_b.reshape(b, t, W_B), q_i, w_i, dsa_new, idx_new, cache['dsa_kv'],
                                       cache['idx_k'], pt)
            win_out = win_new
            o_c, s_new = gdn_decode(l, hproj, qkv_off, z_off, a_c, b_c, conv_prev, cache['gdn'], p['gdn_conv_w'][l],
                                    p['gdn_a_log'][l], p['gdn_dt_bias'][l], p['gdn_norm_g'][l])
            conv_new = jnp.concatenate([conv_prev, qkv_c], axis=1)[:, -(GDN_CONV - 1):]
        mix = branch_mix(o_a.reshape(b * t, W_A), o_b.reshape(b * t, W_B), o_c.reshape(b * t, W_C),
                         hproj.reshape(b * t, PROJ_WIDTH), PROJ_OFF['merge'],
                         p['w_branch_a'], p['w_branch_b'], p['w_branch_c'], l)
        x = dense(mix, p['w_mix_out'], l, resid=x).reshape(b, t, D_MODEL)
        mem_src, mem_layer = (mem_kv[None], 0) if prompt else (cache['mem_kv'], l)
        x = mem_attention(x, p['norm_mem_g'][l], mem_src, mem_layer, p['w_mem_q'], p['w_mem_o'], l)
        x = dense(ffn_act(x, p['norm_ffn_g'][l], p['w_ffn_gate'], p['w_ffn_up'], l), p['w_ffn_down'], l, resid=x)
        out['nsa_kv'].append(nsa_new)
        out['dsa_kv'].append(dsa_new)
        out['idx_k'].append(idx_new)
        out['win_kv'].append(win_out)
        out['gdn'].append(s_new)
        out['conv'].append(conv_new)
        if prompt:
            out['mem_kv'].append(mem_kv)
    y = rms_norm(x, p['norm_final_g'])
    out = {nm: jnp.stack(v) for nm, v in out.items()}
    if not prompt:
        wlen = cache['win_kv'].shape[2]
        out['win_kv'] = jnp.concatenate([cache['win_kv'], out['win_kv']], axis=2)[:, :, -wlen:]
    return y, out


def kernel(x_prompt, x_sample, mem_prompt, cache_nsa_kv, cache_dsa_kv, cache_dsa_idx_k, cache_win_kv, cache_mem_kv, state_gdn, state_conv, page_table, norm_mix_g, w_in, nsa_cmp_pe, nsa_cmp_w1, nsa_cmp_w2, gdn_conv_w, gdn_a_log, gdn_dt_bias, gdn_norm_g, w_branch_a, w_branch_b, w_branch_c, w_mix_out, norm_mem_g, mem_norm_g, w_mem_q, w_mem_kv, w_mem_o, norm_ffn_g, w_ffn_gate, w_ffn_up, w_ffn_down, norm_final_g):
    p = dict(norm_mix_g=norm_mix_g, w_in=w_in, nsa_cmp_pe=nsa_cmp_pe, nsa_cmp_w1=nsa_cmp_w1,
             nsa_cmp_w2=nsa_cmp_w2, gdn_conv_w=gdn_conv_w, gdn_a_log=gdn_a_log, gdn_dt_bias=gdn_dt_bias,
             gdn_norm_g=gdn_norm_g, w_branch_a=w_branch_a, w_branch_b=w_branch_b, w_branch_c=w_branch_c,
             w_mix_out=w_mix_out, norm_mem_g=norm_mem_g, mem_norm_g=mem_norm_g, w_mem_q=w_mem_q,
             w_mem_kv=w_mem_kv, w_mem_o=w_mem_o, norm_ffn_g=norm_ffn_g, w_ffn_gate=w_ffn_gate,
             w_ffn_up=w_ffn_up, w_ffn_down=w_ffn_down, norm_final_g=norm_final_g)
    for nm in ('w_branch_a', 'w_branch_b', 'w_branch_c', 'w_mix_out', 'w_mem_q', 'w_mem_kv', 'w_mem_o',
               'w_ffn_gate', 'w_ffn_up', 'w_ffn_down'):
        p[nm] = p[nm].astype(jnp.bfloat16)
    p['w_in'] = permute_w_in(w_in.astype(jnp.bfloat16))
    cache = dict(nsa_kv=cache_nsa_kv, dsa_kv=cache_dsa_kv, idx_k=cache_dsa_idx_k, win_kv=cache_win_kv,
                 mem_kv=cache_mem_kv, gdn=state_gdn, conv=state_conv, page_table=page_table)
    y_prompt, sp = run_group(x_prompt, mem_prompt, None, p)
    y_sample, ss = run_group(x_sample, None, cache, p)
    return (y_prompt, y_sample,
            sp['nsa_kv'], sp['dsa_kv'], sp['idx_k'], sp['win_kv'], sp['gdn'], sp['conv'], sp['mem_kv'],
            ss['nsa_kv'], ss['dsa_kv'], ss['idx_k'], ss['win_kv'], ss['gdn'], ss['conv'])
```
